```python
import jax, jax.numpy as jnp
from jax import lax
import numpy as np

D_MODEL = 2048
BATCH = 4
SEQ = 2048
DEPTH = 2

HEAD_DIM = 128
N_MIX_HEADS = D_MODEL // HEAD_DIM
A_HEADS = N_MIX_HEADS // 4
B_HEADS = (N_MIX_HEADS - A_HEADS) // 2
C_HEADS = N_MIX_HEADS - A_HEADS - B_HEADS
C_KV_HEADS = 2
A_KEY_DIM = 128
A_CHUNK = 16
DILATED_PATTERNS = ((128, 1), (512, 4), (2048, 16))
C_WINDOW = 128
BLOCK = 128
ROPE_THETA = 500000.0
ROPE_DIM = HEAD_DIM // 4
D_FF = 5632
CONV_WIDTH = 3
LN_EPS = 1e-5
ALPHA = (2 * DEPTH) ** 0.25
BETA = (8 * DEPTH) ** -0.25

A_KEY_WIDTH = A_HEADS * A_KEY_DIM
A_WIDTH = A_HEADS * HEAD_DIM
B_WIDTH = B_HEADS * HEAD_DIM
C_WIDTH = C_HEADS * HEAD_DIM
C_KV_WIDTH = C_KV_HEADS * HEAD_DIM
MIX_WIDTH = A_WIDTH + B_WIDTH + C_WIDTH
PROJ_SIZES = (A_KEY_WIDTH, A_KEY_WIDTH, A_WIDTH, A_WIDTH,
              B_WIDTH, B_WIDTH, B_WIDTH,
              C_WIDTH, C_KV_WIDTH, C_KV_WIDTH)
VALUE_BLOCKS = (2, 6, 9)
IN_WIDTH = sum(PROJ_SIZES)

kernel_name = "hybrid_hgrn2_dilated_swa_sink_convffn"


def layer_norm(x, g, b):
    xf = x.astype(jnp.float32)
    mu = jnp.mean(xf, -1, keepdims=True)
    var = jnp.mean(jnp.square(xf - mu), -1, keepdims=True)
    return ((xf - mu) * lax.rsqrt(var + LN_EPS) * g.astype(jnp.float32) + b.astype(jnp.float32)).astype(x.dtype)


def rms_norm(x, g):
    xf = x.astype(jnp.float32)
    return xf * lax.rsqrt(jnp.mean(jnp.square(xf), -1, keepdims=True) + LN_EPS) * g.astype(jnp.float32)


def rope_tables(seq):
    inv = ROPE_THETA ** (-jnp.arange(0, ROPE_DIM, 2, dtype=jnp.float32) / ROPE_DIM)
    ang = jnp.arange(seq, dtype=jnp.float32)[:, None] * inv[None, :]
    return jnp.cos(ang), jnp.sin(ang)


def partial_rope(x, cos, sin):
    half = ROPE_DIM // 2
    xr = x[..., :ROPE_DIM].astype(jnp.float32)
    x1, x2 = xr[..., :half], xr[..., half:]
    rot = jnp.concatenate([x1 * cos - x2 * sin, x2 * cos + x1 * sin], -1)
    return jnp.concatenate([rot.astype(x.dtype), x[..., ROPE_DIM:]], -1)


def banded_attention(q, k, v, max_lag, sink=None):
    b, h, L, dh = q.shape
    nb = -(-L // BLOCK)
    pad = nb * BLOCK - L
    padw = ((0, 0), (0, 0), (0, pad), (0, 0))
    qb = jnp.pad(q, padw).reshape(b, h, nb, BLOCK, dh)
    kb = jnp.pad(k, padw).reshape(b, h, nb, BLOCK, dh)
    vb = jnp.pad(v, padw).reshape(b, h, nb, BLOCK, dh)
    shift = ((0, 0), (0, 0), (1, 0), (0, 0), (0, 0))
    kk = jnp.concatenate([jnp.pad(kb, shift)[:, :, :-1], kb], axis=3)
    vv = jnp.concatenate([jnp.pad(vb, shift)[:, :, :-1], vb], axis=3)
    s = jnp.einsum('bhnqd,bhnkd->bhnqk', qb, kk, preferred_element_type=jnp.float32) * (dh ** -0.5)
    r = jnp.arange(BLOCK)[:, None]
    c = jnp.arange(2 * BLOCK)[None, :]
    lag = BLOCK + r - c
    kpos = (jnp.arange(nb)[:, None, None] - 1) * BLOCK + c[None]
    valid = ((lag >= 0) & (lag <= max_lag))[None] & (kpos >= 0)
    s = jnp.where(valid, s, -jnp.inf)
    m = jnp.max(s, -1)
    if sink is not None:
        sk = sink.astype(jnp.float32)[None, :, None, None]
        m = jnp.maximum(m, sk)
    e = jnp.exp(s - m[..., None])
    denom = jnp.sum(e, -1)
    if sink is not None:
        denom = denom + jnp.exp(sk - m)
    out = jnp.einsum('bhnqk,bhnkd->bhnqd', e, vv.astype(jnp.float32)) / denom[..., None]
    lse = m + jnp.log(denom)
    out = out.reshape(b, h, nb * BLOCK, dh)[:, :, :L].astype(q.dtype)
    return out, lse.reshape(b, h, nb * BLOCK)[:, :, :L]


def dilated_attention(q, k, v):
    b, h, S, dh = q.shape
    outs, lses = [], []
    for window, dil in DILATED_PATTERNS:
        L = S // dil
        def split(t):
            return t.reshape(b, h, L, dil, dh).transpose(0, 1, 3, 2, 4).reshape(b, h * dil, L, dh)
        o, lse = banded_attention(split(q), split(k), split(v), window // dil)
        outs.append(o.reshape(b, h, dil, L, dh).transpose(0, 1, 3, 2, 4).reshape(b, h, S, dh))
        lses.append(lse.reshape(b, h, dil, L).transpose(0, 1, 3, 2).reshape(b, h, S))
    w = jax.nn.softmax(jnp.stack(lses, 0), axis=0)
    out = jnp.einsum('pbhs,pbhsd->bhsd', w, jnp.stack(outs, 0).astype(jnp.float32))
    return out.astype(q.dtype)


def hgrn2(q, f_logit, i, g, lb, norm_w):
    b, S, _ = q.shape
    n = S // A_CHUNK
    def chunks(t, dim):
        return t.astype(jnp.float32).reshape(b, n, A_CHUNK, A_HEADS, dim).transpose(0, 3, 1, 2, 4)
    qc = jax.nn.silu(chunks(q, A_KEY_DIM))
    fg = lb.astype(jnp.float32).reshape(A_HEADS, 1, 1, A_KEY_DIM) + \
        (1.0 - lb.astype(jnp.float32).reshape(A_HEADS, 1, 1, A_KEY_DIM)) * jax.nn.sigmoid(chunks(f_logit, A_KEY_DIM))
    kc = 1.0 - fg
    vc = chunks(i, HEAD_DIM)
    bcum = jnp.cumsum(jnp.log(fg), axis=3)
    causal = jnp.tril(jnp.ones((A_CHUNK, A_CHUNK), dtype=bool))
    diff = bcum[:, :, :, :, None, :] - bcum[:, :, :, None, :, :]
    decay = jnp.exp(jnp.where(causal[:, :, None], diff, -jnp.inf))
    att = jnp.einsum('bhnik,bhnjk,bhnijk->bhnij', qc, kc, decay)
    o_intra = jnp.einsum('bhnij,bhnjv->bhniv', att, vc)
    blast = bcum[:, :, :, -1:, :]
    upd = jnp.einsum('bhnck,bhncv->bhnkv', kc * jnp.exp(blast - bcum), vc)
    cdec = jnp.exp(blast[:, :, :, 0, :])
    def step(state, inp):
        dec, u = inp
        return dec[..., None] * state + u, state
    s0 = jnp.zeros((b, A_HEADS, A_KEY_DIM, HEAD_DIM), jnp.float32)
    _, s_prev = lax.scan(step, s0, (jnp.moveaxis(cdec, 2, 0), jnp.moveaxis(upd, 2, 0)))
    s_prev = jnp.moveaxis(s_prev, 0, 2)
    o_inter = jnp.einsum('bhnck,bhnkv->bhncv', qc * jnp.exp(bcum), s_prev)
    o = (o_intra + o_inter).transpose(0, 2, 3, 1, 4).reshape(b, S, A_HEADS, HEAD_DIM)
    o = rms_norm(o, norm_w).reshape(b, S, A_WIDTH)
    return (o * jax.nn.silu(g.astype(jnp.float32))).astype(q.dtype)


def mixer_sublayer(x, w_in, lb, a_norm_w, c_sink, w_out, cos, sin):
    b, S, _ = x.shape
    proj = x @ w_in
    idx = [int(t) for t in np.cumsum(PROJ_SIZES)[:-1]]
    qA, fA, iA, gA, qB, kB, vB, qC, kC, vC = jnp.split(proj, idx, axis=-1)
    def heads(t, nh):
        return t.reshape(b, S, nh, HEAD_DIM).transpose(0, 2, 1, 3)
    def merge(t):
        return t.transpose(0, 2, 1, 3).reshape(b, S, -1)
    oA = hgrn2(qA, fA, iA, gA, lb, a_norm_w)
    oB = dilated_attention(partial_rope(heads(qB, B_HEADS), cos, sin),
                           partial_rope(heads(kB, B_HEADS), cos, sin), heads(vB, B_HEADS))
    rep = C_HEADS // C_KV_HEADS
    kCh = jnp.repeat(partial_rope(heads(kC, C_KV_HEADS), cos, sin), rep, axis=1)
    vCh = jnp.repeat(heads(vC, C_KV_HEADS), rep, axis=1)
    oC, _ = banded_attention(partial_rope(heads(qC, C_HEADS), cos, sin), kCh, vCh, C_WINDOW - 1, c_sink)
    mixed = jnp.concatenate([oA, merge(oB), merge(oC)], axis=-1)
    return mixed @ w_out


def conv_ffn(x, w_gate, w_up, conv_w, conv_b, w_down):
    S = x.shape[1]
    g = x @ w_gate
    u = x @ w_up
    gp = jnp.pad(g, ((0, 0), (CONV_WIDTH - 1, 0), (0, 0)))
    gc = conv_b
    for j in range(CONV_WIDTH):
        gc = gc + conv_w[j] * gp[:, j:j + S]
    return (jax.nn.silu(gc) * u) @ w_down


def setup_inputs(seed: int = 0) -> dict:
    key = jax.random.key(seed)
    ks = jax.random.split(key, 24)
    f32 = jnp.float32
    x = jax.random.normal(ks[0], (BATCH, SEQ, D_MODEL), f32)
    blocks = []
    for j, size in enumerate(PROJ_SIZES):
        w = jax.random.normal(ks[1 + j], (DEPTH, D_MODEL, size), f32) * D_MODEL ** -0.5
        if j in VALUE_BLOCKS:
            w = w * BETA
        blocks.append(w)
    w_in = jnp.concatenate(blocks, axis=-1)
    lb_logits = jax.random.normal(ks[11], (DEPTH, A_KEY_WIDTH), f32)
    a_norm_w = 1.0 + 0.02 * jax.random.normal(ks[12], (DEPTH, HEAD_DIM), f32)
    c_sinks = jax.random.normal(ks[13], (DEPTH, C_HEADS), f32)
    w_out = jax.random.normal(ks[14], (DEPTH, MIX_WIDTH, D_MODEL), f32) * (MIX_WIDTH ** -0.5) * BETA
    ln1_g = 1.0 + 0.02 * jax.random.normal(ks[15], (DEPTH, D_MODEL), f32)
    ln1_b = 0.02 * jax.random.normal(ks[16], (DEPTH, D_MODEL), f32)
    w_gate = jax.random.normal(ks[17], (DEPTH, D_MODEL, D_FF), f32) * D_MODEL ** -0.5
    w_up = jax.random.normal(ks[18], (DEPTH, D_MODEL, D_FF), f32) * (D_MODEL ** -0.5) * BETA
    conv_w = jax.random.normal(ks[19], (DEPTH, CONV_WIDTH, D_FF), f32) * CONV_WIDTH ** -0.5
    conv_b = 0.02 * jax.random.normal(ks[20], (DEPTH, D_FF), f32)
    w_down = jax.random.normal(ks[21], (DEPTH, D_FF, D_MODEL), f32) * (D_FF ** -0.5) * BETA
    ln2_g = 1.0 + 0.02 * jax.random.normal(ks[22], (DEPTH, D_MODEL), f32)
    ln2_b = 0.02 * jax.random.normal(ks[23], (DEPTH, D_MODEL), f32)
    return {"x": x, "w_in": w_in, "lb_logits": lb_logits, "a_norm_w": a_norm_w, "c_sinks": c_sinks,
            "w_out": w_out, "ln1_g": ln1_g, "ln1_b": ln1_b, "w_gate": w_gate, "w_up": w_up,
            "conv_w": conv_w, "conv_b": conv_b, "w_down": w_down, "ln2_g": ln2_g, "ln2_b": ln2_b}


def reference(x, w_in, lb_logits, a_norm_w, c_sinks, w_out, ln1_g, ln1_b, w_gate, w_up,
              conv_w, conv_b, w_down, ln2_g, ln2_b):
    cos, sin = rope_tables(x.shape[1])
    lbs = jnp.cumsum(jax.nn.softmax(lb_logits.astype(jnp.float32), axis=0), axis=0)
    lbs = lbs - lbs[0]
    for l in range(DEPTH):
        y = mixer_sublayer(x, w_in[l], lbs[l], a_norm_w[l], c_sinks[l], w_out[l], cos, sin)
        x = layer_norm(ALPHA * x + y, ln1_g[l], ln1_b[l])
        y = conv_ffn(x, w_gate[l], w_up[l], conv_w[l], conv_b[l], w_down[l])
        x = layer_norm(ALPHA * x + y, ln2_g[l], ln2_b[l])
    return x
```

```python
import functools

import numpy as np
import jax
import jax.numpy as jnp
from jax import lax
from jax.experimental import pallas as pl
from jax.experimental.pallas import tpu as pltpu

HEAD_DIM = 128
A_HEADS = 4
B_HEADS = 6
C_HEADS = 6
C_KV_HEADS = 2
C_REP = C_HEADS // C_KV_HEADS
DILATIONS = (1, 4, 16)
ATT_BLOCK = 128
C_MAX_LAG = 127
ROPE_THETA = 500000.0
ROPE_DIM = HEAD_DIM // 4
ROPE_HALF = ROPE_DIM // 2
CONV_WIDTH = 3
LN_EPS = 1e-5

QA_BLK, FA_BLK, IA_BLK, GA_BLK = 0, 4, 8, 12
QB_BLK, KB_BLK, VB_BLK = 16, 22, 28
QC_BLK, KC_BLK, VC_BLK = 34, 40, 42

HGRN_CHUNK = 128
HGRN_LEVELS = (64, 32, 16, 8, 4, 2, 1)

VMEM_LIMIT_BYTES = 56 * 1024 * 1024

F32 = jnp.float32
BF16 = jnp.bfloat16
NEG_BIG = -1e30


def _params(semantics):
    return pltpu.CompilerParams(dimension_semantics=semantics, vmem_limit_bytes=VMEM_LIMIT_BYTES)


def _matmul_kernel(x_ref, w_ref, o_ref):
    o_ref[...] = jnp.dot(x_ref[...], w_ref[...], preferred_element_type=F32).astype(o_ref.dtype)


def _matmul(x, w, bm, bn, out_dtype):
    m, k = x.shape
    n = w.shape[1]
    return pl.pallas_call(
        _matmul_kernel,
        grid=(m // bm, n // bn),
        in_specs=[pl.BlockSpec((bm, k), lambda i, j: (i, 0)),
                  pl.BlockSpec((k, bn), lambda i, j: (0, j))],
        out_specs=pl.BlockSpec((bm, bn), lambda i, j: (i, j)),
        out_shape=jax.ShapeDtypeStruct((m, n), out_dtype),
        compiler_params=_params(("parallel", "parallel")),
        name="in_proj",
    )(x, w)


def _rope_tables(seq):
    inv = ROPE_THETA ** (-jnp.arange(0, ROPE_DIM, 2, dtype=F32) / ROPE_DIM)
    ang = jnp.arange(seq, dtype=F32)[:, None] * inv[None, :]
    cos, sin = jnp.cos(ang), jnp.sin(ang)
    rest = HEAD_DIM - ROPE_DIM
    c = jnp.concatenate([cos, cos, jnp.ones((seq, rest), F32)], -1)
    a = jnp.concatenate([-sin, jnp.zeros((seq, HEAD_DIM - ROPE_HALF), F32)], -1)
    b = jnp.concatenate([jnp.zeros((seq, ROPE_HALF), F32), sin, jnp.zeros((seq, rest), F32)], -1)
    return c, a, b


def _rope(x, c, a, b):
    return (x * c + pltpu.roll(x, HEAD_DIM - ROPE_HALF, 1) * a + pltpu.roll(x, ROPE_HALF, 1) * b)


def _hgrn_constants():
    c = HGRN_CHUNK
    i = np.arange(c)[:, None]
    l = np.arange(c)[None, :]
    secs = [(l <= i), (l > i)]
    lvl = np.full((c, c), -1, np.int32)
    j = l
    for n, h in enumerate(HGRN_LEVELS):
        ref = (i // (2 * h)) * (2 * h) + h - 1
        odd = (i // h) % 2 == 1
        secs.append(np.where(odd, (l > ref) & (l <= i), (l > i) & (l <= ref)))
        pair = (i // (2 * h) == j // (2 * h)) & (i // h != j // h) & (i > j)
        lvl[pair] = n
    lvl[np.arange(c), np.arange(c)] = len(HGRN_LEVELS)
    e_all = np.concatenate(secs, 0).astype(np.float32)
    return jnp.asarray(e_all, BF16), jnp.asarray(lvl)


def _split3(x):
    p1 = x.astype(BF16)
    r1 = x - p1.astype(F32)
    p2 = r1.astype(BF16)
    p3 = (r1 - p2.astype(F32)).astype(BF16)
    return p1, p2, p3


def _dot_nt(a, b, **kw):
    return lax.dot_general(a, b, (((1,), (1,)), ((), ())), preferred_element_type=F32, **kw)


def _dot_tn(a, b, **kw):
    return lax.dot_general(a, b, (((0,), (0,)), ((), ())), preferred_element_type=F32, **kw)


def _hgrn_kernel(q_ref, f_ref, i_ref, g_ref, lb_ref, nw_ref, e_ref, lvl_ref, o_ref, st_ref):
    c = HGRN_CHUNK
    n_chunks = q_ref.shape[0] // c
    hi = lax.Precision.HIGHEST
    st_ref[...] = jnp.zeros_like(st_ref)
    lb = lb_ref[...]
    nw = nw_ref[...]
    row = lax.broadcasted_iota(jnp.int32, (c, 1), 0)

    def chunk(ci, carry):
        rows = pl.ds(pl.multiple_of(ci * c, c), c)
        q = q_ref[rows, :]
        qs = q * jax.nn.sigmoid(q)
        fg = lb + (1.0 - lb) * jax.nn.sigmoid(f_ref[rows, :])
        kk = 1.0 - fg
        v = i_ref[rows, :]
        lf = jnp.log(fg)
        p1, p2, p3 = _split3(lf)
        e_all = e_ref[...]
        gsum = (jnp.dot(e_all, p1, preferred_element_type=F32)
                + jnp.dot(e_all, p2, preferred_element_type=F32)
                + jnp.dot(e_all, p3, preferred_element_type=F32))
        dec = jnp.exp(gsum)
        q_in = qs * dec[0:c]
        k_end = kk * dec[c:2 * c]
        lvl = lvl_ref[...]
        att = jnp.where(lvl == len(HGRN_LEVELS), jnp.sum(qs * kk, axis=-1, keepdims=True), 0.0)
        for n, h in enumerate(HGRN_LEVELS):
            odd = ((row // h) % 2) == 1
            xh = jnp.where(odd, qs, kk) * dec[(2 + n) * c:(3 + n) * c]
            att = att + jnp.where(lvl == n, _dot_nt(xh, xh, precision=hi), 0.0)
        st = st_ref[...]
        o = jnp.dot(att, v, preferred_element_type=F32, precision=hi) + _dot_nt(q_in, st, precision=hi)
        st_ref[...] = st * dec[c - 1:c] + _dot_tn(v, k_end, precision=hi)
        g = g_ref[rows, :]
        o = o * lax.rsqrt(jnp.mean(o * o, axis=-1, keepdims=True) + LN_EPS) * nw
        o_ref[rows, :] = (o * (g * jax.nn.sigmoid(g))).astype(o_ref.dtype)
        return carry

    lax.fori_loop(0, n_chunks, chunk, 0)


def _hgrn2(proj3, lb, norm_w):
    bsz, seq, _ = proj3.shape
    e_all, lvl = _hgrn_constants()
    col = lambda blk: pl.BlockSpec((None, seq, HEAD_DIM), lambda b, h, blk=blk: (b, 0, blk + h))
    const = lambda shape: pl.BlockSpec(shape, lambda b, h: (0,) * len(shape))
    return pl.pallas_call(
        _hgrn_kernel,
        grid=(bsz, A_HEADS),
        in_specs=[col(QA_BLK), col(FA_BLK), col(IA_BLK), col(GA_BLK),
                  pl.BlockSpec((None, 1, HEAD_DIM), lambda b, h: (h, 0, 0)),
                  const((1, HEAD_DIM)), const(e_all.shape), const(lvl.shape)],
        out_specs=pl.BlockSpec((None, seq, HEAD_DIM), lambda b, h: (b, 0, h)),
        out_shape=jax.ShapeDtypeStruct((bsz, seq, A_HEADS * HEAD_DIM), BF16),
        scratch_shapes=[pltpu.VMEM((HEAD_DIM, HEAD_DIM), F32)],
        compiler_params=_params(("parallel", "parallel")),
        name="hgrn2",
    )(proj3, proj3, proj3, proj3, lb.reshape(A_HEADS, 1, HEAD_DIM), norm_w.reshape(1, HEAD_DIM), e_all, lvl)


def _band_masks(max_lag):
    r = lax.broadcasted_iota(jnp.int32, (ATT_BLOCK, ATT_BLOCK), 0)
    col = lax.broadcasted_iota(jnp.int32, (ATT_BLOCK, ATT_BLOCK), 1)
    return col <= r, (ATT_BLOCK + r - col) <= max_lag


def _tile_rows(mask, reps):
    return mask if reps == 1 else jnp.concatenate([mask] * reps, axis=0)


def _scores(q, k, mask):
    s = _dot_nt(q.astype(BF16), k.astype(BF16)) * (HEAD_DIM ** -0.5)
    return jnp.where(mask, s, NEG_BIG)


def _dilated_kernel(q_ref, k_ref, v_ref, c_ref, a_ref, b_ref, o_ref, qr_ref, kr_ref, m_ref, l_ref, acc_ref):
    seq = q_ref.shape[0]
    blk = ATT_BLOCK
    c, a, b = c_ref[...], a_ref[...], b_ref[...]
    qr_ref[...] = _rope(q_ref[...], c, a, b)
    kr_ref[...] = _rope(k_ref[...], c, a, b)
    m_ref[...] = jnp.full_like(m_ref, NEG_BIG)
    l_ref[...] = jnp.zeros_like(l_ref)
    acc_ref[...] = jnp.zeros_like(acc_ref)
    cur_mask, prev_mask = _band_masks(ATT_BLOCK)

    def update(rows, s_list, v_list):
        m_old = m_ref[rows, :]
        m_blk = functools.reduce(jnp.maximum, [jnp.max(s, axis=-1, keepdims=True) for s in s_list])
        m_new = jnp.maximum(m_old, m_blk)
        alpha = jnp.exp(m_old - m_new)
        l_new = alpha * l_ref[rows, :]
        acc = alpha * acc_ref[rows, :]
        for s, v in zip(s_list, v_list):
            e = jnp.exp(s - m_new)
            l_new = l_new + jnp.sum(e, axis=-1, keepdims=True)
            acc = acc + jnp.dot(e.astype(BF16), v.astype(BF16), preferred_element_type=F32)
        m_ref[rows, :] = m_new
        l_ref[rows, :] = l_new
        acc_ref[rows, :] = acc

    for dil in DILATIONS:
        n_blocks = seq // (dil * blk)

        def residue(r, carry, dil=dil, n_blocks=n_blocks):
            def rows_of(n):
                start = r + n * (dil * blk)
                return pl.ds(start, blk, stride=dil) if dil > 1 else pl.ds(pl.multiple_of(start, blk), blk)

            rows0 = rows_of(0)
            update(rows0, [_scores(qr_ref[rows0, :], kr_ref[rows0, :], cur_mask)], [v_ref[rows0, :]])

            def block(n, carry2):
                rows, prev = rows_of(n), rows_of(n - 1)
                q = qr_ref[rows, :]
                update(rows,
                       [_scores(q, kr_ref[prev, :], prev_mask), _scores(q, kr_ref[rows, :], cur_mask)],
                       [v_ref[prev, :], v_ref[rows, :]])
                return carry2

            if n_blocks > 1:
                lax.fori_loop(1, n_blocks, block, 0)
            return carry

        lax.fori_loop(0, dil, residue, 0)

    o_ref[...] = (acc_ref[...] / l_ref[...]).astype(o_ref.dtype)


def _dilated(proj3, tables):
    bsz, seq, _ = proj3.shape
    col = lambda blk: pl.BlockSpec((None, seq, HEAD_DIM), lambda b, h, blk=blk: (b, 0, blk + h))
    tab = pl.BlockSpec((seq, HEAD_DIM), lambda b, h: (0, 0))
    return pl.pallas_call(
        _dilated_kernel,
        grid=(bsz, B_HEADS),
        in_specs=[col(QB_BLK), col(KB_BLK), col(VB_BLK), tab, tab, tab],
        out_specs=pl.BlockSpec((None, seq, HEAD_DIM), lambda b, h: (b, 0, h)),
        out_shape=jax.ShapeDtypeStruct((bsz, seq, B_HEADS * HEAD_DIM), BF16),
        scratch_shapes=[pltpu.VMEM((seq, HEAD_DIM), F32)] * 5,
        compiler_params=_params(("parallel", "parallel")),
        name="dilated_attention",
    )(proj3, proj3, proj3, *tables)


def _swa_kernel(sink_ref, q0_ref, q1_ref, q2_ref, k_ref, v_ref, c_ref, a_ref, b_ref, o_ref, qr_ref, kr_ref):
    seq = k_ref.shape[0]
    blk = ATT_BLOCK
    grp = pl.program_id(1)
    c, a, b = c_ref[...], a_ref[...], b_ref[...]
    for j, q_ref in enumerate((q0_ref, q1_ref, q2_ref)):
        qr_ref[j] = _rope(q_ref[...], c, a, b).astype(BF16)
    kr_ref[...] = _rope(k_ref[...], c, a, b).astype(BF16)
    cur_mask, prev_mask = _band_masks(C_MAX_LAG)
    cur_mask, prev_mask = _tile_rows(cur_mask, C_REP), _tile_rows(prev_mask, C_REP)
    sink = jnp.concatenate([jnp.full((blk, 1), sink_ref[grp * C_REP + j], F32) for j in range(C_REP)], axis=0)

    def block(n, carry):
        rows = pl.ds(pl.multiple_of(n * blk, blk), blk)
        prev = pl.ds(pl.multiple_of(jnp.maximum(n - 1, 0) * blk, blk), blk)
        q = jnp.concatenate([qr_ref[j, rows, :] for j in range(C_REP)], axis=0)
        s_cur = _scores(q, kr_ref[rows, :], cur_mask)
        s_prev = _scores(q, kr_ref[prev, :], jnp.logical_and(prev_mask, n > 0))
        m = jnp.maximum(jnp.maximum(jnp.max(s_cur, -1, keepdims=True), jnp.max(s_prev, -1, keepdims=True)), sink)
        e_cur, e_prev = jnp.exp(s_cur - m), jnp.exp(s_prev - m)
        denom = jnp.sum(e_cur, -1, keepdims=True) + jnp.sum(e_prev, -1, keepdims=True) + jnp.exp(sink - m)
        num = (jnp.dot(e_cur.astype(BF16), v_ref[rows, :].astype(BF16), preferred_element_type=F32)
               + jnp.dot(e_prev.astype(BF16), v_ref[prev, :].astype(BF16), preferred_element_type=F32))
        out = num / denom
        for j in range(C_REP):
            o_ref[rows, j * HEAD_DIM:(j + 1) * HEAD_DIM] = out[j * blk:(j + 1) * blk].astype(o_ref.dtype)
        return carry

    lax.fori_loop(0, seq // blk, block, 0)


def _swa(proj3, sinks, tables):
    bsz, seq, _ = proj3.shape
    qcol = lambda j: pl.BlockSpec((None, seq, HEAD_DIM), lambda b, g, j=j: (b, 0, QC_BLK + g * C_REP + j))
    col = lambda blk: pl.BlockSpec((None, seq, HEAD_DIM), lambda b, g, blk=blk: (b, 0, blk + g))
    tab = pl.BlockSpec((seq, HEAD_DIM), lambda b, g: (0, 0))
    return pl.pallas_call(
        _swa_kernel,
        grid=(bsz, C_KV_HEADS),
        in_specs=[pl.BlockSpec(memory_space=pltpu.SMEM), qcol(0), qcol(1), qcol(2), col(KC_BLK), col(VC_BLK),
                  tab, tab, tab],
        out_specs=pl.BlockSpec((None, seq, C_REP * HEAD_DIM), lambda b, g: (b, 0, g)),
        out_shape=jax.ShapeDtypeStruct((bsz, seq, C_HEADS * HEAD_DIM), BF16),
        scratch_shapes=[pltpu.VMEM((C_REP, seq, HEAD_DIM), BF16), pltpu.VMEM((seq, HEAD_DIM), BF16)],
        compiler_params=_params(("parallel", "parallel")),
        name="swa_sink_attention",
    )(sinks, proj3, proj3, proj3, proj3, proj3, *tables)


def _layer_norm_store(z, g_ref, b_ref, o32_ref, o16_ref):
    mu = jnp.mean(z, axis=-1, keepdims=True)
    zc = z - mu
    var = jnp.mean(zc * zc, axis=-1, keepdims=True)
    y = zc * lax.rsqrt(var + LN_EPS) * g_ref[...] + b_ref[...]
    o32_ref[...] = y
    o16_ref[...] = y.astype(BF16)


def _out_proj_kernel(alpha, oa_ref, ob_ref, oc_ref, wa_ref, wb_ref, wc_ref, x_ref, g_ref, b_ref, o32_ref, o16_ref):
    y = (jnp.dot(oa_ref[...], wa_ref[...], preferred_element_type=F32)
         + jnp.dot(ob_ref[...], wb_ref[...], preferred_element_type=F32)
         + jnp.dot(oc_ref[...], wc_ref[...], preferred_element_type=F32))
    _layer_norm_store(alpha * x_ref[...] + y, g_ref, b_ref, o32_ref, o16_ref)


def _out_proj_ln(oa, ob, oc, w_out, x, g, b, alpha, bm):
    m, d = x.shape
    wa = w_out[:oa.shape[1]]
    wb = w_out[oa.shape[1]:oa.shape[1] + ob.shape[1]]
    wc = w_out[oa.shape[1] + ob.shape[1]:]
    rows = lambda a: pl.BlockSpec((bm, a.shape[1]), lambda i: (i, 0))
    full = lambda a: pl.BlockSpec(a.shape, lambda i: (0, 0))
    return pl.pallas_call(
        functools.partial(_out_proj_kernel, alpha),
        grid=(m // bm,),
        in_specs=[rows(oa), rows(ob), rows(oc), full(wa), full(wb), full(wc), rows(x), full(g), full(b)],
        out_specs=[rows(x), rows(x)],
        out_shape=[jax.ShapeDtypeStruct((m, d), F32), jax.ShapeDtypeStruct((m, d), BF16)],
        compiler_params=_params(("parallel",)),
        name="out_proj_ln",
    )(oa, ob, oc, wa, wb, wc, x, g, b)


def _ffn_up_kernel(tiles_per_seq, x_ref, wg_ref, wu_ref, cw_ref, cb_ref, h_ref, g_ref):
    bm = x_ref.shape[0]
    halo = g_ref.shape[0] - bm

    @pl.when(pl.program_id(1) % tiles_per_seq == 0)
    def _():
        g_ref[0:halo, :] = jnp.zeros((halo, g_ref.shape[1]), F32)

    x = x_ref[...]
    g_ref[halo:, :] = jnp.dot(x, wg_ref[...], preferred_element_type=F32)
    u = jnp.dot(x, wu_ref[...], preferred_element_type=F32)
    gc = cb_ref[...]
    for j in range(CONV_WIDTH):
        lag = CONV_WIDTH - 1 - j
        gc = gc + cw_ref[j:j + 1, :] * g_ref[halo - lag:halo - lag + bm, :]
    h_ref[...] = (gc * jax.nn.sigmoid(gc) * u).astype(h_ref.dtype)
    g_ref[0:halo, :] = g_ref[bm:bm + halo, :]


def _ffn_up(x16, w_gate, w_up, conv_w, conv_b, seq, bm, bn):
    m, d = x16.shape
    f = w_gate.shape[1]
    halo = 8
    wspec = pl.BlockSpec((d, bn), lambda j, i: (0, j))
    return pl.pallas_call(
        functools.partial(_ffn_up_kernel, seq // bm),
        grid=(f // bn, m // bm),
        in_specs=[pl.BlockSpec((bm, d), lambda j, i: (i, 0)), wspec, wspec,
                  pl.BlockSpec((CONV_WIDTH, bn), lambda j, i: (0, j)),
                  pl.BlockSpec((1, bn), lambda j, i: (0, j))],
        out_specs=pl.BlockSpec((bm, bn), lambda j, i: (i, j)),
        out_shape=jax.ShapeDtypeStruct((m, f), BF16),
        scratch_shapes=[pltpu.VMEM((bm + halo, bn), F32)],
        compiler_params=_params(("parallel", "arbitrary")),
        name="ffn_up_conv_gate",
    )(x16, w_gate, w_up, conv_w, conv_b)


def _ffn_down_kernel(alpha, h_ref, w_ref, x_ref, g_ref, b_ref, o32_ref, o16_ref, acc_ref):
    kstep = pl.program_id(1)

    @pl.when(kstep == 0)
    def _():
        acc_ref[...] = alpha * x_ref[...]

    acc_ref[...] += jnp.dot(h_ref[...], w_ref[...], preferred_element_type=F32)

    @pl.when(kstep == pl.num_programs(1) - 1)
    def _():
        _layer_norm_store(acc_ref[...], g_ref, b_ref, o32_ref, o16_ref)


def _ffn_down_ln(h, w_down, x, g, b, alpha, bm, bk):
    m, d = x.shape
    f = h.shape[1]
    rows = pl.BlockSpec((bm, d), lambda i, k: (i, 0))
    vec = pl.BlockSpec((1, d), lambda i, k: (0, 0))
    return pl.pallas_call(
        functools.partial(_ffn_down_kernel, alpha),
        grid=(m // bm, f // bk),
        in_specs=[pl.BlockSpec((bm, bk), lambda i, k: (i, k)), pl.BlockSpec((bk, d), lambda i, k: (k, 0)),
                  rows, vec, vec],
        out_specs=[rows, rows],
        out_shape=[jax.ShapeDtypeStruct((m, d), F32), jax.ShapeDtypeStruct((m, d), BF16)],
        scratch_shapes=[pltpu.VMEM((bm, d), F32)],
        compiler_params=_params(("parallel", "arbitrary")),
        name="ffn_down_ln",
    )(h, w_down, x, g, b)


def kernel(x, w_in, lb_logits, a_norm_w, c_sinks, w_out, ln1_g, ln1_b, w_gate, w_up, conv_w, conv_b, w_down,
           ln2_g, ln2_b):
    bsz, seq, d = x.shape
    depth = w_in.shape[0]
    alpha = (2 * depth) ** 0.25
    tables = _rope_tables(seq)
    lbs = jnp.cumsum(jax.nn.softmax(lb_logits.astype(F32), axis=0), axis=0)
    lbs = lbs - lbs[0]
    x32 = x.reshape(bsz * seq, d)
    x16 = x32.astype(BF16)
    for l in range(depth):
        proj = _matmul(x16, w_in[l].astype(BF16), bm=2048, bn=512, out_dtype=F32)
        proj3 = proj.reshape(bsz, seq, -1)
        oa = _hgrn2(proj3, lbs[l], a_norm_w[l]).reshape(bsz * seq, -1)
        ob = _dilated(proj3, tables).reshape(bsz * seq, -1)
        oc = _swa(proj3, c_sinks[l], tables).reshape(bsz * seq, -1)
        x32, x16 = _out_proj_ln(oa, ob, oc, w_out[l].astype(BF16), x32, ln1_g[l].reshape(1, d),
                                ln1_b[l].reshape(1, d), alpha, bm=256)
        h = _ffn_up(x16, w_gate[l].astype(BF16), w_up[l].astype(BF16), conv_w[l], conv_b[l].reshape(1, -1),
                    seq, bm=1024, bn=512)
        x32, x16 = _ffn_down_ln(h, w_down[l].astype(BF16), x32, ln2_g[l].reshape(1, d), ln2_b[l].reshape(1, d),
                                alpha, bm=512, bk=1408)
    return x32.reshape(bsz, seq, d)
```

```python
import functools

import numpy as np
import jax
import jax.numpy as jnp
from jax import lax
from jax.experimental import pallas as pl
from jax.experimental.pallas import tpu as pltpu

HEAD_DIM = 128
A_HEADS = 4
B_HEADS = 6
C_HEADS = 6
C_KV_HEADS = 2
C_REP = C_HEADS // C_KV_HEADS
DILATIONS = (1, 4, 16)
ATT_BLOCK = 128
C_MAX_LAG = 127
ROPE_THETA = 500000.0
ROPE_DIM = HEAD_DIM // 4
ROPE_HALF = ROPE_DIM // 2
CONV_WIDTH = 3
LN_EPS = 1e-5

QA_BLK, FA_BLK, IA_BLK, GA_BLK = 0, 4, 8, 12
QB_BLK, KB_BLK, VB_BLK = 16, 22, 28
QC_BLK, KC_BLK, VC_BLK = 34, 40, 42

HGRN_CHUNK = 128
HGRN_LEVELS = (64, 32, 16, 8, 4, 2, 1)

VMEM_LIMIT_BYTES = 56 * 1024 * 1024

F32 = jnp.float32
BF16 = jnp.bfloat16
NEG_BIG = -1e30


def _params(semantics):
    return pltpu.CompilerParams(dimension_semantics=semantics, vmem_limit_bytes=VMEM_LIMIT_BYTES)


def _matmul_kernel(x_ref, w_ref, o_ref):
    o_ref[...] = jnp.dot(x_ref[...], w_ref[...], preferred_element_type=F32).astype(o_ref.dtype)


def _matmul(x, w, bm, bn, out_dtype):
    m, k = x.shape
    n = w.shape[1]
    return pl.pallas_call(
        _matmul_kernel,
        grid=(m // bm, n // bn),
        in_specs=[pl.BlockSpec((bm, k), lambda i, j: (i, 0)),
                  pl.BlockSpec((k, bn), lambda i, j: (0, j))],
        out_specs=pl.BlockSpec((bm, bn), lambda i, j: (i, j)),
        out_shape=jax.ShapeDtypeStruct((m, n), out_dtype),
        compiler_params=_params(("parallel", "parallel")),
        name="in_proj",
    )(x, w)


def _rope_tables(seq):
    inv = ROPE_THETA ** (-jnp.arange(0, ROPE_DIM, 2, dtype=F32) / ROPE_DIM)
    ang = jnp.arange(seq, dtype=F32)[:, None] * inv[None, :]
    cos, sin = jnp.cos(ang), jnp.sin(ang)
    rest = HEAD_DIM - ROPE_DIM
    c = jnp.concatenate([cos, cos, jnp.ones((seq, rest), F32)], -1)
    a = jnp.concatenate([-sin, jnp.zeros((seq, HEAD_DIM - ROPE_HALF), F32)], -1)
    b = jnp.concatenate([jnp.zeros((seq, ROPE_HALF), F32), sin, jnp.zeros((seq, rest), F32)], -1)
    return c, a, b


def _rope(x, c, a, b):
    return (x * c + pltpu.roll(x, HEAD_DIM - ROPE_HALF, 1) * a + pltpu.roll(x, ROPE_HALF, 1) * b)


def _hgrn_constants():
    c = HGRN_CHUNK
    i = np.arange(c)[:, None]
    l = np.arange(c)[None, :]
    secs = [(l <= i), (l > i)]
    lvl = np.full((c, c), -1, np.int32)
    j = l
    for n, h in enumerate(HGRN_LEVELS):
        ref = (i // (2 * h)) * (2 * h) + h - 1
        odd = (i // h) % 2 == 1
        secs.append(np.where(odd, (l > ref) & (l <= i), (l > i) & (l <= ref)))
        pair = (i // (2 * h) == j // (2 * h)) & (i // h != j // h) & (i > j)
        lvl[pair] = n
    lvl[np.arange(c), np.arange(c)] = len(HGRN_LEVELS)
    e_all = np.concatenate(secs, 0).astype(np.float32)
    return jnp.asarray(e_all, BF16), jnp.asarray(lvl)


def _split3(x):
    p1 = x.astype(BF16)
    r1 = x - p1.astype(F32)
    p2 = r1.astype(BF16)
    p3 = (r1 - p2.astype(F32)).astype(BF16)
    return p1, p2, p3


def _dot_nt(a, b, **kw):
    return lax.dot_general(a, b, (((1,), (1,)), ((), ())), preferred_element_type=F32, **kw)


def _dot_tn(a, b, **kw):
    return lax.dot_general(a, b, (((0,), (0,)), ((), ())), preferred_element_type=F32, **kw)


def _hgrn_head_chunk(q, fl, v, g, st, lb, nw, e_all, lvl, row):
    c = HGRN_CHUNK
    qs = q * jax.nn.sigmoid(q)
    fg = lb + (1.0 - lb) * jax.nn.sigmoid(fl)
    kk = 1.0 - fg
    p1, p2, p3 = _split3(jnp.log(fg))
    g12 = jnp.dot(e_all, jnp.concatenate([p1, p2], axis=1), preferred_element_type=F32)
    gsum = g12[:, :HEAD_DIM] + g12[:, HEAD_DIM:] + jnp.dot(e_all, p3, preferred_element_type=F32)
    dec = jnp.exp(gsum)
    q_in = qs * dec[0:c]
    k_end = kk * dec[c:2 * c]
    att = jnp.where(lvl == len(HGRN_LEVELS), jnp.sum(qs * kk, axis=-1, keepdims=True), 0.0)
    for n, h in enumerate(HGRN_LEVELS):
        odd = ((row // h) % 2) == 1
        xh = (jnp.where(odd, qs, kk) * dec[(2 + n) * c:(3 + n) * c]).astype(BF16)
        att = att + jnp.where(lvl == n, _dot_nt(xh, xh), 0.0)
    v16 = v.astype(BF16)
    o = (jnp.dot(att.astype(BF16), v16, preferred_element_type=F32)
         + _dot_nt(q_in.astype(BF16), st.astype(BF16)))
    st_new = st * dec[c - 1:c] + _dot_tn(v16, k_end.astype(BF16))
    o = o * lax.rsqrt(jnp.mean(o * o, axis=-1, keepdims=True) + LN_EPS) * nw
    return (o * (g * jax.nn.sigmoid(g))).astype(BF16), st_new


def _hgrn_kernel(q_ref, f_ref, i_ref, g_ref, lb_ref, nw_ref, e_ref, lvl_ref, o_ref, st_ref):
    c = HGRN_CHUNK
    n_chunks = q_ref.shape[0] // c
    st_ref[...] = jnp.zeros_like(st_ref)
    nw = nw_ref[...]
    row = lax.broadcasted_iota(jnp.int32, (c, 1), 0)

    def chunk(ci, carry):
        rows = pl.ds(pl.multiple_of(ci * c, c), c)
        e_all, lvl = e_ref[...], lvl_ref[...]
        results = []
        for h in range(A_HEADS):
            cols = slice(h * HEAD_DIM, (h + 1) * HEAD_DIM)
            results.append(_hgrn_head_chunk(q_ref[rows, cols], f_ref[rows, cols], i_ref[rows, cols],
                                            g_ref[rows, cols], st_ref[h], lb_ref[:, cols], nw, e_all, lvl, row))
        for h, (o, st_new) in enumerate(results):
            o_ref[rows, h * HEAD_DIM:(h + 1) * HEAD_DIM] = o
            st_ref[h] = st_new
        return carry

    lax.fori_loop(0, n_chunks, chunk, 0)


def _hgrn2(proj3, lb, norm_w):
    bsz, seq, _ = proj3.shape
    e_all, lvl = _hgrn_constants()
    width = A_HEADS * HEAD_DIM
    col = lambda blk: pl.BlockSpec((None, seq, width), lambda b, blk=blk: (b, 0, blk // A_HEADS))
    const = lambda shape: pl.BlockSpec(shape, lambda b: (0,) * len(shape))
    return pl.pallas_call(
        _hgrn_kernel,
        grid=(bsz,),
        in_specs=[col(QA_BLK), col(FA_BLK), col(IA_BLK), col(GA_BLK),
                  const((1, width)), const((1, HEAD_DIM)), const(e_all.shape), const(lvl.shape)],
        out_specs=pl.BlockSpec((None, seq, width), lambda b: (b, 0, 0)),
        out_shape=jax.ShapeDtypeStruct((bsz, seq, width), BF16),
        scratch_shapes=[pltpu.VMEM((A_HEADS, HEAD_DIM, HEAD_DIM), F32)],
        compiler_params=_params(("parallel",)),
        name="hgrn2",
    )(proj3, proj3, proj3, proj3, lb.reshape(1, width), norm_w.reshape(1, HEAD_DIM), e_all, lvl)


def _band_masks(max_lag):
    r = lax.broadcasted_iota(jnp.int32, (ATT_BLOCK, ATT_BLOCK), 0)
    col = lax.broadcasted_iota(jnp.int32, (ATT_BLOCK, ATT_BLOCK), 1)
    return col <= r, (ATT_BLOCK + r - col) <= max_lag


def _tile_softmax_many(blocks):
    scale = HEAD_DIM ** -0.5
    s_all = [[jnp.where(mask, _dot_nt(q16, k.astype(BF16)) * scale, NEG_BIG) for k, mask in zip(ks, masks)]
             for q16, ks, _, masks, _ in blocks]
    m_all = []
    for s, (_, _, _, _, m_floor) in zip(s_all, blocks):
        m = jnp.max(functools.reduce(jnp.maximum, s), axis=-1, keepdims=True)
        m_all.append(m if m_floor is None else jnp.maximum(m, m_floor))
    e_all = [[jnp.exp(si - m) for si in s] for s, m in zip(s_all, m_all)]
    acc_all = [functools.reduce(jnp.add, [jnp.dot(ei.astype(BF16), v.astype(BF16), preferred_element_type=F32)
                                          for ei, v in zip(e, vs)])
               for e, (_, _, vs, _, _) in zip(e_all, blocks)]
    l_all = [jnp.sum(functools.reduce(jnp.add, e), axis=-1, keepdims=True) for e in e_all]
    return list(zip(m_all, l_all, acc_all))


DIL_UNROLL = 4


def _dilated_kernel(q_ref, k_ref, v_ref, c_ref, a_ref, b_ref, o_ref, qr_ref, kr_ref, m_ref, l_ref, acc_ref):
    seq = q_ref.shape[0]
    blk = ATT_BLOCK
    c, a, b = c_ref[...], a_ref[...], b_ref[...]
    qr_ref[...] = _rope(q_ref[...], c, a, b)
    kr_ref[...] = _rope(k_ref[...], c, a, b)
    cur_mask, prev_mask = _band_masks(ATT_BLOCK)

    def run_group(specs, first):
        loaded = []
        for rows, prev, prev_ok in specs:
            ks, vs, masks = [kr_ref[rows, :]], [v_ref[rows, :]], [cur_mask]
            if prev is not None:
                ks.append(kr_ref[prev, :])
                vs.append(v_ref[prev, :])
                masks.append(prev_mask if prev_ok is None else jnp.logical_and(prev_mask, prev_ok))
            state = None if first else (m_ref[rows, :], l_ref[rows, :], acc_ref[rows, :])
            loaded.append((qr_ref[rows, :].astype(BF16), ks, vs, masks, state))
        pieces = _tile_softmax_many([(q16, ks, vs, masks, None if state is None else state[0])
                                     for q16, ks, vs, masks, state in loaded])
        results = []
        for (m, l, acc), (_, _, _, _, state) in zip(pieces, loaded):
            if state is None:
                shape = (blk, HEAD_DIM)
                results.append((jnp.broadcast_to(m, shape), jnp.broadcast_to(l, shape), acc))
            else:
                m_old, l_old, acc_old = state
                alpha = jnp.exp(m_old - m)
                results.append((m, alpha * l_old + l, alpha * acc_old + acc))
        for (rows, _, _), (m, l, acc) in zip(specs, results):
            m_ref[rows, :] = m
            l_ref[rows, :] = l
            acc_ref[rows, :] = acc

    def contiguous(n):
        return pl.ds(pl.multiple_of(n * blk, blk), blk)

    def dense_group(g, carry):
        specs = []
        for u in range(DIL_UNROLL):
            n = g * DIL_UNROLL + u
            specs.append((contiguous(n), contiguous(jnp.maximum(n - 1, 0)), n > 0))
        run_group(specs, first=True)
        return carry

    lax.fori_loop(0, seq // (blk * DIL_UNROLL), dense_group, 0)

    for dil in DILATIONS[1:]:
        n_blocks = seq // (dil * blk)

        def strided(r, n, dil=dil):
            return pl.ds(r + n * (dil * blk), blk, stride=dil)

        def head_group(g, carry, strided=strided):
            run_group([(strided(g * DIL_UNROLL + u, 0), None, None) for u in range(DIL_UNROLL)], first=False)
            return carry

        lax.fori_loop(0, dil // DIL_UNROLL, head_group, 0)

        def tail_group(t, carry, strided=strided, dil=dil):
            n = 1 + t // (dil // DIL_UNROLL)
            g = t % (dil // DIL_UNROLL)
            run_group([(strided(g * DIL_UNROLL + u, n), strided(g * DIL_UNROLL + u, n - 1), None)
                       for u in range(DIL_UNROLL)], first=False)
            return carry

        if n_blocks > 1:
            lax.fori_loop(0, (n_blocks - 1) * (dil // DIL_UNROLL), tail_group, 0)

    o_ref[...] = (acc_ref[...] / l_ref[...]).astype(o_ref.dtype)


def _dilated(proj3, tables):
    bsz, seq, _ = proj3.shape
    col = lambda blk: pl.BlockSpec((None, seq, HEAD_DIM), lambda b, h, blk=blk: (b, 0, blk + h))
    tab = pl.BlockSpec((seq, HEAD_DIM), lambda b, h: (0, 0))
    return pl.pallas_call(
        _dilated_kernel,
        grid=(bsz, B_HEADS),
        in_specs=[col(QB_BLK), col(KB_BLK), col(VB_BLK), tab, tab, tab],
        out_specs=pl.BlockSpec((None, seq, HEAD_DIM), lambda b, h: (b, 0, h)),
        out_shape=jax.ShapeDtypeStruct((bsz, seq, B_HEADS * HEAD_DIM), BF16),
        scratch_shapes=[pltpu.VMEM((seq, HEAD_DIM), F32)] * 5,
        compiler_params=_params(("parallel", "parallel")),
        name="dilated_attention",
    )(proj3, proj3, proj3, *tables)


SWA_UNROLL = 2


def _swa_kernel(sink_ref, q0_ref, q1_ref, q2_ref, k_ref, v_ref, c_ref, a_ref, b_ref, o_ref, qr_ref, kr_ref):
    seq = k_ref.shape[0]
    blk = ATT_BLOCK
    grp = pl.program_id(1)
    c, a, b = c_ref[...], a_ref[...], b_ref[...]
    for j, q_ref in enumerate((q0_ref, q1_ref, q2_ref)):
        qr_ref[j] = _rope(q_ref[...], c, a, b).astype(BF16)
    kr_ref[...] = _rope(k_ref[...], c, a, b).astype(BF16)
    cur_mask, prev_mask = _band_masks(C_MAX_LAG)
    sinks = [jnp.full((blk, 1), sink_ref[grp * C_REP + j], F32) for j in range(C_REP)]

    def group(g, carry):
        blocks, dests = [], []
        for u in range(SWA_UNROLL):
            n = g * SWA_UNROLL + u
            rows = pl.ds(pl.multiple_of(n * blk, blk), blk)
            prev = pl.ds(pl.multiple_of(jnp.maximum(n - 1, 0) * blk, blk), blk)
            ks, vs = [kr_ref[rows, :], kr_ref[prev, :]], [v_ref[rows, :], v_ref[prev, :]]
            masks = [cur_mask, jnp.logical_and(prev_mask, n > 0)]
            for j in range(C_REP):
                blocks.append((qr_ref[j, rows, :], ks, vs, masks, sinks[j]))
                dests.append((rows, j))
        for (rows, j), (m, l, acc) in zip(dests, _tile_softmax_many(blocks)):
            out = acc / (l + jnp.exp(sinks[j] - m))
            o_ref[rows, j * HEAD_DIM:(j + 1) * HEAD_DIM] = out.astype(o_ref.dtype)
        return carry

    lax.fori_loop(0, seq // (blk * SWA_UNROLL), group, 0)


def _swa(proj3, sinks, tables):
    bsz, seq, _ = proj3.shape
    qcol = lambda j: pl.BlockSpec((None, seq, HEAD_DIM), lambda b, g, j=j: (b, 0, QC_BLK + g * C_REP + j))
    col = lambda blk: pl.BlockSpec((None, seq, HEAD_DIM), lambda b, g, blk=blk: (b, 0, blk + g))
    tab = pl.BlockSpec((seq, HEAD_DIM), lambda b, g: (0, 0))
    return pl.pallas_call(
        _swa_kernel,
        grid=(bsz, C_KV_HEADS),
        in_specs=[pl.BlockSpec(memory_space=pltpu.SMEM), qcol(0), qcol(1), qcol(2), col(KC_BLK), col(VC_BLK),
                  tab, tab, tab],
        out_specs=pl.BlockSpec((None, seq, C_REP * HEAD_DIM), lambda b, g: (b, 0, g)),
        out_shape=jax.ShapeDtypeStruct((bsz, seq, C_HEADS * HEAD_DIM), BF16),
        scratch_shapes=[pltpu.VMEM((C_REP, seq, HEAD_DIM), BF16), pltpu.VMEM((seq, HEAD_DIM), BF16)],
        compiler_params=_params(("parallel", "parallel")),
        name="swa_sink_attention",
    )(sinks, proj3, proj3, proj3, proj3, proj3, *tables)


def _layer_norm_store(z, g_ref, b_ref, o32_ref, o16_ref):
    mu = jnp.mean(z, axis=-1, keepdims=True)
    zc = z - mu
    var = jnp.mean(zc * zc, axis=-1, keepdims=True)
    y = zc * lax.rsqrt(var + LN_EPS) * g_ref[...] + b_ref[...]
    o32_ref[...] = y
    o16_ref[...] = y.astype(BF16)


def _out_proj_kernel(alpha, oa_ref, ob_ref, oc_ref, wa_ref, wb_ref, wc_ref, x_ref, g_ref, b_ref, o32_ref, o16_ref):
    y = (jnp.dot(oa_ref[...], wa_ref[...], preferred_element_type=F32)
         + jnp.dot(ob_ref[...], wb_ref[...], preferred_element_type=F32)
         + jnp.dot(oc_ref[...], wc_ref[...], preferred_element_type=F32))
    _layer_norm_store(alpha * x_ref[...] + y, g_ref, b_ref, o32_ref, o16_ref)


def _out_proj_ln(oa, ob, oc, w_out, x, g, b, alpha, bm):
    m, d = x.shape
    wa = w_out[:oa.shape[1]]
    wb = w_out[oa.shape[1]:oa.shape[1] + ob.shape[1]]
    wc = w_out[oa.shape[1] + ob.shape[1]:]
    rows = lambda a: pl.BlockSpec((bm, a.shape[1]), lambda i: (i, 0))
    full = lambda a: pl.BlockSpec(a.shape, lambda i: (0, 0))
    return pl.pallas_call(
        functools.partial(_out_proj_kernel, alpha),
        grid=(m // bm,),
        in_specs=[rows(oa), rows(ob), rows(oc), full(wa), full(wb), full(wc), rows(x), full(g), full(b)],
        out_specs=[rows(x), rows(x)],
        out_shape=[jax.ShapeDtypeStruct((m, d), F32), jax.ShapeDtypeStruct((m, d), BF16)],
        compiler_params=_params(("parallel",)),
        name="out_proj_ln",
    )(oa, ob, oc, wa, wb, wc, x, g, b)


def _ffn_up_kernel(tiles_per_seq, x_ref, wg_ref, wu_ref, cw_ref, cb_ref, h_ref, g_ref):
    bm = x_ref.shape[0]
    halo = g_ref.shape[0] - bm

    @pl.when(pl.program_id(1) % tiles_per_seq == 0)
    def _():
        g_ref[0:halo, :] = jnp.zeros((halo, g_ref.shape[1]), F32)

    x = x_ref[...]
    g_ref[halo:, :] = jnp.dot(x, wg_ref[...], preferred_element_type=F32)
    u = jnp.dot(x, wu_ref[...], preferred_element_type=F32)
    gc = cb_ref[...]
    for j in range(CONV_WIDTH):
        lag = CONV_WIDTH - 1 - j
        gc = gc + cw_ref[j:j + 1, :] * g_ref[halo - lag:halo - lag + bm, :]
    h_ref[...] = (gc * jax.nn.sigmoid(gc) * u).astype(h_ref.dtype)
    g_ref[0:halo, :] = g_ref[bm:bm + halo, :]


def _ffn_up(x16, w_gate, w_up, conv_w, conv_b, seq, bm, bn):
    m, d = x16.shape
    f = w_gate.shape[1]
    halo = 8
    wspec = pl.BlockSpec((d, bn), lambda j, i: (0, j))
    return pl.pallas_call(
        functools.partial(_ffn_up_kernel, seq // bm),
        grid=(f // bn, m // bm),
        in_specs=[pl.BlockSpec((bm, d), lambda j, i: (i, 0)), wspec, wspec,
                  pl.BlockSpec((CONV_WIDTH, bn), lambda j, i: (0, j)),
                  pl.BlockSpec((1, bn), lambda j, i: (0, j))],
        out_specs=pl.BlockSpec((bm, bn), lambda j, i: (i, j)),
        out_shape=jax.ShapeDtypeStruct((m, f), BF16),
        scratch_shapes=[pltpu.VMEM((bm + halo, bn), F32)],
        compiler_params=_params(("parallel", "arbitrary")),
        name="ffn_up_conv_gate",
    )(x16, w_gate, w_up, conv_w, conv_b)


def _ffn_down_kernel(alpha, h_ref, w_ref, x_ref, g_ref, b_ref, o32_ref, o16_ref, acc_ref):
    kstep = pl.program_id(1)

    @pl.when(kstep == 0)
    def _():
        acc_ref[...] = alpha * x_ref[...]

    acc_ref[...] += jnp.dot(h_ref[...], w_ref[...], preferred_element_type=F32)

    @pl.when(kstep == pl.num_programs(1) - 1)
    def _():
        _layer_norm_store(acc_ref[...], g_ref, b_ref, o32_ref, o16_ref)


def _ffn_down_ln(h, w_down, x, g, b, alpha, bm, bk):
    m, d = x.shape
    f = h.shape[1]
    rows = pl.BlockSpec((bm, d), lambda i, k: (i, 0))
    vec = pl.BlockSpec((1, d), lambda i, k: (0, 0))
    return pl.pallas_call(
        functools.partial(_ffn_down_kernel, alpha),
        grid=(m // bm, f // bk),
        in_specs=[pl.BlockSpec((bm, bk), lambda i, k: (i, k)), pl.BlockSpec((bk, d), lambda i, k: (k, 0)),
                  rows, vec, vec],
        out_specs=[rows, rows],
        out_shape=[jax.ShapeDtypeStruct((m, d), F32), jax.ShapeDtypeStruct((m, d), BF16)],
        scratch_shapes=[pltpu.VMEM((bm, d), F32)],
        compiler_params=_params(("parallel", "arbitrary")),
        name="ffn_down_ln",
    )(h, w_down, x, g, b)


def kernel(x, w_in, lb_logits, a_norm_w, c_sinks, w_out, ln1_g, ln1_b, w_gate, w_up, conv_w, conv_b, w_down,
           ln2_g, ln2_b):
    bsz, seq, d = x.shape
    depth = w_in.shape[0]
    alpha = (2 * depth) ** 0.25
    tables = _rope_tables(seq)
    lbs = jnp.cumsum(jax.nn.softmax(lb_logits.astype(F32), axis=0), axis=0)
    lbs = lbs - lbs[0]
    x32 = x.reshape(bsz * seq, d)
    x16 = x32.astype(BF16)
    for l in range(depth):
        proj = _matmul(x16, w_in[l].astype(BF16), bm=2048, bn=512, out_dtype=F32)
        proj3 = proj.reshape(bsz, seq, -1)
        oa = _hgrn2(proj3, lbs[l], a_norm_w[l]).reshape(bsz * seq, -1)
        ob = _dilated(proj3, tables).reshape(bsz * seq, -1)
        oc = _swa(proj3, c_sinks[l], tables).reshape(bsz * seq, -1)
        x32, x16 = _out_proj_ln(oa, ob, oc, w_out[l].astype(BF16), x32, ln1_g[l].reshape(1, d),
                                ln1_b[l].reshape(1, d), alpha, bm=256)
        h = _ffn_up(x16, w_gate[l].astype(BF16), w_up[l].astype(BF16), conv_w[l], conv_b[l].reshape(1, -1),
                    seq, bm=1024, bn=512)
        x32, x16 = _ffn_down_ln(h, w_down[l].astype(BF16), x32, ln2_g[l].reshape(1, d), ln2_b[l].reshape(1, d),
                                alpha, bm=512, bk=1408)
    return x32.reshape(bsz, seq, d)
```

```python
import functools

import numpy as np
import jax
import jax.numpy as jnp
from jax import lax
from jax.experimental import pallas as pl
from jax.experimental.pallas import tpu as pltpu

HEAD_DIM = 128
A_HEADS = 4
B_HEADS = 6
C_HEADS = 6
C_KV_HEADS = 2
C_REP = C_HEADS // C_KV_HEADS
DILATIONS = (1, 4, 16)
ATT_BLOCK = 128
C_MAX_LAG = 127
ROPE_THETA = 500000.0
ROPE_DIM = HEAD_DIM // 4
ROPE_HALF = ROPE_DIM // 2
CONV_WIDTH = 3
LN_EPS = 1e-5

QA_BLK, FA_BLK, IA_BLK, GA_BLK = 0, 4, 8, 12
QB_BLK, KB_BLK, VB_BLK = 16, 22, 28
QC_BLK, KC_BLK, VC_BLK = 34, 40, 42

HGRN_CHUNK = 128
HGRN_LEVELS = (64, 32, 16, 8, 4, 2, 1)

VMEM_LIMIT_BYTES = 56 * 1024 * 1024

F32 = jnp.float32
BF16 = jnp.bfloat16
NEG_BIG = -1e30


def _params(semantics):
    return pltpu.CompilerParams(dimension_semantics=semantics, vmem_limit_bytes=VMEM_LIMIT_BYTES)


def _rope_tables(seq):
    inv = ROPE_THETA ** (-jnp.arange(0, ROPE_DIM, 2, dtype=F32) / ROPE_DIM)
    ang = jnp.arange(seq, dtype=F32)[:, None] * inv[None, :]
    cos, sin = jnp.cos(ang), jnp.sin(ang)
    rest = HEAD_DIM - ROPE_DIM
    c = jnp.concatenate([cos, cos, jnp.ones((seq, rest), F32)], -1)
    a = jnp.concatenate([-sin, jnp.zeros((seq, HEAD_DIM - ROPE_HALF), F32)], -1)
    b = jnp.concatenate([jnp.zeros((seq, ROPE_HALF), F32), sin, jnp.zeros((seq, rest), F32)], -1)
    return c, a, b


def _rope(x, c, a, b):
    return (x * c + pltpu.roll(x, HEAD_DIM - ROPE_HALF, 1) * a + pltpu.roll(x, ROPE_HALF, 1) * b)


def _hgrn_constants():
    c = HGRN_CHUNK
    i = np.arange(c)[:, None]
    l = np.arange(c)[None, :]
    secs = [(l <= i), (l > i)]
    lvl = np.full((c, c), -1, np.int32)
    j = l
    for n, h in enumerate(HGRN_LEVELS):
        ref = (i // (2 * h)) * (2 * h) + h - 1
        odd = (i // h) % 2 == 1
        secs.append(np.where(odd, (l > ref) & (l <= i), (l > i) & (l <= ref)))
        pair = (i // (2 * h) == j // (2 * h)) & (i // h != j // h) & (i > j)
        lvl[pair] = n
    lvl[np.arange(c), np.arange(c)] = len(HGRN_LEVELS)
    e_all = np.concatenate(secs, 0).astype(np.float32)
    return jnp.asarray(e_all, BF16), jnp.asarray(lvl)


def _split3(x):
    p1 = x.astype(BF16)
    r1 = x - p1.astype(F32)
    p2 = r1.astype(BF16)
    p3 = (r1 - p2.astype(F32)).astype(BF16)
    return p1, p2, p3


def _dot_nt(a, b, **kw):
    return lax.dot_general(a, b, (((1,), (1,)), ((), ())), preferred_element_type=F32, **kw)


def _dot_tn(a, b, **kw):
    return lax.dot_general(a, b, (((0,), (0,)), ((), ())), preferred_element_type=F32, **kw)


def _hgrn_head_chunk(q, fl, v, g, st, lb, nw, e_all, lvl, row):
    c = HGRN_CHUNK
    qs = q * jax.nn.sigmoid(q)
    fg = lb + (1.0 - lb) * jax.nn.sigmoid(fl)
    kk = 1.0 - fg
    p1, p2, p3 = _split3(jnp.log(fg))
    g12 = jnp.dot(e_all, jnp.concatenate([p1, p2], axis=1), preferred_element_type=F32)
    gsum = g12[:, :HEAD_DIM] + g12[:, HEAD_DIM:] + jnp.dot(e_all, p3, preferred_element_type=F32)
    dec = jnp.exp(gsum)
    q_in = qs * dec[0:c]
    k_end = kk * dec[c:2 * c]
    att = jnp.where(lvl == len(HGRN_LEVELS), jnp.sum(qs * kk, axis=-1, keepdims=True), 0.0)
    for n, h in enumerate(HGRN_LEVELS):
        odd = ((row // h) % 2) == 1
        xh = (jnp.where(odd, qs, kk) * dec[(2 + n) * c:(3 + n) * c]).astype(BF16)
        att = att + jnp.where(lvl == n, _dot_nt(xh, xh), 0.0)
    v16 = v.astype(BF16)
    o = (jnp.dot(att.astype(BF16), v16, preferred_element_type=F32)
         + _dot_nt(q_in.astype(BF16), st.astype(BF16)))
    st_new = st * dec[c - 1:c] + _dot_tn(v16, k_end.astype(BF16))
    o = o * lax.rsqrt(jnp.mean(o * o, axis=-1, keepdims=True) + LN_EPS) * nw
    return (o * (g * jax.nn.sigmoid(g))).astype(BF16), st_new


def _hgrn_kernel(q_ref, f_ref, i_ref, g_ref, lb_ref, nw_ref, e_ref, lvl_ref, o_ref, st_ref):
    c = HGRN_CHUNK
    n_chunks = q_ref.shape[0] // c
    st_ref[...] = jnp.zeros_like(st_ref)
    nw = nw_ref[...]
    row = lax.broadcasted_iota(jnp.int32, (c, 1), 0)

    def chunk(ci, carry):
        rows = pl.ds(pl.multiple_of(ci * c, c), c)
        e_all, lvl = e_ref[...], lvl_ref[...]
        results = []
        for h in range(A_HEADS):
            cols = slice(h * HEAD_DIM, (h + 1) * HEAD_DIM)
            results.append(_hgrn_head_chunk(q_ref[rows, cols], f_ref[rows, cols], i_ref[rows, cols],
                                            g_ref[rows, cols], st_ref[h], lb_ref[:, cols], nw, e_all, lvl, row))
        for h, (o, st_new) in enumerate(results):
            o_ref[rows, h * HEAD_DIM:(h + 1) * HEAD_DIM] = o
            st_ref[h] = st_new
        return carry

    lax.fori_loop(0, n_chunks, chunk, 0)


def _hgrn2(proj3, lb, norm_w):
    bsz, seq, _ = proj3.shape
    e_all, lvl = _hgrn_constants()
    width = A_HEADS * HEAD_DIM
    col = lambda blk: pl.BlockSpec((None, seq, width), lambda b, blk=blk: (b, 0, blk // A_HEADS))
    const = lambda shape: pl.BlockSpec(shape, lambda b: (0,) * len(shape))
    return pl.pallas_call(
        _hgrn_kernel,
        grid=(bsz,),
        in_specs=[col(QA_BLK), col(FA_BLK), col(IA_BLK), col(GA_BLK),
                  const((1, width)), const((1, HEAD_DIM)), const(e_all.shape), const(lvl.shape)],
        out_specs=pl.BlockSpec((None, seq, width), lambda b: (b, 0, 0)),
        out_shape=jax.ShapeDtypeStruct((bsz, seq, width), BF16),
        scratch_shapes=[pltpu.VMEM((A_HEADS, HEAD_DIM, HEAD_DIM), F32)],
        compiler_params=_params(("parallel",)),
        name="hgrn2",
    )(proj3, proj3, proj3, proj3, lb.reshape(1, width), norm_w.reshape(1, HEAD_DIM), e_all, lvl)


def _band_masks(max_lag):
    r = lax.broadcasted_iota(jnp.int32, (ATT_BLOCK, ATT_BLOCK), 0)
    col = lax.broadcasted_iota(jnp.int32, (ATT_BLOCK, ATT_BLOCK), 1)
    return col <= r, (ATT_BLOCK + r - col) <= max_lag


def _tile_softmax_many(blocks):
    scale = HEAD_DIM ** -0.5
    s_all = [[jnp.where(mask, _dot_nt(q16, k.astype(BF16)) * scale, NEG_BIG) for k, mask in zip(ks, masks)]
             for q16, ks, _, masks, _ in blocks]
    m_all = []
    for s, (_, _, _, _, m_floor) in zip(s_all, blocks):
        m = jnp.max(functools.reduce(jnp.maximum, s), axis=-1, keepdims=True)
        m_all.append(m if m_floor is None else jnp.maximum(m, m_floor))
    e_all = [[jnp.exp(si - m) for si in s] for s, m in zip(s_all, m_all)]
    acc_all = [functools.reduce(jnp.add, [jnp.dot(ei.astype(BF16), v.astype(BF16), preferred_element_type=F32)
                                          for ei, v in zip(e, vs)])
               for e, (_, _, vs, _, _) in zip(e_all, blocks)]
    l_all = [jnp.sum(functools.reduce(jnp.add, e), axis=-1, keepdims=True) for e in e_all]
    return list(zip(m_all, l_all, acc_all))


DIL_UNROLL = 4


def _dilated_kernel(q_ref, k_ref, v_ref, c_ref, a_ref, b_ref, o_ref, qr_ref, kr_ref, m_ref, l_ref, acc_ref):
    seq = q_ref.shape[0]
    blk = ATT_BLOCK
    c, a, b = c_ref[...], a_ref[...], b_ref[...]
    qr_ref[...] = _rope(q_ref[...], c, a, b)
    kr_ref[...] = _rope(k_ref[...], c, a, b)
    cur_mask, prev_mask = _band_masks(ATT_BLOCK)

    def run_group(specs, first):
        loaded = []
        for rows, prev, prev_ok in specs:
            ks, vs, masks = [kr_ref[rows, :]], [v_ref[rows, :]], [cur_mask]
            if prev is not None:
                ks.append(kr_ref[prev, :])
                vs.append(v_ref[prev, :])
                masks.append(prev_mask if prev_ok is None else jnp.logical_and(prev_mask, prev_ok))
            state = None if first else (m_ref[rows, :], l_ref[rows, :], acc_ref[rows, :])
            loaded.append((qr_ref[rows, :].astype(BF16), ks, vs, masks, state))
        pieces = _tile_softmax_many([(q16, ks, vs, masks, None if state is None else state[0])
                                     for q16, ks, vs, masks, state in loaded])
        results = []
        for (m, l, acc), (_, _, _, _, state) in zip(pieces, loaded):
            if state is None:
                shape = (blk, HEAD_DIM)
                results.append((jnp.broadcast_to(m, shape), jnp.broadcast_to(l, shape), acc))
            else:
                m_old, l_old, acc_old = state
                alpha = jnp.exp(m_old - m)
                results.append((m, alpha * l_old + l, alpha * acc_old + acc))
        for (rows, _, _), (m, l, acc) in zip(specs, results):
            m_ref[rows, :] = m
            l_ref[rows, :] = l
            acc_ref[rows, :] = acc

    def contiguous(n):
        return pl.ds(pl.multiple_of(n * blk, blk), blk)

    def dense_group(g, carry):
        specs = []
        for u in range(DIL_UNROLL):
            n = g * DIL_UNROLL + u
            specs.append((contiguous(n), contiguous(jnp.maximum(n - 1, 0)), n > 0))
        run_group(specs, first=True)
        return carry

    lax.fori_loop(0, seq // (blk * DIL_UNROLL), dense_group, 0)

    for dil in DILATIONS[1:]:
        n_blocks = seq // (dil * blk)

        def strided(r, n, dil=dil):
            return pl.ds(r + n * (dil * blk), blk, stride=dil)

        def head_group(g, carry, strided=strided):
            run_group([(strided(g * DIL_UNROLL + u, 0), None, None) for u in range(DIL_UNROLL)], first=False)
            return carry

        lax.fori_loop(0, dil // DIL_UNROLL, head_group, 0)

        def tail_group(t, carry, strided=strided, dil=dil):
            n = 1 + t // (dil // DIL_UNROLL)
            g = t % (dil // DIL_UNROLL)
            run_group([(strided(g * DIL_UNROLL + u, n), strided(g * DIL_UNROLL + u, n - 1), None)
                       for u in range(DIL_UNROLL)], first=False)
            return carry

        if n_blocks > 1:
            lax.fori_loop(0, (n_blocks - 1) * (dil // DIL_UNROLL), tail_group, 0)

    o_ref[...] = (acc_ref[...] / l_ref[...]).astype(o_ref.dtype)


def _dilated(proj3, tables):
    bsz, seq, _ = proj3.shape
    col = lambda blk: pl.BlockSpec((None, seq, HEAD_DIM), lambda b, h, blk=blk: (b, 0, blk + h))
    tab = pl.BlockSpec((seq, HEAD_DIM), lambda b, h: (0, 0))
    return pl.pallas_call(
        _dilated_kernel,
        grid=(bsz, B_HEADS),
        in_specs=[col(QB_BLK), col(KB_BLK), col(VB_BLK), tab, tab, tab],
        out_specs=pl.BlockSpec((None, seq, HEAD_DIM), lambda b, h: (b, 0, h)),
        out_shape=jax.ShapeDtypeStruct((bsz, seq, B_HEADS * HEAD_DIM), BF16),
        scratch_shapes=[pltpu.VMEM((seq, HEAD_DIM), F32)] * 5,
        compiler_params=_params(("parallel", "parallel")),
        name="dilated_attention",
    )(proj3, proj3, proj3, *tables)


SWA_UNROLL = 2


def _swa_kernel(sink_ref, q0_ref, q1_ref, q2_ref, k_ref, v_ref, c_ref, a_ref, b_ref, o_ref, qr_ref, kr_ref):
    seq = k_ref.shape[0]
    blk = ATT_BLOCK
    grp = pl.program_id(1)
    c, a, b = c_ref[...], a_ref[...], b_ref[...]
    for j, q_ref in enumerate((q0_ref, q1_ref, q2_ref)):
        qr_ref[j] = _rope(q_ref[...], c, a, b).astype(BF16)
    kr_ref[...] = _rope(k_ref[...], c, a, b).astype(BF16)
    cur_mask, prev_mask = _band_masks(C_MAX_LAG)
    sinks = [jnp.full((blk, 1), sink_ref[grp * C_REP + j], F32) for j in range(C_REP)]

    def group(g, carry):
        blocks, dests = [], []
        for u in range(SWA_UNROLL):
            n = g * SWA_UNROLL + u
            rows = pl.ds(pl.multiple_of(n * blk, blk), blk)
            prev = pl.ds(pl.multiple_of(jnp.maximum(n - 1, 0) * blk, blk), blk)
            ks, vs = [kr_ref[rows, :], kr_ref[prev, :]], [v_ref[rows, :], v_ref[prev, :]]
            masks = [cur_mask, jnp.logical_and(prev_mask, n > 0)]
            for j in range(C_REP):
                blocks.append((qr_ref[j, rows, :], ks, vs, masks, sinks[j]))
                dests.append((rows, j))
        for (rows, j), (m, l, acc) in zip(dests, _tile_softmax_many(blocks)):
            out = acc / (l + jnp.exp(sinks[j] - m))
            o_ref[rows, j * HEAD_DIM:(j + 1) * HEAD_DIM] = out.astype(o_ref.dtype)
        return carry

    lax.fori_loop(0, seq // (blk * SWA_UNROLL), group, 0)


def _swa(proj3, sinks, tables):
    bsz, seq, _ = proj3.shape
    qcol = lambda j: pl.BlockSpec((None, seq, HEAD_DIM), lambda b, g, j=j: (b, 0, QC_BLK + g * C_REP + j))
    col = lambda blk: pl.BlockSpec((None, seq, HEAD_DIM), lambda b, g, blk=blk: (b, 0, blk + g))
    tab = pl.BlockSpec((seq, HEAD_DIM), lambda b, g: (0, 0))
    return pl.pallas_call(
        _swa_kernel,
        grid=(bsz, C_KV_HEADS),
        in_specs=[pl.BlockSpec(memory_space=pltpu.SMEM), qcol(0), qcol(1), qcol(2), col(KC_BLK), col(VC_BLK),
                  tab, tab, tab],
        out_specs=pl.BlockSpec((None, seq, C_REP * HEAD_DIM), lambda b, g: (b, 0, g)),
        out_shape=jax.ShapeDtypeStruct((bsz, seq, C_HEADS * HEAD_DIM), BF16),
        scratch_shapes=[pltpu.VMEM((C_REP, seq, HEAD_DIM), BF16), pltpu.VMEM((seq, HEAD_DIM), BF16)],
        compiler_params=_params(("parallel", "parallel")),
        name="swa_sink_attention",
    )(sinks, proj3, proj3, proj3, proj3, proj3, *tables)


def _layer_norm_store(z, g_ref, b_ref, o32_ref, o16_ref):
    mu = jnp.mean(z, axis=-1, keepdims=True)
    zc = z - mu
    var = jnp.mean(zc * zc, axis=-1, keepdims=True)
    y = zc * lax.rsqrt(var + LN_EPS) * g_ref[...] + b_ref[...]
    o32_ref[...] = y
    o16_ref[...] = y.astype(BF16)


def _stage_weight_bf16(w_hbm, layer, w16_ref, stage_ref, sem):
    chunk = stage_ref.shape[1]
    n_chunks = w16_ref.shape[0] // chunk

    def copy(c):
        slot = c % 2
        return pltpu.make_async_copy(w_hbm.at[layer, pl.ds(c * chunk, chunk), :], stage_ref.at[slot], sem.at[slot])

    copy(0).start()
    for c in range(n_chunks):
        if c + 1 < n_chunks:
            copy(c + 1).start()
        copy(c).wait()
        w16_ref[c * chunk:(c + 1) * chunk, :] = stage_ref[c % 2].astype(BF16)


WEIGHT_STAGE_ROWS = 256


def _resident_weight_scratch(k, n):
    return [pltpu.VMEM((k, n), BF16), pltpu.VMEM((2, WEIGHT_STAGE_ROWS, n), F32), pltpu.SemaphoreType.DMA((2,))]


def _out_proj_kernel(alpha, layer, oa_ref, ob_ref, oc_ref, w_hbm, x_ref, g_ref, b_ref, o32_ref, o16_ref,
                     w16_ref, stage_ref, sem):
    @pl.when(pl.program_id(0) == 0)
    def _():
        _stage_weight_bf16(w_hbm, layer, w16_ref, stage_ref, sem)

    ka, kb = oa_ref.shape[1], ob_ref.shape[1]
    y = (jnp.dot(oa_ref[...], w16_ref[0:ka, :], preferred_element_type=F32)
         + jnp.dot(ob_ref[...], w16_ref[ka:ka + kb, :], preferred_element_type=F32)
         + jnp.dot(oc_ref[...], w16_ref[ka + kb:, :], preferred_element_type=F32))
    _layer_norm_store(alpha * x_ref[...] + y, g_ref, b_ref, o32_ref, o16_ref)


def _layer_vec(layer, d):
    return pl.BlockSpec((None, 1, d), lambda i, layer=layer: (layer, 0, 0))


def _out_proj_ln(oa, ob, oc, w_out, layer, x, g, b, alpha, bm):
    m, d = x.shape
    rows = lambda a: pl.BlockSpec((bm, a.shape[1]), lambda i: (i, 0))
    return pl.pallas_call(
        functools.partial(_out_proj_kernel, alpha, layer),
        grid=(m // bm,),
        in_specs=[rows(oa), rows(ob), rows(oc), pl.BlockSpec(memory_space=pl.ANY), rows(x),
                  _layer_vec(layer, d), _layer_vec(layer, d)],
        out_specs=[rows(x), rows(x)],
        out_shape=[jax.ShapeDtypeStruct((m, d), F32), jax.ShapeDtypeStruct((m, d), BF16)],
        scratch_shapes=_resident_weight_scratch(w_out.shape[1], d),
        compiler_params=_params(("arbitrary",)),
        name="out_proj_ln",
    )(oa, ob, oc, w_out, x, g, b)


def _in_proj_kernel(x_ref, w_ref, o_ref, w16_ref):
    @pl.when(pl.program_id(1) == 0)
    def _():
        w16_ref[...] = w_ref[...].astype(BF16)

    o_ref[...] = jnp.dot(x_ref[...], w16_ref[...], preferred_element_type=F32).astype(o_ref.dtype)


def _in_proj(x16, w_in, layer, bm, bn):
    m, k = x16.shape
    n = w_in.shape[2]
    return pl.pallas_call(
        _in_proj_kernel,
        grid=(n // bn, m // bm),
        in_specs=[pl.BlockSpec((bm, k), lambda j, i: (i, 0)),
                  pl.BlockSpec((None, k, bn), lambda j, i: (layer, 0, j))],
        out_specs=pl.BlockSpec((bm, bn), lambda j, i: (i, j)),
        out_shape=jax.ShapeDtypeStruct((m, n), F32),
        scratch_shapes=[pltpu.VMEM((k, bn), BF16)],
        compiler_params=_params(("parallel", "arbitrary")),
        name="in_proj",
    )(x16, w_in)


def _ffn_up_kernel(tiles_per_seq, x_ref, wg_ref, wu_ref, cw_ref, cb_ref, h_ref, g_ref, wg16_ref, wu16_ref):
    bm = x_ref.shape[0]
    halo = g_ref.shape[0] - bm

    @pl.when(pl.program_id(1) == 0)
    def _():
        wg16_ref[...] = wg_ref[...].astype(BF16)
        wu16_ref[...] = wu_ref[...].astype(BF16)

    @pl.when(pl.program_id(1) % tiles_per_seq == 0)
    def _():
        g_ref[0:halo, :] = jnp.zeros((halo, g_ref.shape[1]), F32)

    x = x_ref[...]
    g_ref[halo:, :] = jnp.dot(x, wg16_ref[...], preferred_element_type=F32)
    u = jnp.dot(x, wu16_ref[...], preferred_element_type=F32)
    gc = cb_ref[...]
    for j in range(CONV_WIDTH):
        lag = CONV_WIDTH - 1 - j
        gc = gc + cw_ref[j:j + 1, :] * g_ref[halo - lag:halo - lag + bm, :]
    h_ref[...] = (gc * jax.nn.sigmoid(gc) * u).astype(h_ref.dtype)
    g_ref[0:halo, :] = g_ref[bm:bm + halo, :]


def _ffn_up(x16, w_gate, w_up, conv_w, conv_b, layer, seq, bm, bn):
    m, d = x16.shape
    f = w_gate.shape[2]
    halo = 8
    wspec = pl.BlockSpec((None, d, bn), lambda j, i: (layer, 0, j))
    return pl.pallas_call(
        functools.partial(_ffn_up_kernel, seq // bm),
        grid=(f // bn, m // bm),
        in_specs=[pl.BlockSpec((bm, d), lambda j, i: (i, 0)), wspec, wspec,
                  pl.BlockSpec((None, CONV_WIDTH, bn), lambda j, i: (layer, 0, j)),
                  pl.BlockSpec((None, 1, bn), lambda j, i: (layer, 0, j))],
        out_specs=pl.BlockSpec((bm, bn), lambda j, i: (i, j)),
        out_shape=jax.ShapeDtypeStruct((m, f), BF16),
        scratch_shapes=[pltpu.VMEM((bm + halo, bn), F32), pltpu.VMEM((d, bn), BF16), pltpu.VMEM((d, bn), BF16)],
        compiler_params=_params(("parallel", "arbitrary")),
        name="ffn_up_conv_gate",
    )(x16, w_gate, w_up, conv_w, conv_b)


def _ffn_down_kernel(alpha, layer, h_ref, w_hbm, x_ref, g_ref, b_ref, o32_ref, o16_ref, w16_ref, stage_ref, sem):
    @pl.when(pl.program_id(0) == 0)
    def _():
        _stage_weight_bf16(w_hbm, layer, w16_ref, stage_ref, sem)

    y = jnp.dot(h_ref[...], w16_ref[...], preferred_element_type=F32)
    _layer_norm_store(alpha * x_ref[...] + y, g_ref, b_ref, o32_ref, o16_ref)


def _ffn_down_ln(h, w_down, layer, x, g, b, alpha, bm):
    m, d = x.shape
    f = h.shape[1]
    rows = pl.BlockSpec((bm, d), lambda i: (i, 0))
    return pl.pallas_call(
        functools.partial(_ffn_down_kernel, alpha, layer),
        grid=(m // bm,),
        in_specs=[pl.BlockSpec((bm, f), lambda i: (i, 0)), pl.BlockSpec(memory_space=pl.ANY), rows,
                  _layer_vec(layer, d), _layer_vec(layer, d)],
        out_specs=[rows, rows],
        out_shape=[jax.ShapeDtypeStruct((m, d), F32), jax.ShapeDtypeStruct((m, d), BF16)],
        scratch_shapes=_resident_weight_scratch(f, d),
        compiler_params=_params(("arbitrary",)),
        name="ffn_down_ln",
    )(h, w_down, x, g, b)


def kernel(x, w_in, lb_logits, a_norm_w, c_sinks, w_out, ln1_g, ln1_b, w_gate, w_up, conv_w, conv_b, w_down,
           ln2_g, ln2_b):
    bsz, seq, d = x.shape
    depth = w_in.shape[0]
    alpha = (2 * depth) ** 0.25
    tables = _rope_tables(seq)
    lbs = jnp.cumsum(jax.nn.softmax(lb_logits.astype(F32), axis=0), axis=0)
    lbs = lbs - lbs[0]
    vec3 = lambda a: a.reshape(depth, 1, -1)
    ln1_g, ln1_b, ln2_g, ln2_b, conv_b = vec3(ln1_g), vec3(ln1_b), vec3(ln2_g), vec3(ln2_b), vec3(conv_b)
    x32 = x.reshape(bsz * seq, d)
    x16 = x32.astype(BF16)
    for l in range(depth):
        proj3 = _in_proj(x16, w_in, l, bm=2048, bn=512).reshape(bsz, seq, -1)
        oa = _hgrn2(proj3, lbs[l], a_norm_w[l]).reshape(bsz * seq, -1)
        ob = _dilated(proj3, tables).reshape(bsz * seq, -1)
        oc = _swa(proj3, c_sinks[l], tables).reshape(bsz * seq, -1)
        x32, x16 = _out_proj_ln(oa, ob, oc, w_out, l, x32, ln1_g, ln1_b, alpha, bm=512)
        h = _ffn_up(x16, w_gate, w_up, conv_w, conv_b, l, seq, bm=1024, bn=512)
        x32, x16 = _ffn_down_ln(h, w_down, l, x32, ln2_g, ln2_b, alpha, bm=256)
    return x32.reshape(bsz, seq, d)
```

```python
import functools

import numpy as np
import jax
import jax.numpy as jnp
from jax import lax
from jax.experimental import pallas as pl
from jax.experimental.pallas import tpu as pltpu

HEAD_DIM = 128
A_HEADS = 4
B_HEADS = 6
C_HEADS = 6
C_KV_HEADS = 2
C_REP = C_HEADS // C_KV_HEADS
DILATIONS = (1, 4, 16)
ATT_BLOCK = 128
C_MAX_LAG = 127
ROPE_THETA = 500000.0
ROPE_DIM = HEAD_DIM // 4
ROPE_HALF = ROPE_DIM // 2
CONV_WIDTH = 3
LN_EPS = 1e-5

QA_BLK, FA_BLK, IA_BLK, GA_BLK = 0, 4, 8, 12
QB_BLK, KB_BLK, VB_BLK = 16, 22, 28
QC_BLK, KC_BLK, VC_BLK = 34, 40, 42

HGRN_CHUNK = 128
HGRN_LEVELS = (64, 32, 16, 8, 4, 2, 1)

VMEM_LIMIT_BYTES = 56 * 1024 * 1024

F32 = jnp.float32
BF16 = jnp.bfloat16
NEG_BIG = -1e30


def _params(semantics):
    return pltpu.CompilerParams(dimension_semantics=semantics, vmem_limit_bytes=VMEM_LIMIT_BYTES)


def _rope_tables(seq):
    inv = ROPE_THETA ** (-jnp.arange(0, ROPE_DIM, 2, dtype=F32) / ROPE_DIM)
    ang = jnp.arange(seq, dtype=F32)[:, None] * inv[None, :]
    cos, sin = jnp.cos(ang), jnp.sin(ang)
    rest = HEAD_DIM - ROPE_DIM
    c = jnp.concatenate([cos, cos, jnp.ones((seq, rest), F32)], -1)
    a = jnp.concatenate([-sin, jnp.zeros((seq, HEAD_DIM - ROPE_HALF), F32)], -1)
    b = jnp.concatenate([jnp.zeros((seq, ROPE_HALF), F32), sin, jnp.zeros((seq, rest), F32)], -1)
    return c, a, b


def _rope(x, c, a, b):
    return (x * c + pltpu.roll(x, HEAD_DIM - ROPE_HALF, 1) * a + pltpu.roll(x, ROPE_HALF, 1) * b)


def _hgrn_constants():
    c = HGRN_CHUNK
    i = np.arange(c)[:, None]
    j = np.arange(c)[None, :]
    lvl = np.full((c, c), -1, np.int32)
    for n, h in enumerate(HGRN_LEVELS):
        lvl[(i // (2 * h) == j // (2 * h)) & (i // h != j // h) & (i > j)] = n
    lvl[np.arange(c), np.arange(c)] = len(HGRN_LEVELS)
    return jnp.asarray((j <= i).astype(np.float32), BF16), jnp.asarray(lvl)


def _midpoint_log_decay(lf, b, h, row):
    c, dk = b.shape
    odd = ((row // h) % 2) == 1
    if h == 1:
        return jnp.where(odd, lf, 0.0)
    if h == 2:
        below = pltpu.roll(lf, c - 1, 0)
        above = pltpu.roll(lf, 1, 0)
        place = row % 4
        return jnp.where(place == 0, below, jnp.where(place == 1, 0.0, jnp.where(place == 2, lf, lf + above)))
    blocks = b.reshape(c // (2 * h), 2 * h, dk)
    mid = jnp.broadcast_to(blocks[:, h - 1:h, :], blocks.shape).reshape(c, dk)
    return jnp.where(odd, b - mid, mid - b)


def _split3(x):
    p1 = x.astype(BF16)
    r1 = x - p1.astype(F32)
    p2 = r1.astype(BF16)
    p3 = (r1 - p2.astype(F32)).astype(BF16)
    return p1, p2, p3


def _dot_nt(a, b, **kw):
    return lax.dot_general(a, b, (((1,), (1,)), ((), ())), preferred_element_type=F32, **kw)


def _dot_tn(a, b, **kw):
    return lax.dot_general(a, b, (((0,), (0,)), ((), ())), preferred_element_type=F32, **kw)


def _hgrn_chunk_heads(inputs, nw, e_all, lvl, row):
    c = HGRN_CHUNK
    heads = range(len(inputs))
    qs = [q * jax.nn.sigmoid(q) for q, _, _, _, _, _ in inputs]
    fg = [lb + (1.0 - lb) * jax.nn.sigmoid(fl) for _, fl, _, _, _, lb in inputs]
    kk = [1.0 - f for f in fg]
    lf = [jnp.log(f) for f in fg]
    parts = [_split3(x) for x in lf]
    b12 = [jnp.dot(e_all, jnp.concatenate([p1, p2], axis=1), preferred_element_type=F32) for p1, p2, _ in parts]
    b3 = [jnp.dot(e_all, p3, preferred_element_type=F32) for _, _, p3 in parts]
    b = [x[:, :HEAD_DIM] + x[:, HEAD_DIM:] + y for x, y in zip(b12, b3)]
    att = [jnp.where(lvl == len(HGRN_LEVELS), jnp.sum(qs[i] * kk[i], axis=-1, keepdims=True), 0.0) for i in heads]
    for n, h in enumerate(HGRN_LEVELS):
        odd = ((row // h) % 2) == 1
        xh = [(jnp.where(odd, qs[i], kk[i]) * jnp.exp(_midpoint_log_decay(lf[i], b[i], h, row))).astype(BF16)
              for i in heads]
        pair = [_dot_nt(x, x) for x in xh]
        att = [att[i] + jnp.where(lvl == n, pair[i], 0.0) for i in heads]
    b_end = [x[c - 1:c] for x in b]
    q_in = [(qs[i] * jnp.exp(b[i])).astype(BF16) for i in heads]
    k_end = [(kk[i] * jnp.exp(b_end[i] - b[i])).astype(BF16) for i in heads]
    v16 = [v.astype(BF16) for _, _, v, _, _, _ in inputs]
    o = [jnp.dot(att[i].astype(BF16), v16[i], preferred_element_type=F32)
         + _dot_nt(q_in[i], inputs[i][4].astype(BF16)) for i in heads]
    st_new = [inputs[i][4] * jnp.exp(b_end[i]) + _dot_tn(v16[i], k_end[i]) for i in heads]
    o = [x * lax.rsqrt(jnp.mean(x * x, axis=-1, keepdims=True) + LN_EPS) * nw for x in o]
    out = [(o[i] * (inputs[i][3] * jax.nn.sigmoid(inputs[i][3]))).astype(BF16) for i in heads]
    return list(zip(out, st_new))


def _hgrn_kernel(q_ref, f_ref, i_ref, g_ref, lb_ref, nw_ref, e_ref, lvl_ref, o_ref, st_ref):
    c = HGRN_CHUNK
    n_chunks = q_ref.shape[0] // c
    st_ref[...] = jnp.zeros_like(st_ref)
    nw = nw_ref[...]
    row = lax.broadcasted_iota(jnp.int32, (c, 1), 0)

    def chunk(ci, carry):
        rows = pl.ds(pl.multiple_of(ci * c, c), c)
        e_all, lvl = e_ref[...], lvl_ref[...]
        inputs = []
        for h in range(A_HEADS):
            cols = slice(h * HEAD_DIM, (h + 1) * HEAD_DIM)
            inputs.append((q_ref[rows, cols], f_ref[rows, cols], i_ref[rows, cols], g_ref[rows, cols], st_ref[h],
                           lb_ref[:, cols]))
        for h, (o, st_new) in enumerate(_hgrn_chunk_heads(inputs, nw, e_all, lvl, row)):
            o_ref[rows, h * HEAD_DIM:(h + 1) * HEAD_DIM] = o
            st_ref[h] = st_new
        return carry

    lax.fori_loop(0, n_chunks, chunk, 0)


def _hgrn2(proj3, lb, norm_w):
    bsz, seq, _ = proj3.shape
    e_all, lvl = _hgrn_constants()
    width = A_HEADS * HEAD_DIM
    col = lambda blk: pl.BlockSpec((None, seq, width), lambda b, blk=blk: (b, 0, blk // A_HEADS))
    const = lambda shape: pl.BlockSpec(shape, lambda b: (0,) * len(shape))
    return pl.pallas_call(
        _hgrn_kernel,
        grid=(bsz,),
        in_specs=[col(QA_BLK), col(FA_BLK), col(IA_BLK), col(GA_BLK),
                  const((1, width)), const((1, HEAD_DIM)), const(e_all.shape), const(lvl.shape)],
        out_specs=pl.BlockSpec((None, seq, width), lambda b: (b, 0, 0)),
        out_shape=jax.ShapeDtypeStruct((bsz, seq, width), BF16),
        scratch_shapes=[pltpu.VMEM((A_HEADS, HEAD_DIM, HEAD_DIM), F32)],
        compiler_params=_params(("parallel",)),
        name="hgrn2",
    )(proj3, proj3, proj3, proj3, lb.reshape(1, width), norm_w.reshape(1, HEAD_DIM), e_all, lvl)


def _band_masks(max_lag):
    r = lax.broadcasted_iota(jnp.int32, (ATT_BLOCK, ATT_BLOCK), 0)
    col = lax.broadcasted_iota(jnp.int32, (ATT_BLOCK, ATT_BLOCK), 1)
    return col <= r, (ATT_BLOCK + r - col) <= max_lag


def _tile_softmax_many(blocks):
    scale = HEAD_DIM ** -0.5
    s_all = [[jnp.where(mask, _dot_nt(q16, k.astype(BF16)) * scale, NEG_BIG) for k, mask in zip(ks, masks)]
             for q16, ks, _, masks, _ in blocks]
    m_all = []
    for s, (_, _, _, _, m_floor) in zip(s_all, blocks):
        m = jnp.max(functools.reduce(jnp.maximum, s), axis=-1, keepdims=True)
        m_all.append(m if m_floor is None else jnp.maximum(m, m_floor))
    e_all = [[jnp.exp(si - m) for si in s] for s, m in zip(s_all, m_all)]
    acc_all = [functools.reduce(jnp.add, [jnp.dot(ei.astype(BF16), v.astype(BF16), preferred_element_type=F32)
                                          for ei, v in zip(e, vs)])
               for e, (_, _, vs, _, _) in zip(e_all, blocks)]
    l_all = [jnp.sum(functools.reduce(jnp.add, e), axis=-1, keepdims=True) for e in e_all]
    return list(zip(m_all, l_all, acc_all))


DIL_UNROLL = 8


def _dilated_kernel(q_ref, k_ref, v_ref, c_ref, a_ref, b_ref, o_ref, qr_ref, kr_ref, vr_ref, m_ref, l_ref, acc_ref):
    seq = q_ref.shape[0]
    blk = ATT_BLOCK
    widest = DILATIONS[-1]
    per_residue = seq // widest
    for r in range(widest):
        src = pl.ds(r, per_residue, stride=widest)
        dst = slice(r * per_residue, (r + 1) * per_residue)
        c, a, b = c_ref[dst, :], a_ref[dst, :], b_ref[dst, :]
        qr_ref[dst, :] = _rope(q_ref[src, :], c, a, b)
        kr_ref[dst, :] = _rope(k_ref[src, :], c, a, b)
        vr_ref[dst, :] = v_ref[src, :]

    def chunk_rows(dil):
        return blk * dil // widest

    def block_masks(dil):
        def position(i):
            return (widest // dil) * (i % chunk_rows(dil)) + i // chunk_rows(dil)
        row = position(lax.broadcasted_iota(jnp.int32, (blk, blk), 0))
        col = position(lax.broadcasted_iota(jnp.int32, (blk, blk), 1))
        return col <= row, col >= row

    def chunks_of(dil, r, n):
        size = chunk_rows(dil)
        return [pl.ds(pl.multiple_of((r + dil * a) * per_residue + size * n, size), size)
                for a in range(widest // dil)]

    def gather(ref, chunks):
        return jnp.concatenate([ref[rows, :] for rows in chunks], axis=0)

    def scatter(ref, chunks, value):
        size = value.shape[0] // len(chunks)
        for a, rows in enumerate(chunks):
            ref[rows, :] = value[a * size:(a + 1) * size]

    def run_group(specs, masks_of_pattern, first):
        cur_mask, prev_mask = masks_of_pattern
        loaded = []
        for rows, prev, prev_ok in specs:
            ks, vs, masks = [gather(kr_ref, rows)], [gather(vr_ref, rows)], [cur_mask]
            if prev is not None:
                ks.append(gather(kr_ref, prev))
                vs.append(gather(vr_ref, prev))
                masks.append(jnp.logical_and(prev_mask, prev_ok))
            state = None if first else (gather(m_ref, rows), gather(l_ref, rows), gather(acc_ref, rows))
            loaded.append((gather(qr_ref, rows).astype(BF16), ks, vs, masks, state))
        pieces = _tile_softmax_many([(q16, ks, vs, masks, None if state is None else state[0])
                                     for q16, ks, vs, masks, state in loaded])
        results = []
        for (m, l, acc), (_, _, _, _, state) in zip(pieces, loaded):
            if state is None:
                shape = (blk, HEAD_DIM)
                results.append((jnp.broadcast_to(m, shape), jnp.broadcast_to(l, shape), acc))
            else:
                m_old, l_old, acc_old = state
                alpha = jnp.exp(m_old - m)
                results.append((m, alpha * l_old + l, alpha * acc_old + acc))
        for (rows, _, _), (m, l, acc) in zip(specs, results):
            scatter(m_ref, rows, m)
            scatter(l_ref, rows, l)
            scatter(acc_ref, rows, acc)

    n_query_blocks = seq // blk
    for dil in DILATIONS:
        n_blocks = n_query_blocks // dil
        masks_of_pattern = block_masks(dil)

        def group(g, carry, dil=dil, n_blocks=n_blocks, masks_of_pattern=masks_of_pattern):
            specs = []
            for u in range(DIL_UNROLL):
                e = g * DIL_UNROLL + u
                r, n = e % dil, e // dil
                if n_blocks == 1:
                    specs.append((chunks_of(dil, r, n), None, None))
                else:
                    specs.append((chunks_of(dil, r, n), chunks_of(dil, r, jnp.maximum(n - 1, 0)), n > 0))
            run_group(specs, masks_of_pattern, first=(dil == DILATIONS[0]))
            return carry

        lax.fori_loop(0, n_query_blocks // DIL_UNROLL, group, 0)

    for r in range(widest):
        rows = slice(r * per_residue, (r + 1) * per_residue)
        qr_ref[pl.ds(r, per_residue, stride=widest), :] = acc_ref[rows, :] / l_ref[rows, :]
    o_ref[...] = qr_ref[...].astype(o_ref.dtype)


def _residue_major(table, widest):
    seq, width = table.shape
    return table.reshape(seq // widest, widest, width).transpose(1, 0, 2).reshape(seq, width)


def _dilated(proj3, tables):
    bsz, seq, _ = proj3.shape
    col = lambda blk: pl.BlockSpec((None, seq, HEAD_DIM), lambda b, h, blk=blk: (b, 0, blk + h))
    tab = pl.BlockSpec((seq, HEAD_DIM), lambda b, h: (0, 0))
    return pl.pallas_call(
        _dilated_kernel,
        grid=(bsz, B_HEADS),
        in_specs=[col(QB_BLK), col(KB_BLK), col(VB_BLK), tab, tab, tab],
        out_specs=pl.BlockSpec((None, seq, HEAD_DIM), lambda b, h: (b, 0, h)),
        out_shape=jax.ShapeDtypeStruct((bsz, seq, B_HEADS * HEAD_DIM), BF16),
        scratch_shapes=[pltpu.VMEM((seq, HEAD_DIM), F32)] * 6,
        compiler_params=_params(("parallel", "parallel")),
        name="dilated_attention",
    )(proj3, proj3, proj3, *[_residue_major(t, DILATIONS[-1]) for t in tables])


SWA_UNROLL = 2


def _swa_kernel(sink_ref, q0_ref, q1_ref, q2_ref, k_ref, v_ref, c_ref, a_ref, b_ref, o_ref, qr_ref, kr_ref):
    seq = k_ref.shape[0]
    blk = ATT_BLOCK
    grp = pl.program_id(1)
    c, a, b = c_ref[...], a_ref[...], b_ref[...]
    for j, q_ref in enumerate((q0_ref, q1_ref, q2_ref)):
        qr_ref[j] = _rope(q_ref[...], c, a, b).astype(BF16)
    kr_ref[...] = _rope(k_ref[...], c, a, b).astype(BF16)
    cur_mask, prev_mask = _band_masks(C_MAX_LAG)
    sinks = [jnp.full((blk, 1), sink_ref[grp * C_REP + j], F32) for j in range(C_REP)]

    def group(g, carry):
        blocks, dests = [], []
        for u in range(SWA_UNROLL):
            n = g * SWA_UNROLL + u
            rows = pl.ds(pl.multiple_of(n * blk, blk), blk)
            prev = pl.ds(pl.multiple_of(jnp.maximum(n - 1, 0) * blk, blk), blk)
            ks, vs = [kr_ref[rows, :], kr_ref[prev, :]], [v_ref[rows, :], v_ref[prev, :]]
            masks = [cur_mask, jnp.logical_and(prev_mask, n > 0)]
            for j in range(C_REP):
                blocks.append((qr_ref[j, rows, :], ks, vs, masks, sinks[j]))
                dests.append((rows, j))
        for (rows, j), (m, l, acc) in zip(dests, _tile_softmax_many(blocks)):
            out = acc / (l + jnp.exp(sinks[j] - m))
            o_ref[rows, j * HEAD_DIM:(j + 1) * HEAD_DIM] = out.astype(o_ref.dtype)
        return carry

    lax.fori_loop(0, seq // (blk * SWA_UNROLL), group, 0)


def _swa(proj3, sinks, tables):
    bsz, seq, _ = proj3.shape
    qcol = lambda j: pl.BlockSpec((None, seq, HEAD_DIM), lambda b, g, j=j: (b, 0, QC_BLK + g * C_REP + j))
    col = lambda blk: pl.BlockSpec((None, seq, HEAD_DIM), lambda b, g, blk=blk: (b, 0, blk + g))
    tab = pl.BlockSpec((seq, HEAD_DIM), lambda b, g: (0, 0))
    return pl.pallas_call(
        _swa_kernel,
        grid=(bsz, C_KV_HEADS),
        in_specs=[pl.BlockSpec(memory_space=pltpu.SMEM), qcol(0), qcol(1), qcol(2), col(KC_BLK), col(VC_BLK),
                  tab, tab, tab],
        out_specs=pl.BlockSpec((None, seq, C_REP * HEAD_DIM), lambda b, g: (b, 0, g)),
        out_shape=jax.ShapeDtypeStruct((bsz, seq, C_HEADS * HEAD_DIM), BF16),
        scratch_shapes=[pltpu.VMEM((C_REP, seq, HEAD_DIM), BF16), pltpu.VMEM((seq, HEAD_DIM), BF16)],
        compiler_params=_params(("parallel", "parallel")),
        name="swa_sink_attention",
    )(sinks, proj3, proj3, proj3, proj3, proj3, *tables)


def _layer_norm_store(z, g_ref, b_ref, o32_ref, o16_ref):
    mu = jnp.mean(z, axis=-1, keepdims=True)
    zc = z - mu
    var = jnp.mean(zc * zc, axis=-1, keepdims=True)
    y = zc * lax.rsqrt(var + LN_EPS) * g_ref[...] + b_ref[...]
    o32_ref[...] = y
    o16_ref[...] = y.astype(BF16)


def _stage_weight_bf16(w_hbm, layer, w16_ref, stage_ref, sem):
    chunk = stage_ref.shape[1]
    n_chunks = w16_ref.shape[0] // chunk

    def copy(c):
        slot = c % 2
        return pltpu.make_async_copy(w_hbm.at[layer, pl.ds(c * chunk, chunk), :], stage_ref.at[slot], sem.at[slot])

    copy(0).start()
    for c in range(n_chunks):
        if c + 1 < n_chunks:
            copy(c + 1).start()
        copy(c).wait()
        w16_ref[c * chunk:(c + 1) * chunk, :] = stage_ref[c % 2].astype(BF16)


WEIGHT_STAGE_ROWS = 256


def _resident_weight_scratch(k, n):
    return [pltpu.VMEM((k, n), BF16), pltpu.VMEM((2, WEIGHT_STAGE_ROWS, n), F32), pltpu.SemaphoreType.DMA((2,))]


def _out_proj_kernel(alpha, layer, oa_ref, ob_ref, oc_ref, w_hbm, x_ref, g_ref, b_ref, o32_ref, o16_ref,
                     w16_ref, stage_ref, sem):
    @pl.when(pl.program_id(0) == 0)
    def _():
        _stage_weight_bf16(w_hbm, layer, w16_ref, stage_ref, sem)

    ka, kb = oa_ref.shape[1], ob_ref.shape[1]
    y = (jnp.dot(oa_ref[...], w16_ref[0:ka, :], preferred_element_type=F32)
         + jnp.dot(ob_ref[...], w16_ref[ka:ka + kb, :], preferred_element_type=F32)
         + jnp.dot(oc_ref[...], w16_ref[ka + kb:, :], preferred_element_type=F32))
    _layer_norm_store(alpha * x_ref[...] + y, g_ref, b_ref, o32_ref, o16_ref)


def _layer_vec(layer, d):
    return pl.BlockSpec((None, 1, d), lambda i, layer=layer: (layer, 0, 0))


def _out_proj_ln(oa, ob, oc, w_out, layer, x, g, b, alpha, bm):
    m, d = x.shape
    rows = lambda a: pl.BlockSpec((bm, a.shape[1]), lambda i: (i, 0))
    return pl.pallas_call(
        functools.partial(_out_proj_kernel, alpha, layer),
        grid=(m // bm,),
        in_specs=[rows(oa), rows(ob), rows(oc), pl.BlockSpec(memory_space=pl.ANY), rows(x),
                  _layer_vec(layer, d), _layer_vec(layer, d)],
        out_specs=[rows(x), rows(x)],
        out_shape=[jax.ShapeDtypeStruct((m, d), F32), jax.ShapeDtypeStruct((m, d), BF16)],
        scratch_shapes=_resident_weight_scratch(w_out.shape[1], d),
        compiler_params=_params(("arbitrary",)),
        name="out_proj_ln",
    )(oa, ob, oc, w_out, x, g, b)


def _in_proj_kernel(x_ref, w_ref, o_ref, w16_ref):
    @pl.when(pl.program_id(1) == 0)
    def _():
        w16_ref[...] = w_ref[...].astype(BF16)

    o_ref[...] = jnp.dot(x_ref[...], w16_ref[...], preferred_element_type=F32).astype(o_ref.dtype)


def _in_proj(x16, w_in, layer, bm, bn):
    m, k = x16.shape
    n = w_in.shape[2]
    return pl.pallas_call(
        _in_proj_kernel,
        grid=(n // bn, m // bm),
        in_specs=[pl.BlockSpec((bm, k), lambda j, i: (i, 0)),
                  pl.BlockSpec((None, k, bn), lambda j, i: (layer, 0, j))],
        out_specs=pl.BlockSpec((bm, bn), lambda j, i: (i, j)),
        out_shape=jax.ShapeDtypeStruct((m, n), F32),
        scratch_shapes=[pltpu.VMEM((k, bn), BF16)],
        compiler_params=_params(("parallel", "arbitrary")),
        name="in_proj",
    )(x16, w_in)


def _ffn_up_kernel(tiles_per_seq, x_ref, wg_ref, wu_ref, cw_ref, cb_ref, h_ref, g_ref, wg16_ref, wu16_ref):
    bm = x_ref.shape[0]
    halo = g_ref.shape[0] - bm

    @pl.when(pl.program_id(1) == 0)
    def _():
        wg16_ref[...] = wg_ref[...].astype(BF16)
        wu16_ref[...] = wu_ref[...].astype(BF16)

    @pl.when(pl.program_id(1) % tiles_per_seq == 0)
    def _():
        g_ref[0:halo, :] = jnp.zeros((halo, g_ref.shape[1]), F32)

    x = x_ref[...]
    g_ref[halo:, :] = jnp.dot(x, wg16_ref[...], preferred_element_type=F32)
    u = jnp.dot(x, wu16_ref[...], preferred_element_type=F32)
    gc = cb_ref[...]
    for j in range(CONV_WIDTH):
        lag = CONV_WIDTH - 1 - j
        gc = gc + cw_ref[j:j + 1, :] * g_ref[halo - lag:halo - lag + bm, :]
    h_ref[...] = (gc * jax.nn.sigmoid(gc) * u).astype(h_ref.dtype)
    g_ref[0:halo, :] = g_ref[bm:bm + halo, :]


def _ffn_up(x16, w_gate, w_up, conv_w, conv_b, layer, seq, bm, bn):
    m, d = x16.shape
    f = w_gate.shape[2]
    halo = 8
    wspec = pl.BlockSpec((None, d, bn), lambda j, i: (layer, 0, j))
    return pl.pallas_call(
        functools.partial(_ffn_up_kernel, seq // bm),
        grid=(f // bn, m // bm),
        in_specs=[pl.BlockSpec((bm, d), lambda j, i: (i, 0)), wspec, wspec,
                  pl.BlockSpec((None, CONV_WIDTH, bn), lambda j, i: (layer, 0, j)),
                  pl.BlockSpec((None, 1, bn), lambda j, i: (layer, 0, j))],
        out_specs=pl.BlockSpec((bm, bn), lambda j, i: (i, j)),
        out_shape=jax.ShapeDtypeStruct((m, f), BF16),
        scratch_shapes=[pltpu.VMEM((bm + halo, bn), F32), pltpu.VMEM((d, bn), BF16), pltpu.VMEM((d, bn), BF16)],
        compiler_params=_params(("parallel", "arbitrary")),
        name="ffn_up_conv_gate",
    )(x16, w_gate, w_up, conv_w, conv_b)


def _ffn_down_kernel(alpha, layer, h_ref, w_hbm, x_ref, g_ref, b_ref, o32_ref, o16_ref, w16_ref, stage_ref, sem):
    @pl.when(pl.program_id(0) == 0)
    def _():
        _stage_weight_bf16(w_hbm, layer, w16_ref, stage_ref, sem)

    y = jnp.dot(h_ref[...], w16_ref[...], preferred_element_type=F32)
    _layer_norm_store(alpha * x_ref[...] + y, g_ref, b_ref, o32_ref, o16_ref)


def _ffn_down_ln(h, w_down, layer, x, g, b, alpha, bm):
    m, d = x.shape
    f = h.shape[1]
    rows = pl.BlockSpec((bm, d), lambda i: (i, 0))
    return pl.pallas_call(
        functools.partial(_ffn_down_kernel, alpha, layer),
        grid=(m // bm,),
        in_specs=[pl.BlockSpec((bm, f), lambda i: (i, 0)), pl.BlockSpec(memory_space=pl.ANY), rows,
                  _layer_vec(layer, d), _layer_vec(layer, d)],
        out_specs=[rows, rows],
        out_shape=[jax.ShapeDtypeStruct((m, d), F32), jax.ShapeDtypeStruct((m, d), BF16)],
        scratch_shapes=_resident_weight_scratch(f, d),
        compiler_params=_params(("arbitrary",)),
        name="ffn_down_ln",
    )(h, w_down, x, g, b)


def kernel(x, w_in, lb_logits, a_norm_w, c_sinks, w_out, ln1_g, ln1_b, w_gate, w_up, conv_w, conv_b, w_down,
           ln2_g, ln2_b):
    bsz, seq, d = x.shape
    depth = w_in.shape[0]
    alpha = (2 * depth) ** 0.25
    tables = _rope_tables(seq)
    lbs = jnp.cumsum(jax.nn.softmax(lb_logits.astype(F32), axis=0), axis=0)
    lbs = lbs - lbs[0]
    vec3 = lambda a: a.reshape(depth, 1, -1)
    ln1_g, ln1_b, ln2_g, ln2_b, conv_b = vec3(ln1_g), vec3(ln1_b), vec3(ln2_g), vec3(ln2_b), vec3(conv_b)
    x32 = x.reshape(bsz * seq, d)
    x16 = x32.astype(BF16)
    for l in range(depth):
        proj3 = _in_proj(x16, w_in, l, bm=2048, bn=512).reshape(bsz, seq, -1)
        oa = _hgrn2(proj3, lbs[l], a_norm_w[l]).reshape(bsz * seq, -1)
        ob = _dilated(proj3, tables).reshape(bsz * seq, -1)
        oc = _swa(proj3, c_sinks[l], tables).reshape(bsz * seq, -1)
        x32, x16 = _out_proj_ln(oa, ob, oc, w_out, l, x32, ln1_g, ln1_b, alpha, bm=512)
        h = _ffn_up(x16, w_gate, w_up, conv_w, conv_b, l, seq, bm=1024, bn=512)
        x32, x16 = _ffn_down_ln(h, w_down, l, x32, ln2_g, ln2_b, alpha, bm=256)
    return x32.reshape(bsz, seq, d)
```

```python
import functools

import numpy as np
import jax
import jax.numpy as jnp
from jax import lax
from jax.experimental import pallas as pl
from jax.experimental.pallas import tpu as pltpu

HEAD_DIM = 128
A_HEADS = 4
B_HEADS = 6
C_HEADS = 6
C_KV_HEADS = 2
C_REP = C_HEADS // C_KV_HEADS
DILATIONS = (1, 4, 16)
ATT_BLOCK = 128
C_MAX_LAG = 127
ROPE_THETA = 500000.0
ROPE_DIM = HEAD_DIM // 4
ROPE_HALF = ROPE_DIM // 2
CONV_WIDTH = 3
LN_EPS = 1e-5

QA_BLK, FA_BLK, IA_BLK, GA_BLK = 0, 4, 8, 12
QB_BLK, KB_BLK, VB_BLK = 16, 22, 28
QC_BLK, KC_BLK, VC_BLK = 34, 40, 42

HGRN_CHUNK = 128
HGRN_LEVELS = (64, 32, 16, 8, 4, 2, 1)

VMEM_LIMIT_BYTES = 56 * 1024 * 1024

F32 = jnp.float32
BF16 = jnp.bfloat16
NEG_BIG = -1e30


def _params(semantics):
    return pltpu.CompilerParams(dimension_semantics=semantics, vmem_limit_bytes=VMEM_LIMIT_BYTES)


def _rope_tables(seq):
    inv = ROPE_THETA ** (-jnp.arange(0, ROPE_DIM, 2, dtype=F32) / ROPE_DIM)
    ang = jnp.arange(seq, dtype=F32)[:, None] * inv[None, :]
    cos, sin = jnp.cos(ang), jnp.sin(ang)
    rest = HEAD_DIM - ROPE_DIM
    c = jnp.concatenate([cos, cos, jnp.ones((seq, rest), F32)], -1)
    s = jnp.concatenate([sin, sin, jnp.zeros((seq, rest), F32)], -1)
    rot = np.zeros((HEAD_DIM, HEAD_DIM), np.float32)
    lane = np.arange(ROPE_HALF)
    rot[lane + ROPE_HALF, lane] = -1.0
    rot[lane, lane + ROPE_HALF] = 1.0
    return c, s, jnp.asarray(rot, BF16)


def _rope(x, c, s, rot):
    hi = x.astype(BF16)
    lo = (x - hi.astype(F32)).astype(BF16)
    swapped = jnp.dot(hi, rot, preferred_element_type=F32) + jnp.dot(lo, rot, preferred_element_type=F32)
    return x * c + swapped * s


def _hgrn_constants():
    c = HGRN_CHUNK
    i = np.arange(c)[:, None]
    j = np.arange(c)[None, :]
    lvl = np.full((c, c), -1, np.int32)
    for n, h in enumerate(HGRN_LEVELS):
        lvl[(i // (2 * h) == j // (2 * h)) & (i // h != j // h) & (i > j)] = n
    lvl[np.arange(c), np.arange(c)] = len(HGRN_LEVELS)
    return jnp.asarray((j <= i).astype(np.float32), BF16), jnp.asarray(lvl)


def _midpoint_log_decay(lf, b, h, row):
    c, dk = b.shape
    odd = ((row // h) % 2) == 1
    if h == 1:
        return jnp.where(odd, lf, 0.0)
    if h == 2:
        below = pltpu.roll(lf, c - 1, 0)
        above = pltpu.roll(lf, 1, 0)
        place = row % 4
        return jnp.where(place == 0, below, jnp.where(place == 1, 0.0, jnp.where(place == 2, lf, lf + above)))
    blocks = b.reshape(c // (2 * h), 2 * h, dk)
    mid = jnp.broadcast_to(blocks[:, h - 1:h, :], blocks.shape).reshape(c, dk)
    return jnp.where(odd, b - mid, mid - b)


def _split3(x):
    p1 = x.astype(BF16)
    r1 = x - p1.astype(F32)
    p2 = r1.astype(BF16)
    p3 = (r1 - p2.astype(F32)).astype(BF16)
    return p1, p2, p3


def _dot_nt(a, b, **kw):
    return lax.dot_general(a, b, (((1,), (1,)), ((), ())), preferred_element_type=F32, **kw)


def _dot_tn(a, b, **kw):
    return lax.dot_general(a, b, (((0,), (0,)), ((), ())), preferred_element_type=F32, **kw)


def _hgrn_chunk_heads(inputs, nw, e_all, lvl, row):
    c = HGRN_CHUNK
    heads = range(len(inputs))
    qs = [q * jax.nn.sigmoid(q) for q, _, _, _, _, _ in inputs]
    fg = [lb + (1.0 - lb) * jax.nn.sigmoid(fl) for _, fl, _, _, _, lb in inputs]
    kk = [1.0 - f for f in fg]
    lf = [jnp.log2(f) for f in fg]
    parts = [_split3(x) for x in lf]
    b12 = [jnp.dot(e_all, jnp.concatenate([p1, p2], axis=1), preferred_element_type=F32) for p1, p2, _ in parts]
    b3 = [jnp.dot(e_all, p3, preferred_element_type=F32) for _, _, p3 in parts]
    b = [x[:, :HEAD_DIM] + x[:, HEAD_DIM:] + y for x, y in zip(b12, b3)]
    att = [jnp.where(lvl == len(HGRN_LEVELS), jnp.sum(qs[i] * kk[i], axis=-1, keepdims=True), 0.0) for i in heads]
    for n, h in enumerate(HGRN_LEVELS):
        odd = ((row // h) % 2) == 1
        xh = [(jnp.where(odd, qs[i], kk[i]) * jnp.exp2(_midpoint_log_decay(lf[i], b[i], h, row))).astype(BF16)
              for i in heads]
        pair = [_dot_nt(x, x) for x in xh]
        att = [jnp.where(lvl == n, pair[i], att[i]) for i in heads]
    b_end = [x[c - 1:c] for x in b]
    q_in = [(qs[i] * jnp.exp2(b[i])).astype(BF16) for i in heads]
    k_end = [(kk[i] * jnp.exp2(b_end[i] - b[i])).astype(BF16) for i in heads]
    v16 = [v.astype(BF16) for _, _, v, _, _, _ in inputs]
    o = [jnp.dot(att[i].astype(BF16), v16[i], preferred_element_type=F32)
         + _dot_nt(q_in[i], inputs[i][4].astype(BF16)) for i in heads]
    st_new = [inputs[i][4] * jnp.exp2(b_end[i]) + _dot_tn(v16[i], k_end[i]) for i in heads]
    o = [x * lax.rsqrt(jnp.mean(x * x, axis=-1, keepdims=True) + LN_EPS) * nw for x in o]
    out = [(o[i] * (inputs[i][3] * jax.nn.sigmoid(inputs[i][3]))).astype(BF16) for i in heads]
    return list(zip(out, st_new))


def _hgrn_kernel(q_ref, f_ref, i_ref, g_ref, lb_ref, nw_ref, e_ref, lvl_ref, o_ref, st_ref):
    c = HGRN_CHUNK
    n_chunks = q_ref.shape[0] // c
    st_ref[...] = jnp.zeros_like(st_ref)
    nw = nw_ref[...]
    row = lax.broadcasted_iota(jnp.int32, (c, 1), 0)

    def chunk(ci, carry):
        rows = pl.ds(pl.multiple_of(ci * c, c), c)
        e_all, lvl = e_ref[...], lvl_ref[...]
        inputs = []
        for h in range(A_HEADS):
            cols = slice(h * HEAD_DIM, (h + 1) * HEAD_DIM)
            inputs.append((q_ref[rows, cols], f_ref[rows, cols], i_ref[rows, cols], g_ref[rows, cols], st_ref[h],
                           lb_ref[:, cols]))
        for h, (o, st_new) in enumerate(_hgrn_chunk_heads(inputs, nw, e_all, lvl, row)):
            o_ref[rows, h * HEAD_DIM:(h + 1) * HEAD_DIM] = o
            st_ref[h] = st_new
        return carry

    lax.fori_loop(0, n_chunks, chunk, 0)


def _hgrn2(proj3, lb, norm_w):
    bsz, seq, _ = proj3.shape
    e_all, lvl = _hgrn_constants()
    width = A_HEADS * HEAD_DIM
    col = lambda blk: pl.BlockSpec((None, seq, width), lambda b, blk=blk: (b, 0, blk // A_HEADS))
    const = lambda shape: pl.BlockSpec(shape, lambda b: (0,) * len(shape))
    return pl.pallas_call(
        _hgrn_kernel,
        grid=(bsz,),
        in_specs=[col(QA_BLK), col(FA_BLK), col(IA_BLK), col(GA_BLK),
                  const((1, width)), const((1, HEAD_DIM)), const(e_all.shape), const(lvl.shape)],
        out_specs=pl.BlockSpec((None, seq, width), lambda b: (b, 0, 0)),
        out_shape=jax.ShapeDtypeStruct((bsz, seq, width), BF16),
        scratch_shapes=[pltpu.VMEM((A_HEADS, HEAD_DIM, HEAD_DIM), F32)],
        compiler_params=_params(("parallel",)),
        name="hgrn2",
    )(proj3, proj3, proj3, proj3, lb.reshape(1, width), norm_w.reshape(1, HEAD_DIM), e_all, lvl)


def _band_masks(max_lag):
    r = lax.broadcasted_iota(jnp.int32, (ATT_BLOCK, ATT_BLOCK), 0)
    col = lax.broadcasted_iota(jnp.int32, (ATT_BLOCK, ATT_BLOCK), 1)
    return col <= r, (ATT_BLOCK + r - col) <= max_lag


def _tile_softmax_many(blocks):
    scale = HEAD_DIM ** -0.5
    s_all = [[jnp.where(mask, _dot_nt(q16, k.astype(BF16)) * scale, NEG_BIG) for k, mask in zip(ks, masks)]
             for q16, ks, _, masks, _ in blocks]
    m_all = []
    for s, (_, _, _, _, m_floor) in zip(s_all, blocks):
        m = jnp.max(functools.reduce(jnp.maximum, s), axis=-1, keepdims=True)
        m_all.append(m if m_floor is None else jnp.maximum(m, m_floor))
    e_all = [[jnp.exp(si - m) for si in s] for s, m in zip(s_all, m_all)]
    acc_all = [functools.reduce(jnp.add, [jnp.dot(ei.astype(BF16), v.astype(BF16), preferred_element_type=F32)
                                          for ei, v in zip(e, vs)])
               for e, (_, _, vs, _, _) in zip(e_all, blocks)]
    l_all = [jnp.sum(functools.reduce(jnp.add, e), axis=-1, keepdims=True) for e in e_all]
    return list(zip(m_all, l_all, acc_all))


DIL_UNROLL = 8


def _dilated_kernel(q_ref, k_ref, v_ref, c_ref, s_ref, rot_ref, o_ref, qr_ref, kr_ref, vr_ref, m_ref, l_ref, acc_ref):
    seq = q_ref.shape[0]
    blk = ATT_BLOCK
    widest = DILATIONS[-1]
    per_residue = seq // widest
    for r in range(widest):
        src = pl.ds(r, per_residue, stride=widest)
        dst = slice(r * per_residue, (r + 1) * per_residue)
        c, s, rot = c_ref[dst, :], s_ref[dst, :], rot_ref[...]
        qr_ref[dst, :] = _rope(q_ref[src, :], c, s, rot)
        kr_ref[dst, :] = _rope(k_ref[src, :], c, s, rot)
        vr_ref[dst, :] = v_ref[src, :]

    def chunk_rows(dil):
        return blk * dil // widest

    def block_masks(dil):
        def position(i):
            return (widest // dil) * (i % chunk_rows(dil)) + i // chunk_rows(dil)
        row = position(lax.broadcasted_iota(jnp.int32, (blk, blk), 0))
        col = position(lax.broadcasted_iota(jnp.int32, (blk, blk), 1))
        return col <= row, col >= row

    def chunks_of(dil, r, n):
        size = chunk_rows(dil)
        return [pl.ds(pl.multiple_of((r + dil * a) * per_residue + size * n, size), size)
                for a in range(widest // dil)]

    def gather(ref, chunks):
        return jnp.concatenate([ref[rows, :] for rows in chunks], axis=0)

    def scatter(ref, chunks, value):
        size = value.shape[0] // len(chunks)
        for a, rows in enumerate(chunks):
            ref[rows, :] = value[a * size:(a + 1) * size]

    def run_group(specs, masks_of_pattern, first):
        cur_mask, prev_mask = masks_of_pattern
        loaded = []
        for rows, prev, prev_ok in specs:
            ks, vs, masks = [gather(kr_ref, rows)], [gather(vr_ref, rows)], [cur_mask]
            if prev is not None:
                ks.append(gather(kr_ref, prev))
                vs.append(gather(vr_ref, prev))
                masks.append(jnp.logical_and(prev_mask, prev_ok))
            state = None if first else (gather(m_ref, rows), gather(l_ref, rows), gather(acc_ref, rows))
            loaded.append((gather(qr_ref, rows).astype(BF16), ks, vs, masks, state))
        pieces = _tile_softmax_many([(q16, ks, vs, masks, None if state is None else state[0])
                                     for q16, ks, vs, masks, state in loaded])
        results = []
        for (m, l, acc), (_, _, _, _, state) in zip(pieces, loaded):
            if state is None:
                shape = (blk, HEAD_DIM)
                results.append((jnp.broadcast_to(m, shape), jnp.broadcast_to(l, shape), acc))
            else:
                m_old, l_old, acc_old = state
                alpha = jnp.exp(m_old - m)
                results.append((m, alpha * l_old + l, alpha * acc_old + acc))
        for (rows, _, _), (m, l, acc) in zip(specs, results):
            scatter(m_ref, rows, m)
            scatter(l_ref, rows, l)
            scatter(acc_ref, rows, acc)

    n_query_blocks = seq // blk
    for dil in DILATIONS:
        n_blocks = n_query_blocks // dil
        masks_of_pattern = block_masks(dil)

        def group(g, carry, dil=dil, n_blocks=n_blocks, masks_of_pattern=masks_of_pattern):
            specs = []
            for u in range(DIL_UNROLL):
                e = g * DIL_UNROLL + u
                r, n = e % dil, e // dil
                if n_blocks == 1:
                    specs.append((chunks_of(dil, r, n), None, None))
                else:
                    specs.append((chunks_of(dil, r, n), chunks_of(dil, r, jnp.maximum(n - 1, 0)), n > 0))
            run_group(specs, masks_of_pattern, first=(dil == DILATIONS[0]))
            return carry

        lax.fori_loop(0, n_query_blocks // DIL_UNROLL, group, 0)

    for r in range(widest):
        rows = slice(r * per_residue, (r + 1) * per_residue)
        qr_ref[pl.ds(r, per_residue, stride=widest), :] = acc_ref[rows, :] / l_ref[rows, :]
    o_ref[...] = qr_ref[...].astype(o_ref.dtype)


def _residue_major(table, widest):
    seq, width = table.shape
    return table.reshape(seq // widest, widest, width).transpose(1, 0, 2).reshape(seq, width)


def _dilated(proj3, tables):
    bsz, seq, _ = proj3.shape
    col = lambda blk: pl.BlockSpec((None, seq, HEAD_DIM), lambda b, h, blk=blk: (b, 0, blk + h))
    tab = pl.BlockSpec((seq, HEAD_DIM), lambda b, h: (0, 0))
    return pl.pallas_call(
        _dilated_kernel,
        grid=(bsz, B_HEADS),
        in_specs=[col(QB_BLK), col(KB_BLK), col(VB_BLK), tab, tab,
                  pl.BlockSpec((HEAD_DIM, HEAD_DIM), lambda b, h: (0, 0))],
        out_specs=pl.BlockSpec((None, seq, HEAD_DIM), lambda b, h: (b, 0, h)),
        out_shape=jax.ShapeDtypeStruct((bsz, seq, B_HEADS * HEAD_DIM), BF16),
        scratch_shapes=[pltpu.VMEM((seq, HEAD_DIM), F32)] * 6,
        compiler_params=_params(("parallel", "parallel")),
        name="dilated_attention",
    )(proj3, proj3, proj3, _residue_major(tables[0], DILATIONS[-1]), _residue_major(tables[1], DILATIONS[-1]),
      tables[2])


SWA_UNROLL = 2


def _swa_kernel(sink_ref, q0_ref, q1_ref, q2_ref, k_ref, v_ref, c_ref, s_ref, rot_ref, o_ref, qr_ref, kr_ref):
    seq = k_ref.shape[0]
    blk = ATT_BLOCK
    grp = pl.program_id(1)
    c, s, rot = c_ref[...], s_ref[...], rot_ref[...]
    for j, q_ref in enumerate((q0_ref, q1_ref, q2_ref)):
        qr_ref[j] = _rope(q_ref[...], c, s, rot).astype(BF16)
    kr_ref[...] = _rope(k_ref[...], c, s, rot).astype(BF16)
    cur_mask, prev_mask = _band_masks(C_MAX_LAG)
    sinks = [jnp.full((blk, 1), sink_ref[grp * C_REP + j], F32) for j in range(C_REP)]

    def group(g, carry):
        blocks, dests = [], []
        for u in range(SWA_UNROLL):
            n = g * SWA_UNROLL + u
            rows = pl.ds(pl.multiple_of(n * blk, blk), blk)
            prev = pl.ds(pl.multiple_of(jnp.maximum(n - 1, 0) * blk, blk), blk)
            ks, vs = [kr_ref[rows, :], kr_ref[prev, :]], [v_ref[rows, :], v_ref[prev, :]]
            masks = [cur_mask, jnp.logical_and(prev_mask, n > 0)]
            for j in range(C_REP):
                blocks.append((qr_ref[j, rows, :], ks, vs, masks, sinks[j]))
                dests.append((rows, j))
        for (rows, j), (m, l, acc) in zip(dests, _tile_softmax_many(blocks)):
            out = acc / (l + jnp.exp(sinks[j] - m))
            o_ref[rows, j * HEAD_DIM:(j + 1) * HEAD_DIM] = out.astype(o_ref.dtype)
        return carry

    lax.fori_loop(0, seq // (blk * SWA_UNROLL), group, 0)


def _swa(proj3, sinks, tables):
    bsz, seq, _ = proj3.shape
    qcol = lambda j: pl.BlockSpec((None, seq, HEAD_DIM), lambda b, g, j=j: (b, 0, QC_BLK + g * C_REP + j))
    col = lambda blk: pl.BlockSpec((None, seq, HEAD_DIM), lambda b, g, blk=blk: (b, 0, blk + g))
    tab = pl.BlockSpec((seq, HEAD_DIM), lambda b, g: (0, 0))
    return pl.pallas_call(
        _swa_kernel,
        grid=(bsz, C_KV_HEADS),
        in_specs=[pl.BlockSpec(memory_space=pltpu.SMEM), qcol(0), qcol(1), qcol(2), col(KC_BLK), col(VC_BLK),
                  tab, tab, pl.BlockSpec((HEAD_DIM, HEAD_DIM), lambda b, g: (0, 0))],
        out_specs=pl.BlockSpec((None, seq, C_REP * HEAD_DIM), lambda b, g: (b, 0, g)),
        out_shape=jax.ShapeDtypeStruct((bsz, seq, C_HEADS * HEAD_DIM), BF16),
        scratch_shapes=[pltpu.VMEM((C_REP, seq, HEAD_DIM), BF16), pltpu.VMEM((seq, HEAD_DIM), BF16)],
        compiler_params=_params(("parallel", "parallel")),
        name="swa_sink_attention",
    )(sinks, proj3, proj3, proj3, proj3, proj3, *tables)


def _layer_norm_store(z, g_ref, b_ref, o32_ref, o16_ref):
    mu = jnp.mean(z, axis=-1, keepdims=True)
    zc = z - mu
    var = jnp.mean(zc * zc, axis=-1, keepdims=True)
    y = zc * lax.rsqrt(var + LN_EPS) * g_ref[...] + b_ref[...]
    o32_ref[...] = y
    o16_ref[...] = y.astype(BF16)


def _stage_weight_bf16(w_hbm, layer, w16_ref, stage_ref, sem):
    chunk = stage_ref.shape[1]
    n_chunks = w16_ref.shape[0] // chunk

    def copy(c):
        slot = c % 2
        return pltpu.make_async_copy(w_hbm.at[layer, pl.ds(c * chunk, chunk), :], stage_ref.at[slot], sem.at[slot])

    copy(0).start()
    for c in range(n_chunks):
        if c + 1 < n_chunks:
            copy(c + 1).start()
        copy(c).wait()
        w16_ref[c * chunk:(c + 1) * chunk, :] = stage_ref[c % 2].astype(BF16)


WEIGHT_STAGE_ROWS = 256


def _resident_weight_scratch(k, n):
    return [pltpu.VMEM((k, n), BF16), pltpu.VMEM((2, WEIGHT_STAGE_ROWS, n), F32), pltpu.SemaphoreType.DMA((2,))]


def _out_proj_kernel(alpha, layer, oa_ref, ob_ref, oc_ref, w_hbm, x_ref, g_ref, b_ref, o32_ref, o16_ref,
                     w16_ref, stage_ref, sem):
    @pl.when(pl.program_id(0) == 0)
    def _():
        _stage_weight_bf16(w_hbm, layer, w16_ref, stage_ref, sem)

    ka, kb = oa_ref.shape[1], ob_ref.shape[1]
    y = (jnp.dot(oa_ref[...], w16_ref[0:ka, :], preferred_element_type=F32)
         + jnp.dot(ob_ref[...], w16_ref[ka:ka + kb, :], preferred_element_type=F32)
         + jnp.dot(oc_ref[...], w16_ref[ka + kb:, :], preferred_element_type=F32))
    _layer_norm_store(alpha * x_ref[...] + y, g_ref, b_ref, o32_ref, o16_ref)


def _layer_vec(layer, d):
    return pl.BlockSpec((None, 1, d), lambda i, layer=layer: (layer, 0, 0))


def _out_proj_ln(oa, ob, oc, w_out, layer, x, g, b, alpha, bm):
    m, d = x.shape
    rows = lambda a: pl.BlockSpec((bm, a.shape[1]), lambda i: (i, 0))
    return pl.pallas_call(
        functools.partial(_out_proj_kernel, alpha, layer),
        grid=(m // bm,),
        in_specs=[rows(oa), rows(ob), rows(oc), pl.BlockSpec(memory_space=pl.ANY), rows(x),
                  _layer_vec(layer, d), _layer_vec(layer, d)],
        out_specs=[rows(x), rows(x)],
        out_shape=[jax.ShapeDtypeStruct((m, d), F32), jax.ShapeDtypeStruct((m, d), BF16)],
        scratch_shapes=_resident_weight_scratch(w_out.shape[1], d),
        compiler_params=_params(("arbitrary",)),
        name="out_proj_ln",
    )(oa, ob, oc, w_out, x, g, b)


def _in_proj_kernel(x_ref, w_ref, o_ref):
    o_ref[...] = jnp.dot(x_ref[...], w_ref[...].astype(BF16), preferred_element_type=F32).astype(o_ref.dtype)


def _in_proj(x16, w_in, layer, bm, bn):
    m, k = x16.shape
    n = w_in.shape[2]
    return pl.pallas_call(
        _in_proj_kernel,
        grid=(m // bm, n // bn),
        in_specs=[pl.BlockSpec((bm, k), lambda i, j: (i, 0)),
                  pl.BlockSpec((None, k, bn), lambda i, j: (layer, 0, j))],
        out_specs=pl.BlockSpec((bm, bn), lambda i, j: (i, j)),
        out_shape=jax.ShapeDtypeStruct((m, n), F32),
        compiler_params=_params(("parallel", "parallel")),
        name="in_proj",
    )(x16, w_in)


def _ffn_up_kernel(tiles_per_seq, x_ref, wg_ref, wu_ref, cw_ref, cb_ref, h_ref, g_ref, wg16_ref, wu16_ref):
    bm = x_ref.shape[0]
    halo = g_ref.shape[0] - bm

    @pl.when(pl.program_id(1) == 0)
    def _():
        wg16_ref[...] = wg_ref[...].astype(BF16)
        wu16_ref[...] = wu_ref[...].astype(BF16)

    @pl.when(pl.program_id(1) % tiles_per_seq == 0)
    def _():
        g_ref[0:halo, :] = jnp.zeros((halo, g_ref.shape[1]), F32)

    x = x_ref[...]
    g_ref[halo:, :] = jnp.dot(x, wg16_ref[...], preferred_element_type=F32)
    u = jnp.dot(x, wu16_ref[...], preferred_element_type=F32)
    gc = cb_ref[...]
    for j in range(CONV_WIDTH):
        lag = CONV_WIDTH - 1 - j
        gc = gc + cw_ref[j:j + 1, :] * g_ref[halo - lag:halo - lag + bm, :]
    h_ref[...] = (gc * jax.nn.sigmoid(gc) * u).astype(h_ref.dtype)
    g_ref[0:halo, :] = g_ref[bm:bm + halo, :]


def _ffn_up(x16, w_gate, w_up, conv_w, conv_b, layer, seq, bm, bn):
    m, d = x16.shape
    f = w_gate.shape[2]
    halo = 8
    wspec = pl.BlockSpec((None, d, bn), lambda j, i: (layer, 0, j))
    return pl.pallas_call(
        functools.partial(_ffn_up_kernel, seq // bm),
        grid=(f // bn, m // bm),
        in_specs=[pl.BlockSpec((bm, d), lambda j, i: (i, 0)), wspec, wspec,
                  pl.BlockSpec((None, CONV_WIDTH, bn), lambda j, i: (layer, 0, j)),
                  pl.BlockSpec((None, 1, bn), lambda j, i: (layer, 0, j))],
        out_specs=pl.BlockSpec((bm, bn), lambda j, i: (i, j)),
        out_shape=jax.ShapeDtypeStruct((m, f), BF16),
        scratch_shapes=[pltpu.VMEM((bm + halo, bn), F32), pltpu.VMEM((d, bn), BF16), pltpu.VMEM((d, bn), BF16)],
        compiler_params=_params(("parallel", "arbitrary")),
        name="ffn_up_conv_gate",
    )(x16, w_gate, w_up, conv_w, conv_b)


def _ffn_down_kernel(alpha, layer, h_ref, w_hbm, x_ref, g_ref, b_ref, o32_ref, o16_ref, w16_ref, stage_ref, sem):
    @pl.when(pl.program_id(0) == 0)
    def _():
        _stage_weight_bf16(w_hbm, layer, w16_ref, stage_ref, sem)

    y = jnp.dot(h_ref[...], w16_ref[...], preferred_element_type=F32)
    _layer_norm_store(alpha * x_ref[...] + y, g_ref, b_ref, o32_ref, o16_ref)


def _ffn_down_ln(h, w_down, layer, x, g, b, alpha, bm):
    m, d = x.shape
    f = h.shape[1]
    rows = pl.BlockSpec((bm, d), lambda i: (i, 0))
    return pl.pallas_call(
        functools.partial(_ffn_down_kernel, alpha, layer),
        grid=(m // bm,),
        in_specs=[pl.BlockSpec((bm, f), lambda i: (i, 0)), pl.BlockSpec(memory_space=pl.ANY), rows,
                  _layer_vec(layer, d), _layer_vec(layer, d)],
        out_specs=[rows, rows],
        out_shape=[jax.ShapeDtypeStruct((m, d), F32), jax.ShapeDtypeStruct((m, d), BF16)],
        scratch_shapes=_resident_weight_scratch(f, d),
        compiler_params=_params(("arbitrary",)),
        name="ffn_down_ln",
    )(h, w_down, x, g, b)


def kernel(x, w_in, lb_logits, a_norm_w, c_sinks, w_out, ln1_g, ln1_b, w_gate, w_up, conv_w, conv_b, w_down,
           ln2_g, ln2_b):
    bsz, seq, d = x.shape
    depth = w_in.shape[0]
    alpha = (2 * depth) ** 0.25
    tables = _rope_tables(seq)
    lbs = jnp.cumsum(jax.nn.softmax(lb_logits.astype(F32), axis=0), axis=0)
    lbs = lbs - lbs[0]
    vec3 = lambda a: a.reshape(depth, 1, -1)
    ln1_g, ln1_b, ln2_g, ln2_b, conv_b = vec3(ln1_g), vec3(ln1_b), vec3(ln2_g), vec3(ln2_b), vec3(conv_b)
    x32 = x.reshape(bsz * seq, d)
    x16 = x32.astype(BF16)
    for l in range(depth):
        proj3 = _in_proj(x16, w_in, l, bm=2048, bn=512).reshape(bsz, seq, -1)
        oa = _hgrn2(proj3, lbs[l], a_norm_w[l]).reshape(bsz * seq, -1)
        ob = _dilated(proj3, tables).reshape(bsz * seq, -1)
        oc = _swa(proj3, c_sinks[l], tables).reshape(bsz * seq, -1)
        x32, x16 = _out_proj_ln(oa, ob, oc, w_out, l, x32, ln1_g, ln1_b, alpha, bm=512)
        h = _ffn_up(x16, w_gate, w_up, conv_w, conv_b, l, seq, bm=1024, bn=512)
        x32, x16 = _ffn_down_ln(h, w_down, l, x32, ln2_g, ln2_b, alpha, bm=256)
    return x32.reshape(bsz, seq, d)
```

```python
import functools

import numpy as np
import jax
import jax.numpy as jnp
from jax import lax
from jax.experimental import pallas as pl
from jax.experimental.pallas import tpu as pltpu

HEAD_DIM = 128
A_HEADS = 4
B_HEADS = 6
C_HEADS = 6
C_KV_HEADS = 2
C_REP = C_HEADS // C_KV_HEADS
DILATIONS = (1, 4, 16)
ATT_BLOCK = 128
C_MAX_LAG = 127
ROPE_THETA = 500000.0
ROPE_DIM = HEAD_DIM // 4
ROPE_HALF = ROPE_DIM // 2
CONV_WIDTH = 3
LN_EPS = 1e-5

QA_BLK, FA_BLK, IA_BLK, GA_BLK = 0, 4, 8, 12
QB_BLK, KB_BLK, VB_BLK = 16, 22, 28
QC_BLK, KC_BLK, VC_BLK = 34, 40, 42

HGRN_CHUNK = 128
HGRN_LEVELS = (64, 32, 16, 8, 4, 2, 1)

VMEM_LIMIT_BYTES = 56 * 1024 * 1024

F32 = jnp.float32
BF16 = jnp.bfloat16
NEG_BIG = -1e30


def _params(semantics):
    return pltpu.CompilerParams(dimension_semantics=semantics, vmem_limit_bytes=VMEM_LIMIT_BYTES)


def _rope_tables(seq):
    inv = ROPE_THETA ** (-jnp.arange(0, ROPE_DIM, 2, dtype=F32) / ROPE_DIM)
    ang = jnp.arange(seq, dtype=F32)[:, None] * inv[None, :]
    cos, sin = jnp.cos(ang), jnp.sin(ang)
    rest = HEAD_DIM - ROPE_DIM
    c = jnp.concatenate([cos, cos, jnp.ones((seq, rest), F32)], -1)
    s = jnp.concatenate([sin, sin, jnp.zeros((seq, rest), F32)], -1)
    rot = np.zeros((HEAD_DIM, HEAD_DIM), np.float32)
    lane = np.arange(ROPE_HALF)
    rot[lane + ROPE_HALF, lane] = -1.0
    rot[lane, lane + ROPE_HALF] = 1.0
    return c, s, jnp.asarray(rot, BF16)


def _rope(x, c, s, rot):
    hi = x.astype(BF16)
    lo = (x - hi.astype(F32)).astype(BF16)
    swapped = jnp.dot(hi, rot, preferred_element_type=F32) + jnp.dot(lo, rot, preferred_element_type=F32)
    return x * c + swapped * s


def _hgrn_constants():
    c = HGRN_CHUNK
    i = np.arange(c)[:, None]
    j = np.arange(c)[None, :]
    lvl = np.full((c, c), -1, np.int32)
    for n, h in enumerate(HGRN_LEVELS):
        lvl[(i // (2 * h) == j // (2 * h)) & (i // h != j // h) & (i > j)] = n
    lvl[np.arange(c), np.arange(c)] = len(HGRN_LEVELS)
    return jnp.asarray((j <= i).astype(np.float32), BF16), jnp.asarray(lvl)


def _midpoint_log_decay(lf, b, h, row):
    c, dk = b.shape
    odd = ((row // h) % 2) == 1
    if h == 1:
        return jnp.where(odd, lf, 0.0)
    if h == 2:
        below = pltpu.roll(lf, c - 1, 0)
        above = pltpu.roll(lf, 1, 0)
        place = row % 4
        return jnp.where(place == 0, below, jnp.where(place == 1, 0.0, jnp.where(place == 2, lf, lf + above)))
    blocks = b.reshape(c // (2 * h), 2 * h, dk)
    mid = jnp.broadcast_to(blocks[:, h - 1:h, :], blocks.shape).reshape(c, dk)
    return jnp.where(odd, b - mid, mid - b)


def _split3(x):
    p1 = x.astype(BF16)
    r1 = x - p1.astype(F32)
    p2 = r1.astype(BF16)
    p3 = (r1 - p2.astype(F32)).astype(BF16)
    return p1, p2, p3


def _dot_nt(a, b, **kw):
    return lax.dot_general(a, b, (((1,), (1,)), ((), ())), preferred_element_type=F32, **kw)


def _dot_tn(a, b, **kw):
    return lax.dot_general(a, b, (((0,), (0,)), ((), ())), preferred_element_type=F32, **kw)


def _hgrn_chunk_heads(inputs, nw, e_all, lvl, row):
    c = HGRN_CHUNK
    heads = range(len(inputs))
    qs = [q * jax.nn.sigmoid(q) for q, _, _, _, _, _ in inputs]
    fg = [lb + (1.0 - lb) * jax.nn.sigmoid(fl) for _, fl, _, _, _, lb in inputs]
    kk = [1.0 - f for f in fg]
    lf = [jnp.log2(f) for f in fg]
    parts = [_split3(x) for x in lf]
    b12 = [jnp.dot(e_all, jnp.concatenate([p1, p2], axis=1), preferred_element_type=F32) for p1, p2, _ in parts]
    b3 = [jnp.dot(e_all, p3, preferred_element_type=F32) for _, _, p3 in parts]
    b = [x[:, :HEAD_DIM] + x[:, HEAD_DIM:] + y for x, y in zip(b12, b3)]
    att = [jnp.where(lvl == len(HGRN_LEVELS), jnp.sum(qs[i] * kk[i], axis=-1, keepdims=True), 0.0) for i in heads]
    for n, h in enumerate(HGRN_LEVELS):
        odd = ((row // h) % 2) == 1
        xh = [(jnp.where(odd, qs[i], kk[i]) * jnp.exp2(_midpoint_log_decay(lf[i], b[i], h, row))).astype(BF16)
              for i in heads]
        pair = [_dot_nt(x, x) for x in xh]
        att = [jnp.where(lvl == n, pair[i], att[i]) for i in heads]
    b_end = [x[c - 1:c] for x in b]
    q_in = [(qs[i] * jnp.exp2(b[i])).astype(BF16) for i in heads]
    k_end = [(kk[i] * jnp.exp2(b_end[i] - b[i])).astype(BF16) for i in heads]
    v16 = [v.astype(BF16) for _, _, v, _, _, _ in inputs]
    o = [jnp.dot(att[i].astype(BF16), v16[i], preferred_element_type=F32)
         + _dot_nt(q_in[i], inputs[i][4].astype(BF16)) for i in heads]
    st_new = [inputs[i][4] * jnp.exp2(b_end[i]) + _dot_tn(v16[i], k_end[i]) for i in heads]
    o = [x * lax.rsqrt(jnp.mean(x * x, axis=-1, keepdims=True) + LN_EPS) * nw for x in o]
    out = [(o[i] * (inputs[i][3] * jax.nn.sigmoid(inputs[i][3]))).astype(BF16) for i in heads]
    return list(zip(out, st_new))


def _hgrn_kernel(q_ref, f_ref, i_ref, g_ref, lb_ref, nw_ref, e_ref, lvl_ref, o_ref, st_ref):
    c = HGRN_CHUNK
    n_chunks = q_ref.shape[0] // c
    st_ref[...] = jnp.zeros_like(st_ref)
    nw = nw_ref[...]
    row = lax.broadcasted_iota(jnp.int32, (c, 1), 0)

    def chunk(ci, carry):
        rows = pl.ds(pl.multiple_of(ci * c, c), c)
        e_all, lvl = e_ref[...], lvl_ref[...]
        inputs = []
        for h in range(A_HEADS):
            cols = slice(h * HEAD_DIM, (h + 1) * HEAD_DIM)
            inputs.append((q_ref[rows, cols], f_ref[rows, cols], i_ref[rows, cols], g_ref[rows, cols], st_ref[h],
                           lb_ref[:, cols]))
        for h, (o, st_new) in enumerate(_hgrn_chunk_heads(inputs, nw, e_all, lvl, row)):
            o_ref[rows, h * HEAD_DIM:(h + 1) * HEAD_DIM] = o
            st_ref[h] = st_new
        return carry

    lax.fori_loop(0, n_chunks, chunk, 0)


def _hgrn2(proj3, lb, norm_w):
    bsz, seq, _ = proj3.shape
    e_all, lvl = _hgrn_constants()
    width = A_HEADS * HEAD_DIM
    col = lambda blk: pl.BlockSpec((None, seq, width), lambda b, blk=blk: (b, 0, blk // A_HEADS))
    const = lambda shape: pl.BlockSpec(shape, lambda b: (0,) * len(shape))
    return pl.pallas_call(
        _hgrn_kernel,
        grid=(bsz,),
        in_specs=[col(QA_BLK), col(FA_BLK), col(IA_BLK), col(GA_BLK),
                  const((1, width)), const((1, HEAD_DIM)), const(e_all.shape), const(lvl.shape)],
        out_specs=pl.BlockSpec((None, seq, width), lambda b: (b, 0, 0)),
        out_shape=jax.ShapeDtypeStruct((bsz, seq, width), BF16),
        scratch_shapes=[pltpu.VMEM((A_HEADS, HEAD_DIM, HEAD_DIM), F32)],
        compiler_params=_params(("parallel",)),
        name="hgrn2",
    )(proj3, proj3, proj3, proj3, lb.reshape(1, width), norm_w.reshape(1, HEAD_DIM), e_all, lvl)


def _band_masks(max_lag):
    r = lax.broadcasted_iota(jnp.int32, (ATT_BLOCK, ATT_BLOCK), 0)
    col = lax.broadcasted_iota(jnp.int32, (ATT_BLOCK, ATT_BLOCK), 1)
    return col <= r, (ATT_BLOCK + r - col) <= max_lag


def _tile_softmax_many(blocks):
    scale = HEAD_DIM ** -0.5
    s_all = [[jnp.where(mask, _dot_nt(q16, k.astype(BF16)) * scale, NEG_BIG) for k, mask in zip(ks, masks)]
             for q16, ks, _, masks, _ in blocks]
    m_all = []
    for s, (_, _, _, _, m_floor) in zip(s_all, blocks):
        m = jnp.max(functools.reduce(jnp.maximum, s), axis=-1, keepdims=True)
        m_all.append(m if m_floor is None else jnp.maximum(m, m_floor))
    e_all = [[jnp.exp(si - m) for si in s] for s, m in zip(s_all, m_all)]
    acc_all = [functools.reduce(jnp.add, [jnp.dot(ei.astype(BF16), v.astype(BF16), preferred_element_type=F32)
                                          for ei, v in zip(e, vs)])
               for e, (_, _, vs, _, _) in zip(e_all, blocks)]
    l_all = [jnp.sum(functools.reduce(jnp.add, e), axis=-1, keepdims=True) for e in e_all]
    return list(zip(m_all, l_all, acc_all))


DIL_UNROLL = 16


def _dilated_kernel(q_ref, k_ref, v_ref, c_ref, s_ref, rot_ref, o_ref, qr_ref, kr_ref, vr_ref, m_ref, l_ref, acc_ref):
    seq = q_ref.shape[0]
    blk = ATT_BLOCK
    widest = DILATIONS[-1]
    per_residue = seq // widest
    for r in range(widest):
        src = pl.ds(r, per_residue, stride=widest)
        dst = slice(r * per_residue, (r + 1) * per_residue)
        c, s, rot = c_ref[dst, :], s_ref[dst, :], rot_ref[...]
        qr_ref[dst, :] = _rope(q_ref[src, :], c, s, rot)
        kr_ref[dst, :] = _rope(k_ref[src, :], c, s, rot)
        vr_ref[dst, :] = v_ref[src, :]

    def chunk_rows(dil):
        return blk * dil // widest

    def block_masks(dil):
        def position(i):
            return (widest // dil) * (i % chunk_rows(dil)) + i // chunk_rows(dil)
        row = position(lax.broadcasted_iota(jnp.int32, (blk, blk), 0))
        col = position(lax.broadcasted_iota(jnp.int32, (blk, blk), 1))
        return col <= row, col >= row

    def chunks_of(dil, r, n):
        size = chunk_rows(dil)
        return [pl.ds(pl.multiple_of((r + dil * a) * per_residue + size * n, size), size)
                for a in range(widest // dil)]

    def gather(ref, chunks):
        return jnp.concatenate([ref[rows, :] for rows in chunks], axis=0)

    def scatter(ref, chunks, value):
        size = value.shape[0] // len(chunks)
        for a, rows in enumerate(chunks):
            ref[rows, :] = value[a * size:(a + 1) * size]

    def run_group(specs, masks_of_pattern, first):
        cur_mask, prev_mask = masks_of_pattern
        loaded = []
        for rows, prev, prev_ok in specs:
            ks, vs, masks = [gather(kr_ref, rows)], [gather(vr_ref, rows)], [cur_mask]
            if prev is not None:
                ks.append(gather(kr_ref, prev))
                vs.append(gather(vr_ref, prev))
                masks.append(jnp.logical_and(prev_mask, prev_ok))
            state = None if first else (gather(m_ref, rows), gather(l_ref, rows), gather(acc_ref, rows))
            loaded.append((gather(qr_ref, rows).astype(BF16), ks, vs, masks, state))
        pieces = _tile_softmax_many([(q16, ks, vs, masks, None if state is None else state[0])
                                     for q16, ks, vs, masks, state in loaded])
        results = []
        for (m, l, acc), (_, _, _, _, state) in zip(pieces, loaded):
            if state is None:
                shape = (blk, HEAD_DIM)
                results.append((jnp.broadcast_to(m, shape), jnp.broadcast_to(l, shape), acc))
            else:
                m_old, l_old, acc_old = state
                alpha = jnp.exp(m_old - m)
                results.append((m, alpha * l_old + l, alpha * acc_old + acc))
        for (rows, _, _), (m, l, acc) in zip(specs, results):
            scatter(m_ref, rows, m)
            scatter(l_ref, rows, l)
            scatter(acc_ref, rows, acc)

    n_query_blocks = seq // blk
    for dil in DILATIONS:
        n_blocks = n_query_blocks // dil
        masks_of_pattern = block_masks(dil)

        def group(g, carry, dil=dil, n_blocks=n_blocks, masks_of_pattern=masks_of_pattern):
            specs = []
            for u in range(DIL_UNROLL):
                e = g * DIL_UNROLL + u
                r, n = e % dil, e // dil
                if n_blocks == 1:
                    specs.append((chunks_of(dil, r, n), None, None))
                else:
                    specs.append((chunks_of(dil, r, n), chunks_of(dil, r, jnp.maximum(n - 1, 0)), n > 0))
            run_group(specs, masks_of_pattern, first=(dil == DILATIONS[0]))
            return carry

        lax.fori_loop(0, n_query_blocks // DIL_UNROLL, group, 0)

    for r in range(widest):
        rows = slice(r * per_residue, (r + 1) * per_residue)
        qr_ref[pl.ds(r, per_residue, stride=widest), :] = acc_ref[rows, :] / l_ref[rows, :]
    o_ref[...] = qr_ref[...].astype(o_ref.dtype)


def _residue_major(table, widest):
    seq, width = table.shape
    return table.reshape(seq // widest, widest, width).transpose(1, 0, 2).reshape(seq, width)


def _dilated(proj3, tables):
    bsz, seq, _ = proj3.shape
    col = lambda blk: pl.BlockSpec((None, seq, HEAD_DIM), lambda b, h, blk=blk: (b, 0, blk + h))
    tab = pl.BlockSpec((seq, HEAD_DIM), lambda b, h: (0, 0))
    return pl.pallas_call(
        _dilated_kernel,
        grid=(bsz, B_HEADS),
        in_specs=[col(QB_BLK), col(KB_BLK), col(VB_BLK), tab, tab,
                  pl.BlockSpec((HEAD_DIM, HEAD_DIM), lambda b, h: (0, 0))],
        out_specs=pl.BlockSpec((None, seq, HEAD_DIM), lambda b, h: (b, 0, h)),
        out_shape=jax.ShapeDtypeStruct((bsz, seq, B_HEADS * HEAD_DIM), BF16),
        scratch_shapes=[pltpu.VMEM((seq, HEAD_DIM), F32)] * 6,
        compiler_params=_params(("parallel", "parallel")),
        name="dilated_attention",
    )(proj3, proj3, proj3, _residue_major(tables[0], DILATIONS[-1]), _residue_major(tables[1], DILATIONS[-1]),
      tables[2])


SWA_UNROLL = 4


def _swa_kernel(sink_ref, q0_ref, q1_ref, q2_ref, k_ref, v_ref, c_ref, s_ref, rot_ref, o_ref, qr_ref, kr_ref):
    seq = k_ref.shape[0]
    blk = ATT_BLOCK
    grp = pl.program_id(1)
    c, s, rot = c_ref[...], s_ref[...], rot_ref[...]
    for j, q_ref in enumerate((q0_ref, q1_ref, q2_ref)):
        qr_ref[j] = _rope(q_ref[...], c, s, rot).astype(BF16)
    kr_ref[...] = _rope(k_ref[...], c, s, rot).astype(BF16)
    cur_mask, prev_mask = _band_masks(C_MAX_LAG)
    sinks = [jnp.full((blk, 1), sink_ref[grp * C_REP + j], F32) for j in range(C_REP)]

    def group(g, carry):
        blocks, dests = [], []
        for u in range(SWA_UNROLL):
            n = g * SWA_UNROLL + u
            rows = pl.ds(pl.multiple_of(n * blk, blk), blk)
            prev = pl.ds(pl.multiple_of(jnp.maximum(n - 1, 0) * blk, blk), blk)
            ks, vs = [kr_ref[rows, :], kr_ref[prev, :]], [v_ref[rows, :], v_ref[prev, :]]
            masks = [cur_mask, jnp.logical_and(prev_mask, n > 0)]
            for j in range(C_REP):
                blocks.append((qr_ref[j, rows, :], ks, vs, masks, sinks[j]))
                dests.append((rows, j))
        for (rows, j), (m, l, acc) in zip(dests, _tile_softmax_many(blocks)):
            out = acc / (l + jnp.exp(sinks[j] - m))
            o_ref[rows, j * HEAD_DIM:(j + 1) * HEAD_DIM] = out.astype(o_ref.dtype)
        return carry

    lax.fori_loop(0, seq // (blk * SWA_UNROLL), group, 0)


def _swa(proj3, sinks, tables):
    bsz, seq, _ = proj3.shape
    qcol = lambda j: pl.BlockSpec((None, seq, HEAD_DIM), lambda b, g, j=j: (b, 0, QC_BLK + g * C_REP + j))
    col = lambda blk: pl.BlockSpec((None, seq, HEAD_DIM), lambda b, g, blk=blk: (b, 0, blk + g))
    tab = pl.BlockSpec((seq, HEAD_DIM), lambda b, g: (0, 0))
    return pl.pallas_call(
        _swa_kernel,
        grid=(bsz, C_KV_HEADS),
        in_specs=[pl.BlockSpec(memory_space=pltpu.SMEM), qcol(0), qcol(1), qcol(2), col(KC_BLK), col(VC_BLK),
                  tab, tab, pl.BlockSpec((HEAD_DIM, HEAD_DIM), lambda b, g: (0, 0))],
        out_specs=pl.BlockSpec((None, seq, C_REP * HEAD_DIM), lambda b, g: (b, 0, g)),
        out_shape=jax.ShapeDtypeStruct((bsz, seq, C_HEADS * HEAD_DIM), BF16),
        scratch_shapes=[pltpu.VMEM((C_REP, seq, HEAD_DIM), BF16), pltpu.VMEM((seq, HEAD_DIM), BF16)],
        compiler_params=_params(("parallel", "parallel")),
        name="swa_sink_attention",
    )(sinks, proj3, proj3, proj3, proj3, proj3, *tables)


def _layer_norm_store(z, g_ref, b_ref, o32_ref, o16_ref):
    mu = jnp.mean(z, axis=-1, keepdims=True)
    zc = z - mu
    var = jnp.mean(zc * zc, axis=-1, keepdims=True)
    y = zc * lax.rsqrt(var + LN_EPS) * g_ref[...] + b_ref[...]
    o32_ref[...] = y
    o16_ref[...] = y.astype(BF16)


def _stage_weight_bf16(w_hbm, layer, w16_ref, stage_ref, sem):
    chunk = stage_ref.shape[1]
    n_chunks = w16_ref.shape[0] // chunk

    def copy(c):
        slot = c % 2
        return pltpu.make_async_copy(w_hbm.at[layer, pl.ds(c * chunk, chunk), :], stage_ref.at[slot], sem.at[slot])

    copy(0).start()
    for c in range(n_chunks):
        if c + 1 < n_chunks:
            copy(c + 1).start()
        copy(c).wait()
        w16_ref[c * chunk:(c + 1) * chunk, :] = stage_ref[c % 2].astype(BF16)


WEIGHT_STAGE_ROWS = 256


def _resident_weight_scratch(k, n):
    return [pltpu.VMEM((k, n), BF16), pltpu.VMEM((2, WEIGHT_STAGE_ROWS, n), F32), pltpu.SemaphoreType.DMA((2,))]


def _out_proj_kernel(alpha, layer, oa_ref, ob_ref, oc_ref, w_hbm, x_ref, g_ref, b_ref, o32_ref, o16_ref,
                     w16_ref, stage_ref, sem):
    @pl.when(pl.program_id(0) == 0)
    def _():
        _stage_weight_bf16(w_hbm, layer, w16_ref, stage_ref, sem)

    ka, kb = oa_ref.shape[1], ob_ref.shape[1]
    y = (jnp.dot(oa_ref[...], w16_ref[0:ka, :], preferred_element_type=F32)
         + jnp.dot(ob_ref[...], w16_ref[ka:ka + kb, :], preferred_element_type=F32)
         + jnp.dot(oc_ref[...], w16_ref[ka + kb:, :], preferred_element_type=F32))
    _layer_norm_store(alpha * x_ref[...] + y, g_ref, b_ref, o32_ref, o16_ref)


def _layer_vec(layer, d):
    return pl.BlockSpec((None, 1, d), lambda i, layer=layer: (layer, 0, 0))


def _out_proj_ln(oa, ob, oc, w_out, layer, x, g, b, alpha, bm):
    m, d = x.shape
    rows = lambda a: pl.BlockSpec((bm, a.shape[1]), lambda i: (i, 0))
    return pl.pallas_call(
        functools.partial(_out_proj_kernel, alpha, layer),
        grid=(m // bm,),
        in_specs=[rows(oa), rows(ob), rows(oc), pl.BlockSpec(memory_space=pl.ANY), rows(x),
                  _layer_vec(layer, d), _layer_vec(layer, d)],
        out_specs=[rows(x), rows(x)],
        out_shape=[jax.ShapeDtypeStruct((m, d), F32), jax.ShapeDtypeStruct((m, d), BF16)],
        scratch_shapes=_resident_weight_scratch(w_out.shape[1], d),
        compiler_params=_params(("arbitrary",)),
        name="out_proj_ln",
    )(oa, ob, oc, w_out, x, g, b)


def _in_proj_kernel(x_ref, w_ref, o_ref):
    o_ref[...] = jnp.dot(x_ref[...], w_ref[...].astype(BF16), preferred_element_type=F32).astype(o_ref.dtype)


def _in_proj(x16, w_in, layer, bm, bn):
    m, k = x16.shape
    n = w_in.shape[2]
    return pl.pallas_call(
        _in_proj_kernel,
        grid=(m // bm, n // bn),
        in_specs=[pl.BlockSpec((bm, k), lambda i, j: (i, 0)),
                  pl.BlockSpec((None, k, bn), lambda i, j: (layer, 0, j))],
        out_specs=pl.BlockSpec((bm, bn), lambda i, j: (i, j)),
        out_shape=jax.ShapeDtypeStruct((m, n), F32),
        compiler_params=_params(("parallel", "parallel")),
        name="in_proj",
    )(x16, w_in)


def _ffn_up_kernel(tiles_per_seq, x_ref, wg_ref, wu_ref, cw_ref, cb_ref, h_ref, g_ref, wg16_ref, wu16_ref):
    bm = x_ref.shape[0]
    halo = g_ref.shape[0] - bm

    @pl.when(pl.program_id(1) == 0)
    def _():
        wg16_ref[...] = wg_ref[...].astype(BF16)
        wu16_ref[...] = wu_ref[...].astype(BF16)

    @pl.when(pl.program_id(1) % tiles_per_seq == 0)
    def _():
        g_ref[0:halo, :] = jnp.zeros((halo, g_ref.shape[1]), F32)

    x = x_ref[...]
    g_ref[halo:, :] = jnp.dot(x, wg16_ref[...], preferred_element_type=F32)
    u = jnp.dot(x, wu16_ref[...], preferred_element_type=F32)
    gc = cb_ref[...]
    for j in range(CONV_WIDTH):
        lag = CONV_WIDTH - 1 - j
        gc = gc + cw_ref[j:j + 1, :] * g_ref[halo - lag:halo - lag + bm, :]
    h_ref[...] = (gc * jax.nn.sigmoid(gc) * u).astype(h_ref.dtype)
    g_ref[0:halo, :] = g_ref[bm:bm + halo, :]


def _ffn_up(x16, w_gate, w_up, conv_w, conv_b, layer, seq, bm, bn):
    m, d = x16.shape
    f = w_gate.shape[2]
    halo = 8
    wspec = pl.BlockSpec((None, d, bn), lambda j, i: (layer, 0, j))
    return pl.pallas_call(
        functools.partial(_ffn_up_kernel, seq // bm),
        grid=(f // bn, m // bm),
        in_specs=[pl.BlockSpec((bm, d), lambda j, i: (i, 0)), wspec, wspec,
                  pl.BlockSpec((None, CONV_WIDTH, bn), lambda j, i: (layer, 0, j)),
                  pl.BlockSpec((None, 1, bn), lambda j, i: (layer, 0, j))],
        out_specs=pl.BlockSpec((bm, bn), lambda j, i: (i, j)),
        out_shape=jax.ShapeDtypeStruct((m, f), BF16),
        scratch_shapes=[pltpu.VMEM((bm + halo, bn), F32), pltpu.VMEM((d, bn), BF16), pltpu.VMEM((d, bn), BF16)],
        compiler_params=_params(("parallel", "arbitrary")),
        name="ffn_up_conv_gate",
    )(x16, w_gate, w_up, conv_w, conv_b)


def _ffn_down_kernel(alpha, layer, h_ref, w_hbm, x_ref, g_ref, b_ref, o32_ref, o16_ref, w16_ref, stage_ref, sem):
    @pl.when(pl.program_id(0) == 0)
    def _():
        _stage_weight_bf16(w_hbm, layer, w16_ref, stage_ref, sem)

    y = jnp.dot(h_ref[...], w16_ref[...], preferred_element_type=F32)
    _layer_norm_store(alpha * x_ref[...] + y, g_ref, b_ref, o32_ref, o16_ref)


def _ffn_down_ln(h, w_down, layer, x, g, b, alpha, bm):
    m, d = x.shape
    f = h.shape[1]
    rows = pl.BlockSpec((bm, d), lambda i: (i, 0))
    return pl.pallas_call(
        functools.partial(_ffn_down_kernel, alpha, layer),
        grid=(m // bm,),
        in_specs=[pl.BlockSpec((bm, f), lambda i: (i, 0)), pl.BlockSpec(memory_space=pl.ANY), rows,
                  _layer_vec(layer, d), _layer_vec(layer, d)],
        out_specs=[rows, rows],
        out_shape=[jax.ShapeDtypeStruct((m, d), F32), jax.ShapeDtypeStruct((m, d), BF16)],
        scratch_shapes=_resident_weight_scratch(f, d),
        compiler_params=_params(("arbitrary",)),
        name="ffn_down_ln",
    )(h, w_down, x, g, b)


def kernel(x, w_in, lb_logits, a_norm_w, c_sinks, w_out, ln1_g, ln1_b, w_gate, w_up, conv_w, conv_b, w_down,
           ln2_g, ln2_b):
    bsz, seq, d = x.shape
    depth = w_in.shape[0]
    alpha = (2 * depth) ** 0.25
    tables = _rope_tables(seq)
    lbs = jnp.cumsum(jax.nn.softmax(lb_logits.astype(F32), axis=0), axis=0)
    lbs = lbs - lbs[0]
    vec3 = lambda a: a.reshape(depth, 1, -1)
    ln1_g, ln1_b, ln2_g, ln2_b, conv_b = vec3(ln1_g), vec3(ln1_b), vec3(ln2_g), vec3(ln2_b), vec3(conv_b)
    x32 = x.reshape(bsz * seq, d)
    x16 = x32.astype(BF16)
    for l in range(depth):
        proj3 = _in_proj(x16, w_in, l, bm=2048, bn=512).reshape(bsz, seq, -1)
        oa = _hgrn2(proj3, lbs[l], a_norm_w[l]).reshape(bsz * seq, -1)
        ob = _dilated(proj3, tables).reshape(bsz * seq, -1)
        oc = _swa(proj3, c_sinks[l], tables).reshape(bsz * seq, -1)
        x32, x16 = _out_proj_ln(oa, ob, oc, w_out, l, x32, ln1_g, ln1_b, alpha, bm=512)
        h = _ffn_up(x16, w_gate, w_up, conv_w, conv_b, l, seq, bm=1024, bn=512)
        x32, x16 = _ffn_down_ln(h, w_down, l, x32, ln2_g, ln2_b, alpha, bm=256)
    return x32.reshape(bsz, seq, d)
```

```python
import functools

import numpy as np
import jax
import jax.numpy as jnp
from jax import lax
from jax.experimental import pallas as pl
from jax.experimental.pallas import tpu as pltpu

HEAD_DIM = 128
A_HEADS = 4
B_HEADS = 6
C_HEADS = 6
C_KV_HEADS = 2
C_REP = C_HEADS // C_KV_HEADS
DILATIONS = (1, 4, 16)
ATT_BLOCK = 128
C_MAX_LAG = 127
ROPE_THETA = 500000.0
ROPE_DIM = HEAD_DIM // 4
ROPE_HALF = ROPE_DIM // 2
CONV_WIDTH = 3
LN_EPS = 1e-5

QA_BLK, FA_BLK, IA_BLK, GA_BLK = 0, 4, 8, 12
QB_BLK, KB_BLK, VB_BLK = 16, 22, 28
QC_BLK, KC_BLK, VC_BLK = 34, 40, 42

HGRN_CHUNK = 128
HGRN_LEVELS = (64, 32, 16, 8, 4, 2, 1)

VMEM_LIMIT_BYTES = 56 * 1024 * 1024

F32 = jnp.float32
BF16 = jnp.bfloat16
NEG_BIG = -1e30
LOG2_E = 1.4426950408889634


def _params(semantics):
    return pltpu.CompilerParams(dimension_semantics=semantics, vmem_limit_bytes=VMEM_LIMIT_BYTES)


def _rope_tables(seq):
    inv = ROPE_THETA ** (-jnp.arange(0, ROPE_DIM, 2, dtype=F32) / ROPE_DIM)
    ang = jnp.arange(seq, dtype=F32)[:, None] * inv[None, :]
    cos, sin = jnp.cos(ang), jnp.sin(ang)
    rest = HEAD_DIM - ROPE_DIM
    c = jnp.concatenate([cos, cos, jnp.ones((seq, rest), F32)], -1)
    s = jnp.concatenate([sin, sin, jnp.zeros((seq, rest), F32)], -1)
    rot = np.zeros((HEAD_DIM, HEAD_DIM), np.float32)
    lane = np.arange(ROPE_HALF)
    rot[lane + ROPE_HALF, lane] = -1.0
    rot[lane, lane + ROPE_HALF] = 1.0
    return c, s, jnp.asarray(rot, BF16)


def _rope(x, c, s, rot):
    hi = x.astype(BF16)
    lo = (x - hi.astype(F32)).astype(BF16)
    swapped = jnp.dot(hi, rot, preferred_element_type=F32) + jnp.dot(lo, rot, preferred_element_type=F32)
    return x * c + swapped * s


def _hgrn_constants():
    c = HGRN_CHUNK
    i = np.arange(c)[:, None]
    j = np.arange(c)[None, :]
    lvl = np.full((c, c), -1, np.int32)
    for n, h in enumerate(HGRN_LEVELS):
        lvl[(i // (2 * h) == j // (2 * h)) & (i // h != j // h) & (i > j)] = n
    lvl[np.arange(c), np.arange(c)] = len(HGRN_LEVELS)
    return jnp.asarray((j <= i).astype(np.float32), BF16), jnp.asarray(lvl)


def _midpoint_log_decay(lf, b, h, row):
    c, dk = b.shape
    odd = ((row // h) % 2) == 1
    if h == 1:
        return jnp.where(odd, lf, 0.0)
    if h == 2:
        below = pltpu.roll(lf, c - 1, 0)
        above = pltpu.roll(lf, 1, 0)
        place = row % 4
        return jnp.where(place == 0, below, jnp.where(place == 1, 0.0, jnp.where(place == 2, lf, lf + above)))
    blocks = b.reshape(c // (2 * h), 2 * h, dk)
    mid = jnp.broadcast_to(blocks[:, h - 1:h, :], blocks.shape).reshape(c, dk)
    return jnp.where(odd, b - mid, mid - b)


def _split3(x):
    p1 = x.astype(BF16)
    r1 = x - p1.astype(F32)
    p2 = r1.astype(BF16)
    p3 = (r1 - p2.astype(F32)).astype(BF16)
    return p1, p2, p3


def _dot_nt(a, b, **kw):
    return lax.dot_general(a, b, (((1,), (1,)), ((), ())), preferred_element_type=F32, **kw)


def _dot_tn(a, b, **kw):
    return lax.dot_general(a, b, (((0,), (0,)), ((), ())), preferred_element_type=F32, **kw)


def _hgrn_chunk_heads(inputs, nw, e_all, lvl, row):
    c = HGRN_CHUNK
    heads = range(len(inputs))
    qs = [q * jax.nn.sigmoid(q) for q, _, _, _, _, _ in inputs]
    fg = [lb + (1.0 - lb) * jax.nn.sigmoid(fl) for _, fl, _, _, _, lb in inputs]
    kk = [1.0 - f for f in fg]
    lf = [jnp.log2(f) for f in fg]
    parts = [_split3(x) for x in lf]
    b12 = [jnp.dot(e_all, jnp.concatenate([p1, p2], axis=1), preferred_element_type=F32) for p1, p2, _ in parts]
    b3 = [jnp.dot(e_all, p3, preferred_element_type=F32) for _, _, p3 in parts]
    b = [x[:, :HEAD_DIM] + x[:, HEAD_DIM:] + y for x, y in zip(b12, b3)]
    att = [jnp.where(lvl == len(HGRN_LEVELS), jnp.sum(qs[i] * kk[i], axis=-1, keepdims=True), 0.0) for i in heads]
    for n, h in enumerate(HGRN_LEVELS):
        odd = ((row // h) % 2) == 1
        xh = [(jnp.where(odd, qs[i], kk[i]) * jnp.exp2(_midpoint_log_decay(lf[i], b[i], h, row))).astype(BF16)
              for i in heads]
        pair = [_dot_nt(x, x) for x in xh]
        att = [jnp.where(lvl == n, pair[i], att[i]) for i in heads]
    b_end = [x[c - 1:c] for x in b]
    q_in = [(qs[i] * jnp.exp2(b[i])).astype(BF16) for i in heads]
    k_end = [(kk[i] * jnp.exp2(b_end[i] - b[i])).astype(BF16) for i in heads]
    v16 = [v.astype(BF16) for _, _, v, _, _, _ in inputs]
    o = [jnp.dot(att[i].astype(BF16), v16[i], preferred_element_type=F32)
         + _dot_nt(q_in[i], inputs[i][4].astype(BF16)) for i in heads]
    st_new = [inputs[i][4] * jnp.exp2(b_end[i]) + _dot_tn(v16[i], k_end[i]) for i in heads]
    o = [x * lax.rsqrt(jnp.mean(x * x, axis=-1, keepdims=True) + LN_EPS) * nw for x in o]
    out = [(o[i] * (inputs[i][3] * jax.nn.sigmoid(inputs[i][3]))).astype(BF16) for i in heads]
    return list(zip(out, st_new))


HGRN_BATCH = 2
HGRN_SEQ_TILE = 1024


def _hgrn_kernel(q_ref, f_ref, i_ref, g_ref, lb_ref, nw_ref, e_ref, lvl_ref, o_ref, st_ref):
    c = HGRN_CHUNK
    n_batch, tile_rows = q_ref.shape[0], q_ref.shape[1]

    @pl.when(pl.program_id(1) == 0)
    def _():
        st_ref[...] = jnp.zeros_like(st_ref)

    nw = nw_ref[...]
    row = lax.broadcasted_iota(jnp.int32, (c, 1), 0)

    def chunk(ci, carry):
        rows = pl.ds(pl.multiple_of(ci * c, c), c)
        e_all, lvl = e_ref[...], lvl_ref[...]
        inputs = []
        for n in range(n_batch):
            for h in range(A_HEADS):
                cols = slice(h * HEAD_DIM, (h + 1) * HEAD_DIM)
                inputs.append((q_ref[n, rows, cols], f_ref[n, rows, cols], i_ref[n, rows, cols], g_ref[n, rows, cols],
                               st_ref[n * A_HEADS + h], lb_ref[:, cols]))
        for k, (o, st_new) in enumerate(_hgrn_chunk_heads(inputs, nw, e_all, lvl, row)):
            n, h = divmod(k, A_HEADS)
            o_ref[n, rows, h * HEAD_DIM:(h + 1) * HEAD_DIM] = o
            st_ref[k] = st_new
        return carry

    lax.fori_loop(0, tile_rows // c, chunk, 0)


def _hgrn2(proj3, lb, norm_w):
    bsz, seq, _ = proj3.shape
    e_all, lvl = _hgrn_constants()
    width = A_HEADS * HEAD_DIM
    col = lambda blk: pl.BlockSpec((HGRN_BATCH, HGRN_SEQ_TILE, width), lambda b, s, blk=blk: (b, s, blk // A_HEADS))
    const = lambda shape: pl.BlockSpec(shape, lambda b, s: (0,) * len(shape))
    return pl.pallas_call(
        _hgrn_kernel,
        grid=(bsz // HGRN_BATCH, seq // HGRN_SEQ_TILE),
        in_specs=[col(QA_BLK), col(FA_BLK), col(IA_BLK), col(GA_BLK),
                  const((1, width)), const((1, HEAD_DIM)), const(e_all.shape), const(lvl.shape)],
        out_specs=pl.BlockSpec((HGRN_BATCH, HGRN_SEQ_TILE, width), lambda b, s: (b, s, 0)),
        out_shape=jax.ShapeDtypeStruct((bsz, seq, width), BF16),
        scratch_shapes=[pltpu.VMEM((HGRN_BATCH * A_HEADS, HEAD_DIM, HEAD_DIM), F32)],
        compiler_params=_params(("parallel", "arbitrary")),
        name="hgrn2",
    )(proj3, proj3, proj3, proj3, lb.reshape(1, width), norm_w.reshape(1, HEAD_DIM), e_all, lvl)


def _band_masks(max_lag):
    r = lax.broadcasted_iota(jnp.int32, (ATT_BLOCK, ATT_BLOCK), 0)
    col = lax.broadcasted_iota(jnp.int32, (ATT_BLOCK, ATT_BLOCK), 1)
    return col <= r, (ATT_BLOCK + r - col) <= max_lag


def _tile_softmax_many(blocks):
    scale = HEAD_DIM ** -0.5 * LOG2_E
    s_all = [[jnp.where(mask, _dot_nt(q16, k.astype(BF16)) * scale, NEG_BIG) for k, mask in zip(ks, masks)]
             for q16, ks, _, masks, _ in blocks]
    m_all = []
    for s, (_, _, _, _, m_floor) in zip(s_all, blocks):
        m = jnp.max(functools.reduce(jnp.maximum, s), axis=-1, keepdims=True)
        m_all.append(m if m_floor is None else jnp.maximum(m, m_floor))
    e_all = [[jnp.exp2(si - m) for si in s] for s, m in zip(s_all, m_all)]
    acc_all = [functools.reduce(jnp.add, [jnp.dot(ei.astype(BF16), v.astype(BF16), preferred_element_type=F32)
                                          for ei, v in zip(e, vs)])
               for e, (_, _, vs, _, _) in zip(e_all, blocks)]
    l_all = [jnp.sum(functools.reduce(jnp.add, e), axis=-1, keepdims=True) for e in e_all]
    return list(zip(m_all, l_all, acc_all))


DIL_UNROLL = 16


def _dilated_kernel(q_ref, k_ref, v_ref, c_ref, s_ref, rot_ref, o_ref, qr_ref, kr_ref, vr_ref, m_ref, l_ref, acc_ref):
    seq = q_ref.shape[0]
    blk = ATT_BLOCK
    widest = DILATIONS[-1]
    per_residue = seq // widest
    for r in range(widest):
        src = pl.ds(r, per_residue, stride=widest)
        dst = slice(r * per_residue, (r + 1) * per_residue)
        c, s, rot = c_ref[dst, :], s_ref[dst, :], rot_ref[...]
        qr_ref[dst, :] = _rope(q_ref[src, :], c, s, rot)
        kr_ref[dst, :] = _rope(k_ref[src, :], c, s, rot)
        vr_ref[dst, :] = v_ref[src, :]

    def chunk_rows(dil):
        return blk * dil // widest

    def block_masks(dil):
        def position(i):
            return (widest // dil) * (i % chunk_rows(dil)) + i // chunk_rows(dil)
        row = position(lax.broadcasted_iota(jnp.int32, (blk, blk), 0))
        col = position(lax.broadcasted_iota(jnp.int32, (blk, blk), 1))
        return col <= row, col >= row

    def chunks_of(dil, r, n):
        size = chunk_rows(dil)
        return [pl.ds(pl.multiple_of((r + dil * a) * per_residue + size * n, size), size)
                for a in range(widest // dil)]

    def gather(ref, chunks):
        return jnp.concatenate([ref[rows, :] for rows in chunks], axis=0)

    def scatter(ref, chunks, value):
        size = value.shape[0] // len(chunks)
        for a, rows in enumerate(chunks):
            ref[rows, :] = value[a * size:(a + 1) * size]

    def run_group(specs, masks_of_pattern, first):
        cur_mask, prev_mask = masks_of_pattern
        loaded = []
        for rows, prev, prev_ok in specs:
            ks, vs, masks = [gather(kr_ref, rows)], [gather(vr_ref, rows)], [cur_mask]
            if prev is not None:
                ks.append(gather(kr_ref, prev))
                vs.append(gather(vr_ref, prev))
                masks.append(jnp.logical_and(prev_mask, prev_ok))
            state = None if first else (gather(m_ref, rows), gather(l_ref, rows), gather(acc_ref, rows))
            loaded.append((gather(qr_ref, rows).astype(BF16), ks, vs, masks, state))
        pieces = _tile_softmax_many([(q16, ks, vs, masks, None if state is None else state[0])
                                     for q16, ks, vs, masks, state in loaded])
        results = []
        for (m, l, acc), (_, _, _, _, state) in zip(pieces, loaded):
            if state is None:
                shape = (blk, HEAD_DIM)
                results.append((jnp.broadcast_to(m, shape), jnp.broadcast_to(l, shape), acc))
            else:
                m_old, l_old, acc_old = state
                alpha = jnp.exp2(m_old - m)
                results.append((m, alpha * l_old + l, alpha * acc_old + acc))
        for (rows, _, _), (m, l, acc) in zip(specs, results):
            scatter(m_ref, rows, m)
            scatter(l_ref, rows, l)
            scatter(acc_ref, rows, acc)

    n_query_blocks = seq // blk
    for dil in DILATIONS:
        n_blocks = n_query_blocks // dil
        masks_of_pattern = block_masks(dil)

        def group(g, carry, dil=dil, n_blocks=n_blocks, masks_of_pattern=masks_of_pattern):
            specs = []
            for u in range(DIL_UNROLL):
                e = g * DIL_UNROLL + u
                r, n = e % dil, e // dil
                if n_blocks == 1:
                    specs.append((chunks_of(dil, r, n), None, None))
                else:
                    specs.append((chunks_of(dil, r, n), chunks_of(dil, r, jnp.maximum(n - 1, 0)), n > 0))
            run_group(specs, masks_of_pattern, first=(dil == DILATIONS[0]))
            return carry

        lax.fori_loop(0, n_query_blocks // DIL_UNROLL, group, 0)

    for r in range(widest):
        rows = slice(r * per_residue, (r + 1) * per_residue)
        qr_ref[pl.ds(r, per_residue, stride=widest), :] = acc_ref[rows, :] / l_ref[rows, :]
    o_ref[...] = qr_ref[...].astype(o_ref.dtype)


def _residue_major(table, widest):
    seq, width = table.shape
    return table.reshape(seq // widest, widest, width).transpose(1, 0, 2).reshape(seq, width)


def _dilated(proj3, tables):
    bsz, seq, _ = proj3.shape
    col = lambda blk: pl.BlockSpec((None, seq, HEAD_DIM), lambda b, h, blk=blk: (b, 0, blk + h))
    tab = pl.BlockSpec((seq, HEAD_DIM), lambda b, h: (0, 0))
    return pl.pallas_call(
        _dilated_kernel,
        grid=(bsz, B_HEADS),
        in_specs=[col(QB_BLK), col(KB_BLK), col(VB_BLK), tab, tab,
                  pl.BlockSpec((HEAD_DIM, HEAD_DIM), lambda b, h: (0, 0))],
        out_specs=pl.BlockSpec((None, seq, HEAD_DIM), lambda b, h: (b, 0, h)),
        out_shape=jax.ShapeDtypeStruct((bsz, seq, B_HEADS * HEAD_DIM), BF16),
        scratch_shapes=[pltpu.VMEM((seq, HEAD_DIM), F32)] * 6,
        compiler_params=_params(("parallel", "parallel")),
        name="dilated_attention",
    )(proj3, proj3, proj3, _residue_major(tables[0], DILATIONS[-1]), _residue_major(tables[1], DILATIONS[-1]),
      tables[2])


SWA_UNROLL = 4


def _swa_kernel(sink_ref, q0_ref, q1_ref, q2_ref, k_ref, v_ref, c_ref, s_ref, rot_ref, o_ref, qr_ref, kr_ref):
    seq = k_ref.shape[0]
    blk = ATT_BLOCK
    grp = pl.program_id(1)
    c, s, rot = c_ref[...], s_ref[...], rot_ref[...]
    for j, q_ref in enumerate((q0_ref, q1_ref, q2_ref)):
        qr_ref[j] = _rope(q_ref[...], c, s, rot).astype(BF16)
    kr_ref[...] = _rope(k_ref[...], c, s, rot).astype(BF16)
    cur_mask, prev_mask = _band_masks(C_MAX_LAG)
    sinks = [jnp.full((blk, 1), sink_ref[grp * C_REP + j] * LOG2_E, F32) for j in range(C_REP)]

    def group(g, carry):
        blocks, dests = [], []
        for u in range(SWA_UNROLL):
            n = g * SWA_UNROLL + u
            rows = pl.ds(pl.multiple_of(n * blk, blk), blk)
            prev = pl.ds(pl.multiple_of(jnp.maximum(n - 1, 0) * blk, blk), blk)
            ks, vs = [kr_ref[rows, :], kr_ref[prev, :]], [v_ref[rows, :], v_ref[prev, :]]
            masks = [cur_mask, jnp.logical_and(prev_mask, n > 0)]
            for j in range(C_REP):
                blocks.append((qr_ref[j, rows, :], ks, vs, masks, sinks[j]))
                dests.append((rows, j))
        for (rows, j), (m, l, acc) in zip(dests, _tile_softmax_many(blocks)):
            out = acc / (l + jnp.exp2(sinks[j] - m))
            o_ref[rows, j * HEAD_DIM:(j + 1) * HEAD_DIM] = out.astype(o_ref.dtype)
        return carry

    lax.fori_loop(0, seq // (blk * SWA_UNROLL), group, 0)


def _swa(proj3, sinks, tables):
    bsz, seq, _ = proj3.shape
    qcol = lambda j: pl.BlockSpec((None, seq, HEAD_DIM), lambda b, g, j=j: (b, 0, QC_BLK + g * C_REP + j))
    col = lambda blk: pl.BlockSpec((None, seq, HEAD_DIM), lambda b, g, blk=blk: (b, 0, blk + g))
    tab = pl.BlockSpec((seq, HEAD_DIM), lambda b, g: (0, 0))
    return pl.pallas_call(
        _swa_kernel,
        grid=(bsz, C_KV_HEADS),
        in_specs=[pl.BlockSpec(memory_space=pltpu.SMEM), qcol(0), qcol(1), qcol(2), col(KC_BLK), col(VC_BLK),
                  tab, tab, pl.BlockSpec((HEAD_DIM, HEAD_DIM), lambda b, g: (0, 0))],
        out_specs=pl.BlockSpec((None, seq, C_REP * HEAD_DIM), lambda b, g: (b, 0, g)),
        out_shape=jax.ShapeDtypeStruct((bsz, seq, C_HEADS * HEAD_DIM), BF16),
        scratch_shapes=[pltpu.VMEM((C_REP, seq, HEAD_DIM), BF16), pltpu.VMEM((seq, HEAD_DIM), BF16)],
        compiler_params=_params(("parallel", "parallel")),
        name="swa_sink_attention",
    )(sinks, proj3, proj3, proj3, proj3, proj3, *tables)


def _layer_norm_store(z, g_ref, b_ref, o32_ref, o16_ref):
    mu = jnp.mean(z, axis=-1, keepdims=True)
    zc = z - mu
    var = jnp.mean(zc * zc, axis=-1, keepdims=True)
    y = zc * lax.rsqrt(var + LN_EPS) * g_ref[...] + b_ref[...]
    o32_ref[...] = y
    o16_ref[...] = y.astype(BF16)


def _stage_weight_bf16(w_hbm, layer, w16_ref, stage_ref, sem):
    chunk = stage_ref.shape[1]
    n_chunks = w16_ref.shape[0] // chunk

    def copy(c):
        slot = c % 2
        return pltpu.make_async_copy(w_hbm.at[layer, pl.ds(c * chunk, chunk), :], stage_ref.at[slot], sem.at[slot])

    copy(0).start()
    for c in range(n_chunks):
        if c + 1 < n_chunks:
            copy(c + 1).start()
        copy(c).wait()
        w16_ref[c * chunk:(c + 1) * chunk, :] = stage_ref[c % 2].astype(BF16)


WEIGHT_STAGE_ROWS = 256


def _resident_weight_scratch(k, n):
    return [pltpu.VMEM((k, n), BF16), pltpu.VMEM((2, WEIGHT_STAGE_ROWS, n), F32), pltpu.SemaphoreType.DMA((2,))]


def _out_proj_kernel(alpha, layer, oa_ref, ob_ref, oc_ref, w_hbm, x_ref, g_ref, b_ref, o32_ref, o16_ref,
                     w16_ref, stage_ref, sem):
    @pl.when(pl.program_id(0) == 0)
    def _():
        _stage_weight_bf16(w_hbm, layer, w16_ref, stage_ref, sem)

    ka, kb = oa_ref.shape[1], ob_ref.shape[1]
    y = (jnp.dot(oa_ref[...], w16_ref[0:ka, :], preferred_element_type=F32)
         + jnp.dot(ob_ref[...], w16_ref[ka:ka + kb, :], preferred_element_type=F32)
         + jnp.dot(oc_ref[...], w16_ref[ka + kb:, :], preferred_element_type=F32))
    _layer_norm_store(alpha * x_ref[...] + y, g_ref, b_ref, o32_ref, o16_ref)


def _layer_vec(layer, d):
    return pl.BlockSpec((None, 1, d), lambda i, layer=layer: (layer, 0, 0))


def _out_proj_ln(oa, ob, oc, w_out, layer, x, g, b, alpha, bm):
    m, d = x.shape
    rows = lambda a: pl.BlockSpec((bm, a.shape[1]), lambda i: (i, 0))
    return pl.pallas_call(
        functools.partial(_out_proj_kernel, alpha, layer),
        grid=(m // bm,),
        in_specs=[rows(oa), rows(ob), rows(oc), pl.BlockSpec(memory_space=pl.ANY), rows(x),
                  _layer_vec(layer, d), _layer_vec(layer, d)],
        out_specs=[rows(x), rows(x)],
        out_shape=[jax.ShapeDtypeStruct((m, d), F32), jax.ShapeDtypeStruct((m, d), BF16)],
        scratch_shapes=_resident_weight_scratch(w_out.shape[1], d),
        compiler_params=_params(("arbitrary",)),
        name="out_proj_ln",
    )(oa, ob, oc, w_out, x, g, b)


def _in_proj_kernel(x_ref, w_ref, o_ref):
    o_ref[...] = jnp.dot(x_ref[...], w_ref[...].astype(BF16), preferred_element_type=F32).astype(o_ref.dtype)


def _in_proj(x16, w_in, layer, bm, bn):
    m, k = x16.shape
    n = w_in.shape[2]
    return pl.pallas_call(
        _in_proj_kernel,
        grid=(m // bm, n // bn),
        in_specs=[pl.BlockSpec((bm, k), lambda i, j: (i, 0)),
                  pl.BlockSpec((None, k, bn), lambda i, j: (layer, 0, j))],
        out_specs=pl.BlockSpec((bm, bn), lambda i, j: (i, j)),
        out_shape=jax.ShapeDtypeStruct((m, n), F32),
        compiler_params=_params(("parallel", "parallel")),
        name="in_proj",
    )(x16, w_in)


def _ffn_up_kernel(tiles_per_seq, x_ref, wg_ref, wu_ref, cw_ref, cb_ref, h_ref, g_ref, w16_ref):
    bm = x_ref.shape[0]
    bn = h_ref.shape[1]
    halo = g_ref.shape[0] - bm

    @pl.when(pl.program_id(1) == 0)
    def _():
        w16_ref[:, 0:bn] = wg_ref[...].astype(BF16)
        w16_ref[:, bn:] = wu_ref[...].astype(BF16)

    @pl.when(pl.program_id(1) % tiles_per_seq == 0)
    def _():
        g_ref[0:halo, :] = jnp.zeros((halo, g_ref.shape[1]), F32)

    gu = jnp.dot(x_ref[...], w16_ref[...], preferred_element_type=F32)
    g_ref[halo:, :] = gu[:, 0:bn]
    u = gu[:, bn:]
    gc = cb_ref[...]
    for j in range(CONV_WIDTH):
        lag = CONV_WIDTH - 1 - j
        gc = gc + cw_ref[j:j + 1, :] * g_ref[halo - lag:halo - lag + bm, :]
    h_ref[...] = (gc * jax.nn.sigmoid(gc) * u).astype(h_ref.dtype)
    g_ref[0:halo, :] = g_ref[bm:bm + halo, :]


def _ffn_up(x16, w_gate, w_up, conv_w, conv_b, layer, seq, bm, bn):
    m, d = x16.shape
    f = w_gate.shape[2]
    halo = 8
    wspec = pl.BlockSpec((None, d, bn), lambda j, i: (layer, 0, j))
    return pl.pallas_call(
        functools.partial(_ffn_up_kernel, seq // bm),
        grid=(f // bn, m // bm),
        in_specs=[pl.BlockSpec((bm, d), lambda j, i: (i, 0)), wspec, wspec,
                  pl.BlockSpec((None, CONV_WIDTH, bn), lambda j, i: (layer, 0, j)),
                  pl.BlockSpec((None, 1, bn), lambda j, i: (layer, 0, j))],
        out_specs=pl.BlockSpec((bm, bn), lambda j, i: (i, j)),
        out_shape=jax.ShapeDtypeStruct((m, f), BF16),
        scratch_shapes=[pltpu.VMEM((bm + halo, bn), F32), pltpu.VMEM((d, 2 * bn), BF16)],
        compiler_params=_params(("parallel", "arbitrary")),
        name="ffn_up_conv_gate",
    )(x16, w_gate, w_up, conv_w, conv_b)


def _ffn_down_kernel(alpha, layer, h_ref, w_hbm, x_ref, g_ref, b_ref, o32_ref, o16_ref, w16_ref, stage_ref, sem):
    @pl.when(pl.program_id(0) == 0)
    def _():
        _stage_weight_bf16(w_hbm, layer, w16_ref, stage_ref, sem)

    y = jnp.dot(h_ref[...], w16_ref[...], preferred_element_type=F32)
    _layer_norm_store(alpha * x_ref[...] + y, g_ref, b_ref, o32_ref, o16_ref)


def _ffn_down_ln(h, w_down, layer, x, g, b, alpha, bm):
    m, d = x.shape
    f = h.shape[1]
    rows = pl.BlockSpec((bm, d), lambda i: (i, 0))
    return pl.pallas_call(
        functools.partial(_ffn_down_kernel, alpha, layer),
        grid=(m // bm,),
        in_specs=[pl.BlockSpec((bm, f), lambda i: (i, 0)), pl.BlockSpec(memory_space=pl.ANY), rows,
                  _layer_vec(layer, d), _layer_vec(layer, d)],
        out_specs=[rows, rows],
        out_shape=[jax.ShapeDtypeStruct((m, d), F32), jax.ShapeDtypeStruct((m, d), BF16)],
        scratch_shapes=_resident_weight_scratch(f, d),
        compiler_params=_params(("arbitrary",)),
        name="ffn_down_ln",
    )(h, w_down, x, g, b)


def kernel(x, w_in, lb_logits, a_norm_w, c_sinks, w_out, ln1_g, ln1_b, w_gate, w_up, conv_w, conv_b, w_down,
           ln2_g, ln2_b):
    bsz, seq, d = x.shape
    depth = w_in.shape[0]
    alpha = (2 * depth) ** 0.25
    tables = _rope_tables(seq)
    lbs = jnp.cumsum(jax.nn.softmax(lb_logits.astype(F32), axis=0), axis=0)
    lbs = lbs - lbs[0]
    vec3 = lambda a: a.reshape(depth, 1, -1)
    ln1_g, ln1_b, ln2_g, ln2_b, conv_b = vec3(ln1_g), vec3(ln1_b), vec3(ln2_g), vec3(ln2_b), vec3(conv_b)
    x32 = x.reshape(bsz * seq, d)
    x16 = x32.astype(BF16)
    for l in range(depth):
        proj3 = _in_proj(x16, w_in, l, bm=2048, bn=512).reshape(bsz, seq, -1)
        oa = _hgrn2(proj3, lbs[l], a_norm_w[l]).reshape(bsz * seq, -1)
        ob = _dilated(proj3, tables).reshape(bsz * seq, -1)
        oc = _swa(proj3, c_sinks[l], tables).reshape(bsz * seq, -1)
        x32, x16 = _out_proj_ln(oa, ob, oc, w_out, l, x32, ln1_g, ln1_b, alpha, bm=512)
        h = _ffn_up(x16, w_gate, w_up, conv_w, conv_b, l, seq, bm=1024, bn=512)
        x32, x16 = _ffn_down_ln(h, w_down, l, x32, ln2_g, ln2_b, alpha, bm=256)
    return x32.reshape(bsz, seq, d)
```

```python
import functools

import numpy as np
import jax
import jax.numpy as jnp
from jax import lax
from jax.experimental import pallas as pl
from jax.experimental.pallas import tpu as pltpu

HEAD_DIM = 128
A_HEADS = 4
B_HEADS = 6
C_HEADS = 6
C_KV_HEADS = 2
C_REP = C_HEADS // C_KV_HEADS
DILATIONS = (1, 4, 16)
ATT_BLOCK = 128
C_MAX_LAG = 127
ROPE_THETA = 500000.0
ROPE_DIM = HEAD_DIM // 4
ROPE_HALF = ROPE_DIM // 2
CONV_WIDTH = 3
LN_EPS = 1e-5

QA_BLK, FA_BLK, IA_BLK, GA_BLK = 0, 4, 8, 12
QB_BLK, KB_BLK, VB_BLK = 16, 22, 28
QC_BLK, KC_BLK, VC_BLK = 34, 40, 42

HGRN_CHUNK = 128
HGRN_LEVELS = (64, 32, 16, 8, 4, 2, 1)

VMEM_LIMIT_BYTES = 56 * 1024 * 1024

F32 = jnp.float32
BF16 = jnp.bfloat16
NEG_BIG = -1e30
LOG2_E = 1.4426950408889634


def _params(semantics):
    return pltpu.CompilerParams(dimension_semantics=semantics, vmem_limit_bytes=VMEM_LIMIT_BYTES)


def _rope_tables(seq):
    inv = ROPE_THETA ** (-jnp.arange(0, ROPE_DIM, 2, dtype=F32) / ROPE_DIM)
    ang = jnp.arange(seq, dtype=F32)[:, None] * inv[None, :]
    cos, sin = jnp.cos(ang), jnp.sin(ang)
    rest = HEAD_DIM - ROPE_DIM
    c = jnp.concatenate([cos, cos, jnp.ones((seq, rest), F32)], -1)
    s = jnp.concatenate([sin, sin, jnp.zeros((seq, rest), F32)], -1)
    rot = np.zeros((HEAD_DIM, HEAD_DIM), np.float32)
    lane = np.arange(ROPE_HALF)
    rot[lane + ROPE_HALF, lane] = -1.0
    rot[lane, lane + ROPE_HALF] = 1.0
    return c, s, jnp.asarray(rot, BF16)


def _rope(x, c, s, rot):
    hi = x.astype(BF16)
    lo = (x - hi.astype(F32)).astype(BF16)
    swapped = jnp.dot(hi, rot, preferred_element_type=F32) + jnp.dot(lo, rot, preferred_element_type=F32)
    return x * c + swapped * s


def _hgrn_constants():
    c = HGRN_CHUNK
    i = np.arange(c)[:, None]
    j = np.arange(c)[None, :]
    lvl = np.full((c, c), -1, np.int32)
    for n, h in enumerate(HGRN_LEVELS):
        lvl[(i // (2 * h) == j // (2 * h)) & (i // h != j // h) & (i > j)] = n
    lvl[np.arange(c), np.arange(c)] = len(HGRN_LEVELS)
    return jnp.asarray((j <= i).astype(np.float32), BF16), jnp.asarray(lvl)


def _midpoint_log_decay(lf, b, h, row):
    c, dk = b.shape
    odd = ((row // h) % 2) == 1
    if h == 1:
        return jnp.where(odd, lf, 0.0)
    if h == 2:
        below = pltpu.roll(lf, c - 1, 0)
        above = pltpu.roll(lf, 1, 0)
        place = row % 4
        return jnp.where(place == 0, below, jnp.where(place == 1, 0.0, jnp.where(place == 2, lf, lf + above)))
    blocks = b.reshape(c // (2 * h), 2 * h, dk)
    mid = jnp.broadcast_to(blocks[:, h - 1:h, :], blocks.shape).reshape(c, dk)
    return jnp.where(odd, b - mid, mid - b)


def _split3(x):
    p1 = x.astype(BF16)
    r1 = x - p1.astype(F32)
    p2 = r1.astype(BF16)
    p3 = (r1 - p2.astype(F32)).astype(BF16)
    return p1, p2, p3


def _dot_nt(a, b, **kw):
    return lax.dot_general(a, b, (((1,), (1,)), ((), ())), preferred_element_type=F32, **kw)


def _dot_tn(a, b, **kw):
    return lax.dot_general(a, b, (((0,), (0,)), ((), ())), preferred_element_type=F32, **kw)


def _hgrn_chunk_heads(inputs, nw, e_all, lvl, row):
    c = HGRN_CHUNK
    heads = range(len(inputs))
    qs = [q * jax.nn.sigmoid(q) for q, _, _, _, _, _ in inputs]
    fg = [lb + (1.0 - lb) * jax.nn.sigmoid(fl) for _, fl, _, _, _, lb in inputs]
    kk = [1.0 - f for f in fg]
    lf = [jnp.log2(f) for f in fg]
    parts = [_split3(x) for x in lf]
    b12 = [jnp.dot(e_all, jnp.concatenate([p1, p2], axis=1), preferred_element_type=F32) for p1, p2, _ in parts]
    b3 = [jnp.dot(e_all, p3, preferred_element_type=F32) for _, _, p3 in parts]
    b = [x[:, :HEAD_DIM] + x[:, HEAD_DIM:] + y for x, y in zip(b12, b3)]
    att = [jnp.where(lvl == len(HGRN_LEVELS), jnp.sum(qs[i] * kk[i], axis=-1, keepdims=True), 0.0) for i in heads]
    for n, h in enumerate(HGRN_LEVELS):
        odd = ((row // h) % 2) == 1
        xh = [(jnp.where(odd, qs[i], kk[i]) * jnp.exp2(_midpoint_log_decay(lf[i], b[i], h, row))).astype(BF16)
              for i in heads]
        pair = [_dot_nt(x, x) for x in xh]
        att = [jnp.where(lvl == n, pair[i], att[i]) for i in heads]
    b_end = [x[c - 1:c] for x in b]
    q_in = [(qs[i] * jnp.exp2(b[i])).astype(BF16) for i in heads]
    k_end = [(kk[i] * jnp.exp2(b_end[i] - b[i])).astype(BF16) for i in heads]
    v16 = [v.astype(BF16) for _, _, v, _, _, _ in inputs]
    o = [jnp.dot(att[i].astype(BF16), v16[i], preferred_element_type=F32)
         + _dot_nt(q_in[i], inputs[i][4].astype(BF16)) for i in heads]
    st_new = [inputs[i][4] * jnp.exp2(b_end[i]) + _dot_tn(v16[i], k_end[i]) for i in heads]
    o = [x * lax.rsqrt(jnp.mean(x * x, axis=-1, keepdims=True) + LN_EPS) * nw for x in o]
    out = [(o[i] * (inputs[i][3] * jax.nn.sigmoid(inputs[i][3]))).astype(BF16) for i in heads]
    return list(zip(out, st_new))


HGRN_BATCH = 2
HGRN_SEQ_TILE = 1024


def _hgrn_kernel(q_ref, f_ref, i_ref, g_ref, lb_ref, nw_ref, e_ref, lvl_ref, o_ref, st_ref):
    c = HGRN_CHUNK
    n_batch, tile_rows = q_ref.shape[0], q_ref.shape[1]

    @pl.when(pl.program_id(1) == 0)
    def _():
        st_ref[...] = jnp.zeros_like(st_ref)

    nw = nw_ref[...]
    row = lax.broadcasted_iota(jnp.int32, (c, 1), 0)

    def chunk(ci, carry):
        rows = pl.ds(pl.multiple_of(ci * c, c), c)
        e_all, lvl = e_ref[...], lvl_ref[...]
        inputs = []
        for n in range(n_batch):
            for h in range(A_HEADS):
                cols = slice(h * HEAD_DIM, (h + 1) * HEAD_DIM)
                inputs.append((q_ref[n, rows, cols], f_ref[n, rows, cols], i_ref[n, rows, cols], g_ref[n, rows, cols],
                               st_ref[n * A_HEADS + h], lb_ref[:, cols]))
        for k, (o, st_new) in enumerate(_hgrn_chunk_heads(inputs, nw, e_all, lvl, row)):
            n, h = divmod(k, A_HEADS)
            o_ref[n, rows, h * HEAD_DIM:(h + 1) * HEAD_DIM] = o
            st_ref[k] = st_new
        return carry

    lax.fori_loop(0, tile_rows // c, chunk, 0)


def _hgrn2(proj3, lb, norm_w):
    bsz, seq, _ = proj3.shape
    e_all, lvl = _hgrn_constants()
    width = A_HEADS * HEAD_DIM
    col = lambda blk: pl.BlockSpec((HGRN_BATCH, HGRN_SEQ_TILE, width), lambda b, s, blk=blk: (b, s, blk // A_HEADS))
    const = lambda shape: pl.BlockSpec(shape, lambda b, s: (0,) * len(shape))
    return pl.pallas_call(
        _hgrn_kernel,
        grid=(bsz // HGRN_BATCH, seq // HGRN_SEQ_TILE),
        in_specs=[col(QA_BLK), col(FA_BLK), col(IA_BLK), col(GA_BLK),
                  const((1, width)), const((1, HEAD_DIM)), const(e_all.shape), const(lvl.shape)],
        out_specs=pl.BlockSpec((HGRN_BATCH, HGRN_SEQ_TILE, width), lambda b, s: (b, s, 0)),
        out_shape=jax.ShapeDtypeStruct((bsz, seq, width), BF16),
        scratch_shapes=[pltpu.VMEM((HGRN_BATCH * A_HEADS, HEAD_DIM, HEAD_DIM), F32)],
        compiler_params=_params(("parallel", "arbitrary")),
        name="hgrn2",
    )(proj3, proj3, proj3, proj3, lb.reshape(1, width), norm_w.reshape(1, HEAD_DIM), e_all, lvl)


def _band_masks(max_lag):
    r = lax.broadcasted_iota(jnp.int32, (ATT_BLOCK, ATT_BLOCK), 0)
    col = lax.broadcasted_iota(jnp.int32, (ATT_BLOCK, ATT_BLOCK), 1)
    return col <= r, (ATT_BLOCK + r - col) <= max_lag


def _tile_softmax_many(blocks):
    scale = HEAD_DIM ** -0.5 * LOG2_E
    s_all = [[jnp.where(mask, _dot_nt(q16, k.astype(BF16)) * scale, NEG_BIG) for k, mask in zip(ks, masks)]
             for q16, ks, _, masks, _ in blocks]
    m_all = []
    for s, (_, _, _, _, m_floor) in zip(s_all, blocks):
        m = jnp.max(functools.reduce(jnp.maximum, s), axis=-1, keepdims=True)
        m_all.append(m if m_floor is None else jnp.maximum(m, m_floor))
    e_all = [[jnp.exp2(si - m) for si in s] for s, m in zip(s_all, m_all)]
    acc_all = [functools.reduce(jnp.add, [jnp.dot(ei.astype(BF16), v.astype(BF16), preferred_element_type=F32)
                                          for ei, v in zip(e, vs)])
               for e, (_, _, vs, _, _) in zip(e_all, blocks)]
    l_all = [jnp.sum(functools.reduce(jnp.add, e), axis=-1, keepdims=True) for e in e_all]
    return list(zip(m_all, l_all, acc_all))


DIL_UNROLL = 16


def _dilated_kernel(proj_hbm, c_ref, s_ref, rot_ref, o_ref, qkv_ref, sem, qr_ref, kr_ref, m_ref, l_ref, acc_ref):
    seq = o_ref.shape[0]
    blk = ATT_BLOCK
    widest = DILATIONS[-1]
    per_residue = seq // widest
    step = pl.program_id(0) * pl.num_programs(1) + pl.program_id(1)
    n_steps = pl.num_programs(0) * pl.num_programs(1)
    slot = step % 2

    def qkv_copies(of_step, into_slot):
        b, h = of_step // pl.num_programs(1), of_step % pl.num_programs(1)
        copies = []
        for which, first_blk in enumerate((QB_BLK, KB_BLK, VB_BLK)):
            col = pl.multiple_of((first_blk + h) * HEAD_DIM, HEAD_DIM)
            for r in range(widest):
                copies.append(pltpu.make_async_copy(
                    proj_hbm.at[b, :, r, pl.ds(col, HEAD_DIM)],
                    qkv_ref.at[into_slot, which, pl.ds(r * per_residue, per_residue), :],
                    sem.at[into_slot]))
        return copies

    @pl.when(step == 0)
    def _():
        for copy in qkv_copies(step, slot):
            copy.start()

    @pl.when(step + 1 < n_steps)
    def _():
        for copy in qkv_copies(step + 1, 1 - slot):
            copy.start()

    for copy in qkv_copies(step, slot):
        copy.wait()

    c, s, rot = c_ref[...], s_ref[...], rot_ref[...]
    qr_ref[...] = _rope(qkv_ref[slot, 0], c, s, rot)
    kr_ref[...] = _rope(qkv_ref[slot, 1], c, s, rot)
    vr_ref = qkv_ref.at[slot, 2]

    def chunk_rows(dil):
        return blk * dil // widest

    def block_masks(dil):
        def position(i):
            return (widest // dil) * (i % chunk_rows(dil)) + i // chunk_rows(dil)
        row = position(lax.broadcasted_iota(jnp.int32, (blk, blk), 0))
        col = position(lax.broadcasted_iota(jnp.int32, (blk, blk), 1))
        return col <= row, col >= row

    def chunks_of(dil, r, n):
        size = chunk_rows(dil)
        return [pl.ds(pl.multiple_of((r + dil * a) * per_residue + size * n, size), size)
                for a in range(widest // dil)]

    def gather(ref, chunks):
        return jnp.concatenate([ref[rows, :] for rows in chunks], axis=0)

    def scatter(ref, chunks, value):
        size = value.shape[0] // len(chunks)
        for a, rows in enumerate(chunks):
            ref[rows, :] = value[a * size:(a + 1) * size]

    def run_group(specs, masks_of_pattern, first):
        cur_mask, prev_mask = masks_of_pattern
        loaded = []
        for rows, prev, prev_ok in specs:
            ks, vs, masks = [gather(kr_ref, rows)], [gather(vr_ref, rows)], [cur_mask]
            if prev is not None:
                ks.append(gather(kr_ref, prev))
                vs.append(gather(vr_ref, prev))
                masks.append(jnp.logical_and(prev_mask, prev_ok))
            state = None if first else (gather(m_ref, rows), gather(l_ref, rows), gather(acc_ref, rows))
            loaded.append((gather(qr_ref, rows).astype(BF16), ks, vs, masks, state))
        pieces = _tile_softmax_many([(q16, ks, vs, masks, None if state is None else state[0])
                                     for q16, ks, vs, masks, state in loaded])
        results = []
        for (m, l, acc), (_, _, _, _, state) in zip(pieces, loaded):
            if state is None:
                shape = (blk, HEAD_DIM)
                results.append((jnp.broadcast_to(m, shape), jnp.broadcast_to(l, shape), acc))
            else:
                m_old, l_old, acc_old = state
                alpha = jnp.exp2(m_old - m)
                results.append((m, alpha * l_old + l, alpha * acc_old + acc))
        for (rows, _, _), (m, l, acc) in zip(specs, results):
            scatter(m_ref, rows, m)
            scatter(l_ref, rows, l)
            scatter(acc_ref, rows, acc)

    n_query_blocks = seq // blk
    for dil in DILATIONS:
        n_blocks = n_query_blocks // dil
        masks_of_pattern = block_masks(dil)

        def group(g, carry, dil=dil, n_blocks=n_blocks, masks_of_pattern=masks_of_pattern):
            specs = []
            for u in range(DIL_UNROLL):
                e = g * DIL_UNROLL + u
                r, n = e % dil, e // dil
                if n_blocks == 1:
                    specs.append((chunks_of(dil, r, n), None, None))
                else:
                    specs.append((chunks_of(dil, r, n), chunks_of(dil, r, jnp.maximum(n - 1, 0)), n > 0))
            run_group(specs, masks_of_pattern, first=(dil == DILATIONS[0]))
            return carry

        lax.fori_loop(0, n_query_blocks // DIL_UNROLL, group, 0)

    for r in range(widest):
        rows = slice(r * per_residue, (r + 1) * per_residue)
        qr_ref[pl.ds(r, per_residue, stride=widest), :] = acc_ref[rows, :] / l_ref[rows, :]
    o_ref[...] = qr_ref[...].astype(o_ref.dtype)


def _residue_major(table, widest):
    seq, width = table.shape
    return table.reshape(seq // widest, widest, width).transpose(1, 0, 2).reshape(seq, width)


def _dilated(proj3, tables):
    bsz, seq, width = proj3.shape
    widest = DILATIONS[-1]
    tab = pl.BlockSpec((seq, HEAD_DIM), lambda b, h: (0, 0))
    return pl.pallas_call(
        _dilated_kernel,
        grid=(bsz, B_HEADS),
        in_specs=[pl.BlockSpec(memory_space=pl.ANY), tab, tab, pl.BlockSpec((HEAD_DIM, HEAD_DIM), lambda b, h: (0, 0))],
        out_specs=pl.BlockSpec((None, seq, HEAD_DIM), lambda b, h: (b, 0, h)),
        out_shape=jax.ShapeDtypeStruct((bsz, seq, B_HEADS * HEAD_DIM), BF16),
        scratch_shapes=[pltpu.VMEM((2, 3, seq, HEAD_DIM), F32), pltpu.SemaphoreType.DMA((2,))]
                       + [pltpu.VMEM((seq, HEAD_DIM), F32)] * 5,
        compiler_params=_params(("arbitrary", "arbitrary")),
        name="dilated_attention",
    )(proj3.reshape(bsz, seq // widest, widest, width), _residue_major(tables[0], widest),
      _residue_major(tables[1], widest), tables[2])


SWA_UNROLL = 4


def _swa_kernel(sink_ref, q0_ref, q1_ref, q2_ref, k_ref, v_ref, c_ref, s_ref, rot_ref, o_ref, qr_ref, kr_ref):
    seq = k_ref.shape[0]
    blk = ATT_BLOCK
    grp = pl.program_id(1)
    c, s, rot = c_ref[...], s_ref[...], rot_ref[...]
    for j, q_ref in enumerate((q0_ref, q1_ref, q2_ref)):
        qr_ref[j] = _rope(q_ref[...], c, s, rot).astype(BF16)
    kr_ref[...] = _rope(k_ref[...], c, s, rot).astype(BF16)
    cur_mask, prev_mask = _band_masks(C_MAX_LAG)
    sinks = [jnp.full((blk, 1), sink_ref[grp * C_REP + j] * LOG2_E, F32) for j in range(C_REP)]

    def group(g, carry):
        blocks, dests = [], []
        for u in range(SWA_UNROLL):
            n = g * SWA_UNROLL + u
            rows = pl.ds(pl.multiple_of(n * blk, blk), blk)
            prev = pl.ds(pl.multiple_of(jnp.maximum(n - 1, 0) * blk, blk), blk)
            ks, vs = [kr_ref[rows, :], kr_ref[prev, :]], [v_ref[rows, :], v_ref[prev, :]]
            masks = [cur_mask, jnp.logical_and(prev_mask, n > 0)]
            for j in range(C_REP):
                blocks.append((qr_ref[j, rows, :], ks, vs, masks, sinks[j]))
                dests.append((rows, j))
        for (rows, j), (m, l, acc) in zip(dests, _tile_softmax_many(blocks)):
            out = acc / (l + jnp.exp2(sinks[j] - m))
            o_ref[rows, j * HEAD_DIM:(j + 1) * HEAD_DIM] = out.astype(o_ref.dtype)
        return carry

    lax.fori_loop(0, seq // (blk * SWA_UNROLL), group, 0)


def _swa(proj3, sinks, tables):
    bsz, seq, _ = proj3.shape
    qcol = lambda j: pl.BlockSpec((None, seq, HEAD_DIM), lambda b, g, j=j: (b, 0, QC_BLK + g * C_REP + j))
    col = lambda blk: pl.BlockSpec((None, seq, HEAD_DIM), lambda b, g, blk=blk: (b, 0, blk + g))
    tab = pl.BlockSpec((seq, HEAD_DIM), lambda b, g: (0, 0))
    return pl.pallas_call(
        _swa_kernel,
        grid=(bsz, C_KV_HEADS),
        in_specs=[pl.BlockSpec(memory_space=pltpu.SMEM), qcol(0), qcol(1), qcol(2), col(KC_BLK), col(VC_BLK),
                  tab, tab, pl.BlockSpec((HEAD_DIM, HEAD_DIM), lambda b, g: (0, 0))],
        out_specs=pl.BlockSpec((None, seq, C_REP * HEAD_DIM), lambda b, g: (b, 0, g)),
        out_shape=jax.ShapeDtypeStruct((bsz, seq, C_HEADS * HEAD_DIM), BF16),
        scratch_shapes=[pltpu.VMEM((C_REP, seq, HEAD_DIM), BF16), pltpu.VMEM((seq, HEAD_DIM), BF16)],
        compiler_params=_params(("parallel", "parallel")),
        name="swa_sink_attention",
    )(sinks, proj3, proj3, proj3, proj3, proj3, *tables)


def _layer_norm_store(z, g_ref, b_ref, o32_ref, o16_ref):
    mu = jnp.mean(z, axis=-1, keepdims=True)
    zc = z - mu
    var = jnp.mean(zc * zc, axis=-1, keepdims=True)
    y = zc * lax.rsqrt(var + LN_EPS) * g_ref[...] + b_ref[...]
    o32_ref[...] = y
    o16_ref[...] = y.astype(BF16)


def _stage_weight_bf16(w_hbm, layer, w16_ref, stage_ref, sem):
    chunk = stage_ref.shape[1]
    n_chunks = w16_ref.shape[0] // chunk

    def copy(c):
        slot = c % 2
        return pltpu.make_async_copy(w_hbm.at[layer, pl.ds(c * chunk, chunk), :], stage_ref.at[slot], sem.at[slot])

    copy(0).start()
    for c in range(n_chunks):
        if c + 1 < n_chunks:
            copy(c + 1).start()
        copy(c).wait()
        w16_ref[c * chunk:(c + 1) * chunk, :] = stage_ref[c % 2].astype(BF16)


WEIGHT_STAGE_ROWS = 256


def _resident_weight_scratch(k, n):
    return [pltpu.VMEM((k, n), BF16), pltpu.VMEM((2, WEIGHT_STAGE_ROWS, n), F32), pltpu.SemaphoreType.DMA((2,))]


def _out_proj_kernel(alpha, layer, oa_ref, ob_ref, oc_ref, w_hbm, x_ref, g_ref, b_ref, o32_ref, o16_ref,
                     w16_ref, stage_ref, sem):
    @pl.when(pl.program_id(0) == 0)
    def _():
        _stage_weight_bf16(w_hbm, layer, w16_ref, stage_ref, sem)

    ka, kb = oa_ref.shape[1], ob_ref.shape[1]
    y = (jnp.dot(oa_ref[...], w16_ref[0:ka, :], preferred_element_type=F32)
         + jnp.dot(ob_ref[...], w16_ref[ka:ka + kb, :], preferred_element_type=F32)
         + jnp.dot(oc_ref[...], w16_ref[ka + kb:, :], preferred_element_type=F32))
    _layer_norm_store(alpha * x_ref[...] + y, g_ref, b_ref, o32_ref, o16_ref)


def _layer_vec(layer, d):
    return pl.BlockSpec((None, 1, d), lambda i, layer=layer: (layer, 0, 0))


def _out_proj_ln(oa, ob, oc, w_out, layer, x, g, b, alpha, bm):
    m, d = x.shape
    rows = lambda a: pl.BlockSpec((bm, a.shape[1]), lambda i: (i, 0))
    return pl.pallas_call(
        functools.partial(_out_proj_kernel, alpha, layer),
        grid=(m // bm,),
        in_specs=[rows(oa), rows(ob), rows(oc), pl.BlockSpec(memory_space=pl.ANY), rows(x),
                  _layer_vec(layer, d), _layer_vec(layer, d)],
        out_specs=[rows(x), rows(x)],
        out_shape=[jax.ShapeDtypeStruct((m, d), F32), jax.ShapeDtypeStruct((m, d), BF16)],
        scratch_shapes=_resident_weight_scratch(w_out.shape[1], d),
        compiler_params=_params(("arbitrary",)),
        name="out_proj_ln",
    )(oa, ob, oc, w_out, x, g, b)


def _in_proj_kernel(x_ref, w_ref, o_ref):
    o_ref[...] = jnp.dot(x_ref[...], w_ref[...].astype(BF16), preferred_element_type=F32).astype(o_ref.dtype)


def _in_proj(x16, w_in, layer, bm, bn):
    m, k = x16.shape
    n = w_in.shape[2]
    return pl.pallas_call(
        _in_proj_kernel,
        grid=(m // bm, n // bn),
        in_specs=[pl.BlockSpec((bm, k), lambda i, j: (i, 0)),
                  pl.BlockSpec((None, k, bn), lambda i, j: (layer, 0, j))],
        out_specs=pl.BlockSpec((bm, bn), lambda i, j: (i, j)),
        out_shape=jax.ShapeDtypeStruct((m, n), F32),
        compiler_params=_params(("parallel", "parallel")),
        name="in_proj",
    )(x16, w_in)


def _ffn_up_kernel(tiles_per_seq, x_ref, wg_ref, wu_ref, cw_ref, cb_ref, h_ref, g_ref, w16_ref):
    bm = x_ref.shape[0]
    bn = h_ref.shape[1]
    halo = g_ref.shape[0] - bm

    @pl.when(pl.program_id(1) == 0)
    def _():
        w16_ref[:, 0:bn] = wg_ref[...].astype(BF16)
        w16_ref[:, bn:] = wu_ref[...].astype(BF16)

    @pl.when(pl.program_id(1) % tiles_per_seq == 0)
    def _():
        g_ref[0:halo, :] = jnp.zeros((halo, g_ref.shape[1]), F32)

    gu = jnp.dot(x_ref[...], w16_ref[...], preferred_element_type=F32)
    g_ref[halo:, :] = gu[:, 0:bn]
    u = gu[:, bn:]
    gc = cb_ref[...]
    for j in range(CONV_WIDTH):
        lag = CONV_WIDTH - 1 - j
        gc = gc + cw_ref[j:j + 1, :] * g_ref[halo - lag:halo - lag + bm, :]
    h_ref[...] = (gc * jax.nn.sigmoid(gc) * u).astype(h_ref.dtype)
    g_ref[0:halo, :] = g_ref[bm:bm + halo, :]


def _ffn_up(x16, w_gate, w_up, conv_w, conv_b, layer, seq, bm, bn):
    m, d = x16.shape
    f = w_gate.shape[2]
    halo = 8
    wspec = pl.BlockSpec((None, d, bn), lambda j, i: (layer, 0, j))
    return pl.pallas_call(
        functools.partial(_ffn_up_kernel, seq // bm),
        grid=(f // bn, m // bm),
        in_specs=[pl.BlockSpec((bm, d), lambda j, i: (i, 0)), wspec, wspec,
                  pl.BlockSpec((None, CONV_WIDTH, bn), lambda j, i: (layer, 0, j)),
                  pl.BlockSpec((None, 1, bn), lambda j, i: (layer, 0, j))],
        out_specs=pl.BlockSpec((bm, bn), lambda j, i: (i, j)),
        out_shape=jax.ShapeDtypeStruct((m, f), BF16),
        scratch_shapes=[pltpu.VMEM((bm + halo, bn), F32), pltpu.VMEM((d, 2 * bn), BF16)],
        compiler_params=_params(("parallel", "arbitrary")),
        name="ffn_up_conv_gate",
    )(x16, w_gate, w_up, conv_w, conv_b)


def _ffn_down_kernel(alpha, layer, h_ref, w_hbm, x_ref, g_ref, b_ref, o32_ref, o16_ref, w16_ref, stage_ref, sem):
    @pl.when(pl.program_id(0) == 0)
    def _():
        _stage_weight_bf16(w_hbm, layer, w16_ref, stage_ref, sem)

    y = jnp.dot(h_ref[...], w16_ref[...], preferred_element_type=F32)
    _layer_norm_store(alpha * x_ref[...] + y, g_ref, b_ref, o32_ref, o16_ref)


def _ffn_down_ln(h, w_down, layer, x, g, b, alpha, bm):
    m, d = x.shape
    f = h.shape[1]
    rows = pl.BlockSpec((bm, d), lambda i: (i, 0))
    return pl.pallas_call(
        functools.partial(_ffn_down_kernel, alpha, layer),
        grid=(m // bm,),
        in_specs=[pl.BlockSpec((bm, f), lambda i: (i, 0)), pl.BlockSpec(memory_space=pl.ANY), rows,
                  _layer_vec(layer, d), _layer_vec(layer, d)],
        out_specs=[rows, rows],
        out_shape=[jax.ShapeDtypeStruct((m, d), F32), jax.ShapeDtypeStruct((m, d), BF16)],
        scratch_shapes=_resident_weight_scratch(f, d),
        compiler_params=_params(("arbitrary",)),
        name="ffn_down_ln",
    )(h, w_down, x, g, b)


def kernel(x, w_in, lb_logits, a_norm_w, c_sinks, w_out, ln1_g, ln1_b, w_gate, w_up, conv_w, conv_b, w_down,
           ln2_g, ln2_b):
    bsz, seq, d = x.shape
    depth = w_in.shape[0]
    alpha = (2 * depth) ** 0.25
    tables = _rope_tables(seq)
    lbs = jnp.cumsum(jax.nn.softmax(lb_logits.astype(F32), axis=0), axis=0)
    lbs = lbs - lbs[0]
    vec3 = lambda a: a.reshape(depth, 1, -1)
    ln1_g, ln1_b, ln2_g, ln2_b, conv_b = vec3(ln1_g), vec3(ln1_b), vec3(ln2_g), vec3(ln2_b), vec3(conv_b)
    x32 = x.reshape(bsz * seq, d)
    x16 = x32.astype(BF16)
    for l in range(depth):
        proj3 = _in_proj(x16, w_in, l, bm=2048, bn=512).reshape(bsz, seq, -1)
        oa = _hgrn2(proj3, lbs[l], a_norm_w[l]).reshape(bsz * seq, -1)
        ob = _dilated(proj3, tables).reshape(bsz * seq, -1)
        oc = _swa(proj3, c_sinks[l], tables).reshape(bsz * seq, -1)
        x32, x16 = _out_proj_ln(oa, ob, oc, w_out, l, x32, ln1_g, ln1_b, alpha, bm=512)
        h = _ffn_up(x16, w_gate, w_up, conv_w, conv_b, l, seq, bm=1024, bn=512)
        x32, x16 = _ffn_down_ln(h, w_down, l, x32, ln2_g, ln2_b, alpha, bm=256)
    return x32.reshape(bsz, seq, d)
```

```python
import functools
from typing import NamedTuple

import numpy as np
import jax
import jax.numpy as jnp
from jax import lax
from jax.experimental import pallas as pl
from jax.experimental.pallas import tpu as pltpu

HEAD_DIM = 128
A_HEADS = 4
B_HEADS = 6
C_HEADS = 6
C_KV_HEADS = 2
C_REP = C_HEADS // C_KV_HEADS
DILATIONS = (1, 4, 16)
ATT_BLOCK = 128
C_MAX_LAG = 127
ROPE_THETA = 500000.0
ROPE_DIM = HEAD_DIM // 4
ROPE_HALF = ROPE_DIM // 2
CONV_WIDTH = 3
LN_EPS = 1e-5

QA_BLK, FA_BLK, IA_BLK, GA_BLK = 0, 4, 8, 12
QB_BLK, KB_BLK, VB_BLK = 16, 22, 28
QC_BLK, KC_BLK, VC_BLK = 34, 40, 42

HGRN_CHUNK = 128
HGRN_LEVELS = (64, 32, 16, 8, 4, 2, 1)

VMEM_LIMIT_BYTES = 56 * 1024 * 1024

F32 = jnp.float32
BF16 = jnp.bfloat16
NEG_BIG = -1e30
LOG2_E = 1.4426950408889634


def _params(semantics):
    return pltpu.CompilerParams(dimension_semantics=semantics, vmem_limit_bytes=VMEM_LIMIT_BYTES)


SUBLANES = 8
WEIGHT_STAGE_ROWS = 256


class _TilePlan(NamedTuple):
    in_proj: tuple
    out_proj_rows: int
    ffn_up: tuple
    ffn_down_rows: int


def _tile_plan(rows, d_model, d_mix, d_ff):
    plan = _TilePlan(in_proj=(2048, 512), out_proj_rows=512, ffn_up=(2048, 512), ffn_down_rows=256)
    f32, bf16, two = 4, 2, 2
    staged = two * WEIGHT_STAGE_ROWS * d_model * f32
    bm, bn = plan.in_proj
    in_proj = two * bm * d_model * bf16 + two * d_model * bn * f32 + d_model * bn * bf16 + two * bm * bn * f32
    bm = plan.out_proj_rows
    out_proj = d_mix * d_model * bf16 + staged + two * bm * (d_mix * bf16 + d_model * (f32 + f32 + bf16))
    bm, bn = plan.ffn_up
    ffn_up = (two * bm * d_model * bf16 + two * two * d_model * bn * f32 + d_model * two * bn * bf16
              + (bm + SUBLANES) * bn * f32 + two * bm * bn * bf16 + bm * two * bn * f32)
    bm = plan.ffn_down_rows
    ffn_down = d_ff * d_model * bf16 + staged + two * bm * (d_ff * bf16 + d_model * (f32 + f32 + bf16))
    for name, need in (("in_proj", in_proj), ("out_proj", out_proj), ("ffn_up", ffn_up), ("ffn_down", ffn_down)):
        assert need <= VMEM_LIMIT_BYTES, (name, need)
    assert rows % max(plan.in_proj[0], plan.ffn_up[0]) == 0 and d_ff % plan.ffn_up[1] == 0
    return plan


def _rope_tables(seq):
    inv = ROPE_THETA ** (-jnp.arange(0, ROPE_DIM, 2, dtype=F32) / ROPE_DIM)
    ang = jnp.arange(seq, dtype=F32)[:, None] * inv[None, :]
    cos, sin = jnp.cos(ang), jnp.sin(ang)
    rest = HEAD_DIM - ROPE_DIM
    c = jnp.concatenate([cos, cos, jnp.ones((seq, rest), F32)], -1)
    s = jnp.concatenate([sin, sin, jnp.zeros((seq, rest), F32)], -1)
    rot = np.zeros((HEAD_DIM, HEAD_DIM), np.float32)
    lane = np.arange(ROPE_HALF)
    rot[lane + ROPE_HALF, lane] = -1.0
    rot[lane, lane + ROPE_HALF] = 1.0
    return c, s, jnp.asarray(rot, BF16)


def _rope(x, c, s, rot):
    hi = x.astype(BF16)
    lo = (x - hi.astype(F32)).astype(BF16)
    swapped = jnp.dot(hi, rot, preferred_element_type=F32) + jnp.dot(lo, rot, preferred_element_type=F32)
    return x * c + swapped * s


def _hgrn_constants():
    c = HGRN_CHUNK
    i = np.arange(c)[:, None]
    j = np.arange(c)[None, :]
    lvl = np.full((c, c), -1, np.int32)
    for n, h in enumerate(HGRN_LEVELS):
        lvl[(i // (2 * h) == j // (2 * h)) & (i // h != j // h) & (i > j)] = n
    lvl[np.arange(c), np.arange(c)] = len(HGRN_LEVELS)
    return jnp.asarray((j <= i).astype(np.float32), BF16), jnp.asarray(lvl)


def _midpoint_log_decay(lf, b, h, row):
    c, dk = b.shape
    odd = ((row // h) % 2) == 1
    if h == 1:
        return jnp.where(odd, lf, 0.0)
    if h == 2:
        below = pltpu.roll(lf, c - 1, 0)
        above = pltpu.roll(lf, 1, 0)
        place = row % 4
        return jnp.where(place == 0, below, jnp.where(place == 1, 0.0, jnp.where(place == 2, lf, lf + above)))
    blocks = b.reshape(c // (2 * h), 2 * h, dk)
    mid = jnp.broadcast_to(blocks[:, h - 1:h, :], blocks.shape).reshape(c, dk)
    return jnp.where(odd, b - mid, mid - b)


def _split3(x):
    p1 = x.astype(BF16)
    r1 = x - p1.astype(F32)
    p2 = r1.astype(BF16)
    p3 = (r1 - p2.astype(F32)).astype(BF16)
    return p1, p2, p3


def _dot_nt(a, b, **kw):
    return lax.dot_general(a, b, (((1,), (1,)), ((), ())), preferred_element_type=F32, **kw)


def _dot_tn(a, b, **kw):
    return lax.dot_general(a, b, (((0,), (0,)), ((), ())), preferred_element_type=F32, **kw)


def _hgrn_chunk_heads(inputs, nw, e_all, lvl, row):
    c = HGRN_CHUNK
    heads = range(len(inputs))
    qs = [q * jax.nn.sigmoid(q) for q, _, _, _, _, _ in inputs]
    fg = [lb + (1.0 - lb) * jax.nn.sigmoid(fl) for _, fl, _, _, _, lb in inputs]
    kk = [1.0 - f for f in fg]
    lf = [jnp.log2(f) for f in fg]
    parts = [_split3(x) for x in lf]
    b12 = [jnp.dot(e_all, jnp.concatenate([p1, p2], axis=1), preferred_element_type=F32) for p1, p2, _ in parts]
    b3 = [jnp.dot(e_all, p3, preferred_element_type=F32) for _, _, p3 in parts]
    b = [x[:, :HEAD_DIM] + x[:, HEAD_DIM:] + y for x, y in zip(b12, b3)]
    att = [jnp.where(lvl == len(HGRN_LEVELS), jnp.sum(qs[i] * kk[i], axis=-1, keepdims=True), 0.0) for i in heads]
    for n, h in enumerate(HGRN_LEVELS):
        odd = ((row // h) % 2) == 1
        xh = [(jnp.where(odd, qs[i], kk[i]) * jnp.exp2(_midpoint_log_decay(lf[i], b[i], h, row))).astype(BF16)
              for i in heads]
        pair = [_dot_nt(x, x) for x in xh]
        att = [jnp.where(lvl == n, pair[i], att[i]) for i in heads]
    b_end = [x[c - 1:c] for x in b]
    q_in = [(qs[i] * jnp.exp2(b[i])).astype(BF16) for i in heads]
    k_end = [(kk[i] * jnp.exp2(b_end[i] - b[i])).astype(BF16) for i in heads]
    v16 = [v.astype(BF16) for _, _, v, _, _, _ in inputs]
    o = [jnp.dot(att[i].astype(BF16), v16[i], preferred_element_type=F32)
         + _dot_nt(q_in[i], inputs[i][4].astype(BF16)) for i in heads]
    st_new = [inputs[i][4] * jnp.exp2(b_end[i]) + _dot_tn(v16[i], k_end[i]) for i in heads]
    o = [x * lax.rsqrt(jnp.mean(x * x, axis=-1, keepdims=True) + LN_EPS) * nw for x in o]
    out = [(o[i] * (inputs[i][3] * jax.nn.sigmoid(inputs[i][3]))).astype(BF16) for i in heads]
    return list(zip(out, st_new))


HGRN_BATCH = 2
HGRN_SEQ_TILE = 1024


def _hgrn_kernel(q_ref, f_ref, i_ref, g_ref, lb_ref, nw_ref, e_ref, lvl_ref, o_ref, st_ref):
    c = HGRN_CHUNK
    n_batch, tile_rows = q_ref.shape[0], q_ref.shape[1]

    @pl.when(pl.program_id(1) == 0)
    def _():
        st_ref[...] = jnp.zeros_like(st_ref)

    nw = nw_ref[...]
    row = lax.broadcasted_iota(jnp.int32, (c, 1), 0)

    def chunk(ci, carry):
        rows = pl.ds(pl.multiple_of(ci * c, c), c)
        e_all, lvl = e_ref[...], lvl_ref[...]
        inputs = []
        for n in range(n_batch):
            for h in range(A_HEADS):
                cols = slice(h * HEAD_DIM, (h + 1) * HEAD_DIM)
                inputs.append((q_ref[n, rows, cols], f_ref[n, rows, cols], i_ref[n, rows, cols], g_ref[n, rows, cols],
                               st_ref[n * A_HEADS + h], lb_ref[:, cols]))
        for k, (o, st_new) in enumerate(_hgrn_chunk_heads(inputs, nw, e_all, lvl, row)):
            n, h = divmod(k, A_HEADS)
            o_ref[n, rows, h * HEAD_DIM:(h + 1) * HEAD_DIM] = o
            st_ref[k] = st_new
        return carry

    lax.fori_loop(0, tile_rows // c, chunk, 0)


def _hgrn2(proj3, lb, norm_w):
    bsz, seq, _ = proj3.shape
    e_all, lvl = _hgrn_constants()
    width = A_HEADS * HEAD_DIM
    col = lambda blk: pl.BlockSpec((HGRN_BATCH, HGRN_SEQ_TILE, width), lambda b, s, blk=blk: (b, s, blk // A_HEADS))
    const = lambda shape: pl.BlockSpec(shape, lambda b, s: (0,) * len(shape))
    return pl.pallas_call(
        _hgrn_kernel,
        grid=(bsz // HGRN_BATCH, seq // HGRN_SEQ_TILE),
        in_specs=[col(QA_BLK), col(FA_BLK), col(IA_BLK), col(GA_BLK),
                  const((1, width)), const((1, HEAD_DIM)), const(e_all.shape), const(lvl.shape)],
        out_specs=pl.BlockSpec((HGRN_BATCH, HGRN_SEQ_TILE, width), lambda b, s: (b, s, 0)),
        out_shape=jax.ShapeDtypeStruct((bsz, seq, width), BF16),
        scratch_shapes=[pltpu.VMEM((HGRN_BATCH * A_HEADS, HEAD_DIM, HEAD_DIM), F32)],
        compiler_params=_params(("parallel", "arbitrary")),
        name="hgrn2",
    )(proj3, proj3, proj3, proj3, lb.reshape(1, width), norm_w.reshape(1, HEAD_DIM), e_all, lvl)


def _band_masks(max_lag):
    r = lax.broadcasted_iota(jnp.int32, (ATT_BLOCK, ATT_BLOCK), 0)
    col = lax.broadcasted_iota(jnp.int32, (ATT_BLOCK, ATT_BLOCK), 1)
    return col <= r, (ATT_BLOCK + r - col) <= max_lag


def _tile_softmax_many(blocks):
    scale = HEAD_DIM ** -0.5 * LOG2_E
    s_all = [[jnp.where(mask, _dot_nt(q16, k.astype(BF16)) * scale, NEG_BIG) for k, mask in zip(ks, masks)]
             for q16, ks, _, masks, _ in blocks]
    m_all = []
    for s, (_, _, _, _, m_floor) in zip(s_all, blocks):
        m = jnp.max(functools.reduce(jnp.maximum, s), axis=-1, keepdims=True)
        m_all.append(m if m_floor is None else jnp.maximum(m, m_floor))
    e_all = [[jnp.exp2(si - m) for si in s] for s, m in zip(s_all, m_all)]
    acc_all = [functools.reduce(jnp.add, [jnp.dot(ei.astype(BF16), v.astype(BF16), preferred_element_type=F32)
                                          for ei, v in zip(e, vs)])
               for e, (_, _, vs, _, _) in zip(e_all, blocks)]
    l_all = [jnp.sum(functools.reduce(jnp.add, e), axis=-1, keepdims=True) for e in e_all]
    return list(zip(m_all, l_all, acc_all))


DIL_UNROLL = 16


def _dilated_kernel(proj_hbm, c_ref, s_ref, rot_ref, o_ref, qkv_ref, sem, qr_ref, kr_ref, m_ref, l_ref, acc_ref):
    seq = o_ref.shape[0]
    blk = ATT_BLOCK
    widest = DILATIONS[-1]
    per_residue = seq // widest
    step = pl.program_id(0) * pl.num_programs(1) + pl.program_id(1)
    n_steps = pl.num_programs(0) * pl.num_programs(1)
    slot = step % 2

    def qkv_copies(of_step, into_slot):
        b, h = of_step // pl.num_programs(1), of_step % pl.num_programs(1)
        copies = []
        for which, first_blk in enumerate((QB_BLK, KB_BLK, VB_BLK)):
            col = pl.multiple_of((first_blk + h) * HEAD_DIM, HEAD_DIM)
            for r in range(widest):
                copies.append(pltpu.make_async_copy(
                    proj_hbm.at[b, :, r, pl.ds(col, HEAD_DIM)],
                    qkv_ref.at[into_slot, which, pl.ds(r * per_residue, per_residue), :],
                    sem.at[into_slot]))
        return copies

    @pl.when(step == 0)
    def _():
        for copy in qkv_copies(step, slot):
            copy.start()

    @pl.when(step + 1 < n_steps)
    def _():
        for copy in qkv_copies(step + 1, 1 - slot):
            copy.start()

    for copy in qkv_copies(step, slot):
        copy.wait()

    c, s, rot = c_ref[...], s_ref[...], rot_ref[...]
    qr_ref[...] = _rope(qkv_ref[slot, 0], c, s, rot)
    kr_ref[...] = _rope(qkv_ref[slot, 1], c, s, rot)
    vr_ref = qkv_ref.at[slot, 2]

    def chunk_rows(dil):
        return blk * dil // widest

    def block_masks(dil):
        def position(i):
            return (widest // dil) * (i % chunk_rows(dil)) + i // chunk_rows(dil)
        row = position(lax.broadcasted_iota(jnp.int32, (blk, blk), 0))
        col = position(lax.broadcasted_iota(jnp.int32, (blk, blk), 1))
        return col <= row, col >= row

    def chunks_of(dil, r, n):
        size = chunk_rows(dil)
        return [pl.ds(pl.multiple_of((r + dil * a) * per_residue + size * n, size), size)
                for a in range(widest // dil)]

    def gather(ref, chunks):
        return jnp.concatenate([ref[rows, :] for rows in chunks], axis=0)

    def scatter(ref, chunks, value):
        size = value.shape[0] // len(chunks)
        for a, rows in enumerate(chunks):
            ref[rows, :] = value[a * size:(a + 1) * size]

    def run_group(specs, masks_of_pattern, first):
        cur_mask, prev_mask = masks_of_pattern
        loaded = []
        for rows, prev, prev_ok in specs:
            ks, vs, masks = [gather(kr_ref, rows)], [gather(vr_ref, rows)], [cur_mask]
            if prev is not None:
                ks.append(gather(kr_ref, prev))
                vs.append(gather(vr_ref, prev))
                masks.append(jnp.logical_and(prev_mask, prev_ok))
            state = None if first else (gather(m_ref, rows), gather(l_ref, rows), gather(acc_ref, rows))
            loaded.append((gather(qr_ref, rows).astype(BF16), ks, vs, masks, state))
        pieces = _tile_softmax_many([(q16, ks, vs, masks, None if state is None else state[0])
                                     for q16, ks, vs, masks, state in loaded])
        results = []
        for (m, l, acc), (_, _, _, _, state) in zip(pieces, loaded):
            if state is None:
                shape = (blk, HEAD_DIM)
                results.append((jnp.broadcast_to(m, shape), jnp.broadcast_to(l, shape), acc))
            else:
                m_old, l_old, acc_old = state
                alpha = jnp.exp2(m_old - m)
                results.append((m, alpha * l_old + l, alpha * acc_old + acc))
        for (rows, _, _), (m, l, acc) in zip(specs, results):
            scatter(m_ref, rows, m)
            scatter(l_ref, rows, l)
            scatter(acc_ref, rows, acc)

    n_query_blocks = seq // blk
    for dil in DILATIONS:
        n_blocks = n_query_blocks // dil
        masks_of_pattern = block_masks(dil)

        def group(g, carry, dil=dil, n_blocks=n_blocks, masks_of_pattern=masks_of_pattern):
            specs = []
            for u in range(DIL_UNROLL):
                e = g * DIL_UNROLL + u
                r, n = e % dil, e // dil
                if n_blocks == 1:
                    specs.append((chunks_of(dil, r, n), None, None))
                else:
                    specs.append((chunks_of(dil, r, n), chunks_of(dil, r, jnp.maximum(n - 1, 0)), n > 0))
            run_group(specs, masks_of_pattern, first=(dil == DILATIONS[0]))
            return carry

        lax.fori_loop(0, n_query_blocks // DIL_UNROLL, group, 0)

    for r in range(widest):
        rows = slice(r * per_residue, (r + 1) * per_residue)
        qr_ref[pl.ds(r, per_residue, stride=widest), :] = acc_ref[rows, :] / l_ref[rows, :]
    o_ref[...] = qr_ref[...].astype(o_ref.dtype)


def _residue_major(table, widest):
    seq, width = table.shape
    return table.reshape(seq // widest, widest, width).transpose(1, 0, 2).reshape(seq, width)


def _dilated(proj3, tables):
    bsz, seq, width = proj3.shape
    widest = DILATIONS[-1]
    tab = pl.BlockSpec((seq, HEAD_DIM), lambda b, h: (0, 0))
    return pl.pallas_call(
        _dilated_kernel,
        grid=(bsz, B_HEADS),
        in_specs=[pl.BlockSpec(memory_space=pl.ANY), tab, tab, pl.BlockSpec((HEAD_DIM, HEAD_DIM), lambda b, h: (0, 0))],
        out_specs=pl.BlockSpec((None, seq, HEAD_DIM), lambda b, h: (b, 0, h)),
        out_shape=jax.ShapeDtypeStruct((bsz, seq, B_HEADS * HEAD_DIM), BF16),
        scratch_shapes=[pltpu.VMEM((2, 3, seq, HEAD_DIM), F32), pltpu.SemaphoreType.DMA((2,))]
                       + [pltpu.VMEM((seq, HEAD_DIM), F32)] * 5,
        compiler_params=_params(("arbitrary", "arbitrary")),
        name="dilated_attention",
    )(proj3.reshape(bsz, seq // widest, widest, width), _residue_major(tables[0], widest),
      _residue_major(tables[1], widest), tables[2])


SWA_UNROLL = 4


def _swa_kernel(sink_ref, q0_ref, q1_ref, q2_ref, k_ref, v_ref, c_ref, s_ref, rot_ref, o_ref, qr_ref, kr_ref):
    seq = k_ref.shape[0]
    blk = ATT_BLOCK
    grp = pl.program_id(1)
    c, s, rot = c_ref[...], s_ref[...], rot_ref[...]
    for j, q_ref in enumerate((q0_ref, q1_ref, q2_ref)):
        qr_ref[j] = _rope(q_ref[...], c, s, rot).astype(BF16)
    kr_ref[...] = _rope(k_ref[...], c, s, rot).astype(BF16)
    cur_mask, prev_mask = _band_masks(C_MAX_LAG)
    sinks = [jnp.full((blk, 1), sink_ref[grp * C_REP + j] * LOG2_E, F32) for j in range(C_REP)]

    def group(g, carry):
        blocks, dests = [], []
        for u in range(SWA_UNROLL):
            n = g * SWA_UNROLL + u
            rows = pl.ds(pl.multiple_of(n * blk, blk), blk)
            prev = pl.ds(pl.multiple_of(jnp.maximum(n - 1, 0) * blk, blk), blk)
            ks, vs = [kr_ref[rows, :], kr_ref[prev, :]], [v_ref[rows, :], v_ref[prev, :]]
            masks = [cur_mask, jnp.logical_and(prev_mask, n > 0)]
            for j in range(C_REP):
                blocks.append((qr_ref[j, rows, :], ks, vs, masks, sinks[j]))
                dests.append((rows, j))
        for (rows, j), (m, l, acc) in zip(dests, _tile_softmax_many(blocks)):
            out = acc / (l + jnp.exp2(sinks[j] - m))
            o_ref[rows, j * HEAD_DIM:(j + 1) * HEAD_DIM] = out.astype(o_ref.dtype)
        return carry

    lax.fori_loop(0, seq // (blk * SWA_UNROLL), group, 0)


def _swa(proj3, sinks, tables):
    bsz, seq, _ = proj3.shape
    qcol = lambda j: pl.BlockSpec((None, seq, HEAD_DIM), lambda b, g, j=j: (b, 0, QC_BLK + g * C_REP + j))
    col = lambda blk: pl.BlockSpec((None, seq, HEAD_DIM), lambda b, g, blk=blk: (b, 0, blk + g))
    tab = pl.BlockSpec((seq, HEAD_DIM), lambda b, g: (0, 0))
    return pl.pallas_call(
        _swa_kernel,
        grid=(bsz, C_KV_HEADS),
        in_specs=[pl.BlockSpec(memory_space=pltpu.SMEM), qcol(0), qcol(1), qcol(2), col(KC_BLK), col(VC_BLK),
                  tab, tab, pl.BlockSpec((HEAD_DIM, HEAD_DIM), lambda b, g: (0, 0))],
        out_specs=pl.BlockSpec((None, seq, C_REP * HEAD_DIM), lambda b, g: (b, 0, g)),
        out_shape=jax.ShapeDtypeStruct((bsz, seq, C_HEADS * HEAD_DIM), BF16),
        scratch_shapes=[pltpu.VMEM((C_REP, seq, HEAD_DIM), BF16), pltpu.VMEM((seq, HEAD_DIM), BF16)],
        compiler_params=_params(("parallel", "parallel")),
        name="swa_sink_attention",
    )(sinks, proj3, proj3, proj3, proj3, proj3, *tables)


def _layer_norm_store(z, g_ref, b_ref, o32_ref, o16_ref):
    mu = jnp.mean(z, axis=-1, keepdims=True)
    zc = z - mu
    var = jnp.mean(zc * zc, axis=-1, keepdims=True)
    y = zc * lax.rsqrt(var + LN_EPS) * g_ref[...] + b_ref[...]
    o32_ref[...] = y
    o16_ref[...] = y.astype(BF16)


def _stage_weight_bf16(w_hbm, layer, w16_ref, stage_ref, sem):
    chunk = stage_ref.shape[1]
    n_chunks = w16_ref.shape[0] // chunk

    def copy(c):
        slot = c % 2
        return pltpu.make_async_copy(w_hbm.at[layer, pl.ds(c * chunk, chunk), :], stage_ref.at[slot], sem.at[slot])

    copy(0).start()
    for c in range(n_chunks):
        if c + 1 < n_chunks:
            copy(c + 1).start()
        copy(c).wait()
        w16_ref[c * chunk:(c + 1) * chunk, :] = stage_ref[c % 2].astype(BF16)


def _resident_weight_scratch(k, n):
    return [pltpu.VMEM((k, n), BF16), pltpu.VMEM((2, WEIGHT_STAGE_ROWS, n), F32), pltpu.SemaphoreType.DMA((2,))]


def _out_proj_kernel(alpha, layer, oa_ref, ob_ref, oc_ref, w_hbm, x_ref, g_ref, b_ref, o32_ref, o16_ref,
                     w16_ref, stage_ref, sem):
    @pl.when(pl.program_id(0) == 0)
    def _():
        _stage_weight_bf16(w_hbm, layer, w16_ref, stage_ref, sem)

    ka, kb = oa_ref.shape[1], ob_ref.shape[1]
    y = (jnp.dot(oa_ref[...], w16_ref[0:ka, :], preferred_element_type=F32)
         + jnp.dot(ob_ref[...], w16_ref[ka:ka + kb, :], preferred_element_type=F32)
         + jnp.dot(oc_ref[...], w16_ref[ka + kb:, :], preferred_element_type=F32))
    _layer_norm_store(alpha * x_ref[...] + y, g_ref, b_ref, o32_ref, o16_ref)


def _layer_vec(layer, d):
    return pl.BlockSpec((None, 1, d), lambda i, layer=layer: (layer, 0, 0))


def _out_proj_ln(oa, ob, oc, w_out, layer, x, g, b, alpha, bm):
    m, d = x.shape
    rows = lambda a: pl.BlockSpec((bm, a.shape[1]), lambda i: (i, 0))
    return pl.pallas_call(
        functools.partial(_out_proj_kernel, alpha, layer),
        grid=(m // bm,),
        in_specs=[rows(oa), rows(ob), rows(oc), pl.BlockSpec(memory_space=pl.ANY), rows(x),
                  _layer_vec(layer, d), _layer_vec(layer, d)],
        out_specs=[rows(x), rows(x)],
        out_shape=[jax.ShapeDtypeStruct((m, d), F32), jax.ShapeDtypeStruct((m, d), BF16)],
        scratch_shapes=_resident_weight_scratch(w_out.shape[1], d),
        compiler_params=_params(("arbitrary",)),
        name="out_proj_ln",
    )(oa, ob, oc, w_out, x, g, b)


def _in_proj_kernel(x_ref, w_ref, o_ref):
    o_ref[...] = jnp.dot(x_ref[...], w_ref[...].astype(BF16), preferred_element_type=F32).astype(o_ref.dtype)


def _in_proj(x16, w_in, layer, bm, bn):
    m, k = x16.shape
    n = w_in.shape[2]
    return pl.pallas_call(
        _in_proj_kernel,
        grid=(m // bm, n // bn),
        in_specs=[pl.BlockSpec((bm, k), lambda i, j: (i, 0)),
                  pl.BlockSpec((None, k, bn), lambda i, j: (layer, 0, j))],
        out_specs=pl.BlockSpec((bm, bn), lambda i, j: (i, j)),
        out_shape=jax.ShapeDtypeStruct((m, n), F32),
        compiler_params=_params(("parallel", "parallel")),
        name="in_proj",
    )(x16, w_in)


FFN_UP_SUBTILES = 2


def _ffn_up_kernel(tiles_per_seq, x_ref, wg_ref, wu_ref, cw_ref, cb_ref, h_ref, g_ref, w16_ref):
    bm = x_ref.shape[0]
    bn = h_ref.shape[1]
    halo = g_ref.shape[0] - bm

    @pl.when(pl.program_id(1) == 0)
    def _():
        w16_ref[:, 0:bn] = wg_ref[...].astype(BF16)
        w16_ref[:, bn:] = wu_ref[...].astype(BF16)

    @pl.when(pl.program_id(1) % tiles_per_seq == 0)
    def _():
        g_ref[0:halo, :] = jnp.zeros((halo, g_ref.shape[1]), F32)

    sub = bm // FFN_UP_SUBTILES
    for t in range(FFN_UP_SUBTILES):
        first = halo + t * sub
        gu = jnp.dot(x_ref[t * sub:(t + 1) * sub, :], w16_ref[...], preferred_element_type=F32)
        g_ref[first:first + sub, :] = gu[:, 0:bn]
        u = gu[:, bn:]
        gc = cb_ref[...]
        for j in range(CONV_WIDTH):
            lag = CONV_WIDTH - 1 - j
            gc = gc + cw_ref[j:j + 1, :] * g_ref[first - lag:first - lag + sub, :]
        h_ref[t * sub:(t + 1) * sub, :] = (gc * jax.nn.sigmoid(gc) * u).astype(h_ref.dtype)
    g_ref[0:halo, :] = g_ref[bm:bm + halo, :]


def _ffn_up(x16, w_gate, w_up, conv_w, conv_b, layer, seq, bm, bn):
    m, d = x16.shape
    f = w_gate.shape[2]
    halo = SUBLANES
    wspec = pl.BlockSpec((None, d, bn), lambda j, i: (layer, 0, j))
    return pl.pallas_call(
        functools.partial(_ffn_up_kernel, seq // bm),
        grid=(f // bn, m // bm),
        in_specs=[pl.BlockSpec((bm, d), lambda j, i: (i, 0)), wspec, wspec,
                  pl.BlockSpec((None, CONV_WIDTH, bn), lambda j, i: (layer, 0, j)),
                  pl.BlockSpec((None, 1, bn), lambda j, i: (layer, 0, j))],
        out_specs=pl.BlockSpec((bm, bn), lambda j, i: (i, j)),
        out_shape=jax.ShapeDtypeStruct((m, f), BF16),
        scratch_shapes=[pltpu.VMEM((bm + halo, bn), F32), pltpu.VMEM((d, 2 * bn), BF16)],
        compiler_params=_params(("parallel", "arbitrary")),
        name="ffn_up_conv_gate",
    )(x16, w_gate, w_up, conv_w, conv_b)


def _ffn_down_kernel(alpha, layer, h_ref, w_hbm, x_ref, g_ref, b_ref, o32_ref, o16_ref, w16_ref, stage_ref, sem):
    @pl.when(pl.program_id(0) == 0)
    def _():
        _stage_weight_bf16(w_hbm, layer, w16_ref, stage_ref, sem)

    y = jnp.dot(h_ref[...], w16_ref[...], preferred_element_type=F32)
    _layer_norm_store(alpha * x_ref[...] + y, g_ref, b_ref, o32_ref, o16_ref)


def _ffn_down_ln(h, w_down, layer, x, g, b, alpha, bm):
    m, d = x.shape
    f = h.shape[1]
    rows = pl.BlockSpec((bm, d), lambda i: (i, 0))
    return pl.pallas_call(
        functools.partial(_ffn_down_kernel, alpha, layer),
        grid=(m // bm,),
        in_specs=[pl.BlockSpec((bm, f), lambda i: (i, 0)), pl.BlockSpec(memory_space=pl.ANY), rows,
                  _layer_vec(layer, d), _layer_vec(layer, d)],
        out_specs=[rows, rows],
        out_shape=[jax.ShapeDtypeStruct((m, d), F32), jax.ShapeDtypeStruct((m, d), BF16)],
        scratch_shapes=_resident_weight_scratch(f, d),
        compiler_params=_params(("arbitrary",)),
        name="ffn_down_ln",
    )(h, w_down, x, g, b)


def kernel(x, w_in, lb_logits, a_norm_w, c_sinks, w_out, ln1_g, ln1_b, w_gate, w_up, conv_w, conv_b, w_down,
           ln2_g, ln2_b):
    bsz, seq, d = x.shape
    depth = w_in.shape[0]
    alpha = (2 * depth) ** 0.25
    tables = _rope_tables(seq)
    lbs = jnp.cumsum(jax.nn.softmax(lb_logits.astype(F32), axis=0), axis=0)
    lbs = lbs - lbs[0]
    vec3 = lambda a: a.reshape(depth, 1, -1)
    ln1_g, ln1_b, ln2_g, ln2_b, conv_b = vec3(ln1_g), vec3(ln1_b), vec3(ln2_g), vec3(ln2_b), vec3(conv_b)
    x32 = x.reshape(bsz * seq, d)
    x16 = x32.astype(BF16)
    plan = _tile_plan(bsz * seq, d, w_out.shape[1], w_gate.shape[2])
    for l in range(depth):
        proj3 = _in_proj(x16, w_in, l, *plan.in_proj).reshape(bsz, seq, -1)
        oa = _hgrn2(proj3, lbs[l], a_norm_w[l]).reshape(bsz * seq, -1)
        ob = _dilated(proj3, tables).reshape(bsz * seq, -1)
        oc = _swa(proj3, c_sinks[l], tables).reshape(bsz * seq, -1)
        x32, x16 = _out_proj_ln(oa, ob, oc, w_out, l, x32, ln1_g, ln1_b, alpha, plan.out_proj_rows)
        h = _ffn_up(x16, w_gate, w_up, conv_w, conv_b, l, seq, *plan.ffn_up)
        x32, x16 = _ffn_down_ln(h, w_down, l, x32, ln2_g, ln2_b, alpha, plan.ffn_down_rows)
    return x32.reshape(bsz, seq, d)
```

```python
import functools
from typing import NamedTuple

import numpy as np
import jax
import jax.numpy as jnp
from jax import lax
from jax.experimental import pallas as pl
from jax.experimental.pallas import tpu as pltpu

HEAD_DIM = 128
A_HEADS = 4
B_HEADS = 6
C_HEADS = 6
C_KV_HEADS = 2
C_REP = C_HEADS // C_KV_HEADS
DILATIONS = (1, 4, 16)
ATT_BLOCK = 128
C_MAX_LAG = 127
ROPE_THETA = 500000.0
ROPE_DIM = HEAD_DIM // 4
ROPE_HALF = ROPE_DIM // 2
CONV_WIDTH = 3
LN_EPS = 1e-5

QA_BLK, FA_BLK, IA_BLK, GA_BLK = 0, 4, 8, 12
QB_BLK, KB_BLK, VB_BLK = 16, 22, 28
QC_BLK, KC_BLK, VC_BLK = 34, 40, 42

HGRN_CHUNK = 128
HGRN_LEVELS = (64, 32, 16, 8, 4, 2, 1)

VMEM_LIMIT_BYTES = 60 * 1024 * 1024

F32 = jnp.float32
BF16 = jnp.bfloat16
NEG_BIG = -1e30
LOG2_E = 1.4426950408889634


def _params(semantics):
    return pltpu.CompilerParams(dimension_semantics=semantics, vmem_limit_bytes=VMEM_LIMIT_BYTES)


SUBLANES = 8
WEIGHT_STAGE_ROWS = 128


class _TilePlan(NamedTuple):
    in_proj: tuple
    out_proj_rows: int
    ffn_up: tuple
    ffn_down_rows: int


def _tile_plan(rows, d_model, d_mix, d_ff):
    plan = _TilePlan(in_proj=(2048, 512), out_proj_rows=512, ffn_up=(2048, 512), ffn_down_rows=512)
    f32, bf16, two = 4, 2, 2
    staged = two * WEIGHT_STAGE_ROWS * d_model * f32
    bm, bn = plan.in_proj
    in_proj = two * bm * d_model * bf16 + two * d_model * bn * f32 + d_model * bn * bf16 + two * bm * bn * f32
    bm = plan.out_proj_rows
    out_proj = d_mix * d_model * bf16 + staged + two * bm * (d_mix * bf16 + d_model * (f32 + f32 + bf16))
    bm, bn = plan.ffn_up
    ffn_up = (two * bm * d_model * bf16 + two * two * d_model * bn * f32 + d_model * two * bn * bf16
              + (bm + SUBLANES) * bn * f32 + two * bm * bn * bf16 + bm * two * bn * f32)
    bm = plan.ffn_down_rows
    ffn_down = d_ff * d_model * bf16 + staged + two * bm * (d_ff * bf16 + d_model * (f32 + f32 + bf16))
    for name, need in (("in_proj", in_proj), ("out_proj", out_proj), ("ffn_up", ffn_up), ("ffn_down", ffn_down)):
        assert need <= VMEM_LIMIT_BYTES, (name, need)
    assert rows % max(plan.in_proj[0], plan.ffn_up[0]) == 0 and d_ff % plan.ffn_up[1] == 0
    return plan


def _rope_tables(seq):
    inv = ROPE_THETA ** (-jnp.arange(0, ROPE_DIM, 2, dtype=F32) / ROPE_DIM)
    ang = jnp.arange(seq, dtype=F32)[:, None] * inv[None, :]
    cos, sin = jnp.cos(ang), jnp.sin(ang)
    rest = HEAD_DIM - ROPE_DIM
    c = jnp.concatenate([cos, cos, jnp.ones((seq, rest), F32)], -1)
    s = jnp.concatenate([sin, sin, jnp.zeros((seq, rest), F32)], -1)
    rot = np.zeros((HEAD_DIM, HEAD_DIM), np.float32)
    lane = np.arange(ROPE_HALF)
    rot[lane + ROPE_HALF, lane] = -1.0
    rot[lane, lane + ROPE_HALF] = 1.0
    return c, s, jnp.asarray(rot, BF16)


def _rope(x, c, s, rot):
    hi = x.astype(BF16)
    lo = (x - hi.astype(F32)).astype(BF16)
    swapped = jnp.dot(hi, rot, preferred_element_type=F32) + jnp.dot(lo, rot, preferred_element_type=F32)
    return x * c + swapped * s


def _hgrn_constants():
    c = HGRN_CHUNK
    i = np.arange(c)[:, None]
    j = np.arange(c)[None, :]
    lvl = np.full((c, c), -1, np.int32)
    for n, h in enumerate(HGRN_LEVELS):
        lvl[(i // (2 * h) == j // (2 * h)) & (i // h != j // h) & (i > j)] = n
    lvl[np.arange(c), np.arange(c)] = len(HGRN_LEVELS)
    return jnp.asarray((j <= i).astype(np.float32), BF16), jnp.asarray(lvl)


def _midpoint_log_decay(lf, b, h, row):
    c, dk = b.shape
    odd = ((row // h) % 2) == 1
    if h == 1:
        return jnp.where(odd, lf, 0.0)
    if h == 2:
        below = pltpu.roll(lf, c - 1, 0)
        above = pltpu.roll(lf, 1, 0)
        place = row % 4
        return jnp.where(place == 0, below, jnp.where(place == 1, 0.0, jnp.where(place == 2, lf, lf + above)))
    blocks = b.reshape(c // (2 * h), 2 * h, dk)
    mid = jnp.broadcast_to(blocks[:, h - 1:h, :], blocks.shape).reshape(c, dk)
    return jnp.where(odd, b - mid, mid - b)


def _split3(x):
    p1 = x.astype(BF16)
    r1 = x - p1.astype(F32)
    p2 = r1.astype(BF16)
    p3 = (r1 - p2.astype(F32)).astype(BF16)
    return p1, p2, p3


def _dot_nt(a, b, **kw):
    return lax.dot_general(a, b, (((1,), (1,)), ((), ())), preferred_element_type=F32, **kw)


def _dot_tn(a, b, **kw):
    return lax.dot_general(a, b, (((0,), (0,)), ((), ())), preferred_element_type=F32, **kw)


def _hgrn_chunk_heads(inputs, nw, e_all, lvl, row):
    c = HGRN_CHUNK
    heads = range(len(inputs))
    qs = [q * jax.nn.sigmoid(q) for q, _, _, _, _, _ in inputs]
    fg = [lb + (1.0 - lb) * jax.nn.sigmoid(fl) for _, fl, _, _, _, lb in inputs]
    kk = [1.0 - f for f in fg]
    lf = [jnp.log2(f) for f in fg]
    parts = [_split3(x) for x in lf]
    b12 = [jnp.dot(e_all, jnp.concatenate([p1, p2], axis=1), preferred_element_type=F32) for p1, p2, _ in parts]
    b3 = [jnp.dot(e_all, p3, preferred_element_type=F32) for _, _, p3 in parts]
    b = [x[:, :HEAD_DIM] + x[:, HEAD_DIM:] + y for x, y in zip(b12, b3)]
    att = [jnp.where(lvl == len(HGRN_LEVELS), jnp.sum(qs[i] * kk[i], axis=-1, keepdims=True), 0.0) for i in heads]
    for n, h in enumerate(HGRN_LEVELS):
        odd = ((row // h) % 2) == 1
        xh = [(jnp.where(odd, qs[i], kk[i]) * jnp.exp2(_midpoint_log_decay(lf[i], b[i], h, row))).astype(BF16)
              for i in heads]
        pair = [_dot_nt(x, x) for x in xh]
        att = [jnp.where(lvl == n, pair[i], att[i]) for i in heads]
    b_end = [x[c - 1:c] for x in b]
    q_in = [(qs[i] * jnp.exp2(b[i])).astype(BF16) for i in heads]
    k_end = [(kk[i] * jnp.exp2(b_end[i] - b[i])).astype(BF16) for i in heads]
    v16 = [v.astype(BF16) for _, _, v, _, _, _ in inputs]
    o = [jnp.dot(att[i].astype(BF16), v16[i], preferred_element_type=F32)
         + _dot_nt(q_in[i], inputs[i][4].astype(BF16)) for i in heads]
    st_new = [inputs[i][4] * jnp.exp2(b_end[i]) + _dot_tn(v16[i], k_end[i]) for i in heads]
    o = [x * lax.rsqrt(jnp.mean(x * x, axis=-1, keepdims=True) + LN_EPS) * nw for x in o]
    out = [(o[i] * (inputs[i][3] * jax.nn.sigmoid(inputs[i][3]))).astype(BF16) for i in heads]
    return list(zip(out, st_new))


HGRN_BATCH = 2
HGRN_SEQ_TILE = 1024


def _hgrn_kernel(q_ref, f_ref, i_ref, g_ref, lb_ref, nw_ref, e_ref, lvl_ref, o_ref, st_ref):
    c = HGRN_CHUNK
    n_batch, tile_rows = q_ref.shape[0], q_ref.shape[1]

    @pl.when(pl.program_id(1) == 0)
    def _():
        st_ref[...] = jnp.zeros_like(st_ref)

    nw = nw_ref[...]
    row = lax.broadcasted_iota(jnp.int32, (c, 1), 0)

    def chunk(ci, carry):
        rows = pl.ds(pl.multiple_of(ci * c, c), c)
        e_all, lvl = e_ref[...], lvl_ref[...]
        inputs = []
        for n in range(n_batch):
            for h in range(A_HEADS):
                cols = slice(h * HEAD_DIM, (h + 1) * HEAD_DIM)
                inputs.append((q_ref[n, rows, cols], f_ref[n, rows, cols], i_ref[n, rows, cols], g_ref[n, rows, cols],
                               st_ref[n * A_HEADS + h], lb_ref[:, cols]))
        for k, (o, st_new) in enumerate(_hgrn_chunk_heads(inputs, nw, e_all, lvl, row)):
            n, h = divmod(k, A_HEADS)
            o_ref[n, rows, h * HEAD_DIM:(h + 1) * HEAD_DIM] = o
            st_ref[k] = st_new
        return carry

    lax.fori_loop(0, tile_rows // c, chunk, 0)


def _hgrn2(proj3, lb, norm_w):
    bsz, seq, _ = proj3.shape
    e_all, lvl = _hgrn_constants()
    width = A_HEADS * HEAD_DIM
    col = lambda blk: pl.BlockSpec((HGRN_BATCH, HGRN_SEQ_TILE, width), lambda b, s, blk=blk: (b, s, blk // A_HEADS))
    const = lambda shape: pl.BlockSpec(shape, lambda b, s: (0,) * len(shape))
    return pl.pallas_call(
        _hgrn_kernel,
        grid=(bsz // HGRN_BATCH, seq // HGRN_SEQ_TILE),
        in_specs=[col(QA_BLK), col(FA_BLK), col(IA_BLK), col(GA_BLK),
                  const((1, width)), const((1, HEAD_DIM)), const(e_all.shape), const(lvl.shape)],
        out_specs=pl.BlockSpec((HGRN_BATCH, HGRN_SEQ_TILE, width), lambda b, s: (b, s, 0)),
        out_shape=jax.ShapeDtypeStruct((bsz, seq, width), BF16),
        scratch_shapes=[pltpu.VMEM((HGRN_BATCH * A_HEADS, HEAD_DIM, HEAD_DIM), F32)],
        compiler_params=_params(("parallel", "arbitrary")),
        name="hgrn2",
    )(proj3, proj3, proj3, proj3, lb.reshape(1, width), norm_w.reshape(1, HEAD_DIM), e_all, lvl)


def _band_masks(max_lag):
    r = lax.broadcasted_iota(jnp.int32, (ATT_BLOCK, ATT_BLOCK), 0)
    col = lax.broadcasted_iota(jnp.int32, (ATT_BLOCK, ATT_BLOCK), 1)
    return col <= r, (ATT_BLOCK + r - col) <= max_lag


def _tile_softmax_many(blocks):
    scale = HEAD_DIM ** -0.5 * LOG2_E
    s_all = [[jnp.where(mask, _dot_nt(q16, k.astype(BF16)) * scale, NEG_BIG) for k, mask in zip(ks, masks)]
             for q16, ks, _, masks, _ in blocks]
    m_all = []
    for s, (_, _, _, _, m_floor) in zip(s_all, blocks):
        m = jnp.max(functools.reduce(jnp.maximum, s), axis=-1, keepdims=True)
        m_all.append(m if m_floor is None else jnp.maximum(m, m_floor))
    e_all = [[jnp.exp2(si - m) for si in s] for s, m in zip(s_all, m_all)]
    acc_all = [functools.reduce(jnp.add, [jnp.dot(ei.astype(BF16), v.astype(BF16), preferred_element_type=F32)
                                          for ei, v in zip(e, vs)])
               for e, (_, _, vs, _, _) in zip(e_all, blocks)]
    l_all = [jnp.sum(functools.reduce(jnp.add, e), axis=-1, keepdims=True) for e in e_all]
    return list(zip(m_all, l_all, acc_all))


DIL_UNROLL = 16


def _dilated_kernel(proj_hbm, c_ref, s_ref, rot_ref, o_ref, qkv_ref, sem, qr_ref, kr_ref, m_ref, l_ref, acc_ref):
    seq = o_ref.shape[0]
    blk = ATT_BLOCK
    widest = DILATIONS[-1]
    per_residue = seq // widest
    step = pl.program_id(0) * pl.num_programs(1) + pl.program_id(1)
    n_steps = pl.num_programs(0) * pl.num_programs(1)
    slot = step % 2

    def qkv_copies(of_step, into_slot):
        b, h = of_step // pl.num_programs(1), of_step % pl.num_programs(1)
        copies = []
        for which, first_blk in enumerate((QB_BLK, KB_BLK, VB_BLK)):
            col = pl.multiple_of((first_blk + h) * HEAD_DIM, HEAD_DIM)
            for r in range(widest):
                copies.append(pltpu.make_async_copy(
                    proj_hbm.at[b, :, r, pl.ds(col, HEAD_DIM)],
                    qkv_ref.at[into_slot, which, pl.ds(r * per_residue, per_residue), :],
                    sem.at[into_slot]))
        return copies

    @pl.when(step == 0)
    def _():
        for copy in qkv_copies(step, slot):
            copy.start()

    @pl.when(step + 1 < n_steps)
    def _():
        for copy in qkv_copies(step + 1, 1 - slot):
            copy.start()

    for copy in qkv_copies(step, slot):
        copy.wait()

    c, s, rot = c_ref[...], s_ref[...], rot_ref[...]
    qr_ref[...] = _rope(qkv_ref[slot, 0], c, s, rot)
    kr_ref[...] = _rope(qkv_ref[slot, 1], c, s, rot)
    vr_ref = qkv_ref.at[slot, 2]

    def chunk_rows(dil):
        return blk * dil // widest

    def block_masks(dil):
        def position(i):
            return (widest // dil) * (i % chunk_rows(dil)) + i // chunk_rows(dil)
        row = position(lax.broadcasted_iota(jnp.int32, (blk, blk), 0))
        col = position(lax.broadcasted_iota(jnp.int32, (blk, blk), 1))
        return col <= row, col >= row

    def chunks_of(dil, r, n):
        size = chunk_rows(dil)
        return [pl.ds(pl.multiple_of((r + dil * a) * per_residue + size * n, size), size)
                for a in range(widest // dil)]

    def gather(ref, chunks):
        return jnp.concatenate([ref[rows, :] for rows in chunks], axis=0)

    def scatter(ref, chunks, value):
        size = value.shape[0] // len(chunks)
        for a, rows in enumerate(chunks):
            ref[rows, :] = value[a * size:(a + 1) * size]

    def run_group(specs, masks_of_pattern, first):
        cur_mask, prev_mask = masks_of_pattern
        loaded = []
        for rows, prev, prev_ok in specs:
            ks, vs, masks = [gather(kr_ref, rows)], [gather(vr_ref, rows)], [cur_mask]
            if prev is not None:
                ks.append(gather(kr_ref, prev))
                vs.append(gather(vr_ref, prev))
                masks.append(jnp.logical_and(prev_mask, prev_ok))
            state = None if first else (gather(m_ref, rows), gather(l_ref, rows), gather(acc_ref, rows))
            loaded.append((gather(qr_ref, rows).astype(BF16), ks, vs, masks, state))
        pieces = _tile_softmax_many([(q16, ks, vs, masks, None if state is None else state[0])
                                     for q16, ks, vs, masks, state in loaded])
        results = []
        for (m, l, acc), (_, _, _, _, state) in zip(pieces, loaded):
            if state is None:
                shape = (blk, HEAD_DIM)
                results.append((jnp.broadcast_to(m, shape), jnp.broadcast_to(l, shape), acc))
            else:
                m_old, l_old, acc_old = state
                alpha = jnp.exp2(m_old - m)
                results.append((m, alpha * l_old + l, alpha * acc_old + acc))
        for (rows, _, _), (m, l, acc) in zip(specs, results):
            scatter(m_ref, rows, m)
            scatter(l_ref, rows, l)
            scatter(acc_ref, rows, acc)

    n_query_blocks = seq // blk
    for dil in DILATIONS:
        n_blocks = n_query_blocks // dil
        masks_of_pattern = block_masks(dil)

        def group(g, carry, dil=dil, n_blocks=n_blocks, masks_of_pattern=masks_of_pattern):
            specs = []
            for u in range(DIL_UNROLL):
                e = g * DIL_UNROLL + u
                r, n = e % dil, e // dil
                if n_blocks == 1:
                    specs.append((chunks_of(dil, r, n), None, None))
                else:
                    specs.append((chunks_of(dil, r, n), chunks_of(dil, r, jnp.maximum(n - 1, 0)), n > 0))
            run_group(specs, masks_of_pattern, first=(dil == DILATIONS[0]))
            return carry

        lax.fori_loop(0, n_query_blocks // DIL_UNROLL, group, 0)

    for r in range(widest):
        rows = slice(r * per_residue, (r + 1) * per_residue)
        qr_ref[pl.ds(r, per_residue, stride=widest), :] = acc_ref[rows, :] / l_ref[rows, :]
    o_ref[...] = qr_ref[...].astype(o_ref.dtype)


def _residue_major(table, widest):
    seq, width = table.shape
    return table.reshape(seq // widest, widest, width).transpose(1, 0, 2).reshape(seq, width)


def _dilated(proj3, tables):
    bsz, seq, width = proj3.shape
    widest = DILATIONS[-1]
    tab = pl.BlockSpec((seq, HEAD_DIM), lambda b, h: (0, 0))
    return pl.pallas_call(
        _dilated_kernel,
        grid=(bsz, B_HEADS),
        in_specs=[pl.BlockSpec(memory_space=pl.ANY), tab, tab, pl.BlockSpec((HEAD_DIM, HEAD_DIM), lambda b, h: (0, 0))],
        out_specs=pl.BlockSpec((None, seq, HEAD_DIM), lambda b, h: (b, 0, h)),
        out_shape=jax.ShapeDtypeStruct((bsz, seq, B_HEADS * HEAD_DIM), BF16),
        scratch_shapes=[pltpu.VMEM((2, 3, seq, HEAD_DIM), F32), pltpu.SemaphoreType.DMA((2,))]
                       + [pltpu.VMEM((seq, HEAD_DIM), F32)] * 5,
        compiler_params=_params(("arbitrary", "arbitrary")),
        name="dilated_attention",
    )(proj3.reshape(bsz, seq // widest, widest, width), _residue_major(tables[0], widest),
      _residue_major(tables[1], widest), tables[2])


SWA_UNROLL = 4


def _swa_kernel(sink_ref, q0_ref, q1_ref, q2_ref, k_ref, v_ref, c_ref, s_ref, rot_ref, o_ref, qr_ref, kr_ref):
    seq = k_ref.shape[0]
    blk = ATT_BLOCK
    grp = pl.program_id(1)
    c, s, rot = c_ref[...], s_ref[...], rot_ref[...]
    for j, q_ref in enumerate((q0_ref, q1_ref, q2_ref)):
        qr_ref[j] = _rope(q_ref[...], c, s, rot).astype(BF16)
    kr_ref[...] = _rope(k_ref[...], c, s, rot).astype(BF16)
    cur_mask, prev_mask = _band_masks(C_MAX_LAG)
    sinks = [jnp.full((blk, 1), sink_ref[grp * C_REP + j] * LOG2_E, F32) for j in range(C_REP)]

    def group(g, carry):
        blocks, dests = [], []
        for u in range(SWA_UNROLL):
            n = g * SWA_UNROLL + u
            rows = pl.ds(pl.multiple_of(n * blk, blk), blk)
            prev = pl.ds(pl.multiple_of(jnp.maximum(n - 1, 0) * blk, blk), blk)
            ks, vs = [kr_ref[rows, :], kr_ref[prev, :]], [v_ref[rows, :], v_ref[prev, :]]
            masks = [cur_mask, jnp.logical_and(prev_mask, n > 0)]
            for j in range(C_REP):
                blocks.append((qr_ref[j, rows, :], ks, vs, masks, sinks[j]))
                dests.append((rows, j))
        for (rows, j), (m, l, acc) in zip(dests, _tile_softmax_many(blocks)):
            out = acc / (l + jnp.exp2(sinks[j] - m))
            o_ref[rows, j * HEAD_DIM:(j + 1) * HEAD_DIM] = out.astype(o_ref.dtype)
        return carry

    lax.fori_loop(0, seq // (blk * SWA_UNROLL), group, 0)


def _swa(proj3, sinks, tables):
    bsz, seq, _ = proj3.shape
    qcol = lambda j: pl.BlockSpec((None, seq, HEAD_DIM), lambda b, g, j=j: (b, 0, QC_BLK + g * C_REP + j))
    col = lambda blk: pl.BlockSpec((None, seq, HEAD_DIM), lambda b, g, blk=blk: (b, 0, blk + g))
    tab = pl.BlockSpec((seq, HEAD_DIM), lambda b, g: (0, 0))
    return pl.pallas_call(
        _swa_kernel,
        grid=(bsz, C_KV_HEADS),
        in_specs=[pl.BlockSpec(memory_space=pltpu.SMEM), qcol(0), qcol(1), qcol(2), col(KC_BLK), col(VC_BLK),
                  tab, tab, pl.BlockSpec((HEAD_DIM, HEAD_DIM), lambda b, g: (0, 0))],
        out_specs=pl.BlockSpec((None, seq, C_REP * HEAD_DIM), lambda b, g: (b, 0, g)),
        out_shape=jax.ShapeDtypeStruct((bsz, seq, C_HEADS * HEAD_DIM), BF16),
        scratch_shapes=[pltpu.VMEM((C_REP, seq, HEAD_DIM), BF16), pltpu.VMEM((seq, HEAD_DIM), BF16)],
        compiler_params=_params(("parallel", "parallel")),
        name="swa_sink_attention",
    )(sinks, proj3, proj3, proj3, proj3, proj3, *tables)


def _layer_norm_store(z, g_ref, b_ref, o32_ref, o16_ref):
    mu = jnp.mean(z, axis=-1, keepdims=True)
    zc = z - mu
    var = jnp.mean(zc * zc, axis=-1, keepdims=True)
    y = zc * lax.rsqrt(var + LN_EPS) * g_ref[...] + b_ref[...]
    o32_ref[...] = y
    o16_ref[...] = y.astype(BF16)


def _stage_weight_bf16(w_hbm, layer, w16_ref, stage_ref, sem):
    chunk = stage_ref.shape[1]
    n_chunks = w16_ref.shape[0] // chunk

    def copy(c):
        slot = c % 2
        return pltpu.make_async_copy(w_hbm.at[layer, pl.ds(c * chunk, chunk), :], stage_ref.at[slot], sem.at[slot])

    copy(0).start()
    for c in range(n_chunks):
        if c + 1 < n_chunks:
            copy(c + 1).start()
        copy(c).wait()
        w16_ref[c * chunk:(c + 1) * chunk, :] = stage_ref[c % 2].astype(BF16)


def _resident_weight_scratch(k, n):
    return [pltpu.VMEM((k, n), BF16), pltpu.VMEM((2, WEIGHT_STAGE_ROWS, n), F32), pltpu.SemaphoreType.DMA((2,))]


def _out_proj_kernel(alpha, layer, oa_ref, ob_ref, oc_ref, w_hbm, x_ref, g_ref, b_ref, o32_ref, o16_ref,
                     w16_ref, stage_ref, sem):
    @pl.when(pl.program_id(0) == 0)
    def _():
        _stage_weight_bf16(w_hbm, layer, w16_ref, stage_ref, sem)

    ka, kb = oa_ref.shape[1], ob_ref.shape[1]
    y = (jnp.dot(oa_ref[...], w16_ref[0:ka, :], preferred_element_type=F32)
         + jnp.dot(ob_ref[...], w16_ref[ka:ka + kb, :], preferred_element_type=F32)
         + jnp.dot(oc_ref[...], w16_ref[ka + kb:, :], preferred_element_type=F32))
    _layer_norm_store(alpha * x_ref[...] + y, g_ref, b_ref, o32_ref, o16_ref)


def _layer_vec(layer, d):
    return pl.BlockSpec((None, 1, d), lambda i, layer=layer: (layer, 0, 0))


def _out_proj_ln(oa, ob, oc, w_out, layer, x, g, b, alpha, bm):
    m, d = x.shape
    rows = lambda a: pl.BlockSpec((bm, a.shape[1]), lambda i: (i, 0))
    return pl.pallas_call(
        functools.partial(_out_proj_kernel, alpha, layer),
        grid=(m // bm,),
        in_specs=[rows(oa), rows(ob), rows(oc), pl.BlockSpec(memory_space=pl.ANY), rows(x),
                  _layer_vec(layer, d), _layer_vec(layer, d)],
        out_specs=[rows(x), rows(x)],
        out_shape=[jax.ShapeDtypeStruct((m, d), F32), jax.ShapeDtypeStruct((m, d), BF16)],
        scratch_shapes=_resident_weight_scratch(w_out.shape[1], d),
        compiler_params=_params(("arbitrary",)),
        name="out_proj_ln",
    )(oa, ob, oc, w_out, x, g, b)


def _in_proj_kernel(x_ref, w_ref, o_ref):
    o_ref[...] = jnp.dot(x_ref[...], w_ref[...].astype(BF16), preferred_element_type=F32).astype(o_ref.dtype)


def _in_proj(x16, w_in, layer, bm, bn):
    m, k = x16.shape
    n = w_in.shape[2]
    return pl.pallas_call(
        _in_proj_kernel,
        grid=(m // bm, n // bn),
        in_specs=[pl.BlockSpec((bm, k), lambda i, j: (i, 0)),
                  pl.BlockSpec((None, k, bn), lambda i, j: (layer, 0, j))],
        out_specs=pl.BlockSpec((bm, bn), lambda i, j: (i, j)),
        out_shape=jax.ShapeDtypeStruct((m, n), F32),
        compiler_params=_params(("parallel", "parallel")),
        name="in_proj",
    )(x16, w_in)


FFN_UP_SUBTILES = 2


def _ffn_up_kernel(tiles_per_seq, x_ref, wg_ref, wu_ref, cw_ref, cb_ref, h_ref, g_ref, w16_ref):
    bm = x_ref.shape[0]
    bn = h_ref.shape[1]
    halo = g_ref.shape[0] - bm

    @pl.when(pl.program_id(1) == 0)
    def _():
        w16_ref[:, 0:bn] = wg_ref[...].astype(BF16)
        w16_ref[:, bn:] = wu_ref[...].astype(BF16)

    @pl.when(pl.program_id(1) % tiles_per_seq == 0)
    def _():
        g_ref[0:halo, :] = jnp.zeros((halo, g_ref.shape[1]), F32)

    sub = bm // FFN_UP_SUBTILES
    for t in range(FFN_UP_SUBTILES):
        first = halo + t * sub
        gu = jnp.dot(x_ref[t * sub:(t + 1) * sub, :], w16_ref[...], preferred_element_type=F32)
        g_ref[first:first + sub, :] = gu[:, 0:bn]
        u = gu[:, bn:]
        gc = cb_ref[...]
        for j in range(CONV_WIDTH):
            lag = CONV_WIDTH - 1 - j
            gc = gc + cw_ref[j:j + 1, :] * g_ref[first - lag:first - lag + sub, :]
        h_ref[t * sub:(t + 1) * sub, :] = (gc * jax.nn.sigmoid(gc) * u).astype(h_ref.dtype)
    g_ref[0:halo, :] = g_ref[bm:bm + halo, :]


def _ffn_up(x16, w_gate, w_up, conv_w, conv_b, layer, seq, bm, bn):
    m, d = x16.shape
    f = w_gate.shape[2]
    halo = SUBLANES
    wspec = pl.BlockSpec((None, d, bn), lambda j, i: (layer, 0, j))
    return pl.pallas_call(
        functools.partial(_ffn_up_kernel, seq // bm),
        grid=(f // bn, m // bm),
        in_specs=[pl.BlockSpec((bm, d), lambda j, i: (i, 0)), wspec, wspec,
                  pl.BlockSpec((None, CONV_WIDTH, bn), lambda j, i: (layer, 0, j)),
                  pl.BlockSpec((None, 1, bn), lambda j, i: (layer, 0, j))],
        out_specs=pl.BlockSpec((bm, bn), lambda j, i: (i, j)),
        out_shape=jax.ShapeDtypeStruct((m, f), BF16),
        scratch_shapes=[pltpu.VMEM((bm + halo, bn), F32), pltpu.VMEM((d, 2 * bn), BF16)],
        compiler_params=_params(("parallel", "arbitrary")),
        name="ffn_up_conv_gate",
    )(x16, w_gate, w_up, conv_w, conv_b)


FFN_DOWN_SUBTILES = 2


def _ffn_down_kernel(alpha, layer, h_ref, w_hbm, x_ref, g_ref, b_ref, o32_ref, o16_ref, w16_ref, stage_ref, sem):
    @pl.when(pl.program_id(0) == 0)
    def _():
        _stage_weight_bf16(w_hbm, layer, w16_ref, stage_ref, sem)

    sub = x_ref.shape[0] // FFN_DOWN_SUBTILES
    for t in range(FFN_DOWN_SUBTILES):
        rows = slice(t * sub, (t + 1) * sub)
        y = jnp.dot(h_ref[rows, :], w16_ref[...], preferred_element_type=F32)
        _layer_norm_store(alpha * x_ref[rows, :] + y, g_ref, b_ref, o32_ref.at[rows, :], o16_ref.at[rows, :])


def _ffn_down_ln(h, w_down, layer, x, g, b, alpha, bm):
    m, d = x.shape
    f = h.shape[1]
    rows = pl.BlockSpec((bm, d), lambda i: (i, 0))
    return pl.pallas_call(
        functools.partial(_ffn_down_kernel, alpha, layer),
        grid=(m // bm,),
        in_specs=[pl.BlockSpec((bm, f), lambda i: (i, 0)), pl.BlockSpec(memory_space=pl.ANY), rows,
                  _layer_vec(layer, d), _layer_vec(layer, d)],
        out_specs=[rows, rows],
        out_shape=[jax.ShapeDtypeStruct((m, d), F32), jax.ShapeDtypeStruct((m, d), BF16)],
        scratch_shapes=_resident_weight_scratch(f, d),
        compiler_params=_params(("arbitrary",)),
        name="ffn_down_ln",
    )(h, w_down, x, g, b)


def kernel(x, w_in, lb_logits, a_norm_w, c_sinks, w_out, ln1_g, ln1_b, w_gate, w_up, conv_w, conv_b, w_down,
           ln2_g, ln2_b):
    bsz, seq, d = x.shape
    depth = w_in.shape[0]
    alpha = (2 * depth) ** 0.25
    tables = _rope_tables(seq)
    lbs = jnp.cumsum(jax.nn.softmax(lb_logits.astype(F32), axis=0), axis=0)
    lbs = lbs - lbs[0]
    vec3 = lambda a: a.reshape(depth, 1, -1)
    ln1_g, ln1_b, ln2_g, ln2_b, conv_b = vec3(ln1_g), vec3(ln1_b), vec3(ln2_g), vec3(ln2_b), vec3(conv_b)
    x32 = x.reshape(bsz * seq, d)
    x16 = x32.astype(BF16)
    plan = _tile_plan(bsz * seq, d, w_out.shape[1], w_gate.shape[2])
    for l in range(depth):
        proj3 = _in_proj(x16, w_in, l, *plan.in_proj).reshape(bsz, seq, -1)
        oa = _hgrn2(proj3, lbs[l], a_norm_w[l]).reshape(bsz * seq, -1)
        ob = _dilated(proj3, tables).reshape(bsz * seq, -1)
        oc = _swa(proj3, c_sinks[l], tables).reshape(bsz * seq, -1)
        x32, x16 = _out_proj_ln(oa, ob, oc, w_out, l, x32, ln1_g, ln1_b, alpha, plan.out_proj_rows)
        h = _ffn_up(x16, w_gate, w_up, conv_w, conv_b, l, seq, *plan.ffn_up)
        x32, x16 = _ffn_down_ln(h, w_down, l, x32, ln2_g, ln2_b, alpha, plan.ffn_down_rows)
    return x32.reshape(bsz, seq, d)
```

```python
import functools
from typing import NamedTuple

import numpy as np
import jax
import jax.numpy as jnp
from jax import lax
from jax.experimental import pallas as pl
from jax.experimental.pallas import tpu as pltpu

HEAD_DIM = 128
A_HEADS = 4
B_HEADS = 6
C_HEADS = 6
C_KV_HEADS = 2
C_REP = C_HEADS // C_KV_HEADS
DILATIONS = (1, 4, 16)
ATT_BLOCK = 128
C_MAX_LAG = 127
ROPE_THETA = 500000.0
ROPE_DIM = HEAD_DIM // 4
ROPE_HALF = ROPE_DIM // 2
CONV_WIDTH = 3
LN_EPS = 1e-5

QA_BLK, FA_BLK, IA_BLK, GA_BLK = 0, 4, 8, 12
QB_BLK, KB_BLK, VB_BLK = 16, 22, 28
QC_BLK, KC_BLK, VC_BLK = 34, 40, 42

HGRN_CHUNK = 128
HGRN_LEVELS = (64, 32, 16, 8, 4, 2, 1)

VMEM_LIMIT_BYTES = 56 * 1024 * 1024

F32 = jnp.float32
BF16 = jnp.bfloat16
NEG_BIG = -1e30
LOG2_E = 1.4426950408889634


def _params(semantics):
    return pltpu.CompilerParams(dimension_semantics=semantics, vmem_limit_bytes=VMEM_LIMIT_BYTES)


SUBLANES = 8
WEIGHT_STAGE_ROWS = 512


class _TilePlan(NamedTuple):
    in_proj: tuple
    out_proj_rows: int
    ffn_up: tuple
    ffn_down_rows: int


def _tile_plan(rows, d_model, d_mix, d_ff):
    plan = _TilePlan(in_proj=(2048, 512), out_proj_rows=512, ffn_up=(2048, 512), ffn_down_rows=256)
    f32, bf16, two = 4, 2, 2
    staged = two * WEIGHT_STAGE_ROWS * d_model * f32
    bm, bn = plan.in_proj
    in_proj = two * bm * d_model * bf16 + two * d_model * bn * f32 + d_model * bn * bf16 + two * bm * bn * f32
    bm = plan.out_proj_rows
    out_proj = d_mix * d_model * bf16 + staged + two * bm * (d_mix * bf16 + d_model * (f32 + f32 + bf16))
    bm, bn = plan.ffn_up
    ffn_up = (two * bm * d_model * bf16 + two * two * d_model * bn * f32 + d_model * two * bn * bf16
              + (bm + SUBLANES) * bn * f32 + two * bm * bn * bf16 + bm * two * bn * f32)
    bm = plan.ffn_down_rows
    ffn_down = d_ff * d_model * bf16 + staged + two * bm * (d_ff * bf16 + d_model * (f32 + f32 + bf16))
    for name, need in (("in_proj", in_proj), ("out_proj", out_proj), ("ffn_up", ffn_up), ("ffn_down", ffn_down)):
        assert need <= VMEM_LIMIT_BYTES, (name, need)
    assert rows % max(plan.in_proj[0], plan.ffn_up[0]) == 0 and d_ff % plan.ffn_up[1] == 0
    return plan


def _rope_tables(seq):
    inv = ROPE_THETA ** (-jnp.arange(0, ROPE_DIM, 2, dtype=F32) / ROPE_DIM)
    ang = jnp.arange(seq, dtype=F32)[:, None] * inv[None, :]
    cos, sin = jnp.cos(ang), jnp.sin(ang)
    rest = HEAD_DIM - ROPE_DIM
    c = jnp.concatenate([cos, cos, jnp.ones((seq, rest), F32)], -1)
    s = jnp.concatenate([sin, sin, jnp.zeros((seq, rest), F32)], -1)
    rot = np.zeros((HEAD_DIM, HEAD_DIM), np.float32)
    lane = np.arange(ROPE_HALF)
    rot[lane + ROPE_HALF, lane] = -1.0
    rot[lane, lane + ROPE_HALF] = 1.0
    return c, s, jnp.asarray(rot, BF16)


def _rope(x, c, s, rot):
    hi = x.astype(BF16)
    lo = (x - hi.astype(F32)).astype(BF16)
    swapped = jnp.dot(hi, rot, preferred_element_type=F32) + jnp.dot(lo, rot, preferred_element_type=F32)
    return x * c + swapped * s


def _hgrn_constants():
    c = HGRN_CHUNK
    i = np.arange(c)[:, None]
    j = np.arange(c)[None, :]
    lvl = np.full((c, c), -1, np.int32)
    for n, h in enumerate(HGRN_LEVELS):
        lvl[(i // (2 * h) == j // (2 * h)) & (i // h != j // h) & (i > j)] = n
    lvl[np.arange(c), np.arange(c)] = len(HGRN_LEVELS)
    return jnp.asarray((j <= i).astype(np.float32), BF16), jnp.asarray(lvl)


def _midpoint_log_decay(lf, b, h, row):
    c, dk = b.shape
    odd = ((row // h) % 2) == 1
    if h == 1:
        return jnp.where(odd, lf, 0.0)
    if h == 2:
        below = pltpu.roll(lf, c - 1, 0)
        above = pltpu.roll(lf, 1, 0)
        place = row % 4
        return jnp.where(place == 0, below, jnp.where(place == 1, 0.0, jnp.where(place == 2, lf, lf + above)))
    blocks = b.reshape(c // (2 * h), 2 * h, dk)
    mid = jnp.broadcast_to(blocks[:, h - 1:h, :], blocks.shape).reshape(c, dk)
    return jnp.where(odd, b - mid, mid - b)


def _split3(x):
    p1 = x.astype(BF16)
    r1 = x - p1.astype(F32)
    p2 = r1.astype(BF16)
    p3 = (r1 - p2.astype(F32)).astype(BF16)
    return p1, p2, p3


def _dot_nt(a, b, **kw):
    return lax.dot_general(a, b, (((1,), (1,)), ((), ())), preferred_element_type=F32, **kw)


def _dot_tn(a, b, **kw):
    return lax.dot_general(a, b, (((0,), (0,)), ((), ())), preferred_element_type=F32, **kw)


def _hgrn_chunk_heads(inputs, nw, e_all, lvl, row):
    c = HGRN_CHUNK
    heads = range(len(inputs))
    qs = [q * jax.nn.sigmoid(q) for q, _, _, _, _, _ in inputs]
    fg = [lb + (1.0 - lb) * jax.nn.sigmoid(fl) for _, fl, _, _, _, lb in inputs]
    kk = [1.0 - f for f in fg]
    lf = [jnp.log2(f) for f in fg]
    parts = [_split3(x) for x in lf]
    b12 = [jnp.dot(e_all, jnp.concatenate([p1, p2], axis=1), preferred_element_type=F32) for p1, p2, _ in parts]
    b3 = [jnp.dot(e_all, p3, preferred_element_type=F32) for _, _, p3 in parts]
    b = [x[:, :HEAD_DIM] + x[:, HEAD_DIM:] + y for x, y in zip(b12, b3)]
    att = [jnp.where(lvl == len(HGRN_LEVELS), jnp.sum(qs[i] * kk[i], axis=-1, keepdims=True), 0.0) for i in heads]
    for n, h in enumerate(HGRN_LEVELS):
        odd = ((row // h) % 2) == 1
        xh = [(jnp.where(odd, qs[i], kk[i]) * jnp.exp2(_midpoint_log_decay(lf[i], b[i], h, row))).astype(BF16)
              for i in heads]
        pair = [_dot_nt(x, x) for x in xh]
        att = [jnp.where(lvl == n, pair[i], att[i]) for i in heads]
    b_end = [x[c - 1:c] for x in b]
    q_in = [(qs[i] * jnp.exp2(b[i])).astype(BF16) for i in heads]
    k_end = [(kk[i] * jnp.exp2(b_end[i] - b[i])).astype(BF16) for i in heads]
    v16 = [v.astype(BF16) for _, _, v, _, _, _ in inputs]
    o = [jnp.dot(att[i].astype(BF16), v16[i], preferred_element_type=F32)
         + _dot_nt(q_in[i], inputs[i][4].astype(BF16)) for i in heads]
    st_new = [inputs[i][4] * jnp.exp2(b_end[i]) + _dot_tn(v16[i], k_end[i]) for i in heads]
    o = [x * lax.rsqrt(jnp.mean(x * x, axis=-1, keepdims=True) + LN_EPS) * nw for x in o]
    out = [(o[i] * (inputs[i][3] * jax.nn.sigmoid(inputs[i][3]))).astype(BF16) for i in heads]
    return list(zip(out, st_new))


HGRN_BATCH = 2
HGRN_SEQ_TILE = 1024


def _hgrn_kernel(q_ref, f_ref, i_ref, g_ref, lb_ref, nw_ref, e_ref, lvl_ref, o_ref, st_ref):
    c = HGRN_CHUNK
    n_batch, tile_rows = q_ref.shape[0], q_ref.shape[1]

    @pl.when(pl.program_id(1) == 0)
    def _():
        st_ref[...] = jnp.zeros_like(st_ref)

    nw = nw_ref[...]
    row = lax.broadcasted_iota(jnp.int32, (c, 1), 0)

    def chunk(ci, carry):
        rows = pl.ds(pl.multiple_of(ci * c, c), c)
        e_all, lvl = e_ref[...], lvl_ref[...]
        inputs = []
        for n in range(n_batch):
            for h in range(A_HEADS):
                cols = slice(h * HEAD_DIM, (h + 1) * HEAD_DIM)
                inputs.append((q_ref[n, rows, cols], f_ref[n, rows, cols], i_ref[n, rows, cols], g_ref[n, rows, cols],
                               st_ref[n * A_HEADS + h], lb_ref[:, cols]))
        for k, (o, st_new) in enumerate(_hgrn_chunk_heads(inputs, nw, e_all, lvl, row)):
            n, h = divmod(k, A_HEADS)
            o_ref[n, rows, h * HEAD_DIM:(h + 1) * HEAD_DIM] = o
            st_ref[k] = st_new
        return carry

    lax.fori_loop(0, tile_rows // c, chunk, 0)


def _hgrn2(proj3, lb, norm_w):
    bsz, seq, _ = proj3.shape
    e_all, lvl = _hgrn_constants()
    width = A_HEADS * HEAD_DIM
    col = lambda blk: pl.BlockSpec((HGRN_BATCH, HGRN_SEQ_TILE, width), lambda b, s, blk=blk: (b, s, blk // A_HEADS))
    const = lambda shape: pl.BlockSpec(shape, lambda b, s: (0,) * len(shape))
    return pl.pallas_call(
        _hgrn_kernel,
        grid=(bsz // HGRN_BATCH, seq // HGRN_SEQ_TILE),
        in_specs=[col(QA_BLK), col(FA_BLK), col(IA_BLK), col(GA_BLK),
                  const((1, width)), const((1, HEAD_DIM)), const(e_all.shape), const(lvl.shape)],
        out_specs=pl.BlockSpec((HGRN_BATCH, HGRN_SEQ_TILE, width), lambda b, s: (b, s, 0)),
        out_shape=jax.ShapeDtypeStruct((bsz, seq, width), BF16),
        scratch_shapes=[pltpu.VMEM((HGRN_BATCH * A_HEADS, HEAD_DIM, HEAD_DIM), F32)],
        compiler_params=_params(("parallel", "arbitrary")),
        name="hgrn2",
    )(proj3, proj3, proj3, proj3, lb.reshape(1, width), norm_w.reshape(1, HEAD_DIM), e_all, lvl)


def _band_masks(max_lag):
    r = lax.broadcasted_iota(jnp.int32, (ATT_BLOCK, ATT_BLOCK), 0)
    col = lax.broadcasted_iota(jnp.int32, (ATT_BLOCK, ATT_BLOCK), 1)
    return col <= r, (ATT_BLOCK + r - col) <= max_lag


def _tile_softmax_many(blocks):
    scale = HEAD_DIM ** -0.5 * LOG2_E
    s_all = [[jnp.where(mask, _dot_nt(q16, k.astype(BF16)) * scale, NEG_BIG) for k, mask in zip(ks, masks)]
             for q16, ks, _, masks, _ in blocks]
    m_all = []
    for s, (_, _, _, _, m_floor) in zip(s_all, blocks):
        m = jnp.max(functools.reduce(jnp.maximum, s), axis=-1, keepdims=True)
        m_all.append(m if m_floor is None else jnp.maximum(m, m_floor))
    e_all = [[jnp.exp2(si - m) for si in s] for s, m in zip(s_all, m_all)]
    acc_all = [functools.reduce(jnp.add, [jnp.dot(ei.astype(BF16), v.astype(BF16), preferred_element_type=F32)
                                          for ei, v in zip(e, vs)])
               for e, (_, _, vs, _, _) in zip(e_all, blocks)]
    l_all = [jnp.sum(functools.reduce(jnp.add, e), axis=-1, keepdims=True) for e in e_all]
    return list(zip(m_all, l_all, acc_all))


DIL_UNROLL = 16


def _dilated_kernel(proj_hbm, c_ref, s_ref, rot_ref, o_ref, qkv_ref, sem, qr_ref, kr_ref, m_ref, l_ref, acc_ref):
    seq = o_ref.shape[0]
    blk = ATT_BLOCK
    widest = DILATIONS[-1]
    per_residue = seq // widest
    step = pl.program_id(0) * pl.num_programs(1) + pl.program_id(1)
    n_steps = pl.num_programs(0) * pl.num_programs(1)
    slot = step % 2

    def qkv_copies(of_step, into_slot):
        b, h = of_step // pl.num_programs(1), of_step % pl.num_programs(1)
        copies = []
        for which, first_blk in enumerate((QB_BLK, KB_BLK, VB_BLK)):
            col = pl.multiple_of((first_blk + h) * HEAD_DIM, HEAD_DIM)
            for r in range(widest):
                copies.append(pltpu.make_async_copy(
                    proj_hbm.at[b, :, r, pl.ds(col, HEAD_DIM)],
                    qkv_ref.at[into_slot, which, pl.ds(r * per_residue, per_residue), :],
                    sem.at[into_slot]))
        return copies

    @pl.when(step == 0)
    def _():
        for copy in qkv_copies(step, slot):
            copy.start()

    @pl.when(step + 1 < n_steps)
    def _():
        for copy in qkv_copies(step + 1, 1 - slot):
            copy.start()

    for copy in qkv_copies(step, slot):
        copy.wait()

    c, s, rot = c_ref[...], s_ref[...], rot_ref[...]
    qr_ref[...] = _rope(qkv_ref[slot, 0], c, s, rot)
    kr_ref[...] = _rope(qkv_ref[slot, 1], c, s, rot)
    vr_ref = qkv_ref.at[slot, 2]

    def chunk_rows(dil):
        return blk * dil // widest

    def block_masks(dil):
        def position(i):
            return (widest // dil) * (i % chunk_rows(dil)) + i // chunk_rows(dil)
        row = position(lax.broadcasted_iota(jnp.int32, (blk, blk), 0))
        col = position(lax.broadcasted_iota(jnp.int32, (blk, blk), 1))
        return col <= row, col >= row

    def chunks_of(dil, r, n):
        size = chunk_rows(dil)
        return [pl.ds(pl.multiple_of((r + dil * a) * per_residue + size * n, size), size)
                for a in range(widest // dil)]

    def gather(ref, chunks):
        return jnp.concatenate([ref[rows, :] for rows in chunks], axis=0)

    def scatter(ref, chunks, value):
        size = value.shape[0] // len(chunks)
        for a, rows in enumerate(chunks):
            ref[rows, :] = value[a * size:(a + 1) * size]

    def run_group(specs, masks_of_pattern, first):
        cur_mask, prev_mask = masks_of_pattern
        loaded = []
        for rows, prev, prev_ok in specs:
            ks, vs, masks = [gather(kr_ref, rows)], [gather(vr_ref, rows)], [cur_mask]
            if prev is not None:
                ks.append(gather(kr_ref, prev))
                vs.append(gather(vr_ref, prev))
                masks.append(jnp.logical_and(prev_mask, prev_ok))
            state = None if first else (gather(m_ref, rows), gather(l_ref, rows), gather(acc_ref, rows))
            loaded.append((gather(qr_ref, rows).astype(BF16), ks, vs, masks, state))
        pieces = _tile_softmax_many([(q16, ks, vs, masks, None if state is None else state[0])
                                     for q16, ks, vs, masks, state in loaded])
        results = []
        for (m, l, acc), (_, _, _, _, state) in zip(pieces, loaded):
            if state is None:
                shape = (blk, HEAD_DIM)
                results.append((jnp.broadcast_to(m, shape), jnp.broadcast_to(l, shape), acc))
            else:
                m_old, l_old, acc_old = state
                alpha = jnp.exp2(m_old - m)
                results.append((m, alpha * l_old + l, alpha * acc_old + acc))
        for (rows, _, _), (m, l, acc) in zip(specs, results):
            scatter(m_ref, rows, m)
            scatter(l_ref, rows, l)
            scatter(acc_ref, rows, acc)

    n_query_blocks = seq // blk
    for dil in DILATIONS:
        n_blocks = n_query_blocks // dil
        masks_of_pattern = block_masks(dil)

        def group(g, carry, dil=dil, n_blocks=n_blocks, masks_of_pattern=masks_of_pattern):
            specs = []
            for u in range(DIL_UNROLL):
                e = g * DIL_UNROLL + u
                r, n = e % dil, e // dil
                if n_blocks == 1:
                    specs.append((chunks_of(dil, r, n), None, None))
                else:
                    specs.append((chunks_of(dil, r, n), chunks_of(dil, r, jnp.maximum(n - 1, 0)), n > 0))
            run_group(specs, masks_of_pattern, first=(dil == DILATIONS[0]))
            return carry

        lax.fori_loop(0, n_query_blocks // DIL_UNROLL, group, 0)

    for r in range(widest):
        rows = slice(r * per_residue, (r + 1) * per_residue)
        qr_ref[pl.ds(r, per_residue, stride=widest), :] = acc_ref[rows, :] / l_ref[rows, :]
    o_ref[...] = qr_ref[...].astype(o_ref.dtype)


def _residue_major(table, widest):
    seq, width = table.shape
    return table.reshape(seq // widest, widest, width).transpose(1, 0, 2).reshape(seq, width)


def _dilated(proj3, tables):
    bsz, seq, width = proj3.shape
    widest = DILATIONS[-1]
    tab = pl.BlockSpec((seq, HEAD_DIM), lambda b, h: (0, 0))
    return pl.pallas_call(
        _dilated_kernel,
        grid=(bsz, B_HEADS),
        in_specs=[pl.BlockSpec(memory_space=pl.ANY), tab, tab, pl.BlockSpec((HEAD_DIM, HEAD_DIM), lambda b, h: (0, 0))],
        out_specs=pl.BlockSpec((None, seq, HEAD_DIM), lambda b, h: (b, 0, h)),
        out_shape=jax.ShapeDtypeStruct((bsz, seq, B_HEADS * HEAD_DIM), BF16),
        scratch_shapes=[pltpu.VMEM((2, 3, seq, HEAD_DIM), F32), pltpu.SemaphoreType.DMA((2,))]
                       + [pltpu.VMEM((seq, HEAD_DIM), F32)] * 5,
        compiler_params=_params(("arbitrary", "arbitrary")),
        name="dilated_attention",
    )(proj3.reshape(bsz, seq // widest, widest, width), _residue_major(tables[0], widest),
      _residue_major(tables[1], widest), tables[2])


SWA_UNROLL = 4


def _swa_kernel(sink_ref, q0_ref, q1_ref, q2_ref, k_ref, v_ref, c_ref, s_ref, rot_ref, o_ref, qr_ref, kr_ref):
    seq = k_ref.shape[0]
    blk = ATT_BLOCK
    grp = pl.program_id(1)
    c, s, rot = c_ref[...], s_ref[...], rot_ref[...]
    for j, q_ref in enumerate((q0_ref, q1_ref, q2_ref)):
        qr_ref[j] = _rope(q_ref[...], c, s, rot).astype(BF16)
    kr_ref[...] = _rope(k_ref[...], c, s, rot).astype(BF16)
    cur_mask, prev_mask = _band_masks(C_MAX_LAG)
    sinks = [jnp.full((blk, 1), sink_ref[grp * C_REP + j] * LOG2_E, F32) for j in range(C_REP)]

    def group(g, carry):
        blocks, dests = [], []
        for u in range(SWA_UNROLL):
            n = g * SWA_UNROLL + u
            rows = pl.ds(pl.multiple_of(n * blk, blk), blk)
            prev = pl.ds(pl.multiple_of(jnp.maximum(n - 1, 0) * blk, blk), blk)
            ks, vs = [kr_ref[rows, :], kr_ref[prev, :]], [v_ref[rows, :], v_ref[prev, :]]
            masks = [cur_mask, jnp.logical_and(prev_mask, n > 0)]
            for j in range(C_REP):
                blocks.append((qr_ref[j, rows, :], ks, vs, masks, sinks[j]))
                dests.append((rows, j))
        for (rows, j), (m, l, acc) in zip(dests, _tile_softmax_many(blocks)):
            out = acc / (l + jnp.exp2(sinks[j] - m))
            o_ref[rows, j * HEAD_DIM:(j + 1) * HEAD_DIM] = out.astype(o_ref.dtype)
        return carry

    lax.fori_loop(0, seq // (blk * SWA_UNROLL), group, 0)


def _swa(proj3, sinks, tables):
    bsz, seq, _ = proj3.shape
    qcol = lambda j: pl.BlockSpec((None, seq, HEAD_DIM), lambda b, g, j=j: (b, 0, QC_BLK + g * C_REP + j))
    col = lambda blk: pl.BlockSpec((None, seq, HEAD_DIM), lambda b, g, blk=blk: (b, 0, blk + g))
    tab = pl.BlockSpec((seq, HEAD_DIM), lambda b, g: (0, 0))
    return pl.pallas_call(
        _swa_kernel,
        grid=(bsz, C_KV_HEADS),
        in_specs=[pl.BlockSpec(memory_space=pltpu.SMEM), qcol(0), qcol(1), qcol(2), col(KC_BLK), col(VC_BLK),
                  tab, tab, pl.BlockSpec((HEAD_DIM, HEAD_DIM), lambda b, g: (0, 0))],
        out_specs=pl.BlockSpec((None, seq, C_REP * HEAD_DIM), lambda b, g: (b, 0, g)),
        out_shape=jax.ShapeDtypeStruct((bsz, seq, C_HEADS * HEAD_DIM), BF16),
        scratch_shapes=[pltpu.VMEM((C_REP, seq, HEAD_DIM), BF16), pltpu.VMEM((seq, HEAD_DIM), BF16)],
        compiler_params=_params(("parallel", "parallel")),
        name="swa_sink_attention",
    )(sinks, proj3, proj3, proj3, proj3, proj3, *tables)


def _layer_norm_store(z, g_ref, b_ref, o32_ref, o16_ref):
    mu = jnp.mean(z, axis=-1, keepdims=True)
    zc = z - mu
    var = jnp.mean(zc * zc, axis=-1, keepdims=True)
    y = zc * lax.rsqrt(var + LN_EPS) * g_ref[...] + b_ref[...]
    o32_ref[...] = y
    o16_ref[...] = y.astype(BF16)


def _stage_weight_bf16(w_hbm, layer, w16_ref, stage_ref, sem):
    chunk = stage_ref.shape[1]
    n_chunks = w16_ref.shape[0] // chunk

    def copy(c):
        slot = c % 2
        return pltpu.make_async_copy(w_hbm.at[layer, pl.ds(c * chunk, chunk), :], stage_ref.at[slot], sem.at[slot])

    copy(0).start()
    for c in range(n_chunks):
        if c + 1 < n_chunks:
            copy(c + 1).start()
        copy(c).wait()
        w16_ref[c * chunk:(c + 1) * chunk, :] = stage_ref[c % 2].astype(BF16)


def _resident_weight_scratch(k, n):
    return [pltpu.VMEM((k, n), BF16), pltpu.VMEM((2, WEIGHT_STAGE_ROWS, n), F32), pltpu.SemaphoreType.DMA((2,))]


def _out_proj_kernel(alpha, layer, oa_ref, ob_ref, oc_ref, w_hbm, x_ref, g_ref, b_ref, o32_ref, o16_ref,
                     w16_ref, stage_ref, sem):
    @pl.when(pl.program_id(0) == 0)
    def _():
        _stage_weight_bf16(w_hbm, layer, w16_ref, stage_ref, sem)

    ka, kb = oa_ref.shape[1], ob_ref.shape[1]
    y = (jnp.dot(oa_ref[...], w16_ref[0:ka, :], preferred_element_type=F32)
         + jnp.dot(ob_ref[...], w16_ref[ka:ka + kb, :], preferred_element_type=F32)
         + jnp.dot(oc_ref[...], w16_ref[ka + kb:, :], preferred_element_type=F32))
    _layer_norm_store(alpha * x_ref[...] + y, g_ref, b_ref, o32_ref, o16_ref)


def _layer_vec(layer, d):
    return pl.BlockSpec((None, 1, d), lambda i, layer=layer: (layer, 0, 0))


def _out_proj_ln(oa, ob, oc, w_out, layer, x, g, b, alpha, bm):
    m, d = x.shape
    rows = lambda a: pl.BlockSpec((bm, a.shape[1]), lambda i: (i, 0))
    return pl.pallas_call(
        functools.partial(_out_proj_kernel, alpha, layer),
        grid=(m // bm,),
        in_specs=[rows(oa), rows(ob), rows(oc), pl.BlockSpec(memory_space=pl.ANY), rows(x),
                  _layer_vec(layer, d), _layer_vec(layer, d)],
        out_specs=[rows(x), rows(x)],
        out_shape=[jax.ShapeDtypeStruct((m, d), F32), jax.ShapeDtypeStruct((m, d), BF16)],
        scratch_shapes=_resident_weight_scratch(w_out.shape[1], d),
        compiler_params=_params(("arbitrary",)),
        name="out_proj_ln",
    )(oa, ob, oc, w_out, x, g, b)


def _in_proj_kernel(x_ref, w_ref, o_ref):
    o_ref[...] = jnp.dot(x_ref[...], w_ref[...].astype(BF16), preferred_element_type=F32).astype(o_ref.dtype)


def _in_proj(x16, w_in, layer, bm, bn):
    m, k = x16.shape
    n = w_in.shape[2]
    return pl.pallas_call(
        _in_proj_kernel,
        grid=(m // bm, n // bn),
        in_specs=[pl.BlockSpec((bm, k), lambda i, j: (i, 0)),
                  pl.BlockSpec((None, k, bn), lambda i, j: (layer, 0, j))],
        out_specs=pl.BlockSpec((bm, bn), lambda i, j: (i, j)),
        out_shape=jax.ShapeDtypeStruct((m, n), F32),
        compiler_params=_params(("parallel", "parallel")),
        name="in_proj",
    )(x16, w_in)


FFN_UP_SUBTILES = 2


def _ffn_up_kernel(tiles_per_seq, x_ref, wg_ref, wu_ref, cw_ref, cb_ref, h_ref, g_ref, w16_ref):
    bm = x_ref.shape[0]
    bn = h_ref.shape[1]
    halo = g_ref.shape[0] - bm

    @pl.when(pl.program_id(1) == 0)
    def _():
        w16_ref[:, 0:bn] = wg_ref[...].astype(BF16)
        w16_ref[:, bn:] = wu_ref[...].astype(BF16)

    @pl.when(pl.program_id(1) % tiles_per_seq == 0)
    def _():
        g_ref[0:halo, :] = jnp.zeros((halo, g_ref.shape[1]), F32)

    sub = bm // FFN_UP_SUBTILES
    for t in range(FFN_UP_SUBTILES):
        first = halo + t * sub
        gu = jnp.dot(x_ref[t * sub:(t + 1) * sub, :], w16_ref[...], preferred_element_type=F32)
        g_ref[first:first + sub, :] = gu[:, 0:bn]
        u = gu[:, bn:]
        gc = cb_ref[...]
        for j in range(CONV_WIDTH):
            lag = CONV_WIDTH - 1 - j
            gc = gc + cw_ref[j:j + 1, :] * g_ref[first - lag:first - lag + sub, :]
        h_ref[t * sub:(t + 1) * sub, :] = (gc * jax.nn.sigmoid(gc) * u).astype(h_ref.dtype)
    g_ref[0:halo, :] = g_ref[bm:bm + halo, :]


def _ffn_up(x16, w_gate, w_up, conv_w, conv_b, layer, seq, bm, bn):
    m, d = x16.shape
    f = w_gate.shape[2]
    halo = SUBLANES
    wspec = pl.BlockSpec((None, d, bn), lambda j, i: (layer, 0, j))
    return pl.pallas_call(
        functools.partial(_ffn_up_kernel, seq // bm),
        grid=(f // bn, m // bm),
        in_specs=[pl.BlockSpec((bm, d), lambda j, i: (i, 0)), wspec, wspec,
                  pl.BlockSpec((None, CONV_WIDTH, bn), lambda j, i: (layer, 0, j)),
                  pl.BlockSpec((None, 1, bn), lambda j, i: (layer, 0, j))],
        out_specs=pl.BlockSpec((bm, bn), lambda j, i: (i, j)),
        out_shape=jax.ShapeDtypeStruct((m, f), BF16),
        scratch_shapes=[pltpu.VMEM((bm + halo, bn), F32), pltpu.VMEM((d, 2 * bn), BF16)],
        compiler_params=_params(("parallel", "arbitrary")),
        name="ffn_up_conv_gate",
    )(x16, w_gate, w_up, conv_w, conv_b)


def _ffn_down_kernel(alpha, layer, h_ref, w_hbm, x_ref, g_ref, b_ref, o32_ref, o16_ref, w16_ref, stage_ref, sem):
    @pl.when(pl.program_id(0) == 0)
    def _():
        _stage_weight_bf16(w_hbm, layer, w16_ref, stage_ref, sem)

    y = jnp.dot(h_ref[...], w16_ref[...], preferred_element_type=F32)
    _layer_norm_store(alpha * x_ref[...] + y, g_ref, b_ref, o32_ref, o16_ref)


def _ffn_down_ln(h, w_down, layer, x, g, b, alpha, bm):
    m, d = x.shape
    f = h.shape[1]
    rows = pl.BlockSpec((bm, d), lambda i: (i, 0))
    return pl.pallas_call(
        functools.partial(_ffn_down_kernel, alpha, layer),
        grid=(m // bm,),
        in_specs=[pl.BlockSpec((bm, f), lambda i: (i, 0)), pl.BlockSpec(memory_space=pl.ANY), rows,
                  _layer_vec(layer, d), _layer_vec(layer, d)],
        out_specs=[rows, rows],
        out_shape=[jax.ShapeDtypeStruct((m, d), F32), jax.ShapeDtypeStruct((m, d), BF16)],
        scratch_shapes=_resident_weight_scratch(f, d),
        compiler_params=_params(("arbitrary",)),
        name="ffn_down_ln",
    )(h, w_down, x, g, b)


def kernel(x, w_in, lb_logits, a_norm_w, c_sinks, w_out, ln1_g, ln1_b, w_gate, w_up, conv_w, conv_b, w_down,
           ln2_g, ln2_b):
    bsz, seq, d = x.shape
    depth = w_in.shape[0]
    alpha = (2 * depth) ** 0.25
    tables = _rope_tables(seq)
    lbs = jnp.cumsum(jax.nn.softmax(lb_logits.astype(F32), axis=0), axis=0)
    lbs = lbs - lbs[0]
    vec3 = lambda a: a.reshape(depth, 1, -1)
    ln1_g, ln1_b, ln2_g, ln2_b, conv_b = vec3(ln1_g), vec3(ln1_b), vec3(ln2_g), vec3(ln2_b), vec3(conv_b)
    x32 = x.reshape(bsz * seq, d)
    x16 = x32.astype(BF16)
    plan = _tile_plan(bsz * seq, d, w_out.shape[1], w_gate.shape[2])
    for l in range(depth):
        proj3 = _in_proj(x16, w_in, l, *plan.in_proj).reshape(bsz, seq, -1)
        oa = _hgrn2(proj3, lbs[l], a_norm_w[l]).reshape(bsz * seq, -1)
        ob = _dilated(proj3, tables).reshape(bsz * seq, -1)
        oc = _swa(proj3, c_sinks[l], tables).reshape(bsz * seq, -1)
        x32, x16 = _out_proj_ln(oa, ob, oc, w_out, l, x32, ln1_g, ln1_b, alpha, plan.out_proj_rows)
        h = _ffn_up(x16, w_gate, w_up, conv_w, conv_b, l, seq, *plan.ffn_up)
        x32, x16 = _ffn_down_ln(h, w_down, l, x32, ln2_g, ln2_b, alpha, plan.ffn_down_rows)
    return x32.reshape(bsz, seq, d)
```

```python
import functools
from typing import NamedTuple

import numpy as np
import jax
import jax.numpy as jnp
from jax import lax
from jax.experimental import pallas as pl
from jax.experimental.pallas import tpu as pltpu

HEAD_DIM = 128
A_HEADS = 4
B_HEADS = 6
C_HEADS = 6
C_KV_HEADS = 2
C_REP = C_HEADS // C_KV_HEADS
DILATIONS = (1, 4, 16)
ATT_BLOCK = 128
C_MAX_LAG = 127
ROPE_THETA = 500000.0
ROPE_DIM = HEAD_DIM // 4
ROPE_HALF = ROPE_DIM // 2
CONV_WIDTH = 3
LN_EPS = 1e-5

QA_BLK, FA_BLK, IA_BLK, GA_BLK = 0, 4, 8, 12
QB_BLK, KB_BLK, VB_BLK = 16, 22, 28
QC_BLK, KC_BLK, VC_BLK = 34, 40, 42

HGRN_CHUNK = 128
HGRN_LEVELS = (64, 32, 16, 8, 4, 2, 1)

VMEM_LIMIT_BYTES = 56 * 1024 * 1024

F32 = jnp.float32
BF16 = jnp.bfloat16
NEG_BIG = -1e30
LOG2_E = 1.4426950408889634


def _params(semantics):
    return pltpu.CompilerParams(dimension_semantics=semantics, vmem_limit_bytes=VMEM_LIMIT_BYTES)


SUBLANES = 8
WEIGHT_STAGE_ROWS = 512


class _TilePlan(NamedTuple):
    in_proj: tuple
    out_proj_rows: int
    ffn_up: tuple
    ffn_down_rows: int


def _tile_plan(rows, d_model, d_mix, d_ff):
    plan = _TilePlan(in_proj=(2048, 512), out_proj_rows=512, ffn_up=(2048, 512), ffn_down_rows=256)
    f32, bf16, two = 4, 2, 2
    staged = two * WEIGHT_STAGE_ROWS * d_model * f32
    bm, bn = plan.in_proj
    in_proj = two * bm * d_model * bf16 + two * d_model * bn * f32 + d_model * bn * bf16 + two * bm * bn * f32
    bm = plan.out_proj_rows
    out_proj = d_mix * d_model * bf16 + staged + two * bm * (d_mix * bf16 + d_model * (f32 + f32 + bf16))
    bm, bn = plan.ffn_up
    ffn_up = (two * bm * d_model * bf16 + two * two * d_model * bn * f32 + d_model * two * bn * bf16
              + (bm + SUBLANES) * bn * f32 + two * bm * bn * bf16 + bm * two * bn * f32)
    bm = plan.ffn_down_rows
    ffn_down = d_ff * d_model * bf16 + staged + two * bm * (d_ff * bf16 + d_model * (f32 + f32 + bf16))
    for name, need in (("in_proj", in_proj), ("out_proj", out_proj), ("ffn_up", ffn_up), ("ffn_down", ffn_down)):
        assert need <= VMEM_LIMIT_BYTES, (name, need)
    assert rows % max(plan.in_proj[0], plan.ffn_up[0]) == 0 and d_ff % plan.ffn_up[1] == 0
    return plan


def _rope_tables(seq):
    inv = ROPE_THETA ** (-jnp.arange(0, ROPE_DIM, 2, dtype=F32) / ROPE_DIM)
    ang = jnp.arange(seq, dtype=F32)[:, None] * inv[None, :]
    cos, sin = jnp.cos(ang), jnp.sin(ang)
    rest = HEAD_DIM - ROPE_DIM
    c = jnp.concatenate([cos, cos, jnp.ones((seq, rest), F32)], -1)
    s = jnp.concatenate([sin, sin, jnp.zeros((seq, rest), F32)], -1)
    rot = np.zeros((HEAD_DIM, HEAD_DIM), np.float32)
    lane = np.arange(ROPE_HALF)
    rot[lane + ROPE_HALF, lane] = -1.0
    rot[lane, lane + ROPE_HALF] = 1.0
    return c, s, jnp.asarray(np.concatenate([rot, rot], 0), BF16)


def _rope(x, c, s, rot2):
    hi = x.astype(BF16)
    lo = (x - hi.astype(F32)).astype(BF16)
    swapped = jnp.dot(jnp.concatenate([hi, lo], axis=1), rot2, preferred_element_type=F32)
    return x * c + swapped * s


def _hgrn_constants():
    c = HGRN_CHUNK
    i = np.arange(c)[:, None]
    j = np.arange(c)[None, :]
    lvl = np.full((c, c), -1, np.int32)
    for n, h in enumerate(HGRN_LEVELS):
        lvl[(i // (2 * h) == j // (2 * h)) & (i // h != j // h) & (i > j)] = n
    lvl[np.arange(c), np.arange(c)] = len(HGRN_LEVELS)
    return jnp.asarray((j <= i).astype(np.float32), BF16), jnp.asarray(lvl)


def _midpoint_log_decay(lf, b, h, row):
    c, dk = b.shape
    odd = ((row // h) % 2) == 1
    if h == 1:
        return jnp.where(odd, lf, 0.0)
    if h == 2:
        below = pltpu.roll(lf, c - 1, 0)
        above = pltpu.roll(lf, 1, 0)
        place = row % 4
        return jnp.where(place == 0, below, jnp.where(place == 1, 0.0, jnp.where(place == 2, lf, lf + above)))
    blocks = b.reshape(c // (2 * h), 2 * h, dk)
    mid = jnp.broadcast_to(blocks[:, h - 1:h, :], blocks.shape).reshape(c, dk)
    return jnp.where(odd, b - mid, mid - b)


def _split3(x):
    p1 = x.astype(BF16)
    r1 = x - p1.astype(F32)
    p2 = r1.astype(BF16)
    p3 = (r1 - p2.astype(F32)).astype(BF16)
    return p1, p2, p3


def _dot_nt(a, b, **kw):
    return lax.dot_general(a, b, (((1,), (1,)), ((), ())), preferred_element_type=F32, **kw)


def _dot_tn(a, b, **kw):
    return lax.dot_general(a, b, (((0,), (0,)), ((), ())), preferred_element_type=F32, **kw)


def _hgrn_chunk_heads(inputs, nw, e_all, lvl, row):
    c = HGRN_CHUNK
    heads = range(len(inputs))
    qs = [q * jax.nn.sigmoid(q) for q, _, _, _, _, _ in inputs]
    fg = [lb + (1.0 - lb) * jax.nn.sigmoid(fl) for _, fl, _, _, _, lb in inputs]
    kk = [1.0 - f for f in fg]
    lf = [jnp.log2(f) for f in fg]
    parts = [_split3(x) for x in lf]
    b12 = [jnp.dot(e_all, jnp.concatenate([p1, p2], axis=1), preferred_element_type=F32) for p1, p2, _ in parts]
    b3 = [jnp.dot(e_all, p3, preferred_element_type=F32) for _, _, p3 in parts]
    b = [x[:, :HEAD_DIM] + x[:, HEAD_DIM:] + y for x, y in zip(b12, b3)]
    att = [jnp.where(lvl == len(HGRN_LEVELS), jnp.sum(qs[i] * kk[i], axis=-1, keepdims=True), 0.0) for i in heads]
    for n, h in enumerate(HGRN_LEVELS):
        odd = ((row // h) % 2) == 1
        xh = [(jnp.where(odd, qs[i], kk[i]) * jnp.exp2(_midpoint_log_decay(lf[i], b[i], h, row))).astype(BF16)
              for i in heads]
        pair = [_dot_nt(x, x) for x in xh]
        att = [jnp.where(lvl == n, pair[i], att[i]) for i in heads]
    b_end = [x[c - 1:c] for x in b]
    q_in = [(qs[i] * jnp.exp2(b[i])).astype(BF16) for i in heads]
    k_end = [(kk[i] * jnp.exp2(b_end[i] - b[i])).astype(BF16) for i in heads]
    v16 = [v.astype(BF16) for _, _, v, _, _, _ in inputs]
    o = [jnp.dot(att[i].astype(BF16), v16[i], preferred_element_type=F32)
         + _dot_nt(q_in[i], inputs[i][4].astype(BF16)) for i in heads]
    st_new = [inputs[i][4] * jnp.exp2(b_end[i]) + _dot_tn(v16[i], k_end[i]) for i in heads]
    o = [x * lax.rsqrt(jnp.mean(x * x, axis=-1, keepdims=True) + LN_EPS) * nw for x in o]
    out = [(o[i] * (inputs[i][3] * jax.nn.sigmoid(inputs[i][3]))).astype(BF16) for i in heads]
    return list(zip(out, st_new))


HGRN_BATCH = 2
HGRN_SEQ_TILE = 1024


def _hgrn_kernel(q_ref, f_ref, i_ref, g_ref, lb_ref, nw_ref, e_ref, lvl_ref, o_ref, st_ref):
    c = HGRN_CHUNK
    n_batch, tile_rows = q_ref.shape[0], q_ref.shape[1]

    @pl.when(pl.program_id(1) == 0)
    def _():
        st_ref[...] = jnp.zeros_like(st_ref)

    nw = nw_ref[...]
    row = lax.broadcasted_iota(jnp.int32, (c, 1), 0)

    def chunk(ci, carry):
        rows = pl.ds(pl.multiple_of(ci * c, c), c)
        e_all, lvl = e_ref[...], lvl_ref[...]
        inputs = []
        for n in range(n_batch):
            for h in range(A_HEADS):
                cols = slice(h * HEAD_DIM, (h + 1) * HEAD_DIM)
                inputs.append((q_ref[n, rows, cols], f_ref[n, rows, cols], i_ref[n, rows, cols], g_ref[n, rows, cols],
                               st_ref[n * A_HEADS + h], lb_ref[:, cols]))
        for k, (o, st_new) in enumerate(_hgrn_chunk_heads(inputs, nw, e_all, lvl, row)):
            n, h = divmod(k, A_HEADS)
            o_ref[n, rows, h * HEAD_DIM:(h + 1) * HEAD_DIM] = o
            st_ref[k] = st_new
        return carry

    lax.fori_loop(0, tile_rows // c, chunk, 0)


def _hgrn2(proj3, lb, norm_w):
    bsz, seq, _ = proj3.shape
    e_all, lvl = _hgrn_constants()
    width = A_HEADS * HEAD_DIM
    col = lambda blk: pl.BlockSpec((HGRN_BATCH, HGRN_SEQ_TILE, width), lambda b, s, blk=blk: (b, s, blk // A_HEADS))
    const = lambda shape: pl.BlockSpec(shape, lambda b, s: (0,) * len(shape))
    return pl.pallas_call(
        _hgrn_kernel,
        grid=(bsz // HGRN_BATCH, seq // HGRN_SEQ_TILE),
        in_specs=[col(QA_BLK), col(FA_BLK), col(IA_BLK), col(GA_BLK),
                  const((1, width)), const((1, HEAD_DIM)), const(e_all.shape), const(lvl.shape)],
        out_specs=pl.BlockSpec((HGRN_BATCH, HGRN_SEQ_TILE, width), lambda b, s: (b, s, 0)),
        out_shape=jax.ShapeDtypeStruct((bsz, seq, width), BF16),
        scratch_shapes=[pltpu.VMEM((HGRN_BATCH * A_HEADS, HEAD_DIM, HEAD_DIM), F32)],
        compiler_params=_params(("parallel", "arbitrary")),
        name="hgrn2",
    )(proj3, proj3, proj3, proj3, lb.reshape(1, width), norm_w.reshape(1, HEAD_DIM), e_all, lvl)


def _band_masks(max_lag):
    r = lax.broadcasted_iota(jnp.int32, (ATT_BLOCK, ATT_BLOCK), 0)
    col = lax.broadcasted_iota(jnp.int32, (ATT_BLOCK, ATT_BLOCK), 1)
    return col <= r, (ATT_BLOCK + r - col) <= max_lag


def _tile_softmax_many(blocks):
    scale = HEAD_DIM ** -0.5 * LOG2_E
    s_all = [[jnp.where(mask, _dot_nt(q16, k.astype(BF16)) * scale, NEG_BIG) for k, mask in zip(ks, masks)]
             for q16, ks, _, masks, _ in blocks]
    m_all = []
    for s, (_, _, _, _, m_floor) in zip(s_all, blocks):
        m = jnp.max(functools.reduce(jnp.maximum, s), axis=-1, keepdims=True)
        m_all.append(m if m_floor is None else jnp.maximum(m, m_floor))
    e_all = [[jnp.exp2(si - m) for si in s] for s, m in zip(s_all, m_all)]
    acc_all = [functools.reduce(jnp.add, [jnp.dot(ei.astype(BF16), v.astype(BF16), preferred_element_type=F32)
                                          for ei, v in zip(e, vs)])
               for e, (_, _, vs, _, _) in zip(e_all, blocks)]
    l_all = [jnp.sum(functools.reduce(jnp.add, e), axis=-1, keepdims=True) for e in e_all]
    return list(zip(m_all, l_all, acc_all))


DIL_UNROLL = 16


def _dilated_kernel(proj_hbm, c_ref, s_ref, rot_ref, o_ref, qkv_ref, sem, qr_ref, kr_ref, m_ref, l_ref, acc_ref):
    seq = o_ref.shape[0]
    blk = ATT_BLOCK
    widest = DILATIONS[-1]
    per_residue = seq // widest
    step = pl.program_id(0) * pl.num_programs(1) + pl.program_id(1)
    n_steps = pl.num_programs(0) * pl.num_programs(1)
    slot = step % 2

    def qkv_copies(of_step, into_slot):
        b, h = of_step // pl.num_programs(1), of_step % pl.num_programs(1)
        copies = []
        for which, first_blk in enumerate((QB_BLK, KB_BLK, VB_BLK)):
            col = pl.multiple_of((first_blk + h) * HEAD_DIM, HEAD_DIM)
            for r in range(widest):
                copies.append(pltpu.make_async_copy(
                    proj_hbm.at[b, :, r, pl.ds(col, HEAD_DIM)],
                    qkv_ref.at[into_slot, which, pl.ds(r * per_residue, per_residue), :],
                    sem.at[into_slot]))
        return copies

    @pl.when(step == 0)
    def _():
        for copy in qkv_copies(step, slot):
            copy.start()

    @pl.when(step + 1 < n_steps)
    def _():
        for copy in qkv_copies(step + 1, 1 - slot):
            copy.start()

    for copy in qkv_copies(step, slot):
        copy.wait()

    c, s, rot = c_ref[...], s_ref[...], rot_ref[...]
    qr_ref[...] = _rope(qkv_ref[slot, 0], c, s, rot)
    kr_ref[...] = _rope(qkv_ref[slot, 1], c, s, rot)
    vr_ref = qkv_ref.at[slot, 2]

    def chunk_rows(dil):
        return blk * dil // widest

    def block_masks(dil):
        def position(i):
            return (widest // dil) * (i % chunk_rows(dil)) + i // chunk_rows(dil)
        row = position(lax.broadcasted_iota(jnp.int32, (blk, blk), 0))
        col = position(lax.broadcasted_iota(jnp.int32, (blk, blk), 1))
        return col <= row, col >= row

    def chunks_of(dil, r, n):
        size = chunk_rows(dil)
        return [pl.ds(pl.multiple_of((r + dil * a) * per_residue + size * n, size), size)
                for a in range(widest // dil)]

    def gather(ref, chunks):
        return jnp.concatenate([ref[rows, :] for rows in chunks], axis=0)

    def scatter(ref, chunks, value):
        size = value.shape[0] // len(chunks)
        for a, rows in enumerate(chunks):
            ref[rows, :] = value[a * size:(a + 1) * size]

    def run_group(specs, masks_of_pattern, first):
        cur_mask, prev_mask = masks_of_pattern
        loaded = []
        for rows, prev, prev_ok in specs:
            ks, vs, masks = [gather(kr_ref, rows)], [gather(vr_ref, rows)], [cur_mask]
            if prev is not None:
                ks.append(gather(kr_ref, prev))
                vs.append(gather(vr_ref, prev))
                masks.append(jnp.logical_and(prev_mask, prev_ok))
            state = None if first else (gather(m_ref, rows), gather(l_ref, rows), gather(acc_ref, rows))
            loaded.append((gather(qr_ref, rows).astype(BF16), ks, vs, masks, state))
        pieces = _tile_softmax_many([(q16, ks, vs, masks, None if state is None else state[0])
                                     for q16, ks, vs, masks, state in loaded])
        results = []
        for (m, l, acc), (_, _, _, _, state) in zip(pieces, loaded):
            if state is None:
                shape = (blk, HEAD_DIM)
                results.append((jnp.broadcast_to(m, shape), jnp.broadcast_to(l, shape), acc))
            else:
                m_old, l_old, acc_old = state
                alpha = jnp.exp2(m_old - m)
                results.append((m, alpha * l_old + l, alpha * acc_old + acc))
        for (rows, _, _), (m, l, acc) in zip(specs, results):
            scatter(m_ref, rows, m)
            scatter(l_ref, rows, l)
            scatter(acc_ref, rows, acc)

    n_query_blocks = seq // blk
    for dil in DILATIONS:
        n_blocks = n_query_blocks // dil
        masks_of_pattern = block_masks(dil)

        def group(g, carry, dil=dil, n_blocks=n_blocks, masks_of_pattern=masks_of_pattern):
            specs = []
            for u in range(DIL_UNROLL):
                e = g * DIL_UNROLL + u
                r, n = e % dil, e // dil
                if n_blocks == 1:
                    specs.append((chunks_of(dil, r, n), None, None))
                else:
                    specs.append((chunks_of(dil, r, n), chunks_of(dil, r, jnp.maximum(n - 1, 0)), n > 0))
            run_group(specs, masks_of_pattern, first=(dil == DILATIONS[0]))
            return carry

        lax.fori_loop(0, n_query_blocks // DIL_UNROLL, group, 0)

    for r in range(widest):
        rows = slice(r * per_residue, (r + 1) * per_residue)
        qr_ref[pl.ds(r, per_residue, stride=widest), :] = acc_ref[rows, :] / l_ref[rows, :]
    o_ref[...] = qr_ref[...].astype(o_ref.dtype)


def _residue_major(table, widest):
    seq, width = table.shape
    return table.reshape(seq // widest, widest, width).transpose(1, 0, 2).reshape(seq, width)


def _dilated(proj3, tables):
    bsz, seq, width = proj3.shape
    widest = DILATIONS[-1]
    tab = pl.BlockSpec((seq, HEAD_DIM), lambda b, h: (0, 0))
    return pl.pallas_call(
        _dilated_kernel,
        grid=(bsz, B_HEADS),
        in_specs=[pl.BlockSpec(memory_space=pl.ANY), tab, tab, pl.BlockSpec(tables[2].shape, lambda b, h: (0, 0))],
        out_specs=pl.BlockSpec((None, seq, HEAD_DIM), lambda b, h: (b, 0, h)),
        out_shape=jax.ShapeDtypeStruct((bsz, seq, B_HEADS * HEAD_DIM), BF16),
        scratch_shapes=[pltpu.VMEM((2, 3, seq, HEAD_DIM), F32), pltpu.SemaphoreType.DMA((2,))]
                       + [pltpu.VMEM((seq, HEAD_DIM), F32)] * 5,
        compiler_params=_params(("arbitrary", "arbitrary")),
        name="dilated_attention",
    )(proj3.reshape(bsz, seq // widest, widest, width), _residue_major(tables[0], widest),
      _residue_major(tables[1], widest), tables[2])


SWA_UNROLL = 16


def _swa_kernel(sink_ref, q0_ref, q1_ref, q2_ref, k_ref, v_ref, c_ref, s_ref, rot_ref, o_ref, qr_ref, kr_ref):
    seq = k_ref.shape[0]
    blk = ATT_BLOCK
    grp = pl.program_id(1)
    c, s, rot = c_ref[...], s_ref[...], rot_ref[...]
    for j, q_ref in enumerate((q0_ref, q1_ref, q2_ref)):
        qr_ref[j] = _rope(q_ref[...], c, s, rot).astype(BF16)
    kr_ref[...] = _rope(k_ref[...], c, s, rot).astype(BF16)
    cur_mask, prev_mask = _band_masks(C_MAX_LAG)
    sinks = [jnp.full((blk, 1), sink_ref[grp * C_REP + j] * LOG2_E, F32) for j in range(C_REP)]

    def group(g, carry):
        blocks, dests = [], []
        for u in range(SWA_UNROLL):
            n = g * SWA_UNROLL + u
            rows = pl.ds(pl.multiple_of(n * blk, blk), blk)
            prev = pl.ds(pl.multiple_of(jnp.maximum(n - 1, 0) * blk, blk), blk)
            ks, vs = [kr_ref[rows, :], kr_ref[prev, :]], [v_ref[rows, :], v_ref[prev, :]]
            masks = [cur_mask, jnp.logical_and(prev_mask, n > 0)]
            for j in range(C_REP):
                blocks.append((qr_ref[j, rows, :], ks, vs, masks, sinks[j]))
                dests.append((rows, j))
        for (rows, j), (m, l, acc) in zip(dests, _tile_softmax_many(blocks)):
            out = acc / (l + jnp.exp2(sinks[j] - m))
            o_ref[rows, j * HEAD_DIM:(j + 1) * HEAD_DIM] = out.astype(o_ref.dtype)
        return carry

    lax.fori_loop(0, seq // (blk * SWA_UNROLL), group, 0)


def _swa(proj3, sinks, tables):
    bsz, seq, _ = proj3.shape
    qcol = lambda j: pl.BlockSpec((None, seq, HEAD_DIM), lambda b, g, j=j: (b, 0, QC_BLK + g * C_REP + j))
    col = lambda blk: pl.BlockSpec((None, seq, HEAD_DIM), lambda b, g, blk=blk: (b, 0, blk + g))
    tab = pl.BlockSpec((seq, HEAD_DIM), lambda b, g: (0, 0))
    return pl.pallas_call(
        _swa_kernel,
        grid=(bsz, C_KV_HEADS),
        in_specs=[pl.BlockSpec(memory_space=pltpu.SMEM), qcol(0), qcol(1), qcol(2), col(KC_BLK), col(VC_BLK),
                  tab, tab, pl.BlockSpec(tables[2].shape, lambda b, g: (0, 0))],
        out_specs=pl.BlockSpec((None, seq, C_REP * HEAD_DIM), lambda b, g: (b, 0, g)),
        out_shape=jax.ShapeDtypeStruct((bsz, seq, C_HEADS * HEAD_DIM), BF16),
        scratch_shapes=[pltpu.VMEM((C_REP, seq, HEAD_DIM), BF16), pltpu.VMEM((seq, HEAD_DIM), BF16)],
        compiler_params=_params(("parallel", "parallel")),
        name="swa_sink_attention",
    )(sinks, proj3, proj3, proj3, proj3, proj3, *tables)


def _layer_norm_store(z, g_ref, b_ref, o32_ref, o16_ref):
    mu = jnp.mean(z, axis=-1, keepdims=True)
    zc = z - mu
    var = jnp.mean(zc * zc, axis=-1, keepdims=True)
    y = zc * lax.rsqrt(var + LN_EPS) * g_ref[...] + b_ref[...]
    o32_ref[...] = y
    o16_ref[...] = y.astype(BF16)


def _stage_weight_bf16(w_hbm, layer, w16_ref, stage_ref, sem):
    chunk = stage_ref.shape[1]
    n_chunks = w16_ref.shape[0] // chunk

    def copy(c):
        slot = c % 2
        return pltpu.make_async_copy(w_hbm.at[layer, pl.ds(c * chunk, chunk), :], stage_ref.at[slot], sem.at[slot])

    copy(0).start()
    for c in range(n_chunks):
        if c + 1 < n_chunks:
            copy(c + 1).start()
        copy(c).wait()
        w16_ref[c * chunk:(c + 1) * chunk, :] = stage_ref[c % 2].astype(BF16)


def _resident_weight_scratch(k, n):
    return [pltpu.VMEM((k, n), BF16), pltpu.VMEM((2, WEIGHT_STAGE_ROWS, n), F32), pltpu.SemaphoreType.DMA((2,))]


def _out_proj_kernel(alpha, layer, oa_ref, ob_ref, oc_ref, w_hbm, x_ref, g_ref, b_ref, o32_ref, o16_ref,
                     w16_ref, stage_ref, sem):
    @pl.when(pl.program_id(0) == 0)
    def _():
        _stage_weight_bf16(w_hbm, layer, w16_ref, stage_ref, sem)

    ka, kb = oa_ref.shape[1], ob_ref.shape[1]
    y = (jnp.dot(oa_ref[...], w16_ref[0:ka, :], preferred_element_type=F32)
         + jnp.dot(ob_ref[...], w16_ref[ka:ka + kb, :], preferred_element_type=F32)
         + jnp.dot(oc_ref[...], w16_ref[ka + kb:, :], preferred_element_type=F32))
    _layer_norm_store(alpha * x_ref[...] + y, g_ref, b_ref, o32_ref, o16_ref)


def _layer_vec(layer, d):
    return pl.BlockSpec((None, 1, d), lambda i, layer=layer: (layer, 0, 0))


def _out_proj_ln(oa, ob, oc, w_out, layer, x, g, b, alpha, bm):
    m, d = x.shape
    rows = lambda a: pl.BlockSpec((bm, a.shape[1]), lambda i: (i, 0))
    return pl.pallas_call(
        functools.partial(_out_proj_kernel, alpha, layer),
        grid=(m // bm,),
        in_specs=[rows(oa), rows(ob), rows(oc), pl.BlockSpec(memory_space=pl.ANY), rows(x),
                  _layer_vec(layer, d), _layer_vec(layer, d)],
        out_specs=[rows(x), rows(x)],
        out_shape=[jax.ShapeDtypeStruct((m, d), F32), jax.ShapeDtypeStruct((m, d), BF16)],
        scratch_shapes=_resident_weight_scratch(w_out.shape[1], d),
        compiler_params=_params(("arbitrary",)),
        name="out_proj_ln",
    )(oa, ob, oc, w_out, x, g, b)


def _in_proj_kernel(x_ref, w_ref, o_ref):
    o_ref[...] = jnp.dot(x_ref[...], w_ref[...].astype(BF16), preferred_element_type=F32).astype(o_ref.dtype)


def _in_proj(x16, w_in, layer, bm, bn):
    m, k = x16.shape
    n = w_in.shape[2]
    return pl.pallas_call(
        _in_proj_kernel,
        grid=(m // bm, n // bn),
        in_specs=[pl.BlockSpec((bm, k), lambda i, j: (i, 0)),
                  pl.BlockSpec((None, k, bn), lambda i, j: (layer, 0, j))],
        out_specs=pl.BlockSpec((bm, bn), lambda i, j: (i, j)),
        out_shape=jax.ShapeDtypeStruct((m, n), F32),
        compiler_params=_params(("parallel", "parallel")),
        name="in_proj",
    )(x16, w_in)


FFN_UP_SUBTILES = 2


def _ffn_up_kernel(tiles_per_seq, x_ref, wg_ref, wu_ref, cw_ref, cb_ref, h_ref, g_ref, w16_ref):
    bm = x_ref.shape[0]
    bn = h_ref.shape[1]
    halo = g_ref.shape[0] - bm

    @pl.when(pl.program_id(1) == 0)
    def _():
        w16_ref[:, 0:bn] = wg_ref[...].astype(BF16)
        w16_ref[:, bn:] = wu_ref[...].astype(BF16)

    @pl.when(pl.program_id(1) % tiles_per_seq == 0)
    def _():
        g_ref[0:halo, :] = jnp.zeros((halo, g_ref.shape[1]), F32)

    sub = bm // FFN_UP_SUBTILES
    for t in range(FFN_UP_SUBTILES):
        first = halo + t * sub
        gu = jnp.dot(x_ref[t * sub:(t + 1) * sub, :], w16_ref[...], preferred_element_type=F32)
        g_ref[first:first + sub, :] = gu[:, 0:bn]
        u = gu[:, bn:]
        gc = cb_ref[...]
        for j in range(CONV_WIDTH):
            lag = CONV_WIDTH - 1 - j
            gc = gc + cw_ref[j:j + 1, :] * g_ref[first - lag:first - lag + sub, :]
        h_ref[t * sub:(t + 1) * sub, :] = (gc * jax.nn.sigmoid(gc) * u).astype(h_ref.dtype)
    g_ref[0:halo, :] = g_ref[bm:bm + halo, :]


def _ffn_up(x16, w_gate, w_up, conv_w, conv_b, layer, seq, bm, bn):
    m, d = x16.shape
    f = w_gate.shape[2]
    halo = SUBLANES
    wspec = pl.BlockSpec((None, d, bn), lambda j, i: (layer, 0, j))
    return pl.pallas_call(
        functools.partial(_ffn_up_kernel, seq // bm),
        grid=(f // bn, m // bm),
        in_specs=[pl.BlockSpec((bm, d), lambda j, i: (i, 0)), wspec, wspec,
                  pl.BlockSpec((None, CONV_WIDTH, bn), lambda j, i: (layer, 0, j)),
                  pl.BlockSpec((None, 1, bn), lambda j, i: (layer, 0, j))],
        out_specs=pl.BlockSpec((bm, bn), lambda j, i: (i, j)),
        out_shape=jax.ShapeDtypeStruct((m, f), BF16),
        scratch_shapes=[pltpu.VMEM((bm + halo, bn), F32), pltpu.VMEM((d, 2 * bn), BF16)],
        compiler_params=_params(("parallel", "arbitrary")),
        name="ffn_up_conv_gate",
    )(x16, w_gate, w_up, conv_w, conv_b)


def _ffn_down_kernel(alpha, layer, h_ref, w_hbm, x_ref, g_ref, b_ref, o32_ref, o16_ref, w16_ref, stage_ref, sem):
    @pl.when(pl.program_id(0) == 0)
    def _():
        _stage_weight_bf16(w_hbm, layer, w16_ref, stage_ref, sem)

    y = jnp.dot(h_ref[...], w16_ref[...], preferred_element_type=F32)
    _layer_norm_store(alpha * x_ref[...] + y, g_ref, b_ref, o32_ref, o16_ref)


def _ffn_down_ln(h, w_down, layer, x, g, b, alpha, bm):
    m, d = x.shape
    f = h.shape[1]
    rows = pl.BlockSpec((bm, d), lambda i: (i, 0))
    return pl.pallas_call(
        functools.partial(_ffn_down_kernel, alpha, layer),
        grid=(m // bm,),
        in_specs=[pl.BlockSpec((bm, f), lambda i: (i, 0)), pl.BlockSpec(memory_space=pl.ANY), rows,
                  _layer_vec(layer, d), _layer_vec(layer, d)],
        out_specs=[rows, rows],
        out_shape=[jax.ShapeDtypeStruct((m, d), F32), jax.ShapeDtypeStruct((m, d), BF16)],
        scratch_shapes=_resident_weight_scratch(f, d),
        compiler_params=_params(("arbitrary",)),
        name="ffn_down_ln",
    )(h, w_down, x, g, b)


def kernel(x, w_in, lb_logits, a_norm_w, c_sinks, w_out, ln1_g, ln1_b, w_gate, w_up, conv_w, conv_b, w_down,
           ln2_g, ln2_b):
    bsz, seq, d = x.shape
    depth = w_in.shape[0]
    alpha = (2 * depth) ** 0.25
    tables = _rope_tables(seq)
    lbs = jnp.cumsum(jax.nn.softmax(lb_logits.astype(F32), axis=0), axis=0)
    lbs = lbs - lbs[0]
    vec3 = lambda a: a.reshape(depth, 1, -1)
    ln1_g, ln1_b, ln2_g, ln2_b, conv_b = vec3(ln1_g), vec3(ln1_b), vec3(ln2_g), vec3(ln2_b), vec3(conv_b)
    x32 = x.reshape(bsz * seq, d)
    x16 = x32.astype(BF16)
    plan = _tile_plan(bsz * seq, d, w_out.shape[1], w_gate.shape[2])
    for l in range(depth):
        proj3 = _in_proj(x16, w_in, l, *plan.in_proj).reshape(bsz, seq, -1)
        oa = _hgrn2(proj3, lbs[l], a_norm_w[l]).reshape(bsz * seq, -1)
        ob = _dilated(proj3, tables).reshape(bsz * seq, -1)
        oc = _swa(proj3, c_sinks[l], tables).reshape(bsz * seq, -1)
        x32, x16 = _out_proj_ln(oa, ob, oc, w_out, l, x32, ln1_g, ln1_b, alpha, plan.out_proj_rows)
        h = _ffn_up(x16, w_gate, w_up, conv_w, conv_b, l, seq, *plan.ffn_up)
        x32, x16 = _ffn_down_ln(h, w_down, l, x32, ln2_g, ln2_b, alpha, plan.ffn_down_rows)
    return x32.reshape(bsz, seq, d)
```

```python
import functools
from typing import NamedTuple

import numpy as np
import jax
import jax.numpy as jnp
from jax import lax
from jax.experimental import pallas as pl
from jax.experimental.pallas import tpu as pltpu

HEAD_DIM = 128
A_HEADS = 4
B_HEADS = 6
C_HEADS = 6
C_KV_HEADS = 2
C_REP = C_HEADS // C_KV_HEADS
DILATIONS = (1, 4, 16)
ATT_BLOCK = 128
C_MAX_LAG = 127
ROPE_THETA = 500000.0
ROPE_DIM = HEAD_DIM // 4
ROPE_HALF = ROPE_DIM // 2
CONV_WIDTH = 3
LN_EPS = 1e-5

QA_BLK, FA_BLK, IA_BLK, GA_BLK = 0, 4, 8, 12
QB_BLK, KB_BLK, VB_BLK = 16, 22, 28
QC_BLK, KC_BLK, VC_BLK = 34, 40, 42

HGRN_CHUNK = 128
HGRN_LEVELS = (64, 32, 16, 8, 4, 2, 1)

VMEM_LIMIT_BYTES = 60 * 1024 * 1024

F32 = jnp.float32
BF16 = jnp.bfloat16
NEG_BIG = -1e30
LOG2_E = 1.4426950408889634


def _params(semantics):
    return pltpu.CompilerParams(dimension_semantics=semantics, vmem_limit_bytes=VMEM_LIMIT_BYTES)


SUBLANES = 8
WEIGHT_STAGE_ROWS = 512


class _TilePlan(NamedTuple):
    in_proj: tuple
    out_proj_rows: int
    ffn_up: tuple
    ffn_down_rows: int


def _tile_plan(rows, d_model, d_mix, d_ff):
    plan = _TilePlan(in_proj=(2048, 512), out_proj_rows=512, ffn_up=(2048, 512), ffn_down_rows=256)
    f32, bf16, two = 4, 2, 2
    staged = two * WEIGHT_STAGE_ROWS * d_model * f32
    bm, bn = plan.in_proj
    in_proj = (two * bm * d_model * f32 + bm * d_model * bf16
               + two * d_model * bn * f32 + d_model * bn * bf16 + two * bm * bn * f32)
    bm = plan.out_proj_rows
    out_proj = d_mix * d_model * bf16 + staged + two * bm * (d_mix * bf16 + d_model * (f32 + f32 + bf16))
    bm, bn = plan.ffn_up
    ffn_up = (two * bm * d_model * bf16 + two * two * d_model * bn * f32 + d_model * two * bn * bf16
              + (bm + SUBLANES) * bn * f32 + two * bm * bn * bf16 + bm * two * bn * f32)
    bm = plan.ffn_down_rows
    ffn_down = d_ff * d_model * bf16 + staged + two * bm * (d_ff * bf16 + d_model * (f32 + f32 + bf16))
    for name, need in (("in_proj", in_proj), ("out_proj", out_proj), ("ffn_up", ffn_up), ("ffn_down", ffn_down)):
        assert need <= VMEM_LIMIT_BYTES, (name, need)
    assert rows % max(plan.in_proj[0], plan.ffn_up[0]) == 0 and d_ff % plan.ffn_up[1] == 0
    return plan


def _rope_tables(seq):
    inv = ROPE_THETA ** (-jnp.arange(0, ROPE_DIM, 2, dtype=F32) / ROPE_DIM)
    ang = jnp.arange(seq, dtype=F32)[:, None] * inv[None, :]
    cos, sin = jnp.cos(ang), jnp.sin(ang)
    rest = HEAD_DIM - ROPE_DIM
    c = jnp.concatenate([cos, cos, jnp.ones((seq, rest), F32)], -1)
    s = jnp.concatenate([sin, sin, jnp.zeros((seq, rest), F32)], -1)
    rot = np.zeros((HEAD_DIM, HEAD_DIM), np.float32)
    lane = np.arange(ROPE_HALF)
    rot[lane + ROPE_HALF, lane] = -1.0
    rot[lane, lane + ROPE_HALF] = 1.0
    return c, s, jnp.asarray(np.concatenate([rot, rot], 0), BF16)


def _rope(x, c, s, rot2):
    hi = x.astype(BF16)
    lo = (x - hi.astype(F32)).astype(BF16)
    swapped = jnp.dot(jnp.concatenate([hi, lo], axis=1), rot2, preferred_element_type=F32)
    return x * c + swapped * s


def _hgrn_constants():
    c = HGRN_CHUNK
    i = np.arange(c)[:, None]
    j = np.arange(c)[None, :]
    lvl = np.full((c, c), -1, np.int32)
    for n, h in enumerate(HGRN_LEVELS):
        lvl[(i // (2 * h) == j // (2 * h)) & (i // h != j // h) & (i > j)] = n
    lvl[np.arange(c), np.arange(c)] = len(HGRN_LEVELS)
    return jnp.asarray((j <= i).astype(np.float32), BF16), jnp.asarray(lvl)


def _midpoint_log_decay(lf, b, h, row):
    c, dk = b.shape
    odd = ((row // h) % 2) == 1
    if h == 1:
        return jnp.where(odd, lf, 0.0)
    if h == 2:
        below = pltpu.roll(lf, c - 1, 0)
        above = pltpu.roll(lf, 1, 0)
        place = row % 4
        return jnp.where(place == 0, below, jnp.where(place == 1, 0.0, jnp.where(place == 2, lf, lf + above)))
    blocks = b.reshape(c // (2 * h), 2 * h, dk)
    mid = jnp.broadcast_to(blocks[:, h - 1:h, :], blocks.shape).reshape(c, dk)
    return jnp.where(odd, b - mid, mid - b)


def _split3(x):
    p1 = x.astype(BF16)
    r1 = x - p1.astype(F32)
    p2 = r1.astype(BF16)
    p3 = (r1 - p2.astype(F32)).astype(BF16)
    return p1, p2, p3


def _dot_nt(a, b, **kw):
    return lax.dot_general(a, b, (((1,), (1,)), ((), ())), preferred_element_type=F32, **kw)


def _dot_tn(a, b, **kw):
    return lax.dot_general(a, b, (((0,), (0,)), ((), ())), preferred_element_type=F32, **kw)


def _hgrn_chunk_heads(inputs, nw, e_all, lvl, row):
    c = HGRN_CHUNK
    heads = range(len(inputs))
    qs = [q * jax.nn.sigmoid(q) for q, _, _, _, _, _ in inputs]
    fg = [lb + (1.0 - lb) * jax.nn.sigmoid(fl) for _, fl, _, _, _, lb in inputs]
    kk = [1.0 - f for f in fg]
    lf = [jnp.log2(f) for f in fg]
    parts = [_split3(x) for x in lf]
    b12 = [jnp.dot(e_all, jnp.concatenate([p1, p2], axis=1), preferred_element_type=F32) for p1, p2, _ in parts]
    b3 = [jnp.dot(e_all, p3, preferred_element_type=F32) for _, _, p3 in parts]
    b = [x[:, :HEAD_DIM] + x[:, HEAD_DIM:] + y for x, y in zip(b12, b3)]
    att = [jnp.where(lvl == len(HGRN_LEVELS), jnp.sum(qs[i] * kk[i], axis=-1, keepdims=True), 0.0) for i in heads]
    for n, h in enumerate(HGRN_LEVELS):
        odd = ((row // h) % 2) == 1
        xh = [(jnp.where(odd, qs[i], kk[i]) * jnp.exp2(_midpoint_log_decay(lf[i], b[i], h, row))).astype(BF16)
              for i in heads]
        pair = [_dot_nt(x, x) for x in xh]
        att = [jnp.where(lvl == n, pair[i], att[i]) for i in heads]
    b_end = [x[c - 1:c] for x in b]
    q_in = [(qs[i] * jnp.exp2(b[i])).astype(BF16) for i in heads]
    k_end = [(kk[i] * jnp.exp2(b_end[i] - b[i])).astype(BF16) for i in heads]
    v16 = [v.astype(BF16) for _, _, v, _, _, _ in inputs]
    o = [jnp.dot(att[i].astype(BF16), v16[i], preferred_element_type=F32)
         + _dot_nt(q_in[i], inputs[i][4].astype(BF16)) for i in heads]
    st_new = [inputs[i][4] * jnp.exp2(b_end[i]) + _dot_tn(v16[i], k_end[i]) for i in heads]
    o = [x * lax.rsqrt(jnp.mean(x * x, axis=-1, keepdims=True) + LN_EPS) * nw for x in o]
    out = [(o[i] * (inputs[i][3] * jax.nn.sigmoid(inputs[i][3]))).astype(BF16) for i in heads]
    return list(zip(out, st_new))


HGRN_BATCH = 2
HGRN_SEQ_TILE = 1024


def _hgrn_kernel(q_ref, f_ref, i_ref, g_ref, lb_ref, nw_ref, e_ref, lvl_ref, o_ref, st_ref):
    c = HGRN_CHUNK
    n_batch, tile_rows = q_ref.shape[0], q_ref.shape[1]

    @pl.when(pl.program_id(1) == 0)
    def _():
        st_ref[...] = jnp.zeros_like(st_ref)

    nw = nw_ref[...]
    row = lax.broadcasted_iota(jnp.int32, (c, 1), 0)

    def chunk(ci, carry):
        rows = pl.ds(pl.multiple_of(ci * c, c), c)
        e_all, lvl = e_ref[...], lvl_ref[...]
        inputs = []
        for n in range(n_batch):
            for h in range(A_HEADS):
                cols = slice(h * HEAD_DIM, (h + 1) * HEAD_DIM)
                inputs.append((q_ref[n, rows, cols], f_ref[n, rows, cols], i_ref[n, rows, cols], g_ref[n, rows, cols],
                               st_ref[n * A_HEADS + h], lb_ref[:, cols]))
        for k, (o, st_new) in enumerate(_hgrn_chunk_heads(inputs, nw, e_all, lvl, row)):
            n, h = divmod(k, A_HEADS)
            o_ref[n, rows, h * HEAD_DIM:(h + 1) * HEAD_DIM] = o
            st_ref[k] = st_new
        return carry

    lax.fori_loop(0, tile_rows // c, chunk, 0)


def _hgrn2(proj3, lb, norm_w):
    bsz, seq, _ = proj3.shape
    e_all, lvl = _hgrn_constants()
    width = A_HEADS * HEAD_DIM
    col = lambda blk: pl.BlockSpec((HGRN_BATCH, HGRN_SEQ_TILE, width), lambda b, s, blk=blk: (b, s, blk // A_HEADS))
    const = lambda shape: pl.BlockSpec(shape, lambda b, s: (0,) * len(shape))
    return pl.pallas_call(
        _hgrn_kernel,
        grid=(bsz // HGRN_BATCH, seq // HGRN_SEQ_TILE),
        in_specs=[col(QA_BLK), col(FA_BLK), col(IA_BLK), col(GA_BLK),
                  const((1, width)), const((1, HEAD_DIM)), const(e_all.shape), const(lvl.shape)],
        out_specs=pl.BlockSpec((HGRN_BATCH, HGRN_SEQ_TILE, width), lambda b, s: (b, s, 0)),
        out_shape=jax.ShapeDtypeStruct((bsz, seq, width), BF16),
        scratch_shapes=[pltpu.VMEM((HGRN_BATCH * A_HEADS, HEAD_DIM, HEAD_DIM), F32)],
        compiler_params=_params(("parallel", "arbitrary")),
        name="hgrn2",
    )(proj3, proj3, proj3, proj3, lb.reshape(1, width), norm_w.reshape(1, HEAD_DIM), e_all, lvl)


def _band_masks(max_lag):
    r = lax.broadcasted_iota(jnp.int32, (ATT_BLOCK, ATT_BLOCK), 0)
    col = lax.broadcasted_iota(jnp.int32, (ATT_BLOCK, ATT_BLOCK), 1)
    return col <= r, (ATT_BLOCK + r - col) <= max_lag


def _tile_softmax_many(blocks):
    scale = HEAD_DIM ** -0.5 * LOG2_E
    s_all = [[jnp.where(mask, _dot_nt(q16, k.astype(BF16)) * scale, NEG_BIG) for k, mask in zip(ks, masks)]
             for q16, ks, _, masks, _ in blocks]
    m_all = []
    for s, (_, _, _, _, m_floor) in zip(s_all, blocks):
        m = jnp.max(functools.reduce(jnp.maximum, s), axis=-1, keepdims=True)
        m_all.append(m if m_floor is None else jnp.maximum(m, m_floor))
    e_all = [[jnp.exp2(si - m) for si in s] for s, m in zip(s_all, m_all)]
    acc_all = [functools.reduce(jnp.add, [jnp.dot(ei.astype(BF16), v.astype(BF16), preferred_element_type=F32)
                                          for ei, v in zip(e, vs)])
               for e, (_, _, vs, _, _) in zip(e_all, blocks)]
    l_all = [jnp.sum(functools.reduce(jnp.add, e), axis=-1, keepdims=True) for e in e_all]
    return list(zip(m_all, l_all, acc_all))


DIL_UNROLL = 16


def _dilated_kernel(proj_hbm, c_ref, s_ref, rot_ref, o_ref, qkv_ref, sem, qr_ref, kr_ref, m_ref, l_ref, acc_ref):
    seq = o_ref.shape[0]
    blk = ATT_BLOCK
    widest = DILATIONS[-1]
    per_residue = seq // widest
    step = pl.program_id(0) * pl.num_programs(1) + pl.program_id(1)
    n_steps = pl.num_programs(0) * pl.num_programs(1)
    slot = step % 2

    def qkv_copies(of_step, into_slot):
        b, h = of_step // pl.num_programs(1), of_step % pl.num_programs(1)
        copies = []
        for which, first_blk in enumerate((QB_BLK, KB_BLK, VB_BLK)):
            col = pl.multiple_of((first_blk + h) * HEAD_DIM, HEAD_DIM)
            for r in range(widest):
                copies.append(pltpu.make_async_copy(
                    proj_hbm.at[b, :, r, pl.ds(col, HEAD_DIM)],
                    qkv_ref.at[into_slot, which, pl.ds(r * per_residue, per_residue), :],
                    sem.at[into_slot]))
        return copies

    @pl.when(step == 0)
    def _():
        for copy in qkv_copies(step, slot):
            copy.start()

    @pl.when(step + 1 < n_steps)
    def _():
        for copy in qkv_copies(step + 1, 1 - slot):
            copy.start()

    for copy in qkv_copies(step, slot):
        copy.wait()

    c, s, rot = c_ref[...], s_ref[...], rot_ref[...]
    qr_ref[...] = _rope(qkv_ref[slot, 0], c, s, rot)
    kr_ref[...] = _rope(qkv_ref[slot, 1], c, s, rot)
    vr_ref = qkv_ref.at[slot, 2]

    def chunk_rows(dil):
        return blk * dil // widest

    def block_masks(dil):
        def position(i):
            return (widest // dil) * (i % chunk_rows(dil)) + i // chunk_rows(dil)
        row = position(lax.broadcasted_iota(jnp.int32, (blk, blk), 0))
        col = position(lax.broadcasted_iota(jnp.int32, (blk, blk), 1))
        return col <= row, col >= row

    def chunks_of(dil, r, n):
        size = chunk_rows(dil)
        return [pl.ds(pl.multiple_of((r + dil * a) * per_residue + size * n, size), size)
                for a in range(widest // dil)]

    def gather(ref, chunks):
        return jnp.concatenate([ref[rows, :] for rows in chunks], axis=0)

    def scatter(ref, chunks, value):
        size = value.shape[0] // len(chunks)
        for a, rows in enumerate(chunks):
            ref[rows, :] = value[a * size:(a + 1) * size]

    def run_group(specs, masks_of_pattern, first):
        cur_mask, prev_mask = masks_of_pattern
        loaded = []
        for rows, prev, prev_ok in specs:
            ks, vs, masks = [gather(kr_ref, rows)], [gather(vr_ref, rows)], [cur_mask]
            if prev is not None:
                ks.append(gather(kr_ref, prev))
                vs.append(gather(vr_ref, prev))
                masks.append(jnp.logical_and(prev_mask, prev_ok))
            state = None if first else (gather(m_ref, rows), gather(l_ref, rows), gather(acc_ref, rows))
            loaded.append((gather(qr_ref, rows).astype(BF16), ks, vs, masks, state))
        pieces = _tile_softmax_many([(q16, ks, vs, masks, None if state is None else state[0])
                                     for q16, ks, vs, masks, state in loaded])
        results = []
        for (m, l, acc), (_, _, _, _, state) in zip(pieces, loaded):
            if state is None:
                shape = (blk, HEAD_DIM)
                results.append((jnp.broadcast_to(m, shape), jnp.broadcast_to(l, shape), acc))
            else:
                m_old, l_old, acc_old = state
                alpha = jnp.exp2(m_old - m)
                results.append((m, alpha * l_old + l, alpha * acc_old + acc))
        for (rows, _, _), (m, l, acc) in zip(specs, results):
            scatter(m_ref, rows, m)
            scatter(l_ref, rows, l)
            scatter(acc_ref, rows, acc)

    n_query_blocks = seq // blk
    for dil in DILATIONS:
        n_blocks = n_query_blocks // dil
        masks_of_pattern = block_masks(dil)

        def group(g, carry, dil=dil, n_blocks=n_blocks, masks_of_pattern=masks_of_pattern):
            specs = []
            for u in range(DIL_UNROLL):
                e = g * DIL_UNROLL + u
                r, n = e % dil, e // dil
                if n_blocks == 1:
                    specs.append((chunks_of(dil, r, n), None, None))
                else:
                    specs.append((chunks_of(dil, r, n), chunks_of(dil, r, jnp.maximum(n - 1, 0)), n > 0))
            run_group(specs, masks_of_pattern, first=(dil == DILATIONS[0]))
            return carry

        lax.fori_loop(0, n_query_blocks // DIL_UNROLL, group, 0)

    for r in range(widest):
        rows = slice(r * per_residue, (r + 1) * per_residue)
        qr_ref[pl.ds(r, per_residue, stride=widest), :] = acc_ref[rows, :] / l_ref[rows, :]
    o_ref[...] = qr_ref[...].astype(o_ref.dtype)


def _residue_major(table, widest):
    seq, width = table.shape
    return table.reshape(seq // widest, widest, width).transpose(1, 0, 2).reshape(seq, width)


def _dilated(proj3, tables):
    bsz, seq, width = proj3.shape
    widest = DILATIONS[-1]
    tab = pl.BlockSpec((seq, HEAD_DIM), lambda b, h: (0, 0))
    return pl.pallas_call(
        _dilated_kernel,
        grid=(bsz, B_HEADS),
        in_specs=[pl.BlockSpec(memory_space=pl.ANY), tab, tab, pl.BlockSpec(tables[2].shape, lambda b, h: (0, 0))],
        out_specs=pl.BlockSpec((None, seq, HEAD_DIM), lambda b, h: (b, 0, h)),
        out_shape=jax.ShapeDtypeStruct((bsz, seq, B_HEADS * HEAD_DIM), BF16),
        scratch_shapes=[pltpu.VMEM((2, 3, seq, HEAD_DIM), F32), pltpu.SemaphoreType.DMA((2,))]
                       + [pltpu.VMEM((seq, HEAD_DIM), F32)] * 5,
        compiler_params=_params(("arbitrary", "arbitrary")),
        name="dilated_attention",
    )(proj3.reshape(bsz, seq // widest, widest, width), _residue_major(tables[0], widest),
      _residue_major(tables[1], widest), tables[2])


SWA_UNROLL = 16


def _swa_kernel(sink_ref, q0_ref, q1_ref, q2_ref, k_ref, v_ref, c_ref, s_ref, rot_ref, o_ref, qr_ref, kr_ref):
    seq = k_ref.shape[0]
    blk = ATT_BLOCK
    grp = pl.program_id(1)
    c, s, rot = c_ref[...], s_ref[...], rot_ref[...]
    for j, q_ref in enumerate((q0_ref, q1_ref, q2_ref)):
        qr_ref[j] = _rope(q_ref[...], c, s, rot).astype(BF16)
    kr_ref[...] = _rope(k_ref[...], c, s, rot).astype(BF16)
    cur_mask, prev_mask = _band_masks(C_MAX_LAG)
    sinks = [jnp.full((blk, 1), sink_ref[grp * C_REP + j] * LOG2_E, F32) for j in range(C_REP)]

    def group(g, carry):
        blocks, dests = [], []
        for u in range(SWA_UNROLL):
            n = g * SWA_UNROLL + u
            rows = pl.ds(pl.multiple_of(n * blk, blk), blk)
            prev = pl.ds(pl.multiple_of(jnp.maximum(n - 1, 0) * blk, blk), blk)
            ks, vs = [kr_ref[rows, :], kr_ref[prev, :]], [v_ref[rows, :], v_ref[prev, :]]
            masks = [cur_mask, jnp.logical_and(prev_mask, n > 0)]
            for j in range(C_REP):
                blocks.append((qr_ref[j, rows, :], ks, vs, masks, sinks[j]))
                dests.append((rows, j))
        for (rows, j), (m, l, acc) in zip(dests, _tile_softmax_many(blocks)):
            out = acc / (l + jnp.exp2(sinks[j] - m))
            o_ref[rows, j * HEAD_DIM:(j + 1) * HEAD_DIM] = out.astype(o_ref.dtype)
        return carry

    lax.fori_loop(0, seq // (blk * SWA_UNROLL), group, 0)


def _swa(proj3, sinks, tables):
    bsz, seq, _ = proj3.shape
    qcol = lambda j: pl.BlockSpec((None, seq, HEAD_DIM), lambda b, g, j=j: (b, 0, QC_BLK + g * C_REP + j))
    col = lambda blk: pl.BlockSpec((None, seq, HEAD_DIM), lambda b, g, blk=blk: (b, 0, blk + g))
    tab = pl.BlockSpec((seq, HEAD_DIM), lambda b, g: (0, 0))
    return pl.pallas_call(
        _swa_kernel,
        grid=(bsz, C_KV_HEADS),
        in_specs=[pl.BlockSpec(memory_space=pltpu.SMEM), qcol(0), qcol(1), qcol(2), col(KC_BLK), col(VC_BLK),
                  tab, tab, pl.BlockSpec(tables[2].shape, lambda b, g: (0, 0))],
        out_specs=pl.BlockSpec((None, seq, C_REP * HEAD_DIM), lambda b, g: (b, 0, g)),
        out_shape=jax.ShapeDtypeStruct((bsz, seq, C_HEADS * HEAD_DIM), BF16),
        scratch_shapes=[pltpu.VMEM((C_REP, seq, HEAD_DIM), BF16), pltpu.VMEM((seq, HEAD_DIM), BF16)],
        compiler_params=_params(("parallel", "parallel")),
        name="swa_sink_attention",
    )(sinks, proj3, proj3, proj3, proj3, proj3, *tables)


def _layer_norm_store(z, g_ref, b_ref, o32_ref, o16_ref):
    mu = jnp.mean(z, axis=-1, keepdims=True)
    zc = z - mu
    var = jnp.mean(zc * zc, axis=-1, keepdims=True)
    y = zc * lax.rsqrt(var + LN_EPS) * g_ref[...] + b_ref[...]
    o32_ref[...] = y
    o16_ref[...] = y.astype(BF16)


def _stage_weight_bf16(w_hbm, layer, w16_ref, stage_ref, sem):
    chunk = stage_ref.shape[1]
    n_chunks = w16_ref.shape[0] // chunk

    def copy(c):
        slot = c % 2
        return pltpu.make_async_copy(w_hbm.at[layer, pl.ds(c * chunk, chunk), :], stage_ref.at[slot], sem.at[slot])

    copy(0).start()
    for c in range(n_chunks):
        if c + 1 < n_chunks:
            copy(c + 1).start()
        copy(c).wait()
        w16_ref[c * chunk:(c + 1) * chunk, :] = stage_ref[c % 2].astype(BF16)


def _resident_weight_scratch(k, n):
    return [pltpu.VMEM((k, n), BF16), pltpu.VMEM((2, WEIGHT_STAGE_ROWS, n), F32), pltpu.SemaphoreType.DMA((2,))]


def _out_proj_kernel(alpha, layer, oa_ref, ob_ref, oc_ref, w_hbm, x_ref, g_ref, b_ref, o32_ref, o16_ref,
                     w16_ref, stage_ref, sem):
    @pl.when(pl.program_id(0) == 0)
    def _():
        _stage_weight_bf16(w_hbm, layer, w16_ref, stage_ref, sem)

    ka, kb = oa_ref.shape[1], ob_ref.shape[1]
    y = (jnp.dot(oa_ref[...], w16_ref[0:ka, :], preferred_element_type=F32)
         + jnp.dot(ob_ref[...], w16_ref[ka:ka + kb, :], preferred_element_type=F32)
         + jnp.dot(oc_ref[...], w16_ref[ka + kb:, :], preferred_element_type=F32))
    _layer_norm_store(alpha * x_ref[...] + y, g_ref, b_ref, o32_ref, o16_ref)


def _layer_vec(layer, d):
    return pl.BlockSpec((None, 1, d), lambda i, layer=layer: (layer, 0, 0))


def _out_proj_ln(oa, ob, oc, w_out, layer, x, g, b, alpha, bm):
    m, d = x.shape
    rows = lambda a: pl.BlockSpec((bm, a.shape[1]), lambda i: (i, 0))
    return pl.pallas_call(
        functools.partial(_out_proj_kernel, alpha, layer),
        grid=(m // bm,),
        in_specs=[rows(oa), rows(ob), rows(oc), pl.BlockSpec(memory_space=pl.ANY), rows(x),
                  _layer_vec(layer, d), _layer_vec(layer, d)],
        out_specs=[rows(x), rows(x)],
        out_shape=[jax.ShapeDtypeStruct((m, d), F32), jax.ShapeDtypeStruct((m, d), BF16)],
        scratch_shapes=_resident_weight_scratch(w_out.shape[1], d),
        compiler_params=_params(("arbitrary",)),
        name="out_proj_ln",
    )(oa, ob, oc, w_out, x, g, b)


def _in_proj_kernel(x_ref, w_ref, o_ref):
    o_ref[...] = jnp.dot(x_ref[...].astype(BF16), w_ref[...].astype(BF16),
                         preferred_element_type=F32).astype(o_ref.dtype)


def _in_proj(x16, w_in, layer, bm, bn):
    m, k = x16.shape
    n = w_in.shape[2]
    return pl.pallas_call(
        _in_proj_kernel,
        grid=(m // bm, n // bn),
        in_specs=[pl.BlockSpec((bm, k), lambda i, j: (i, 0)),
                  pl.BlockSpec((None, k, bn), lambda i, j: (layer, 0, j))],
        out_specs=pl.BlockSpec((bm, bn), lambda i, j: (i, j)),
        out_shape=jax.ShapeDtypeStruct((m, n), F32),
        compiler_params=_params(("parallel", "parallel")),
        name="in_proj",
    )(x16, w_in)


FFN_UP_SUBTILES = 2


def _ffn_up_kernel(tiles_per_seq, x_ref, wg_ref, wu_ref, cw_ref, cb_ref, h_ref, g_ref, w16_ref):
    bm = x_ref.shape[0]
    bn = h_ref.shape[1]
    halo = g_ref.shape[0] - bm

    @pl.when(pl.program_id(1) == 0)
    def _():
        w16_ref[:, 0:bn] = wg_ref[...].astype(BF16)
        w16_ref[:, bn:] = wu_ref[...].astype(BF16)

    @pl.when(pl.program_id(1) % tiles_per_seq == 0)
    def _():
        g_ref[0:halo, :] = jnp.zeros((halo, g_ref.shape[1]), F32)

    sub = bm // FFN_UP_SUBTILES
    for t in range(FFN_UP_SUBTILES):
        first = halo + t * sub
        gu = jnp.dot(x_ref[t * sub:(t + 1) * sub, :], w16_ref[...], preferred_element_type=F32)
        g_ref[first:first + sub, :] = gu[:, 0:bn]
        u = gu[:, bn:]
        gc = cb_ref[...]
        for j in range(CONV_WIDTH):
            lag = CONV_WIDTH - 1 - j
            gc = gc + cw_ref[j:j + 1, :] * g_ref[first - lag:first - lag + sub, :]
        h_ref[t * sub:(t + 1) * sub, :] = (gc * jax.nn.sigmoid(gc) * u).astype(h_ref.dtype)
    g_ref[0:halo, :] = g_ref[bm:bm + halo, :]


def _ffn_up(x16, w_gate, w_up, conv_w, conv_b, layer, seq, bm, bn):
    m, d = x16.shape
    f = w_gate.shape[2]
    halo = SUBLANES
    wspec = pl.BlockSpec((None, d, bn), lambda j, i: (layer, 0, j))
    return pl.pallas_call(
        functools.partial(_ffn_up_kernel, seq // bm),
        grid=(f // bn, m // bm),
        in_specs=[pl.BlockSpec((bm, d), lambda j, i: (i, 0)), wspec, wspec,
                  pl.BlockSpec((None, CONV_WIDTH, bn), lambda j, i: (layer, 0, j)),
                  pl.BlockSpec((None, 1, bn), lambda j, i: (layer, 0, j))],
        out_specs=pl.BlockSpec((bm, bn), lambda j, i: (i, j)),
        out_shape=jax.ShapeDtypeStruct((m, f), BF16),
        scratch_shapes=[pltpu.VMEM((bm + halo, bn), F32), pltpu.VMEM((d, 2 * bn), BF16)],
        compiler_params=_params(("parallel", "arbitrary")),
        name="ffn_up_conv_gate",
    )(x16, w_gate, w_up, conv_w, conv_b)


def _ffn_down_kernel(alpha, layer, h_ref, w_hbm, x_ref, g_ref, b_ref, o32_ref, o16_ref, w16_ref, stage_ref, sem):
    @pl.when(pl.program_id(0) == 0)
    def _():
        _stage_weight_bf16(w_hbm, layer, w16_ref, stage_ref, sem)

    y = jnp.dot(h_ref[...], w16_ref[...], preferred_element_type=F32)
    _layer_norm_store(alpha * x_ref[...] + y, g_ref, b_ref, o32_ref, o16_ref)


def _ffn_down_ln(h, w_down, layer, x, g, b, alpha, bm):
    m, d = x.shape
    f = h.shape[1]
    rows = pl.BlockSpec((bm, d), lambda i: (i, 0))
    return pl.pallas_call(
        functools.partial(_ffn_down_kernel, alpha, layer),
        grid=(m // bm,),
        in_specs=[pl.BlockSpec((bm, f), lambda i: (i, 0)), pl.BlockSpec(memory_space=pl.ANY), rows,
                  _layer_vec(layer, d), _layer_vec(layer, d)],
        out_specs=[rows, rows],
        out_shape=[jax.ShapeDtypeStruct((m, d), F32), jax.ShapeDtypeStruct((m, d), BF16)],
        scratch_shapes=_resident_weight_scratch(f, d),
        compiler_params=_params(("arbitrary",)),
        name="ffn_down_ln",
    )(h, w_down, x, g, b)


def kernel(x, w_in, lb_logits, a_norm_w, c_sinks, w_out, ln1_g, ln1_b, w_gate, w_up, conv_w, conv_b, w_down,
           ln2_g, ln2_b):
    bsz, seq, d = x.shape
    depth = w_in.shape[0]
    alpha = (2 * depth) ** 0.25
    tables = _rope_tables(seq)
    lbs = jnp.cumsum(jax.nn.softmax(lb_logits.astype(F32), axis=0), axis=0)
    lbs = lbs - lbs[0]
    vec3 = lambda a: a.reshape(depth, 1, -1)
    ln1_g, ln1_b, ln2_g, ln2_b, conv_b = vec3(ln1_g), vec3(ln1_b), vec3(ln2_g), vec3(ln2_b), vec3(conv_b)
    x32 = x.reshape(bsz * seq, d)
    x16 = x32
    plan = _tile_plan(bsz * seq, d, w_out.shape[1], w_gate.shape[2])
    for l in range(depth):
        proj3 = _in_proj(x16, w_in, l, *plan.in_proj).reshape(bsz, seq, -1)
        oa = _hgrn2(proj3, lbs[l], a_norm_w[l]).reshape(bsz * seq, -1)
        ob = _dilated(proj3, tables).reshape(bsz * seq, -1)
        oc = _swa(proj3, c_sinks[l], tables).reshape(bsz * seq, -1)
        x32, x16 = _out_proj_ln(oa, ob, oc, w_out, l, x32, ln1_g, ln1_b, alpha, plan.out_proj_rows)
        h = _ffn_up(x16, w_gate, w_up, conv_w, conv_b, l, seq, *plan.ffn_up)
        x32, x16 = _ffn_down_ln(h, w_down, l, x32, ln2_g, ln2_b, alpha, plan.ffn_down_rows)
    return x32.reshape(bsz, seq, d)
```

```python
import functools
from typing import NamedTuple

import numpy as np
import jax
import jax.numpy as jnp
from jax import lax
from jax.experimental import pallas as pl
from jax.experimental.pallas import tpu as pltpu

HEAD_DIM = 128
A_HEADS = 4
B_HEADS = 6
C_HEADS = 6
C_KV_HEADS = 2
C_REP = C_HEADS // C_KV_HEADS
DILATIONS = (1, 4, 16)
ATT_BLOCK = 128
C_MAX_LAG = 127
ROPE_THETA = 500000.0
ROPE_DIM = HEAD_DIM // 4
ROPE_HALF = ROPE_DIM // 2
CONV_WIDTH = 3
LN_EPS = 1e-5

QA_BLK, FA_BLK, IA_BLK, GA_BLK = 0, 4, 8, 12
QB_BLK, KB_BLK, VB_BLK = 16, 22, 28
QC_BLK, KC_BLK, VC_BLK = 34, 40, 42

HGRN_CHUNK = 128
HGRN_LEVELS = (64, 32, 16, 8, 4, 2, 1)

VMEM_LIMIT_BYTES = 60 * 1024 * 1024

F32 = jnp.float32
BF16 = jnp.bfloat16
NEG_BIG = -1e30
LOG2_E = 1.4426950408889634


def _params(semantics):
    return pltpu.CompilerParams(dimension_semantics=semantics, vmem_limit_bytes=VMEM_LIMIT_BYTES)


SUBLANES = 8
WEIGHT_STAGE_ROWS = 512


class _TilePlan(NamedTuple):
    in_proj: tuple
    out_proj_rows: int
    ffn_up: tuple
    ffn_down_rows: int


def _tile_plan(rows, d_model, d_mix, d_ff):
    plan = _TilePlan(in_proj=(2048, 512), out_proj_rows=512, ffn_up=(2048, 512), ffn_down_rows=256)
    f32, bf16, two = 4, 2, 2
    staged = two * WEIGHT_STAGE_ROWS * d_model * f32
    bm, bn = plan.in_proj
    in_proj = (two * bm * d_model * f32 + bm * d_model * bf16
               + two * d_model * bn * f32 + d_model * bn * bf16 + two * bm * bn * f32)
    bm = plan.out_proj_rows
    out_proj = d_mix * d_model * bf16 + staged + two * bm * (d_mix * bf16 + d_model * (f32 + f32 + bf16))
    bm, bn = plan.ffn_up
    ffn_up = (two * bm * d_model * bf16 + two * two * d_model * bn * f32 + d_model * two * bn * bf16
              + (bm + SUBLANES) * bn * f32 + two * bm * bn * bf16 + bm * two * bn * f32)
    bm = plan.ffn_down_rows
    ffn_down = d_ff * d_model * bf16 + staged + two * bm * (d_ff * bf16 + d_model * (f32 + f32 + bf16))
    for name, need in (("in_proj", in_proj), ("out_proj", out_proj), ("ffn_up", ffn_up), ("ffn_down", ffn_down)):
        assert need <= VMEM_LIMIT_BYTES, (name, need)
    assert rows % max(plan.in_proj[0], plan.ffn_up[0]) == 0 and d_ff % plan.ffn_up[1] == 0
    return plan


def _rope_tables(seq):
    inv = ROPE_THETA ** (-jnp.arange(0, ROPE_DIM, 2, dtype=F32) / ROPE_DIM)
    ang = jnp.arange(seq, dtype=F32)[:, None] * inv[None, :]
    cos, sin = jnp.cos(ang), jnp.sin(ang)
    rest = HEAD_DIM - ROPE_DIM
    c = jnp.concatenate([cos, cos, jnp.ones((seq, rest), F32)], -1)
    s = jnp.concatenate([sin, sin, jnp.zeros((seq, rest), F32)], -1)
    rot = np.zeros((HEAD_DIM, HEAD_DIM), np.float32)
    lane = np.arange(ROPE_HALF)
    rot[lane + ROPE_HALF, lane] = -1.0
    rot[lane, lane + ROPE_HALF] = 1.0
    return c, s, jnp.asarray(np.concatenate([rot, rot], 0), BF16)


def _rope(x, c, s, rot2):
    hi = x.astype(BF16)
    lo = (x - hi.astype(F32)).astype(BF16)
    swapped = jnp.dot(jnp.concatenate([hi, lo], axis=1), rot2, preferred_element_type=F32)
    return x * c + swapped * s


def _hgrn_constants():
    c = HGRN_CHUNK
    i = np.arange(c)[:, None]
    j = np.arange(c)[None, :]
    lvl = np.full((c, c), -1, np.int32)
    for n, h in enumerate(HGRN_LEVELS):
        lvl[(i // (2 * h) == j // (2 * h)) & (i // h != j // h) & (i > j)] = n
    lvl[np.arange(c), np.arange(c)] = len(HGRN_LEVELS)
    return jnp.asarray((j <= i).astype(np.float32), BF16), jnp.asarray(lvl)


def _midpoint_log_decay(lf, b, h, row):
    c, dk = b.shape
    odd = ((row // h) % 2) == 1
    if h == 1:
        return jnp.where(odd, lf, 0.0)
    if h == 2:
        below = pltpu.roll(lf, c - 1, 0)
        above = pltpu.roll(lf, 1, 0)
        place = row % 4
        return jnp.where(place == 0, below, jnp.where(place == 1, 0.0, jnp.where(place == 2, lf, lf + above)))
    blocks = b.reshape(c // (2 * h), 2 * h, dk)
    mid = jnp.broadcast_to(blocks[:, h - 1:h, :], blocks.shape).reshape(c, dk)
    return jnp.where(odd, b - mid, mid - b)


def _split3(x):
    p1 = x.astype(BF16)
    r1 = x - p1.astype(F32)
    p2 = r1.astype(BF16)
    p3 = (r1 - p2.astype(F32)).astype(BF16)
    return p1, p2, p3


def _dot_nt(a, b, **kw):
    return lax.dot_general(a, b, (((1,), (1,)), ((), ())), preferred_element_type=F32, **kw)


def _dot_tn(a, b, **kw):
    return lax.dot_general(a, b, (((0,), (0,)), ((), ())), preferred_element_type=F32, **kw)


def _hgrn_chunk_heads(inputs, nw, e_all, lvl, row):
    c = HGRN_CHUNK
    heads = range(len(inputs))
    qs = [q * jax.nn.sigmoid(q) for q, _, _, _, _, _ in inputs]
    fg = [lb + (1.0 - lb) * jax.nn.sigmoid(fl) for _, fl, _, _, _, lb in inputs]
    kk = [1.0 - f for f in fg]
    lf = [jnp.log2(f) for f in fg]
    parts = [_split3(x) for x in lf]
    b12 = [jnp.dot(e_all, jnp.concatenate([p1, p2], axis=1), preferred_element_type=F32) for p1, p2, _ in parts]
    b3 = [jnp.dot(e_all, p3, preferred_element_type=F32) for _, _, p3 in parts]
    b = [x[:, :HEAD_DIM] + x[:, HEAD_DIM:] + y for x, y in zip(b12, b3)]
    att = [jnp.where(lvl == len(HGRN_LEVELS), jnp.sum(qs[i] * kk[i], axis=-1, keepdims=True), 0.0) for i in heads]
    for n, h in enumerate(HGRN_LEVELS):
        odd = ((row // h) % 2) == 1
        xh = [(jnp.where(odd, qs[i], kk[i]) * jnp.exp2(_midpoint_log_decay(lf[i], b[i], h, row))).astype(BF16)
              for i in heads]
        pair = [_dot_nt(x, x) for x in xh]
        att = [jnp.where(lvl == n, pair[i], att[i]) for i in heads]
    b_end = [x[c - 1:c] for x in b]
    q_in = [(qs[i] * jnp.exp2(b[i])).astype(BF16) for i in heads]
    k_end = [(kk[i] * jnp.exp2(b_end[i] - b[i])).astype(BF16) for i in heads]
    v16 = [v.astype(BF16) for _, _, v, _, _, _ in inputs]
    o = [jnp.dot(att[i].astype(BF16), v16[i], preferred_element_type=F32)
         + _dot_nt(q_in[i], inputs[i][4].astype(BF16)) for i in heads]
    st_new = [inputs[i][4] * jnp.exp2(b_end[i]) + _dot_tn(v16[i], k_end[i]) for i in heads]
    o = [x * lax.rsqrt(jnp.mean(x * x, axis=-1, keepdims=True) + LN_EPS) * nw for x in o]
    out = [(o[i] * (inputs[i][3] * jax.nn.sigmoid(inputs[i][3]))).astype(BF16) for i in heads]
    return list(zip(out, st_new))


HGRN_BATCH = 2
HGRN_SEQ_TILE = 1024


def _hgrn_kernel(q_ref, f_ref, i_ref, g_ref, lb_ref, nw_ref, e_ref, lvl_ref, o_ref, st_ref):
    c = HGRN_CHUNK
    n_batch, tile_rows = q_ref.shape[0], q_ref.shape[1]

    @pl.when(pl.program_id(1) == 0)
    def _():
        st_ref[...] = jnp.zeros_like(st_ref)

    nw = nw_ref[...]
    row = lax.broadcasted_iota(jnp.int32, (c, 1), 0)

    def chunk(ci, carry):
        rows = pl.ds(pl.multiple_of(ci * c, c), c)
        e_all, lvl = e_ref[...], lvl_ref[...]
        inputs = []
        for n in range(n_batch):
            for h in range(A_HEADS):
                cols = slice(h * HEAD_DIM, (h + 1) * HEAD_DIM)
                inputs.append((q_ref[n, rows, cols], f_ref[n, rows, cols], i_ref[n, rows, cols], g_ref[n, rows, cols],
                               st_ref[n * A_HEADS + h], lb_ref[:, cols]))
        for k, (o, st_new) in enumerate(_hgrn_chunk_heads(inputs, nw, e_all, lvl, row)):
            n, h = divmod(k, A_HEADS)
            o_ref[n, rows, h * HEAD_DIM:(h + 1) * HEAD_DIM] = o
            st_ref[k] = st_new
        return carry

    lax.fori_loop(0, tile_rows // c, chunk, 0)


def _hgrn2(proj3, lb, norm_w):
    bsz, seq, _ = proj3.shape
    e_all, lvl = _hgrn_constants()
    width = A_HEADS * HEAD_DIM
    col = lambda blk: pl.BlockSpec((HGRN_BATCH, HGRN_SEQ_TILE, width), lambda b, s, blk=blk: (b, s, blk // A_HEADS))
    const = lambda shape: pl.BlockSpec(shape, lambda b, s: (0,) * len(shape))
    return pl.pallas_call(
        _hgrn_kernel,
        grid=(bsz // HGRN_BATCH, seq // HGRN_SEQ_TILE),
        in_specs=[col(QA_BLK), col(FA_BLK), col(IA_BLK), col(GA_BLK),
                  const((1, width)), const((1, HEAD_DIM)), const(e_all.shape), const(lvl.shape)],
        out_specs=pl.BlockSpec((HGRN_BATCH, HGRN_SEQ_TILE, width), lambda b, s: (b, s, 0)),
        out_shape=jax.ShapeDtypeStruct((bsz, seq, width), BF16),
        scratch_shapes=[pltpu.VMEM((HGRN_BATCH * A_HEADS, HEAD_DIM, HEAD_DIM), F32)],
        compiler_params=_params(("parallel", "arbitrary")),
        name="hgrn2",
    )(proj3, proj3, proj3, proj3, lb.reshape(1, width), norm_w.reshape(1, HEAD_DIM), e_all, lvl)


def _band_masks(max_lag):
    r = lax.broadcasted_iota(jnp.int32, (ATT_BLOCK, ATT_BLOCK), 0)
    col = lax.broadcasted_iota(jnp.int32, (ATT_BLOCK, ATT_BLOCK), 1)
    return col <= r, (ATT_BLOCK + r - col) <= max_lag


def _tile_softmax_many(blocks):
    scale = HEAD_DIM ** -0.5 * LOG2_E
    s_all = [[jnp.where(mask, _dot_nt(q16, k.astype(BF16)) * scale, NEG_BIG) for k, mask in zip(ks, masks)]
             for q16, ks, _, masks, _ in blocks]
    m_all = []
    for s, (_, _, _, _, m_floor) in zip(s_all, blocks):
        m = jnp.max(functools.reduce(jnp.maximum, s), axis=-1, keepdims=True)
        m_all.append(m if m_floor is None else jnp.maximum(m, m_floor))
    e_all = [[jnp.exp2(si - m) for si in s] for s, m in zip(s_all, m_all)]
    acc_all = [functools.reduce(jnp.add, [jnp.dot(ei.astype(BF16), v.astype(BF16), preferred_element_type=F32)
                                          for ei, v in zip(e, vs)])
               for e, (_, _, vs, _, _) in zip(e_all, blocks)]
    l_all = [jnp.sum(functools.reduce(jnp.add, e), axis=-1, keepdims=True) for e in e_all]
    return list(zip(m_all, l_all, acc_all))


DIL_UNROLL = 16


def _dilated_kernel(proj_hbm, c_ref, s_ref, rot_ref, o_ref, qkv_ref, sem, qr_ref, kr_ref, m_ref, l_ref, acc_ref):
    seq = o_ref.shape[0]
    blk = ATT_BLOCK
    widest = DILATIONS[-1]
    per_residue = seq // widest
    step = pl.program_id(0) * pl.num_programs(1) + pl.program_id(1)
    n_steps = pl.num_programs(0) * pl.num_programs(1)
    slot = step % 2

    def qkv_copies(of_step, into_slot):
        b, h = of_step // pl.num_programs(1), of_step % pl.num_programs(1)
        copies = []
        for which, first_blk in enumerate((QB_BLK, KB_BLK, VB_BLK)):
            col = pl.multiple_of((first_blk + h) * HEAD_DIM, HEAD_DIM)
            for r in range(widest):
                copies.append(pltpu.make_async_copy(
                    proj_hbm.at[b, :, r, pl.ds(col, HEAD_DIM)],
                    qkv_ref.at[into_slot, which, pl.ds(r * per_residue, per_residue), :],
                    sem.at[into_slot]))
        return copies

    @pl.when(step == 0)
    def _():
        for copy in qkv_copies(step, slot):
            copy.start()

    @pl.when(step + 1 < n_steps)
    def _():
        for copy in qkv_copies(step + 1, 1 - slot):
            copy.start()

    for copy in qkv_copies(step, slot):
        copy.wait()

    c, s, rot = c_ref[...], s_ref[...], rot_ref[...]
    qr_ref[...] = _rope(qkv_ref[slot, 0], c, s, rot)
    kr_ref[...] = _rope(qkv_ref[slot, 1], c, s, rot)
    vr_ref = qkv_ref.at[slot, 2]

    def chunk_rows(dil):
        return blk * dil // widest

    def block_masks(dil):
        def position(i):
            return (widest // dil) * (i % chunk_rows(dil)) + i // chunk_rows(dil)
        row = position(lax.broadcasted_iota(jnp.int32, (blk, blk), 0))
        col = position(lax.broadcasted_iota(jnp.int32, (blk, blk), 1))
        return col <= row, col >= row

    def chunks_of(dil, r, n):
        size = chunk_rows(dil)
        return [pl.ds(pl.multiple_of((r + dil * a) * per_residue + size * n, size), size)
                for a in range(widest // dil)]

    def gather(ref, chunks):
        return jnp.concatenate([ref[rows, :] for rows in chunks], axis=0)

    def scatter(ref, chunks, value):
        size = value.shape[0] // len(chunks)
        for a, rows in enumerate(chunks):
            ref[rows, :] = value[a * size:(a + 1) * size]

    def run_group(specs, masks_of_pattern, first):
        cur_mask, prev_mask = masks_of_pattern
        loaded = []
        for rows, prev, prev_ok in specs:
            ks, vs, masks = [gather(kr_ref, rows)], [gather(vr_ref, rows)], [cur_mask]
            if prev is not None:
                ks.append(gather(kr_ref, prev))
                vs.append(gather(vr_ref, prev))
                masks.append(jnp.logical_and(prev_mask, prev_ok))
            state = None if first else (gather(m_ref, rows), gather(l_ref, rows), gather(acc_ref, rows))
            loaded.append((gather(qr_ref, rows).astype(BF16), ks, vs, masks, state))
        pieces = _tile_softmax_many([(q16, ks, vs, masks, None if state is None else state[0])
                                     for q16, ks, vs, masks, state in loaded])
        results = []
        for (m, l, acc), (_, _, _, _, state) in zip(pieces, loaded):
            if state is None:
                shape = (blk, HEAD_DIM)
                results.append((jnp.broadcast_to(m, shape), jnp.broadcast_to(l, shape), acc))
            else:
                m_old, l_old, acc_old = state
                alpha = jnp.exp2(m_old - m)
                results.append((m, alpha * l_old + l, alpha * acc_old + acc))
        for (rows, _, _), (m, l, acc) in zip(specs, results):
            scatter(m_ref, rows, m)
            scatter(l_ref, rows, l)
            scatter(acc_ref, rows, acc)

    n_query_blocks = seq // blk
    for dil in DILATIONS:
        n_blocks = n_query_blocks // dil
        masks_of_pattern = block_masks(dil)

        def group(g, carry, dil=dil, n_blocks=n_blocks, masks_of_pattern=masks_of_pattern):
            specs = []
            for u in range(DIL_UNROLL):
                e = g * DIL_UNROLL + u
                r, n = e % dil, e // dil
                if n_blocks == 1:
                    specs.append((chunks_of(dil, r, n), None, None))
                else:
                    specs.append((chunks_of(dil, r, n), chunks_of(dil, r, jnp.maximum(n - 1, 0)), n > 0))
            run_group(specs, masks_of_pattern, first=(dil == DILATIONS[0]))
            return carry

        lax.fori_loop(0, n_query_blocks // DIL_UNROLL, group, 0)

    for r in range(widest):
        rows = slice(r * per_residue, (r + 1) * per_residue)
        qr_ref[pl.ds(r, per_residue, stride=widest), :] = acc_ref[rows, :] / l_ref[rows, :]
    o_ref[...] = qr_ref[...].astype(o_ref.dtype)


def _residue_major(table, widest):
    seq, width = table.shape
    return table.reshape(seq // widest, widest, width).transpose(1, 0, 2).reshape(seq, width)


def _dilated(proj3, tables):
    bsz, seq, width = proj3.shape
    widest = DILATIONS[-1]
    tab = pl.BlockSpec((seq, HEAD_DIM), lambda b, h: (0, 0))
    return pl.pallas_call(
        _dilated_kernel,
        grid=(bsz, B_HEADS),
        in_specs=[pl.BlockSpec(memory_space=pl.ANY), tab, tab, pl.BlockSpec(tables[2].shape, lambda b, h: (0, 0))],
        out_specs=pl.BlockSpec((None, seq, HEAD_DIM), lambda b, h: (b, 0, h)),
        out_shape=jax.ShapeDtypeStruct((bsz, seq, B_HEADS * HEAD_DIM), BF16),
        scratch_shapes=[pltpu.VMEM((2, 3, seq, HEAD_DIM), F32), pltpu.SemaphoreType.DMA((2,))]
                       + [pltpu.VMEM((seq, HEAD_DIM), F32)] * 5,
        compiler_params=_params(("arbitrary", "arbitrary")),
        name="dilated_attention",
    )(proj3.reshape(bsz, seq // widest, widest, width), _residue_major(tables[0], widest),
      _residue_major(tables[1], widest), tables[2])


SWA_UNROLL = 16


def _swa_kernel(sink_ref, q0_ref, q1_ref, q2_ref, k_ref, v_ref, c_ref, s_ref, rot_ref, o_ref, qr_ref, kr_ref):
    seq = k_ref.shape[0]
    blk = ATT_BLOCK
    grp = pl.program_id(1)
    c, s, rot = c_ref[...], s_ref[...], rot_ref[...]
    for j, q_ref in enumerate((q0_ref, q1_ref, q2_ref)):
        qr_ref[j] = _rope(q_ref[...], c, s, rot).astype(BF16)
    kr_ref[...] = _rope(k_ref[...], c, s, rot).astype(BF16)
    cur_mask, prev_mask = _band_masks(C_MAX_LAG)
    sinks = [jnp.full((blk, 1), sink_ref[grp * C_REP + j] * LOG2_E, F32) for j in range(C_REP)]

    def group(g, carry):
        blocks, dests = [], []
        for u in range(SWA_UNROLL):
            n = g * SWA_UNROLL + u
            rows = pl.ds(pl.multiple_of(n * blk, blk), blk)
            prev = pl.ds(pl.multiple_of(jnp.maximum(n - 1, 0) * blk, blk), blk)
            ks, vs = [kr_ref[rows, :], kr_ref[prev, :]], [v_ref[rows, :], v_ref[prev, :]]
            masks = [cur_mask, jnp.logical_and(prev_mask, n > 0)]
            for j in range(C_REP):
                blocks.append((qr_ref[j, rows, :], ks, vs, masks, sinks[j]))
                dests.append((rows, j))
        for (rows, j), (m, l, acc) in zip(dests, _tile_softmax_many(blocks)):
            out = acc / (l + jnp.exp2(sinks[j] - m))
            o_ref[rows, j * HEAD_DIM:(j + 1) * HEAD_DIM] = out.astype(o_ref.dtype)
        return carry

    lax.fori_loop(0, seq // (blk * SWA_UNROLL), group, 0)


def _swa(proj3, sinks, tables):
    bsz, seq, _ = proj3.shape
    qcol = lambda j: pl.BlockSpec((None, seq, HEAD_DIM), lambda b, g, j=j: (b, 0, QC_BLK + g * C_REP + j))
    col = lambda blk: pl.BlockSpec((None, seq, HEAD_DIM), lambda b, g, blk=blk: (b, 0, blk + g))
    tab = pl.BlockSpec((seq, HEAD_DIM), lambda b, g: (0, 0))
    return pl.pallas_call(
        _swa_kernel,
        grid=(bsz, C_KV_HEADS),
        in_specs=[pl.BlockSpec(memory_space=pltpu.SMEM), qcol(0), qcol(1), qcol(2), col(KC_BLK), col(VC_BLK),
                  tab, tab, pl.BlockSpec(tables[2].shape, lambda b, g: (0, 0))],
        out_specs=pl.BlockSpec((None, seq, C_REP * HEAD_DIM), lambda b, g: (b, 0, g)),
        out_shape=jax.ShapeDtypeStruct((bsz, seq, C_HEADS * HEAD_DIM), BF16),
        scratch_shapes=[pltpu.VMEM((C_REP, seq, HEAD_DIM), BF16), pltpu.VMEM((seq, HEAD_DIM), BF16)],
        compiler_params=_params(("parallel", "parallel")),
        name="swa_sink_attention",
    )(sinks, proj3, proj3, proj3, proj3, proj3, *tables)


def _layer_norm_store(z, g_ref, b_ref, o32_ref, o16_ref):
    mu = jnp.mean(z, axis=-1, keepdims=True)
    zc = z - mu
    var = jnp.mean(zc * zc, axis=-1, keepdims=True)
    y = zc * lax.rsqrt(var + LN_EPS) * g_ref[...] + b_ref[...]
    o32_ref[...] = y
    o16_ref[...] = y.astype(BF16)


def _stage_weight_bf16(w_hbm, layer, w16_ref, stage_ref, sem):
    chunk = stage_ref.shape[1]
    n_chunks = w16_ref.shape[0] // chunk

    def copy(c):
        slot = c % 2
        return pltpu.make_async_copy(w_hbm.at[layer, pl.ds(c * chunk, chunk), :], stage_ref.at[slot], sem.at[slot])

    copy(0).start()
    for c in range(n_chunks):
        if c + 1 < n_chunks:
            copy(c + 1).start()
        copy(c).wait()
        w16_ref[c * chunk:(c + 1) * chunk, :] = stage_ref[c % 2].astype(BF16)


def _resident_weight_scratch(k, n):
    return [pltpu.VMEM((k, n), BF16), pltpu.VMEM((2, WEIGHT_STAGE_ROWS, n), F32), pltpu.SemaphoreType.DMA((2,))]


def _out_proj_kernel(alpha, layer, oa_ref, ob_ref, oc_ref, w_hbm, x_ref, g_ref, b_ref, o32_ref, o16_ref,
                     w16_ref, stage_ref, sem):
    @pl.when(pl.program_id(0) == 0)
    def _():
        _stage_weight_bf16(w_hbm, layer, w16_ref, stage_ref, sem)

    ka, kb = oa_ref.shape[1], ob_ref.shape[1]
    y = (jnp.dot(oa_ref[...], w16_ref[0:ka, :], preferred_element_type=F32)
         + jnp.dot(ob_ref[...], w16_ref[ka:ka + kb, :], preferred_element_type=F32)
         + jnp.dot(oc_ref[...], w16_ref[ka + kb:, :], preferred_element_type=F32))
    _layer_norm_store(alpha * x_ref[...] + y, g_ref, b_ref, o32_ref, o16_ref)


def _layer_vec(layer, d):
    return pl.BlockSpec((None, 1, d), lambda i, layer=layer: (layer, 0, 0))


def _out_proj_ln(oa, ob, oc, w_out, layer, x, g, b, alpha, bm):
    m, d = x.shape
    rows = lambda a: pl.BlockSpec((bm, a.shape[1]), lambda i: (i, 0))
    return pl.pallas_call(
        functools.partial(_out_proj_kernel, alpha, layer),
        grid=(m // bm,),
        in_specs=[rows(oa), rows(ob), rows(oc), pl.BlockSpec(memory_space=pl.ANY), rows(x),
                  _layer_vec(layer, d), _layer_vec(layer, d)],
        out_specs=[rows(x), rows(x)],
        out_shape=[jax.ShapeDtypeStruct((m, d), F32), jax.ShapeDtypeStruct((m, d), BF16)],
        scratch_shapes=_resident_weight_scratch(w_out.shape[1], d),
        compiler_params=_params(("arbitrary",)),
        name="out_proj_ln",
    )(oa, ob, oc, w_out, x, g, b)


def _in_proj_kernel(x_ref, w_ref, o_ref):
    o_ref[...] = jnp.dot(x_ref[...], w_ref[...].astype(BF16), preferred_element_type=F32).astype(o_ref.dtype)


def _in_proj_f32_kernel(x_ref, w_ref, o_ref, x16_ref):
    @pl.when(pl.program_id(1) == 0)
    def _():
        x16_ref[...] = x_ref[...].astype(BF16)

    o_ref[...] = jnp.dot(x16_ref[...], w_ref[...].astype(BF16), preferred_element_type=F32).astype(o_ref.dtype)


def _in_proj(x, w_in, layer, bm, bn):
    m, k = x.shape
    n = w_in.shape[2]
    from_f32 = x.dtype == F32
    return pl.pallas_call(
        _in_proj_f32_kernel if from_f32 else _in_proj_kernel,
        grid=(m // bm, n // bn),
        in_specs=[pl.BlockSpec((bm, k), lambda i, j: (i, 0)),
                  pl.BlockSpec((None, k, bn), lambda i, j: (layer, 0, j))],
        out_specs=pl.BlockSpec((bm, bn), lambda i, j: (i, j)),
        out_shape=jax.ShapeDtypeStruct((m, n), F32),
        scratch_shapes=[pltpu.VMEM((bm, k), BF16)] if from_f32 else [],
        compiler_params=_params(("parallel", "arbitrary" if from_f32 else "parallel")),
        name="in_proj",
    )(x, w_in)


FFN_UP_SUBTILES = 2


def _ffn_up_kernel(tiles_per_seq, x_ref, wg_ref, wu_ref, cw_ref, cb_ref, h_ref, g_ref, w16_ref):
    bm = x_ref.shape[0]
    bn = h_ref.shape[1]
    halo = g_ref.shape[0] - bm

    @pl.when(pl.program_id(1) == 0)
    def _():
        w16_ref[:, 0:bn] = wg_ref[...].astype(BF16)
        w16_ref[:, bn:] = wu_ref[...].astype(BF16)

    @pl.when(pl.program_id(1) % tiles_per_seq == 0)
    def _():
        g_ref[0:halo, :] = jnp.zeros((halo, g_ref.shape[1]), F32)

    sub = bm // FFN_UP_SUBTILES
    for t in range(FFN_UP_SUBTILES):
        first = halo + t * sub
        gu = jnp.dot(x_ref[t * sub:(t + 1) * sub, :], w16_ref[...], preferred_element_type=F32)
        g_ref[first:first + sub, :] = gu[:, 0:bn]
        u = gu[:, bn:]
        gc = cb_ref[...]
        for j in range(CONV_WIDTH):
            lag = CONV_WIDTH - 1 - j
            gc = gc + cw_ref[j:j + 1, :] * g_ref[first - lag:first - lag + sub, :]
        h_ref[t * sub:(t + 1) * sub, :] = (gc * jax.nn.sigmoid(gc) * u).astype(h_ref.dtype)
    g_ref[0:halo, :] = g_ref[bm:bm + halo, :]


def _ffn_up(x16, w_gate, w_up, conv_w, conv_b, layer, seq, bm, bn):
    m, d = x16.shape
    f = w_gate.shape[2]
    halo = SUBLANES
    wspec = pl.BlockSpec((None, d, bn), lambda j, i: (layer, 0, j))
    return pl.pallas_call(
        functools.partial(_ffn_up_kernel, seq // bm),
        grid=(f // bn, m // bm),
        in_specs=[pl.BlockSpec((bm, d), lambda j, i: (i, 0)), wspec, wspec,
                  pl.BlockSpec((None, CONV_WIDTH, bn), lambda j, i: (layer, 0, j)),
                  pl.BlockSpec((None, 1, bn), lambda j, i: (layer, 0, j))],
        out_specs=pl.BlockSpec((bm, bn), lambda j, i: (i, j)),
        out_shape=jax.ShapeDtypeStruct((m, f), BF16),
        scratch_shapes=[pltpu.VMEM((bm + halo, bn), F32), pltpu.VMEM((d, 2 * bn), BF16)],
        compiler_params=_params(("parallel", "arbitrary")),
        name="ffn_up_conv_gate",
    )(x16, w_gate, w_up, conv_w, conv_b)


def _ffn_down_kernel(alpha, layer, h_ref, w_hbm, x_ref, g_ref, b_ref, o32_ref, o16_ref, w16_ref, stage_ref, sem):
    @pl.when(pl.program_id(0) == 0)
    def _():
        _stage_weight_bf16(w_hbm, layer, w16_ref, stage_ref, sem)

    y = jnp.dot(h_ref[...], w16_ref[...], preferred_element_type=F32)
    _layer_norm_store(alpha * x_ref[...] + y, g_ref, b_ref, o32_ref, o16_ref)


def _ffn_down_ln(h, w_down, layer, x, g, b, alpha, bm):
    m, d = x.shape
    f = h.shape[1]
    rows = pl.BlockSpec((bm, d), lambda i: (i, 0))
    return pl.pallas_call(
        functools.partial(_ffn_down_kernel, alpha, layer),
        grid=(m // bm,),
        in_specs=[pl.BlockSpec((bm, f), lambda i: (i, 0)), pl.BlockSpec(memory_space=pl.ANY), rows,
                  _layer_vec(layer, d), _layer_vec(layer, d)],
        out_specs=[rows, rows],
        out_shape=[jax.ShapeDtypeStruct((m, d), F32), jax.ShapeDtypeStruct((m, d), BF16)],
        scratch_shapes=_resident_weight_scratch(f, d),
        compiler_params=_params(("arbitrary",)),
        name="ffn_down_ln",
    )(h, w_down, x, g, b)


def kernel(x, w_in, lb_logits, a_norm_w, c_sinks, w_out, ln1_g, ln1_b, w_gate, w_up, conv_w, conv_b, w_down,
           ln2_g, ln2_b):
    bsz, seq, d = x.shape
    depth = w_in.shape[0]
    alpha = (2 * depth) ** 0.25
    tables = _rope_tables(seq)
    lbs = jnp.cumsum(jax.nn.softmax(lb_logits.astype(F32), axis=0), axis=0)
    lbs = lbs - lbs[0]
    vec3 = lambda a: a.reshape(depth, 1, -1)
    ln1_g, ln1_b, ln2_g, ln2_b, conv_b = vec3(ln1_g), vec3(ln1_b), vec3(ln2_g), vec3(ln2_b), vec3(conv_b)
    x32 = x.reshape(bsz * seq, d)
    x16 = x32
    plan = _tile_plan(bsz * seq, d, w_out.shape[1], w_gate.shape[2])
    for l in range(depth):
        proj3 = _in_proj(x16, w_in, l, *plan.in_proj).reshape(bsz, seq, -1)
        oa = _hgrn2(proj3, lbs[l], a_norm_w[l]).reshape(bsz * seq, -1)
        ob = _dilated(proj3, tables).reshape(bsz * seq, -1)
        oc = _swa(proj3, c_sinks[l], tables).reshape(bsz * seq, -1)
        x32, x16 = _out_proj_ln(oa, ob, oc, w_out, l, x32, ln1_g, ln1_b, alpha, plan.out_proj_rows)
        h = _ffn_up(x16, w_gate, w_up, conv_w, conv_b, l, seq, *plan.ffn_up)
        x32, x16 = _ffn_down_ln(h, w_down, l, x32, ln2_g, ln2_b, alpha, plan.ffn_down_rows)
    return x32.reshape(bsz, seq, d)
```

```python
import functools
from typing import NamedTuple

import numpy as np
import jax
import jax.numpy as jnp
from jax import lax
from jax.experimental import pallas as pl
from jax.experimental.pallas import tpu as pltpu

HEAD_DIM = 128
A_HEADS = 4
B_HEADS = 6
C_HEADS = 6
C_KV_HEADS = 2
C_REP = C_HEADS // C_KV_HEADS
DILATIONS = (1, 4, 16)
ATT_BLOCK = 128
C_MAX_LAG = 127
ROPE_THETA = 500000.0
ROPE_DIM = HEAD_DIM // 4
ROPE_HALF = ROPE_DIM // 2
CONV_WIDTH = 3
LN_EPS = 1e-5

QA_BLK, FA_BLK, IA_BLK, GA_BLK = 0, 4, 8, 12
QB_BLK, KB_BLK, VB_BLK = 16, 22, 28
QC_BLK, KC_BLK, VC_BLK = 34, 40, 42

HGRN_CHUNK = 128
HGRN_LEVELS = (64, 32, 16, 8, 4, 2, 1)

VMEM_LIMIT_BYTES = 60 * 1024 * 1024

F32 = jnp.float32
BF16 = jnp.bfloat16
NEG_BIG = -1e30
LOG2_E = 1.4426950408889634


def _params(semantics):
    return pltpu.CompilerParams(dimension_semantics=semantics, vmem_limit_bytes=VMEM_LIMIT_BYTES)


SUBLANES = 8
WEIGHT_STAGE_ROWS = 512


class _TilePlan(NamedTuple):
    in_proj: tuple
    out_proj_rows: int
    ffn_up: tuple
    ffn_down_rows: int


def _tile_plan(rows, d_model, d_mix, d_ff):
    plan = _TilePlan(in_proj=(2048, 512), out_proj_rows=512, ffn_up=(2048, 512), ffn_down_rows=256)
    f32, bf16, two = 4, 2, 2
    staged = two * WEIGHT_STAGE_ROWS * d_model * f32
    bm, bn = plan.in_proj
    in_proj = (two * bm * d_model * f32 + bm * d_model * bf16
               + two * d_model * bn * f32 + d_model * bn * bf16 + two * bm * bn * f32)
    bm = plan.out_proj_rows
    out_proj = d_mix * d_model * bf16 + staged + two * bm * (d_mix * bf16 + d_model * (f32 + f32 + bf16))
    bm, bn = plan.ffn_up
    ffn_up = (two * bm * d_model * bf16 + two * two * d_model * bn * f32 + d_model * two * bn * bf16
              + (bm + SUBLANES) * bn * f32 + two * bm * bn * bf16 + bm * two * bn * f32)
    bm = plan.ffn_down_rows
    ffn_down = d_ff * d_model * bf16 + staged + two * bm * (d_ff * bf16 + d_model * (f32 + f32 + bf16))
    for name, need in (("in_proj", in_proj), ("out_proj", out_proj), ("ffn_up", ffn_up), ("ffn_down", ffn_down)):
        assert need <= VMEM_LIMIT_BYTES, (name, need)
    assert rows % max(plan.in_proj[0], plan.ffn_up[0]) == 0 and d_ff % plan.ffn_up[1] == 0
    return plan


def _rope_tables(seq):
    inv = ROPE_THETA ** (-np.arange(0, ROPE_DIM, 2, dtype=np.float64) / ROPE_DIM)
    ang = np.arange(seq, dtype=np.float64)[:, None] * inv[None, :]
    cos, sin = np.cos(ang), np.sin(ang)
    rest = HEAD_DIM - ROPE_DIM
    c = np.concatenate([cos, cos, np.ones((seq, rest))], -1).astype(np.float32)
    s = np.concatenate([sin, sin, np.zeros((seq, rest))], -1).astype(np.float32)
    rot = np.zeros((HEAD_DIM, HEAD_DIM), np.float32)
    lane = np.arange(ROPE_HALF)
    rot[lane + ROPE_HALF, lane] = -1.0
    rot[lane, lane + ROPE_HALF] = 1.0
    return c, s, jnp.asarray(np.concatenate([rot, rot], 0), BF16)


def _rope(x, c, s, rot2):
    hi = x.astype(BF16)
    lo = (x - hi.astype(F32)).astype(BF16)
    swapped = jnp.dot(jnp.concatenate([hi, lo], axis=1), rot2, preferred_element_type=F32)
    return x * c + swapped * s


def _hgrn_constants():
    c = HGRN_CHUNK
    i = np.arange(c)[:, None]
    j = np.arange(c)[None, :]
    lvl = np.full((c, c), -1, np.int32)
    for n, h in enumerate(HGRN_LEVELS):
        lvl[(i // (2 * h) == j // (2 * h)) & (i // h != j // h) & (i > j)] = n
    lvl[np.arange(c), np.arange(c)] = len(HGRN_LEVELS)
    return jnp.asarray((j <= i).astype(np.float32), BF16), jnp.asarray(lvl)


def _midpoint_log_decay(lf, b, h, row):
    c, dk = b.shape
    odd = ((row // h) % 2) == 1
    if h == 1:
        return jnp.where(odd, lf, 0.0)
    if h == 2:
        below = pltpu.roll(lf, c - 1, 0)
        above = pltpu.roll(lf, 1, 0)
        place = row % 4
        return jnp.where(place == 0, below, jnp.where(place == 1, 0.0, jnp.where(place == 2, lf, lf + above)))
    blocks = b.reshape(c // (2 * h), 2 * h, dk)
    mid = jnp.broadcast_to(blocks[:, h - 1:h, :], blocks.shape).reshape(c, dk)
    return jnp.where(odd, b - mid, mid - b)


def _split3(x):
    p1 = x.astype(BF16)
    r1 = x - p1.astype(F32)
    p2 = r1.astype(BF16)
    p3 = (r1 - p2.astype(F32)).astype(BF16)
    return p1, p2, p3


def _dot_nt(a, b, **kw):
    return lax.dot_general(a, b, (((1,), (1,)), ((), ())), preferred_element_type=F32, **kw)


def _dot_tn(a, b, **kw):
    return lax.dot_general(a, b, (((0,), (0,)), ((), ())), preferred_element_type=F32, **kw)


def _hgrn_chunk_heads(inputs, nw, e_all, lvl, row):
    c = HGRN_CHUNK
    heads = range(len(inputs))
    qs = [q * jax.nn.sigmoid(q) for q, _, _, _, _, _ in inputs]
    fg = [lb + (1.0 - lb) * jax.nn.sigmoid(fl) for _, fl, _, _, _, lb in inputs]
    kk = [1.0 - f for f in fg]
    lf = [jnp.log2(f) for f in fg]
    parts = [_split3(x) for x in lf]
    b12 = [jnp.dot(e_all, jnp.concatenate([p1, p2], axis=1), preferred_element_type=F32) for p1, p2, _ in parts]
    b3 = [jnp.dot(e_all, p3, preferred_element_type=F32) for _, _, p3 in parts]
    b = [x[:, :HEAD_DIM] + x[:, HEAD_DIM:] + y for x, y in zip(b12, b3)]
    att = [jnp.where(lvl == len(HGRN_LEVELS), jnp.sum(qs[i] * kk[i], axis=-1, keepdims=True), 0.0) for i in heads]
    for n, h in enumerate(HGRN_LEVELS):
        odd = ((row // h) % 2) == 1
        xh = [(jnp.where(odd, qs[i], kk[i]) * jnp.exp2(_midpoint_log_decay(lf[i], b[i], h, row))).astype(BF16)
              for i in heads]
        pair = [_dot_nt(x, x) for x in xh]
        att = [jnp.where(lvl == n, pair[i], att[i]) for i in heads]
    b_end = [x[c - 1:c] for x in b]
    q_in = [(qs[i] * jnp.exp2(b[i])).astype(BF16) for i in heads]
    k_end = [(kk[i] * jnp.exp2(b_end[i] - b[i])).astype(BF16) for i in heads]
    v16 = [v.astype(BF16) for _, _, v, _, _, _ in inputs]
    o = [jnp.dot(att[i].astype(BF16), v16[i], preferred_element_type=F32)
         + _dot_nt(q_in[i], inputs[i][4].astype(BF16)) for i in heads]
    st_new = [inputs[i][4] * jnp.exp2(b_end[i]) + _dot_tn(v16[i], k_end[i]) for i in heads]
    o = [x * lax.rsqrt(jnp.mean(x * x, axis=-1, keepdims=True) + LN_EPS) * nw for x in o]
    out = [(o[i] * (inputs[i][3] * jax.nn.sigmoid(inputs[i][3]))).astype(BF16) for i in heads]
    return list(zip(out, st_new))


HGRN_BATCH = 2
HGRN_SEQ_TILE = 1024


def _hgrn_kernel(q_ref, f_ref, i_ref, g_ref, lb_ref, nw_ref, e_ref, lvl_ref, o_ref, st_ref):
    c = HGRN_CHUNK
    n_batch, tile_rows = q_ref.shape[0], q_ref.shape[1]

    @pl.when(pl.program_id(1) == 0)
    def _():
        st_ref[...] = jnp.zeros_like(st_ref)

    nw = nw_ref[...]
    row = lax.broadcasted_iota(jnp.int32, (c, 1), 0)

    def chunk(ci, carry):
        rows = pl.ds(pl.multiple_of(ci * c, c), c)
        e_all, lvl = e_ref[...], lvl_ref[...]
        inputs = []
        for n in range(n_batch):
            for h in range(A_HEADS):
                cols = slice(h * HEAD_DIM, (h + 1) * HEAD_DIM)
                inputs.append((q_ref[n, rows, cols], f_ref[n, rows, cols], i_ref[n, rows, cols], g_ref[n, rows, cols],
                               st_ref[n * A_HEADS + h], lb_ref[:, cols]))
        for k, (o, st_new) in enumerate(_hgrn_chunk_heads(inputs, nw, e_all, lvl, row)):
            n, h = divmod(k, A_HEADS)
            o_ref[n, rows, h * HEAD_DIM:(h + 1) * HEAD_DIM] = o
            st_ref[k] = st_new
        return carry

    lax.fori_loop(0, tile_rows // c, chunk, 0)


def _hgrn2(proj3, lb, norm_w):
    bsz, seq, _ = proj3.shape
    e_all, lvl = _hgrn_constants()
    width = A_HEADS * HEAD_DIM
    col = lambda blk: pl.BlockSpec((HGRN_BATCH, HGRN_SEQ_TILE, width), lambda b, s, blk=blk: (b, s, blk // A_HEADS))
    const = lambda shape: pl.BlockSpec(shape, lambda b, s: (0,) * len(shape))
    return pl.pallas_call(
        _hgrn_kernel,
        grid=(bsz // HGRN_BATCH, seq // HGRN_SEQ_TILE),
        in_specs=[col(QA_BLK), col(FA_BLK), col(IA_BLK), col(GA_BLK),
                  const((1, width)), const((1, HEAD_DIM)), const(e_all.shape), const(lvl.shape)],
        out_specs=pl.BlockSpec((HGRN_BATCH, HGRN_SEQ_TILE, width), lambda b, s: (b, s, 0)),
        out_shape=jax.ShapeDtypeStruct((bsz, seq, width), BF16),
        scratch_shapes=[pltpu.VMEM((HGRN_BATCH * A_HEADS, HEAD_DIM, HEAD_DIM), F32)],
        compiler_params=_params(("parallel", "arbitrary")),
        name="hgrn2",
    )(proj3, proj3, proj3, proj3, lb.reshape(1, width), norm_w.reshape(1, HEAD_DIM), e_all, lvl)


def _band_masks(max_lag):
    r = lax.broadcasted_iota(jnp.int32, (ATT_BLOCK, ATT_BLOCK), 0)
    col = lax.broadcasted_iota(jnp.int32, (ATT_BLOCK, ATT_BLOCK), 1)
    return col <= r, (ATT_BLOCK + r - col) <= max_lag


def _tile_softmax_many(blocks):
    scale = HEAD_DIM ** -0.5 * LOG2_E
    s_all = [[jnp.where(mask, _dot_nt(q16, k.astype(BF16)) * scale, NEG_BIG) for k, mask in zip(ks, masks)]
             for q16, ks, _, masks, _ in blocks]
    m_all = []
    for s, (_, _, _, _, m_floor) in zip(s_all, blocks):
        m = jnp.max(functools.reduce(jnp.maximum, s), axis=-1, keepdims=True)
        m_all.append(m if m_floor is None else jnp.maximum(m, m_floor))
    e_all = [[jnp.exp2(si - m) for si in s] for s, m in zip(s_all, m_all)]
    acc_all = [functools.reduce(jnp.add, [jnp.dot(ei.astype(BF16), v.astype(BF16), preferred_element_type=F32)
                                          for ei, v in zip(e, vs)])
               for e, (_, _, vs, _, _) in zip(e_all, blocks)]
    l_all = [jnp.sum(functools.reduce(jnp.add, e), axis=-1, keepdims=True) for e in e_all]
    return list(zip(m_all, l_all, acc_all))


DIL_UNROLL = 16


def _dilated_kernel(proj_hbm, c_ref, s_ref, rot_ref, o_ref, qkv_ref, sem, qr_ref, kr_ref, m_ref, l_ref, acc_ref):
    seq = o_ref.shape[0]
    blk = ATT_BLOCK
    widest = DILATIONS[-1]
    per_residue = seq // widest
    step = pl.program_id(0) * pl.num_programs(1) + pl.program_id(1)
    n_steps = pl.num_programs(0) * pl.num_programs(1)
    slot = step % 2

    def qkv_copies(of_step, into_slot):
        b, h = of_step // pl.num_programs(1), of_step % pl.num_programs(1)
        copies = []
        for which, first_blk in enumerate((QB_BLK, KB_BLK, VB_BLK)):
            col = pl.multiple_of((first_blk + h) * HEAD_DIM, HEAD_DIM)
            for r in range(widest):
                copies.append(pltpu.make_async_copy(
                    proj_hbm.at[b, :, r, pl.ds(col, HEAD_DIM)],
                    qkv_ref.at[into_slot, which, pl.ds(r * per_residue, per_residue), :],
                    sem.at[into_slot]))
        return copies

    @pl.when(step == 0)
    def _():
        for copy in qkv_copies(step, slot):
            copy.start()

    @pl.when(step + 1 < n_steps)
    def _():
        for copy in qkv_copies(step + 1, 1 - slot):
            copy.start()

    for copy in qkv_copies(step, slot):
        copy.wait()

    c, s, rot = c_ref[...], s_ref[...], rot_ref[...]
    qr_ref[...] = _rope(qkv_ref[slot, 0], c, s, rot)
    kr_ref[...] = _rope(qkv_ref[slot, 1], c, s, rot)
    vr_ref = qkv_ref.at[slot, 2]

    def chunk_rows(dil):
        return blk * dil // widest

    def block_masks(dil):
        def position(i):
            return (widest // dil) * (i % chunk_rows(dil)) + i // chunk_rows(dil)
        row = position(lax.broadcasted_iota(jnp.int32, (blk, blk), 0))
        col = position(lax.broadcasted_iota(jnp.int32, (blk, blk), 1))
        return col <= row, col >= row

    def chunks_of(dil, r, n):
        size = chunk_rows(dil)
        return [pl.ds(pl.multiple_of((r + dil * a) * per_residue + size * n, size), size)
                for a in range(widest // dil)]

    def gather(ref, chunks):
        return jnp.concatenate([ref[rows, :] for rows in chunks], axis=0)

    def scatter(ref, chunks, value):
        size = value.shape[0] // len(chunks)
        for a, rows in enumerate(chunks):
            ref[rows, :] = value[a * size:(a + 1) * size]

    def run_group(specs, masks_of_pattern, first):
        cur_mask, prev_mask = masks_of_pattern
        loaded = []
        for rows, prev, prev_ok in specs:
            ks, vs, masks = [gather(kr_ref, rows)], [gather(vr_ref, rows)], [cur_mask]
            if prev is not None:
                ks.append(gather(kr_ref, prev))
                vs.append(gather(vr_ref, prev))
                masks.append(jnp.logical_and(prev_mask, prev_ok))
            state = None if first else (gather(m_ref, rows), gather(l_ref, rows), gather(acc_ref, rows))
            loaded.append((gather(qr_ref, rows).astype(BF16), ks, vs, masks, state))
        pieces = _tile_softmax_many([(q16, ks, vs, masks, None if state is None else state[0])
                                     for q16, ks, vs, masks, state in loaded])
        results = []
        for (m, l, acc), (_, _, _, _, state) in zip(pieces, loaded):
            if state is None:
                shape = (blk, HEAD_DIM)
                results.append((jnp.broadcast_to(m, shape), jnp.broadcast_to(l, shape), acc))
            else:
                m_old, l_old, acc_old = state
                alpha = jnp.exp2(m_old - m)
                results.append((m, alpha * l_old + l, alpha * acc_old + acc))
        for (rows, _, _), (m, l, acc) in zip(specs, results):
            scatter(m_ref, rows, m)
            scatter(l_ref, rows, l)
            scatter(acc_ref, rows, acc)

    n_query_blocks = seq // blk
    for dil in DILATIONS:
        n_blocks = n_query_blocks // dil
        masks_of_pattern = block_masks(dil)

        def group(g, carry, dil=dil, n_blocks=n_blocks, masks_of_pattern=masks_of_pattern):
            specs = []
            for u in range(DIL_UNROLL):
                e = g * DIL_UNROLL + u
                r, n = e % dil, e // dil
                if n_blocks == 1:
                    specs.append((chunks_of(dil, r, n), None, None))
                else:
                    specs.append((chunks_of(dil, r, n), chunks_of(dil, r, jnp.maximum(n - 1, 0)), n > 0))
            run_group(specs, masks_of_pattern, first=(dil == DILATIONS[0]))
            return carry

        lax.fori_loop(0, n_query_blocks // DIL_UNROLL, group, 0)

    for r in range(widest):
        rows = slice(r * per_residue, (r + 1) * per_residue)
        qr_ref[pl.ds(r, per_residue, stride=widest), :] = acc_ref[rows, :] / l_ref[rows, :]
    o_ref[...] = qr_ref[...].astype(o_ref.dtype)


def _residue_major(table, widest):
    seq, width = table.shape
    return table.reshape(seq // widest, widest, width).transpose(1, 0, 2).reshape(seq, width)


def _dilated(proj3, tables):
    bsz, seq, width = proj3.shape
    widest = DILATIONS[-1]
    tab = pl.BlockSpec((seq, HEAD_DIM), lambda b, h: (0, 0))
    return pl.pallas_call(
        _dilated_kernel,
        grid=(bsz, B_HEADS),
        in_specs=[pl.BlockSpec(memory_space=pl.ANY), tab, tab, pl.BlockSpec(tables[2].shape, lambda b, h: (0, 0))],
        out_specs=pl.BlockSpec((None, seq, HEAD_DIM), lambda b, h: (b, 0, h)),
        out_shape=jax.ShapeDtypeStruct((bsz, seq, B_HEADS * HEAD_DIM), BF16),
        scratch_shapes=[pltpu.VMEM((2, 3, seq, HEAD_DIM), F32), pltpu.SemaphoreType.DMA((2,))]
                       + [pltpu.VMEM((seq, HEAD_DIM), F32)] * 5,
        compiler_params=_params(("arbitrary", "arbitrary")),
        name="dilated_attention",
    )(proj3.reshape(bsz, seq // widest, widest, width), _residue_major(tables[0], widest),
      _residue_major(tables[1], widest), tables[2])


SWA_UNROLL = 16


def _swa_kernel(sink_ref, q0_ref, q1_ref, q2_ref, k_ref, v_ref, c_ref, s_ref, rot_ref, o_ref, qr_ref, kr_ref):
    seq = k_ref.shape[0]
    blk = ATT_BLOCK
    grp = pl.program_id(1)
    c, s, rot = c_ref[...], s_ref[...], rot_ref[...]
    for j, q_ref in enumerate((q0_ref, q1_ref, q2_ref)):
        qr_ref[j] = _rope(q_ref[...], c, s, rot).astype(BF16)
    kr_ref[...] = _rope(k_ref[...], c, s, rot).astype(BF16)
    cur_mask, prev_mask = _band_masks(C_MAX_LAG)
    sinks = [jnp.full((blk, 1), sink_ref[grp * C_REP + j] * LOG2_E, F32) for j in range(C_REP)]

    def group(g, carry):
        blocks, dests = [], []
        for u in range(SWA_UNROLL):
            n = g * SWA_UNROLL + u
            rows = pl.ds(pl.multiple_of(n * blk, blk), blk)
            prev = pl.ds(pl.multiple_of(jnp.maximum(n - 1, 0) * blk, blk), blk)
            ks, vs = [kr_ref[rows, :], kr_ref[prev, :]], [v_ref[rows, :], v_ref[prev, :]]
            masks = [cur_mask, jnp.logical_and(prev_mask, n > 0)]
            for j in range(C_REP):
                blocks.append((qr_ref[j, rows, :], ks, vs, masks, sinks[j]))
                dests.append((rows, j))
        for (rows, j), (m, l, acc) in zip(dests, _tile_softmax_many(blocks)):
            out = acc / (l + jnp.exp2(sinks[j] - m))
            o_ref[rows, j * HEAD_DIM:(j + 1) * HEAD_DIM] = out.astype(o_ref.dtype)
        return carry

    lax.fori_loop(0, seq // (blk * SWA_UNROLL), group, 0)


def _swa(proj3, sinks, tables):
    bsz, seq, _ = proj3.shape
    qcol = lambda j: pl.BlockSpec((None, seq, HEAD_DIM), lambda b, g, j=j: (b, 0, QC_BLK + g * C_REP + j))
    col = lambda blk: pl.BlockSpec((None, seq, HEAD_DIM), lambda b, g, blk=blk: (b, 0, blk + g))
    tab = pl.BlockSpec((seq, HEAD_DIM), lambda b, g: (0, 0))
    return pl.pallas_call(
        _swa_kernel,
        grid=(bsz, C_KV_HEADS),
        in_specs=[pl.BlockSpec(memory_space=pltpu.SMEM), qcol(0), qcol(1), qcol(2), col(KC_BLK), col(VC_BLK),
                  tab, tab, pl.BlockSpec(tables[2].shape, lambda b, g: (0, 0))],
        out_specs=pl.BlockSpec((None, seq, C_REP * HEAD_DIM), lambda b, g: (b, 0, g)),
        out_shape=jax.ShapeDtypeStruct((bsz, seq, C_HEADS * HEAD_DIM), BF16),
        scratch_shapes=[pltpu.VMEM((C_REP, seq, HEAD_DIM), BF16), pltpu.VMEM((seq, HEAD_DIM), BF16)],
        compiler_params=_params(("parallel", "parallel")),
        name="swa_sink_attention",
    )(sinks, proj3, proj3, proj3, proj3, proj3, *tables)


def _layer_norm_store(z, g_ref, b_ref, o32_ref, o16_ref):
    mu = jnp.mean(z, axis=-1, keepdims=True)
    zc = z - mu
    var = jnp.mean(zc * zc, axis=-1, keepdims=True)
    y = zc * lax.rsqrt(var + LN_EPS) * g_ref[...] + b_ref[...]
    o32_ref[...] = y
    o16_ref[...] = y.astype(BF16)


def _stage_weight_bf16(w_hbm, layer, w16_ref, stage_ref, sem):
    chunk = stage_ref.shape[1]
    n_chunks = w16_ref.shape[0] // chunk

    def copy(c):
        slot = c % 2
        return pltpu.make_async_copy(w_hbm.at[layer, pl.ds(c * chunk, chunk), :], stage_ref.at[slot], sem.at[slot])

    copy(0).start()
    for c in range(n_chunks):
        if c + 1 < n_chunks:
            copy(c + 1).start()
        copy(c).wait()
        w16_ref[c * chunk:(c + 1) * chunk, :] = stage_ref[c % 2].astype(BF16)


def _resident_weight_scratch(k, n):
    return [pltpu.VMEM((k, n), BF16), pltpu.VMEM((2, WEIGHT_STAGE_ROWS, n), F32), pltpu.SemaphoreType.DMA((2,))]


def _out_proj_kernel(alpha, layer, oa_ref, ob_ref, oc_ref, w_hbm, x_ref, g_ref, b_ref, o32_ref, o16_ref,
                     w16_ref, stage_ref, sem):
    @pl.when(pl.program_id(0) == 0)
    def _():
        _stage_weight_bf16(w_hbm, layer, w16_ref, stage_ref, sem)

    ka, kb = oa_ref.shape[1], ob_ref.shape[1]
    y = (jnp.dot(oa_ref[...], w16_ref[0:ka, :], preferred_element_type=F32)
         + jnp.dot(ob_ref[...], w16_ref[ka:ka + kb, :], preferred_element_type=F32)
         + jnp.dot(oc_ref[...], w16_ref[ka + kb:, :], preferred_element_type=F32))
    _layer_norm_store(alpha * x_ref[...] + y, g_ref, b_ref, o32_ref, o16_ref)


def _layer_vec(layer, d):
    return pl.BlockSpec((None, 1, d), lambda i, layer=layer: (layer, 0, 0))


def _out_proj_ln(oa, ob, oc, w_out, layer, x, g, b, alpha, bm):
    m, d = x.shape
    rows = lambda a: pl.BlockSpec((bm, a.shape[1]), lambda i: (i, 0))
    return pl.pallas_call(
        functools.partial(_out_proj_kernel, alpha, layer),
        grid=(m // bm,),
        in_specs=[rows(oa), rows(ob), rows(oc), pl.BlockSpec(memory_space=pl.ANY), rows(x),
                  _layer_vec(layer, d), _layer_vec(layer, d)],
        out_specs=[rows(x), rows(x)],
        out_shape=[jax.ShapeDtypeStruct((m, d), F32), jax.ShapeDtypeStruct((m, d), BF16)],
        scratch_shapes=_resident_weight_scratch(w_out.shape[1], d),
        compiler_params=_params(("arbitrary",)),
        name="out_proj_ln",
    )(oa, ob, oc, w_out, x, g, b)


def _in_proj_kernel(x_ref, w_ref, o_ref):
    o_ref[...] = jnp.dot(x_ref[...].astype(BF16), w_ref[...].astype(BF16),
                         preferred_element_type=F32).astype(o_ref.dtype)


def _in_proj(x16, w_in, layer, bm, bn):
    m, k = x16.shape
    n = w_in.shape[2]
    return pl.pallas_call(
        _in_proj_kernel,
        grid=(m // bm, n // bn),
        in_specs=[pl.BlockSpec((bm, k), lambda i, j: (i, 0)),
                  pl.BlockSpec((None, k, bn), lambda i, j: (layer, 0, j))],
        out_specs=pl.BlockSpec((bm, bn), lambda i, j: (i, j)),
        out_shape=jax.ShapeDtypeStruct((m, n), F32),
        compiler_params=_params(("parallel", "parallel")),
        name="in_proj",
    )(x16, w_in)


FFN_UP_SUBTILES = 2


def _ffn_up_kernel(tiles_per_seq, x_ref, wg_ref, wu_ref, cw_ref, cb_ref, h_ref, g_ref, w16_ref):
    bm = x_ref.shape[0]
    bn = h_ref.shape[1]
    halo = g_ref.shape[0] - bm

    @pl.when(pl.program_id(1) == 0)
    def _():
        w16_ref[:, 0:bn] = wg_ref[...].astype(BF16)
        w16_ref[:, bn:] = wu_ref[...].astype(BF16)

    @pl.when(pl.program_id(1) % tiles_per_seq == 0)
    def _():
        g_ref[0:halo, :] = jnp.zeros((halo, g_ref.shape[1]), F32)

    sub = bm // FFN_UP_SUBTILES
    for t in range(FFN_UP_SUBTILES):
        first = halo + t * sub
        gu = jnp.dot(x_ref[t * sub:(t + 1) * sub, :], w16_ref[...], preferred_element_type=F32)
        g_ref[first:first + sub, :] = gu[:, 0:bn]
        u = gu[:, bn:]
        gc = cb_ref[...]
        for j in range(CONV_WIDTH):
            lag = CONV_WIDTH - 1 - j
            gc = gc + cw_ref[j:j + 1, :] * g_ref[first - lag:first - lag + sub, :]
        h_ref[t * sub:(t + 1) * sub, :] = (gc * jax.nn.sigmoid(gc) * u).astype(h_ref.dtype)
    g_ref[0:halo, :] = g_ref[bm:bm + halo, :]


def _ffn_up(x16, w_gate, w_up, conv_w, conv_b, layer, seq, bm, bn):
    m, d = x16.shape
    f = w_gate.shape[2]
    halo = SUBLANES
    wspec = pl.BlockSpec((None, d, bn), lambda j, i: (layer, 0, j))
    return pl.pallas_call(
        functools.partial(_ffn_up_kernel, seq // bm),
        grid=(f // bn, m // bm),
        in_specs=[pl.BlockSpec((bm, d), lambda j, i: (i, 0)), wspec, wspec,
                  pl.BlockSpec((None, CONV_WIDTH, bn), lambda j, i: (layer, 0, j)),
                  pl.BlockSpec((None, 1, bn), lambda j, i: (layer, 0, j))],
        out_specs=pl.BlockSpec((bm, bn), lambda j, i: (i, j)),
        out_shape=jax.ShapeDtypeStruct((m, f), BF16),
        scratch_shapes=[pltpu.VMEM((bm + halo, bn), F32), pltpu.VMEM((d, 2 * bn), BF16)],
        compiler_params=_params(("parallel", "arbitrary")),
        name="ffn_up_conv_gate",
    )(x16, w_gate, w_up, conv_w, conv_b)


def _ffn_down_kernel(alpha, layer, h_ref, w_hbm, x_ref, g_ref, b_ref, o32_ref, o16_ref, w16_ref, stage_ref, sem):
    @pl.when(pl.program_id(0) == 0)
    def _():
        _stage_weight_bf16(w_hbm, layer, w16_ref, stage_ref, sem)

    y = jnp.dot(h_ref[...], w16_ref[...], preferred_element_type=F32)
    _layer_norm_store(alpha * x_ref[...] + y, g_ref, b_ref, o32_ref, o16_ref)


def _ffn_down_ln(h, w_down, layer, x, g, b, alpha, bm):
    m, d = x.shape
    f = h.shape[1]
    rows = pl.BlockSpec((bm, d), lambda i: (i, 0))
    return pl.pallas_call(
        functools.partial(_ffn_down_kernel, alpha, layer),
        grid=(m // bm,),
        in_specs=[pl.BlockSpec((bm, f), lambda i: (i, 0)), pl.BlockSpec(memory_space=pl.ANY), rows,
                  _layer_vec(layer, d), _layer_vec(layer, d)],
        out_specs=[rows, rows],
        out_shape=[jax.ShapeDtypeStruct((m, d), F32), jax.ShapeDtypeStruct((m, d), BF16)],
        scratch_shapes=_resident_weight_scratch(f, d),
        compiler_params=_params(("arbitrary",)),
        name="ffn_down_ln",
    )(h, w_down, x, g, b)


def kernel(x, w_in, lb_logits, a_norm_w, c_sinks, w_out, ln1_g, ln1_b, w_gate, w_up, conv_w, conv_b, w_down,
           ln2_g, ln2_b):
    bsz, seq, d = x.shape
    depth = w_in.shape[0]
    alpha = (2 * depth) ** 0.25
    tables = _rope_tables(seq)
    lbs = jnp.cumsum(jax.nn.softmax(lb_logits.astype(F32), axis=0), axis=0)
    lbs = lbs - lbs[0]
    vec3 = lambda a: a.reshape(depth, 1, -1)
    ln1_g, ln1_b, ln2_g, ln2_b, conv_b = vec3(ln1_g), vec3(ln1_b), vec3(ln2_g), vec3(ln2_b), vec3(conv_b)
    x32 = x.reshape(bsz * seq, d)
    x16 = x32
    plan = _tile_plan(bsz * seq, d, w_out.shape[1], w_gate.shape[2])
    for l in range(depth):
        proj3 = _in_proj(x16, w_in, l, *plan.in_proj).reshape(bsz, seq, -1)
        oa = _hgrn2(proj3, lbs[l], a_norm_w[l]).reshape(bsz * seq, -1)
        ob = _dilated(proj3, tables).reshape(bsz * seq, -1)
        oc = _swa(proj3, c_sinks[l], tables).reshape(bsz * seq, -1)
        x32, x16 = _out_proj_ln(oa, ob, oc, w_out, l, x32, ln1_g, ln1_b, alpha, plan.out_proj_rows)
        h = _ffn_up(x16, w_gate, w_up, conv_w, conv_b, l, seq, *plan.ffn_up)
        x32, x16 = _ffn_down_ln(h, w_down, l, x32, ln2_g, ln2_b, alpha, plan.ffn_down_rows)
    return x32.reshape(bsz, seq, d)
```

```python
import functools
from typing import NamedTuple

import numpy as np
import jax
import jax.numpy as jnp
from jax import lax
from jax.experimental import pallas as pl
from jax.experimental.pallas import tpu as pltpu

HEAD_DIM = 128
A_HEADS = 4
B_HEADS = 6
C_HEADS = 6
C_KV_HEADS = 2
C_REP = C_HEADS // C_KV_HEADS
DILATIONS = (1, 4, 16)
ATT_BLOCK = 128
C_MAX_LAG = 127
ROPE_THETA = 500000.0
ROPE_DIM = HEAD_DIM // 4
ROPE_HALF = ROPE_DIM // 2
CONV_WIDTH = 3
LN_EPS = 1e-5

QA_BLK, FA_BLK, IA_BLK, GA_BLK = 0, 4, 8, 12
QB_BLK, KB_BLK, VB_BLK = 16, 22, 28
QC_BLK, KC_BLK, VC_BLK = 34, 40, 42

HGRN_CHUNK = 128
HGRN_LEVELS = (64, 32, 16, 8, 4, 2, 1)

VMEM_LIMIT_BYTES = 60 * 1024 * 1024

F32 = jnp.float32
BF16 = jnp.bfloat16
NEG_BIG = -1e30
LOG2_E = 1.4426950408889634


def _params(semantics):
    return pltpu.CompilerParams(dimension_semantics=semantics, vmem_limit_bytes=VMEM_LIMIT_BYTES)


SUBLANES = 8
WEIGHT_STAGE_ROWS = 512


class _TilePlan(NamedTuple):
    in_proj: tuple
    out_proj_rows: int
    ffn_up: tuple
    ffn_down_rows: int


def _tile_plan(rows, d_model, d_mix, d_ff):
    plan = _TilePlan(in_proj=(2048, 512), out_proj_rows=512, ffn_up=(2048, 512), ffn_down_rows=256)
    f32, bf16, two = 4, 2, 2
    staged = two * WEIGHT_STAGE_ROWS * d_model * f32
    bm, bn = plan.in_proj
    in_proj = (two * bm * d_model * f32 + bm * d_model * bf16
               + two * d_model * bn * f32 + d_model * bn * bf16 + two * bm * bn * f32)
    bm = plan.out_proj_rows
    out_proj = d_mix * d_model * bf16 + staged + two * bm * (d_mix * bf16 + d_model * (f32 + f32 + bf16))
    bm, bn = plan.ffn_up
    ffn_up = (two * bm * d_model * bf16 + two * two * d_model * bn * f32 + d_model * two * bn * bf16
              + (bm + SUBLANES) * bn * f32 + two * bm * bn * bf16 + bm * two * bn * f32)
    bm = plan.ffn_down_rows
    ffn_down = d_ff * d_model * bf16 + staged + two * bm * (d_ff * bf16 + d_model * (f32 + f32 + bf16))
    for name, need in (("in_proj", in_proj), ("out_proj", out_proj), ("ffn_up", ffn_up), ("ffn_down", ffn_down)):
        assert need <= VMEM_LIMIT_BYTES, (name, need)
    assert rows % max(plan.in_proj[0], plan.ffn_up[0]) == 0 and d_ff % plan.ffn_up[1] == 0
    return plan


def _rope_tables(seq):
    inv = ROPE_THETA ** (-np.arange(0, ROPE_DIM, 2, dtype=np.float64) / ROPE_DIM)
    ang = np.arange(seq, dtype=np.float64)[:, None] * inv[None, :]
    cos, sin = np.cos(ang), np.sin(ang)
    rest = HEAD_DIM - ROPE_DIM
    c = np.concatenate([cos, cos, np.ones((seq, rest))], -1).astype(np.float32)
    s = np.concatenate([sin, sin, np.zeros((seq, rest))], -1).astype(np.float32)
    rot = np.zeros((HEAD_DIM, HEAD_DIM), np.float32)
    lane = np.arange(ROPE_HALF)
    rot[lane + ROPE_HALF, lane] = -1.0
    rot[lane, lane + ROPE_HALF] = 1.0
    return c, s, jnp.asarray(np.concatenate([rot, rot], 0), BF16)


def _rope(x, c, s, rot2):
    hi = x.astype(BF16)
    lo = (x - hi.astype(F32)).astype(BF16)
    swapped = jnp.dot(jnp.concatenate([hi, lo], axis=1), rot2, preferred_element_type=F32)
    return x * c + swapped * s


def _hgrn_constants():
    c = HGRN_CHUNK
    i = np.arange(c)[:, None]
    j = np.arange(c)[None, :]
    lvl = np.full((c, c), -1, np.int32)
    for n, h in enumerate(HGRN_LEVELS):
        lvl[(i // (2 * h) == j // (2 * h)) & (i // h != j // h) & (i > j)] = n
    lvl[np.arange(c), np.arange(c)] = len(HGRN_LEVELS)
    return jnp.asarray((j <= i).astype(np.float32), BF16), jnp.asarray(lvl)


def _midpoint_log_decay(lf, b, h, row):
    c, dk = b.shape
    odd = ((row // h) % 2) == 1
    if h == 1:
        return jnp.where(odd, lf, 0.0)
    if h == 2:
        below = pltpu.roll(lf, c - 1, 0)
        above = pltpu.roll(lf, 1, 0)
        place = row % 4
        return jnp.where(place == 0, below, jnp.where(place == 1, 0.0, jnp.where(place == 2, lf, lf + above)))
    blocks = b.reshape(c // (2 * h), 2 * h, dk)
    mid = jnp.broadcast_to(blocks[:, h - 1:h, :], blocks.shape).reshape(c, dk)
    return jnp.where(odd, b - mid, mid - b)


def _split3(x):
    p1 = x.astype(BF16)
    r1 = x - p1.astype(F32)
    p2 = r1.astype(BF16)
    p3 = (r1 - p2.astype(F32)).astype(BF16)
    return p1, p2, p3


def _dot_nt(a, b, **kw):
    return lax.dot_general(a, b, (((1,), (1,)), ((), ())), preferred_element_type=F32, **kw)


def _dot_tn(a, b, **kw):
    return lax.dot_general(a, b, (((0,), (0,)), ((), ())), preferred_element_type=F32, **kw)


def _hgrn_chunk_heads(inputs, nw, e_all, lvl, row):
    c = HGRN_CHUNK
    heads = range(len(inputs))
    qs = [q * jax.nn.sigmoid(q) for q, _, _, _, _, _ in inputs]
    fg = [lb + (1.0 - lb) * jax.nn.sigmoid(fl) for _, fl, _, _, _, lb in inputs]
    kk = [1.0 - f for f in fg]
    lf = [jnp.log2(f) for f in fg]
    parts = [_split3(x) for x in lf]
    b12 = [jnp.dot(e_all, jnp.concatenate([p1, p2], axis=1), preferred_element_type=F32) for p1, p2, _ in parts]
    b3 = [jnp.dot(e_all, p3, preferred_element_type=F32) for _, _, p3 in parts]
    b = [x[:, :HEAD_DIM] + x[:, HEAD_DIM:] + y for x, y in zip(b12, b3)]
    att = [jnp.where(lvl == len(HGRN_LEVELS), jnp.sum(qs[i] * kk[i], axis=-1, keepdims=True), 0.0) for i in heads]
    for n, h in enumerate(HGRN_LEVELS):
        odd = ((row // h) % 2) == 1
        xh = [(jnp.where(odd, qs[i], kk[i]) * jnp.exp2(_midpoint_log_decay(lf[i], b[i], h, row))).astype(BF16)
              for i in heads]
        pair = [_dot_nt(x, x) for x in xh]
        att = [jnp.where(lvl == n, pair[i], att[i]) for i in heads]
    b_end = [x[c - 1:c] for x in b]
    q_in = [(qs[i] * jnp.exp2(b[i])).astype(BF16) for i in heads]
    k_end = [(kk[i] * jnp.exp2(b_end[i] - b[i])).astype(BF16) for i in heads]
    v16 = [v.astype(BF16) for _, _, v, _, _, _ in inputs]
    o = [jnp.dot(att[i].astype(BF16), v16[i], preferred_element_type=F32)
         + _dot_nt(q_in[i], inputs[i][4].astype(BF16)) for i in heads]
    st_new = [inputs[i][4] * jnp.exp2(b_end[i]) + _dot_tn(v16[i], k_end[i]) for i in heads]
    o = [x * lax.rsqrt(jnp.mean(x * x, axis=-1, keepdims=True) + LN_EPS) * nw for x in o]
    out = [(o[i] * (inputs[i][3] * jax.nn.sigmoid(inputs[i][3]))).astype(BF16) for i in heads]
    return list(zip(out, st_new))


HGRN_BATCH = 2
HGRN_SEQ_TILE = 1024


def _hgrn_kernel(q_ref, f_ref, i_ref, g_ref, lb_ref, nw_ref, e_ref, lvl_ref, o_ref, st_ref):
    c = HGRN_CHUNK
    n_batch, tile_rows = q_ref.shape[0], q_ref.shape[1]

    @pl.when(pl.program_id(1) == 0)
    def _():
        st_ref[...] = jnp.zeros_like(st_ref)

    nw = nw_ref[...]
    row = lax.broadcasted_iota(jnp.int32, (c, 1), 0)

    def chunk(ci, carry):
        rows = pl.ds(pl.multiple_of(ci * c, c), c)
        e_all, lvl = e_ref[...], lvl_ref[...]
        inputs = []
        for n in range(n_batch):
            for h in range(A_HEADS):
                cols = slice(h * HEAD_DIM, (h + 1) * HEAD_DIM)
                inputs.append((q_ref[n, rows, cols], f_ref[n, rows, cols], i_ref[n, rows, cols], g_ref[n, rows, cols],
                               st_ref[n * A_HEADS + h], lb_ref[:, cols]))
        for k, (o, st_new) in enumerate(_hgrn_chunk_heads(inputs, nw, e_all, lvl, row)):
            n, h = divmod(k, A_HEADS)
            o_ref[n, rows, h * HEAD_DIM:(h + 1) * HEAD_DIM] = o
            st_ref[k] = st_new
        return carry

    lax.fori_loop(0, tile_rows // c, chunk, 0)


def _hgrn2(proj3, lb, norm_w):
    bsz, seq, _ = proj3.shape
    e_all, lvl = _hgrn_constants()
    width = A_HEADS * HEAD_DIM
    col = lambda blk: pl.BlockSpec((HGRN_BATCH, HGRN_SEQ_TILE, width), lambda b, s, blk=blk: (b, s, blk // A_HEADS))
    const = lambda shape: pl.BlockSpec(shape, lambda b, s: (0,) * len(shape))
    return pl.pallas_call(
        _hgrn_kernel,
        grid=(bsz // HGRN_BATCH, seq // HGRN_SEQ_TILE),
        in_specs=[col(QA_BLK), col(FA_BLK), col(IA_BLK), col(GA_BLK),
                  const((1, width)), const((1, HEAD_DIM)), const(e_all.shape), const(lvl.shape)],
        out_specs=pl.BlockSpec((HGRN_BATCH, HGRN_SEQ_TILE, width), lambda b, s: (b, s, 0)),
        out_shape=jax.ShapeDtypeStruct((bsz, seq, width), BF16),
        scratch_shapes=[pltpu.VMEM((HGRN_BATCH * A_HEADS, HEAD_DIM, HEAD_DIM), F32)],
        compiler_params=_params(("parallel", "arbitrary")),
        name="hgrn2",
    )(proj3, proj3, proj3, proj3, lb.reshape(1, width), norm_w.reshape(1, HEAD_DIM), e_all, lvl)


def _band_masks(max_lag):
    r = lax.broadcasted_iota(jnp.int32, (ATT_BLOCK, ATT_BLOCK), 0)
    col = lax.broadcasted_iota(jnp.int32, (ATT_BLOCK, ATT_BLOCK), 1)
    return col <= r, (ATT_BLOCK + r - col) <= max_lag


def _tile_softmax_many(blocks):
    scale = HEAD_DIM ** -0.5 * LOG2_E
    s_all = [[jnp.where(mask, _dot_nt(q16, k.astype(BF16)) * scale, NEG_BIG) for k, mask in zip(ks, masks)]
             for q16, ks, _, masks, _ in blocks]
    m_all = []
    for s, (_, _, _, _, m_floor) in zip(s_all, blocks):
        m = jnp.max(functools.reduce(jnp.maximum, s), axis=-1, keepdims=True)
        m_all.append(m if m_floor is None else jnp.maximum(m, m_floor))
    e_all = [[jnp.exp2(si - m) for si in s] for s, m in zip(s_all, m_all)]
    acc_all = [functools.reduce(jnp.add, [jnp.dot(ei.astype(BF16), v.astype(BF16), preferred_element_type=F32)
                                          for ei, v in zip(e, vs)])
               for e, (_, _, vs, _, _) in zip(e_all, blocks)]
    l_all = [jnp.sum(functools.reduce(jnp.add, e), axis=-1, keepdims=True) for e in e_all]
    return list(zip(m_all, l_all, acc_all))


DIL_UNROLL = 16


def _dilated_kernel(proj_hbm, c_ref, s_ref, rot_ref, o_ref, qkv_ref, sem, qr_ref, kr_ref, m_ref, l_ref, acc_ref):
    seq = o_ref.shape[0]
    blk = ATT_BLOCK
    widest = DILATIONS[-1]
    per_residue = seq // widest
    step = pl.program_id(0) * pl.num_programs(1) + pl.program_id(1)
    n_steps = pl.num_programs(0) * pl.num_programs(1)
    slot = step % 2

    def qkv_copies(of_step, into_slot):
        b, h = of_step // pl.num_programs(1), of_step % pl.num_programs(1)
        copies = []
        for which, first_blk in enumerate((QB_BLK, KB_BLK, VB_BLK)):
            col = pl.multiple_of((first_blk + h) * HEAD_DIM, HEAD_DIM)
            for r in range(widest):
                copies.append(pltpu.make_async_copy(
                    proj_hbm.at[b, :, r, pl.ds(col, HEAD_DIM)],
                    qkv_ref.at[into_slot, which, pl.ds(r * per_residue, per_residue), :],
                    sem.at[into_slot]))
        return copies

    @pl.when(step == 0)
    def _():
        for copy in qkv_copies(step, slot):
            copy.start()

    @pl.when(step + 1 < n_steps)
    def _():
        for copy in qkv_copies(step + 1, 1 - slot):
            copy.start()

    for copy in qkv_copies(step, slot):
        copy.wait()

    c, s, rot = c_ref[...], s_ref[...], rot_ref[...]
    qr_ref[...] = _rope(qkv_ref[slot, 0], c, s, rot)
    kr_ref[...] = _rope(qkv_ref[slot, 1], c, s, rot)
    vr_ref = qkv_ref.at[slot, 2]

    def chunk_rows(dil):
        return blk * dil // widest

    def block_masks(dil):
        def position(i):
            return (widest // dil) * (i % chunk_rows(dil)) + i // chunk_rows(dil)
        row = position(lax.broadcasted_iota(jnp.int32, (blk, blk), 0))
        col = position(lax.broadcasted_iota(jnp.int32, (blk, blk), 1))
        return col <= row, col >= row

    def chunks_of(dil, r, n):
        size = chunk_rows(dil)
        return [pl.ds(pl.multiple_of((r + dil * a) * per_residue + size * n, size), size)
                for a in range(widest // dil)]

    def gather(ref, chunks):
        return jnp.concatenate([ref[rows, :] for rows in chunks], axis=0)

    def scatter(ref, chunks, value):
        size = value.shape[0] // len(chunks)
        for a, rows in enumerate(chunks):
            ref[rows, :] = value[a * size:(a + 1) * size]

    def run_group(specs, masks_of_pattern, first):
        cur_mask, prev_mask = masks_of_pattern
        loaded = []
        for rows, prev, prev_ok in specs:
            ks, vs, masks = [gather(kr_ref, rows)], [gather(vr_ref, rows)], [cur_mask]
            if prev is not None:
                ks.append(gather(kr_ref, prev))
                vs.append(gather(vr_ref, prev))
                masks.append(jnp.logical_and(prev_mask, prev_ok))
            state = None if first else (gather(m_ref, rows), gather(l_ref, rows), gather(acc_ref, rows))
            loaded.append((gather(qr_ref, rows).astype(BF16), ks, vs, masks, state))
        pieces = _tile_softmax_many([(q16, ks, vs, masks, None if state is None else state[0])
                                     for q16, ks, vs, masks, state in loaded])
        results = []
        for (m, l, acc), (_, _, _, _, state) in zip(pieces, loaded):
            if state is None:
                shape = (blk, HEAD_DIM)
                results.append((jnp.broadcast_to(m, shape), jnp.broadcast_to(l, shape), acc))
            else:
                m_old, l_old, acc_old = state
                alpha = jnp.exp2(m_old - m)
                results.append((m, alpha * l_old + l, alpha * acc_old + acc))
        for (rows, _, _), (m, l, acc) in zip(specs, results):
            scatter(m_ref, rows, m)
            scatter(l_ref, rows, l)
            scatter(acc_ref, rows, acc)

    n_query_blocks = seq // blk
    for dil in DILATIONS:
        n_blocks = n_query_blocks // dil
        masks_of_pattern = block_masks(dil)

        def group(g, carry, dil=dil, n_blocks=n_blocks, masks_of_pattern=masks_of_pattern):
            specs = []
            for u in range(DIL_UNROLL):
                e = g * DIL_UNROLL + u
                r, n = e % dil, e // dil
                if n_blocks == 1:
                    specs.append((chunks_of(dil, r, n), None, None))
                else:
                    specs.append((chunks_of(dil, r, n), chunks_of(dil, r, jnp.maximum(n - 1, 0)), n > 0))
            run_group(specs, masks_of_pattern, first=(dil == DILATIONS[0]))
            return carry

        lax.fori_loop(0, n_query_blocks // DIL_UNROLL, group, 0)

    for r in range(widest):
        rows = slice(r * per_residue, (r + 1) * per_residue)
        qr_ref[pl.ds(r, per_residue, stride=widest), :] = acc_ref[rows, :] / l_ref[rows, :]
    o_ref[...] = qr_ref[...].astype(o_ref.dtype)


def _residue_major(table, widest):
    seq, width = table.shape
    return table.reshape(seq // widest, widest, width).transpose(1, 0, 2).reshape(seq, width)


def _dilated(proj3, tables):
    bsz, seq, width = proj3.shape
    widest = DILATIONS[-1]
    tab = pl.BlockSpec((seq, HEAD_DIM), lambda b, h: (0, 0))
    return pl.pallas_call(
        _dilated_kernel,
        grid=(bsz, B_HEADS),
        in_specs=[pl.BlockSpec(memory_space=pl.ANY), tab, tab, pl.BlockSpec(tables[2].shape, lambda b, h: (0, 0))],
        out_specs=pl.BlockSpec((None, seq, HEAD_DIM), lambda b, h: (b, 0, h)),
        out_shape=jax.ShapeDtypeStruct((bsz, seq, B_HEADS * HEAD_DIM), BF16),
        scratch_shapes=[pltpu.VMEM((2, 3, seq, HEAD_DIM), F32), pltpu.SemaphoreType.DMA((2,))]
                       + [pltpu.VMEM((seq, HEAD_DIM), F32)] * 5,
        compiler_params=_params(("arbitrary", "arbitrary")),
        name="dilated_attention",
    )(proj3.reshape(bsz, seq // widest, widest, width), _residue_major(tables[0], widest),
      _residue_major(tables[1], widest), tables[2])


SWA_UNROLL = 16


def _swa_kernel(sink_ref, q0_ref, q1_ref, q2_ref, k_ref, v_ref, c_ref, s_ref, rot_ref, o_ref, qr_ref, kr_ref):
    seq = k_ref.shape[0]
    blk = ATT_BLOCK
    grp = pl.program_id(1)
    c, s, rot = c_ref[...], s_ref[...], rot_ref[...]
    for j, q_ref in enumerate((q0_ref, q1_ref, q2_ref)):
        qr_ref[j] = _rope(q_ref[...], c, s, rot).astype(BF16)
    kr_ref[...] = _rope(k_ref[...], c, s, rot).astype(BF16)
    cur_mask, prev_mask = _band_masks(C_MAX_LAG)
    sinks = [jnp.full((blk, 1), sink_ref[grp * C_REP + j] * LOG2_E, F32) for j in range(C_REP)]

    def group(g, carry):
        blocks, dests = [], []
        for u in range(SWA_UNROLL):
            n = g * SWA_UNROLL + u
            rows = pl.ds(pl.multiple_of(n * blk, blk), blk)
            prev = pl.ds(pl.multiple_of(jnp.maximum(n - 1, 0) * blk, blk), blk)
            ks, vs = [kr_ref[rows, :], kr_ref[prev, :]], [v_ref[rows, :], v_ref[prev, :]]
            masks = [cur_mask, jnp.logical_and(prev_mask, n > 0)]
            for j in range(C_REP):
                blocks.append((qr_ref[j, rows, :], ks, vs, masks, sinks[j]))
                dests.append((rows, j))
        for (rows, j), (m, l, acc) in zip(dests, _tile_softmax_many(blocks)):
            out = acc / (l + jnp.exp2(sinks[j] - m))
            o_ref[rows, j * HEAD_DIM:(j + 1) * HEAD_DIM] = out.astype(o_ref.dtype)
        return carry

    lax.fori_loop(0, seq // (blk * SWA_UNROLL), group, 0)


def _swa(proj3, sinks, tables):
    bsz, seq, _ = proj3.shape
    qcol = lambda j: pl.BlockSpec((None, seq, HEAD_DIM), lambda b, g, j=j: (b, 0, QC_BLK + g * C_REP + j))
    col = lambda blk: pl.BlockSpec((None, seq, HEAD_DIM), lambda b, g, blk=blk: (b, 0, blk + g))
    tab = pl.BlockSpec((seq, HEAD_DIM), lambda b, g: (0, 0))
    return pl.pallas_call(
        _swa_kernel,
        grid=(bsz, C_KV_HEADS),
        in_specs=[pl.BlockSpec(memory_space=pltpu.SMEM), qcol(0), qcol(1), qcol(2), col(KC_BLK), col(VC_BLK),
                  tab, tab, pl.BlockSpec(tables[2].shape, lambda b, g: (0, 0))],
        out_specs=pl.BlockSpec((None, seq, C_REP * HEAD_DIM), lambda b, g: (b, 0, g)),
        out_shape=jax.ShapeDtypeStruct((bsz, seq, C_HEADS * HEAD_DIM), BF16),
        scratch_shapes=[pltpu.VMEM((C_REP, seq, HEAD_DIM), BF16), pltpu.VMEM((seq, HEAD_DIM), BF16)],
        compiler_params=_params(("parallel", "parallel")),
        name="swa_sink_attention",
    )(sinks, proj3, proj3, proj3, proj3, proj3, *tables)


def _layer_norm_store(z, layer, g_ref, b_ref, o32_ref, o16_ref):
    mu = jnp.mean(z, axis=-1, keepdims=True)
    zc = z - mu
    var = jnp.mean(zc * zc, axis=-1, keepdims=True)
    y = zc * lax.rsqrt(var + LN_EPS) * g_ref[layer:layer + 1, :] + b_ref[layer:layer + 1, :]
    o32_ref[...] = y
    o16_ref[...] = y.astype(BF16)


def _stage_weight_bf16(w_hbm, layer, w16_ref, stage_ref, sem):
    chunk = stage_ref.shape[1]
    n_chunks = w16_ref.shape[0] // chunk

    def copy(c):
        slot = c % 2
        return pltpu.make_async_copy(w_hbm.at[layer, pl.ds(c * chunk, chunk), :], stage_ref.at[slot], sem.at[slot])

    copy(0).start()
    for c in range(n_chunks):
        if c + 1 < n_chunks:
            copy(c + 1).start()
        copy(c).wait()
        w16_ref[c * chunk:(c + 1) * chunk, :] = stage_ref[c % 2].astype(BF16)


def _resident_weight_scratch(k, n):
    return [pltpu.VMEM((k, n), BF16), pltpu.VMEM((2, WEIGHT_STAGE_ROWS, n), F32), pltpu.SemaphoreType.DMA((2,))]


def _out_proj_kernel(alpha, layer, oa_ref, ob_ref, oc_ref, w_hbm, x_ref, g_ref, b_ref, o32_ref, o16_ref,
                     w16_ref, stage_ref, sem):
    @pl.when(pl.program_id(0) == 0)
    def _():
        _stage_weight_bf16(w_hbm, layer, w16_ref, stage_ref, sem)

    ka, kb = oa_ref.shape[1], ob_ref.shape[1]
    y = (jnp.dot(oa_ref[...], w16_ref[0:ka, :], preferred_element_type=F32)
         + jnp.dot(ob_ref[...], w16_ref[ka:ka + kb, :], preferred_element_type=F32)
         + jnp.dot(oc_ref[...], w16_ref[ka + kb:, :], preferred_element_type=F32))
    _layer_norm_store(alpha * x_ref[...] + y, layer, g_ref, b_ref, o32_ref, o16_ref)


def _all_layers(vec):
    return pl.BlockSpec(vec.shape, lambda i: (0, 0))


def _out_proj_ln(oa, ob, oc, w_out, layer, x, g, b, alpha, bm):
    m, d = x.shape
    rows = lambda a: pl.BlockSpec((bm, a.shape[1]), lambda i: (i, 0))
    return pl.pallas_call(
        functools.partial(_out_proj_kernel, alpha, layer),
        grid=(m // bm,),
        in_specs=[rows(oa), rows(ob), rows(oc), pl.BlockSpec(memory_space=pl.ANY), rows(x),
                  _all_layers(g), _all_layers(b)],
        out_specs=[rows(x), rows(x)],
        out_shape=[jax.ShapeDtypeStruct((m, d), F32), jax.ShapeDtypeStruct((m, d), BF16)],
        scratch_shapes=_resident_weight_scratch(w_out.shape[1], d),
        compiler_params=_params(("arbitrary",)),
        name="out_proj_ln",
    )(oa, ob, oc, w_out, x, g, b)


def _in_proj_kernel(x_ref, w_ref, o_ref):
    o_ref[...] = jnp.dot(x_ref[...].astype(BF16), w_ref[...].astype(BF16),
                         preferred_element_type=F32).astype(o_ref.dtype)


def _in_proj(x16, w_in, layer, bm, bn):
    m, k = x16.shape
    n = w_in.shape[2]
    return pl.pallas_call(
        _in_proj_kernel,
        grid=(m // bm, n // bn),
        in_specs=[pl.BlockSpec((bm, k), lambda i, j: (i, 0)),
                  pl.BlockSpec((None, k, bn), lambda i, j: (layer, 0, j))],
        out_specs=pl.BlockSpec((bm, bn), lambda i, j: (i, j)),
        out_shape=jax.ShapeDtypeStruct((m, n), F32),
        compiler_params=_params(("parallel", "parallel")),
        name="in_proj",
    )(x16, w_in)


FFN_UP_SUBTILES = 2


def _ffn_up_kernel(tiles_per_seq, layer, x_ref, wg_ref, wu_ref, cw_ref, cb_ref, h_ref, g_ref, w16_ref):
    bm = x_ref.shape[0]
    bn = h_ref.shape[1]
    halo = g_ref.shape[0] - bm

    @pl.when(pl.program_id(1) == 0)
    def _():
        w16_ref[:, 0:bn] = wg_ref[...].astype(BF16)
        w16_ref[:, bn:] = wu_ref[...].astype(BF16)

    @pl.when(pl.program_id(1) % tiles_per_seq == 0)
    def _():
        g_ref[0:halo, :] = jnp.zeros((halo, g_ref.shape[1]), F32)

    sub = bm // FFN_UP_SUBTILES
    for t in range(FFN_UP_SUBTILES):
        first = halo + t * sub
        gu = jnp.dot(x_ref[t * sub:(t + 1) * sub, :], w16_ref[...], preferred_element_type=F32)
        g_ref[first:first + sub, :] = gu[:, 0:bn]
        u = gu[:, bn:]
        gc = cb_ref[layer:layer + 1, :]
        for j in range(CONV_WIDTH):
            lag = CONV_WIDTH - 1 - j
            gc = gc + cw_ref[j:j + 1, :] * g_ref[first - lag:first - lag + sub, :]
        h_ref[t * sub:(t + 1) * sub, :] = (gc * jax.nn.sigmoid(gc) * u).astype(h_ref.dtype)
    g_ref[0:halo, :] = g_ref[bm:bm + halo, :]


def _ffn_up(x16, w_gate, w_up, conv_w, conv_b, layer, seq, bm, bn):
    m, d = x16.shape
    f = w_gate.shape[2]
    halo = SUBLANES
    wspec = pl.BlockSpec((None, d, bn), lambda j, i: (layer, 0, j))
    return pl.pallas_call(
        functools.partial(_ffn_up_kernel, seq // bm, layer),
        grid=(f // bn, m // bm),
        in_specs=[pl.BlockSpec((bm, d), lambda j, i: (i, 0)), wspec, wspec,
                  pl.BlockSpec((None, CONV_WIDTH, bn), lambda j, i: (layer, 0, j)),
                  pl.BlockSpec((conv_b.shape[0], bn), lambda j, i: (0, j))],
        out_specs=pl.BlockSpec((bm, bn), lambda j, i: (i, j)),
        out_shape=jax.ShapeDtypeStruct((m, f), BF16),
        scratch_shapes=[pltpu.VMEM((bm + halo, bn), F32), pltpu.VMEM((d, 2 * bn), BF16)],
        compiler_params=_params(("parallel", "arbitrary")),
        name="ffn_up_conv_gate",
    )(x16, w_gate, w_up, conv_w, conv_b)


def _ffn_down_kernel(alpha, layer, h_ref, w_hbm, x_ref, g_ref, b_ref, o32_ref, o16_ref, w16_ref, stage_ref, sem):
    @pl.when(pl.program_id(0) == 0)
    def _():
        _stage_weight_bf16(w_hbm, layer, w16_ref, stage_ref, sem)

    y = jnp.dot(h_ref[...], w16_ref[...], preferred_element_type=F32)
    _layer_norm_store(alpha * x_ref[...] + y, layer, g_ref, b_ref, o32_ref, o16_ref)


def _ffn_down_ln(h, w_down, layer, x, g, b, alpha, bm):
    m, d = x.shape
    f = h.shape[1]
    rows = pl.BlockSpec((bm, d), lambda i: (i, 0))
    return pl.pallas_call(
        functools.partial(_ffn_down_kernel, alpha, layer),
        grid=(m // bm,),
        in_specs=[pl.BlockSpec((bm, f), lambda i: (i, 0)), pl.BlockSpec(memory_space=pl.ANY), rows,
                  _all_layers(g), _all_layers(b)],
        out_specs=[rows, rows],
        out_shape=[jax.ShapeDtypeStruct((m, d), F32), jax.ShapeDtypeStruct((m, d), BF16)],
        scratch_shapes=_resident_weight_scratch(f, d),
        compiler_params=_params(("arbitrary",)),
        name="ffn_down_ln",
    )(h, w_down, x, g, b)


def kernel(x, w_in, lb_logits, a_norm_w, c_sinks, w_out, ln1_g, ln1_b, w_gate, w_up, conv_w, conv_b, w_down,
           ln2_g, ln2_b):
    bsz, seq, d = x.shape
    depth = w_in.shape[0]
    alpha = (2 * depth) ** 0.25
    tables = _rope_tables(seq)
    lbs = jnp.cumsum(jax.nn.softmax(lb_logits.astype(F32), axis=0), axis=0)
    lbs = lbs - lbs[0]
    x32 = x.reshape(bsz * seq, d)
    x16 = x32
    plan = _tile_plan(bsz * seq, d, w_out.shape[1], w_gate.shape[2])
    for l in range(depth):
        proj3 = _in_proj(x16, w_in, l, *plan.in_proj).reshape(bsz, seq, -1)
        oa = _hgrn2(proj3, lbs[l], a_norm_w[l]).reshape(bsz * seq, -1)
        ob = _dilated(proj3, tables).reshape(bsz * seq, -1)
        oc = _swa(proj3, c_sinks[l], tables).reshape(bsz * seq, -1)
        x32, x16 = _out_proj_ln(oa, ob, oc, w_out, l, x32, ln1_g, ln1_b, alpha, plan.out_proj_rows)
        h = _ffn_up(x16, w_gate, w_up, conv_w, conv_b, l, seq, *plan.ffn_up)
        x32, x16 = _ffn_down_ln(h, w_down, l, x32, ln2_g, ln2_b, alpha, plan.ffn_down_rows)
    return x32.reshape(bsz, seq, d)
```

```python
import functools
from typing import NamedTuple

import numpy as np
import jax
import jax.numpy as jnp
from jax import lax
from jax.experimental import pallas as pl
from jax.experimental.pallas import tpu as pltpu

HEAD_DIM = 128
A_HEADS = 4
B_HEADS = 6
C_HEADS = 6
C_KV_HEADS = 2
C_REP = C_HEADS // C_KV_HEADS
DILATIONS = (1, 4, 16)
ATT_BLOCK = 128
C_MAX_LAG = 127
ROPE_THETA = 500000.0
ROPE_DIM = HEAD_DIM // 4
ROPE_HALF = ROPE_DIM // 2
CONV_WIDTH = 3
LN_EPS = 1e-5

QA_BLK, FA_BLK, IA_BLK, GA_BLK = 0, 4, 8, 12
QB_BLK, KB_BLK, VB_BLK = 16, 22, 28
QC_BLK, KC_BLK, VC_BLK = 34, 40, 42

HGRN_CHUNK = 128
HGRN_LEVELS = (64, 32, 16, 8, 4, 2, 1)

VMEM_LIMIT_BYTES = 60 * 1024 * 1024

F32 = jnp.float32
BF16 = jnp.bfloat16
NEG_BIG = -1e30
LOG2_E = 1.4426950408889634


def _params(semantics):
    return pltpu.CompilerParams(dimension_semantics=semantics, vmem_limit_bytes=VMEM_LIMIT_BYTES)


SUBLANES = 8
WEIGHT_STAGE_ROWS = 512


class _TilePlan(NamedTuple):
    in_proj: tuple
    out_proj_rows: int
    ffn_up: tuple
    ffn_down_rows: int


def _tile_plan(rows, d_model, d_mix, d_ff):
    plan = _TilePlan(in_proj=(2048, 512), out_proj_rows=512, ffn_up=(2048, 512), ffn_down_rows=256)
    f32, bf16, two = 4, 2, 2
    staged = two * WEIGHT_STAGE_ROWS * d_model * f32
    bm, bn = plan.in_proj
    in_proj = (two * bm * d_model * f32 + bm * d_model * bf16
               + two * d_model * bn * f32 + d_model * bn * bf16 + two * bm * bn * f32)
    bm = plan.out_proj_rows
    out_proj = d_mix * d_model * bf16 + staged + two * bm * (d_mix * bf16 + d_model * (f32 + f32 + bf16))
    bm, bn = plan.ffn_up
    ffn_up = (two * bm * d_model * bf16 + two * two * d_model * bn * f32 + d_model * two * bn * bf16
              + (bm + SUBLANES) * bn * f32 + two * bm * bn * bf16 + bm * two * bn * f32)
    bm = plan.ffn_down_rows
    ffn_down = d_ff * d_model * bf16 + staged + two * bm * (d_ff * bf16 + d_model * (f32 + f32 + bf16))
    for name, need in (("in_proj", in_proj), ("out_proj", out_proj), ("ffn_up", ffn_up), ("ffn_down", ffn_down)):
        assert need <= VMEM_LIMIT_BYTES, (name, need)
    assert rows % max(plan.in_proj[0], plan.ffn_up[0]) == 0 and d_ff % plan.ffn_up[1] == 0
    return plan


def _rope_tables(seq):
    inv = ROPE_THETA ** (-np.arange(0, ROPE_DIM, 2, dtype=np.float64) / ROPE_DIM)
    ang = np.arange(seq, dtype=np.float64)[:, None] * inv[None, :]
    cos, sin = np.cos(ang), np.sin(ang)
    rest = HEAD_DIM - ROPE_DIM
    c = np.concatenate([cos, cos, np.ones((seq, rest))], -1).astype(np.float32)
    s = np.concatenate([sin, sin, np.zeros((seq, rest))], -1).astype(np.float32)
    rot = np.zeros((HEAD_DIM, HEAD_DIM), np.float32)
    lane = np.arange(ROPE_HALF)
    rot[lane + ROPE_HALF, lane] = -1.0
    rot[lane, lane + ROPE_HALF] = 1.0
    return c, s, jnp.asarray(np.concatenate([rot, rot], 0), BF16)


def _rope(x, c, s, rot2):
    hi = x.astype(BF16)
    lo = (x - hi.astype(F32)).astype(BF16)
    swapped = jnp.dot(jnp.concatenate([hi, lo], axis=1), rot2, preferred_element_type=F32)
    return x * c + swapped * s


def _hgrn_constants():
    c = HGRN_CHUNK
    i = np.arange(c)[:, None]
    j = np.arange(c)[None, :]
    lvl = np.full((c, c), -1, np.int32)
    for n, h in enumerate(HGRN_LEVELS):
        lvl[(i // (2 * h) == j // (2 * h)) & (i // h != j // h) & (i > j)] = n
    lvl[np.arange(c), np.arange(c)] = len(HGRN_LEVELS)
    return jnp.asarray((j <= i).astype(np.float32), BF16), jnp.asarray(lvl)


def _midpoint_log_decay(lf, b, h, row):
    c, dk = b.shape
    odd = ((row // h) % 2) == 1
    if h == 1:
        return jnp.where(odd, lf, 0.0)
    if h == 2:
        below = pltpu.roll(lf, c - 1, 0)
        above = pltpu.roll(lf, 1, 0)
        place = row % 4
        return jnp.where(place == 0, below, jnp.where(place == 1, 0.0, jnp.where(place == 2, lf, lf + above)))
    blocks = b.reshape(c // (2 * h), 2 * h, dk)
    mid = jnp.broadcast_to(blocks[:, h - 1:h, :], blocks.shape).reshape(c, dk)
    return jnp.where(odd, b - mid, mid - b)


def _split3(x):
    p1 = x.astype(BF16)
    r1 = x - p1.astype(F32)
    p2 = r1.astype(BF16)
    p3 = (r1 - p2.astype(F32)).astype(BF16)
    return p1, p2, p3


def _dot_nt(a, b, **kw):
    return lax.dot_general(a, b, (((1,), (1,)), ((), ())), preferred_element_type=F32, **kw)


def _dot_tn(a, b, **kw):
    return lax.dot_general(a, b, (((0,), (0,)), ((), ())), preferred_element_type=F32, **kw)


def _hgrn_chunk_heads(inputs, nw, e_all, lvl, row):
    c = HGRN_CHUNK
    heads = range(len(inputs))
    qs = [q * jax.nn.sigmoid(q) for q, _, _, _, _, _ in inputs]
    fg = [lb + (1.0 - lb) * jax.nn.sigmoid(fl) for _, fl, _, _, _, lb in inputs]
    kk = [1.0 - f for f in fg]
    lf = [jnp.log2(f) for f in fg]
    parts = [_split3(x) for x in lf]
    b12 = [jnp.dot(e_all, jnp.concatenate([p1, p2], axis=1), preferred_element_type=F32) for p1, p2, _ in parts]
    b3 = [jnp.dot(e_all, p3, preferred_element_type=F32) for _, _, p3 in parts]
    b = [x[:, :HEAD_DIM] + x[:, HEAD_DIM:] + y for x, y in zip(b12, b3)]
    att = [jnp.where(lvl == len(HGRN_LEVELS), jnp.sum(qs[i] * kk[i], axis=-1, keepdims=True), 0.0) for i in heads]
    for n, h in enumerate(HGRN_LEVELS):
        odd = ((row // h) % 2) == 1
        xh = [(jnp.where(odd, qs[i], kk[i]) * jnp.exp2(_midpoint_log_decay(lf[i], b[i], h, row))).astype(BF16)
              for i in heads]
        pair = [_dot_nt(x, x) for x in xh]
        att = [jnp.where(lvl == n, pair[i], att[i]) for i in heads]
    b_end = [x[c - 1:c] for x in b]
    q_in = [(qs[i] * jnp.exp2(b[i])).astype(BF16) for i in heads]
    k_end = [(kk[i] * jnp.exp2(b_end[i] - b[i])).astype(BF16) for i in heads]
    v16 = [v.astype(BF16) for _, _, v, _, _, _ in inputs]
    o = [jnp.dot(att[i].astype(BF16), v16[i], preferred_element_type=F32)
         + _dot_nt(q_in[i], inputs[i][4].astype(BF16)) for i in heads]
    st_new = [inputs[i][4] * jnp.exp2(b_end[i]) + _dot_tn(v16[i], k_end[i]) for i in heads]
    o = [x * lax.rsqrt(jnp.mean(x * x, axis=-1, keepdims=True) + LN_EPS) * nw for x in o]
    out = [(o[i] * (inputs[i][3] * jax.nn.sigmoid(inputs[i][3]))).astype(BF16) for i in heads]
    return list(zip(out, st_new))


HGRN_BATCH = 4
HGRN_SEQ_TILE = 512


def _hgrn_kernel(q_ref, f_ref, i_ref, g_ref, lb_ref, nw_ref, e_ref, lvl_ref, o_ref, st_ref):
    c = HGRN_CHUNK
    n_batch, tile_rows = q_ref.shape[0], q_ref.shape[1]

    @pl.when(pl.program_id(1) == 0)
    def _():
        st_ref[...] = jnp.zeros_like(st_ref)

    nw = nw_ref[...]
    row = lax.broadcasted_iota(jnp.int32, (c, 1), 0)

    def chunk(ci, carry):
        rows = pl.ds(pl.multiple_of(ci * c, c), c)
        e_all, lvl = e_ref[...], lvl_ref[...]
        inputs = []
        for n in range(n_batch):
            for h in range(A_HEADS):
                cols = slice(h * HEAD_DIM, (h + 1) * HEAD_DIM)
                inputs.append((q_ref[n, rows, cols], f_ref[n, rows, cols], i_ref[n, rows, cols], g_ref[n, rows, cols],
                               st_ref[n * A_HEADS + h], lb_ref[:, cols]))
        for k, (o, st_new) in enumerate(_hgrn_chunk_heads(inputs, nw, e_all, lvl, row)):
            n, h = divmod(k, A_HEADS)
            o_ref[n, rows, h * HEAD_DIM:(h + 1) * HEAD_DIM] = o
            st_ref[k] = st_new
        return carry

    lax.fori_loop(0, tile_rows // c, chunk, 0)


def _hgrn2(proj3, lb, norm_w):
    bsz, seq, _ = proj3.shape
    e_all, lvl = _hgrn_constants()
    width = A_HEADS * HEAD_DIM
    col = lambda blk: pl.BlockSpec((HGRN_BATCH, HGRN_SEQ_TILE, width), lambda b, s, blk=blk: (b, s, blk // A_HEADS))
    const = lambda shape: pl.BlockSpec(shape, lambda b, s: (0,) * len(shape))
    return pl.pallas_call(
        _hgrn_kernel,
        grid=(bsz // HGRN_BATCH, seq // HGRN_SEQ_TILE),
        in_specs=[col(QA_BLK), col(FA_BLK), col(IA_BLK), col(GA_BLK),
                  const((1, width)), const((1, HEAD_DIM)), const(e_all.shape), const(lvl.shape)],
        out_specs=pl.BlockSpec((HGRN_BATCH, HGRN_SEQ_TILE, width), lambda b, s: (b, s, 0)),
        out_shape=jax.ShapeDtypeStruct((bsz, seq, width), BF16),
        scratch_shapes=[pltpu.VMEM((HGRN_BATCH * A_HEADS, HEAD_DIM, HEAD_DIM), F32)],
        compiler_params=_params(("parallel", "arbitrary")),
        name="hgrn2",
    )(proj3, proj3, proj3, proj3, lb.reshape(1, width), norm_w.reshape(1, HEAD_DIM), e_all, lvl)


def _band_masks(max_lag):
    r = lax.broadcasted_iota(jnp.int32, (ATT_BLOCK, ATT_BLOCK), 0)
    col = lax.broadcasted_iota(jnp.int32, (ATT_BLOCK, ATT_BLOCK), 1)
    return col <= r, (ATT_BLOCK + r - col) <= max_lag


def _tile_softmax_many(blocks):
    scale = HEAD_DIM ** -0.5 * LOG2_E
    s_all = [[jnp.where(mask, _dot_nt(q16, k.astype(BF16)) * scale, NEG_BIG) for k, mask in zip(ks, masks)]
             for q16, ks, _, masks, _ in blocks]
    m_all = []
    for s, (_, _, _, _, m_floor) in zip(s_all, blocks):
        m = jnp.max(functools.reduce(jnp.maximum, s), axis=-1, keepdims=True)
        m_all.append(m if m_floor is None else jnp.maximum(m, m_floor))
    e_all = [[jnp.exp2(si - m) for si in s] for s, m in zip(s_all, m_all)]
    acc_all = [functools.reduce(jnp.add, [jnp.dot(ei.astype(BF16), v.astype(BF16), preferred_element_type=F32)
                                          for ei, v in zip(e, vs)])
               for e, (_, _, vs, _, _) in zip(e_all, blocks)]
    l_all = [jnp.sum(functools.reduce(jnp.add, e), axis=-1, keepdims=True) for e in e_all]
    return list(zip(m_all, l_all, acc_all))


DIL_UNROLL = 16


def _dilated_kernel(proj_hbm, c_ref, s_ref, rot_ref, o_ref, qkv_ref, sem, qr_ref, kr_ref, m_ref, l_ref, acc_ref):
    seq = o_ref.shape[0]
    blk = ATT_BLOCK
    widest = DILATIONS[-1]
    per_residue = seq // widest
    step = pl.program_id(0) * pl.num_programs(1) + pl.program_id(1)
    n_steps = pl.num_programs(0) * pl.num_programs(1)
    slot = step % 2

    def qkv_copies(of_step, into_slot):
        b, h = of_step // pl.num_programs(1), of_step % pl.num_programs(1)
        copies = []
        for which, first_blk in enumerate((QB_BLK, KB_BLK, VB_BLK)):
            col = pl.multiple_of((first_blk + h) * HEAD_DIM, HEAD_DIM)
            for r in range(widest):
                copies.append(pltpu.make_async_copy(
                    proj_hbm.at[b, :, r, pl.ds(col, HEAD_DIM)],
                    qkv_ref.at[into_slot, which, pl.ds(r * per_residue, per_residue), :],
                    sem.at[into_slot]))
        return copies

    @pl.when(step == 0)
    def _():
        for copy in qkv_copies(step, slot):
            copy.start()

    @pl.when(step + 1 < n_steps)
    def _():
        for copy in qkv_copies(step + 1, 1 - slot):
            copy.start()

    for copy in qkv_copies(step, slot):
        copy.wait()

    c, s, rot = c_ref[...], s_ref[...], rot_ref[...]
    qr_ref[...] = _rope(qkv_ref[slot, 0], c, s, rot)
    kr_ref[...] = _rope(qkv_ref[slot, 1], c, s, rot)
    vr_ref = qkv_ref.at[slot, 2]

    def chunk_rows(dil):
        return blk * dil // widest

    def block_masks(dil):
        def position(i):
            return (widest // dil) * (i % chunk_rows(dil)) + i // chunk_rows(dil)
        row = position(lax.broadcasted_iota(jnp.int32, (blk, blk), 0))
        col = position(lax.broadcasted_iota(jnp.int32, (blk, blk), 1))
        return col <= row, col >= row

    def chunks_of(dil, r, n):
        size = chunk_rows(dil)
        return [pl.ds(pl.multiple_of((r + dil * a) * per_residue + size * n, size), size)
                for a in range(widest // dil)]

    def gather(ref, chunks):
        return jnp.concatenate([ref[rows, :] for rows in chunks], axis=0)

    def scatter(ref, chunks, value):
        size = value.shape[0] // len(chunks)
        for a, rows in enumerate(chunks):
            ref[rows, :] = value[a * size:(a + 1) * size]

    def run_group(specs, masks_of_pattern, first):
        cur_mask, prev_mask = masks_of_pattern
        loaded = []
        for rows, prev, prev_ok in specs:
            ks, vs, masks = [gather(kr_ref, rows)], [gather(vr_ref, rows)], [cur_mask]
            if prev is not None:
                ks.append(gather(kr_ref, prev))
                vs.append(gather(vr_ref, prev))
                masks.append(jnp.logical_and(prev_mask, prev_ok))
            state = None if first else (gather(m_ref, rows), gather(l_ref, rows), gather(acc_ref, rows))
            loaded.append((gather(qr_ref, rows).astype(BF16), ks, vs, masks, state))
        pieces = _tile_softmax_many([(q16, ks, vs, masks, None if state is None else state[0])
                                     for q16, ks, vs, masks, state in loaded])
        results = []
        for (m, l, acc), (_, _, _, _, state) in zip(pieces, loaded):
            if state is None:
                shape = (blk, HEAD_DIM)
                results.append((jnp.broadcast_to(m, shape), jnp.broadcast_to(l, shape), acc))
            else:
                m_old, l_old, acc_old = state
                alpha = jnp.exp2(m_old - m)
                results.append((m, alpha * l_old + l, alpha * acc_old + acc))
        for (rows, _, _), (m, l, acc) in zip(specs, results):
            scatter(m_ref, rows, m)
            scatter(l_ref, rows, l)
            scatter(acc_ref, rows, acc)

    n_query_blocks = seq // blk
    for dil in DILATIONS:
        n_blocks = n_query_blocks // dil
        masks_of_pattern = block_masks(dil)

        def group(g, carry, dil=dil, n_blocks=n_blocks, masks_of_pattern=masks_of_pattern):
            specs = []
            for u in range(DIL_UNROLL):
                e = g * DIL_UNROLL + u
                r, n = e % dil, e // dil
                if n_blocks == 1:
                    specs.append((chunks_of(dil, r, n), None, None))
                else:
                    specs.append((chunks_of(dil, r, n), chunks_of(dil, r, jnp.maximum(n - 1, 0)), n > 0))
            run_group(specs, masks_of_pattern, first=(dil == DILATIONS[0]))
            return carry

        lax.fori_loop(0, n_query_blocks // DIL_UNROLL, group, 0)

    for r in range(widest):
        rows = slice(r * per_residue, (r + 1) * per_residue)
        qr_ref[pl.ds(r, per_residue, stride=widest), :] = acc_ref[rows, :] / l_ref[rows, :]
    o_ref[...] = qr_ref[...].astype(o_ref.dtype)


def _residue_major(table, widest):
    seq, width = table.shape
    return table.reshape(seq // widest, widest, width).transpose(1, 0, 2).reshape(seq, width)


def _dilated(proj3, tables):
    bsz, seq, width = proj3.shape
    widest = DILATIONS[-1]
    tab = pl.BlockSpec((seq, HEAD_DIM), lambda b, h: (0, 0))
    return pl.pallas_call(
        _dilated_kernel,
        grid=(bsz, B_HEADS),
        in_specs=[pl.BlockSpec(memory_space=pl.ANY), tab, tab, pl.BlockSpec(tables[2].shape, lambda b, h: (0, 0))],
        out_specs=pl.BlockSpec((None, seq, HEAD_DIM), lambda b, h: (b, 0, h)),
        out_shape=jax.ShapeDtypeStruct((bsz, seq, B_HEADS * HEAD_DIM), BF16),
        scratch_shapes=[pltpu.VMEM((2, 3, seq, HEAD_DIM), F32), pltpu.SemaphoreType.DMA((2,))]
                       + [pltpu.VMEM((seq, HEAD_DIM), F32)] * 5,
        compiler_params=_params(("arbitrary", "arbitrary")),
        name="dilated_attention",
    )(proj3.reshape(bsz, seq // widest, widest, width), _residue_major(tables[0], widest),
      _residue_major(tables[1], widest), tables[2])


SWA_UNROLL = 16


def _swa_kernel(sink_ref, q0_ref, q1_ref, q2_ref, k_ref, v_ref, c_ref, s_ref, rot_ref, o_ref, qr_ref, kr_ref):
    seq = k_ref.shape[0]
    blk = ATT_BLOCK
    grp = pl.program_id(1)
    c, s, rot = c_ref[...], s_ref[...], rot_ref[...]
    for j, q_ref in enumerate((q0_ref, q1_ref, q2_ref)):
        qr_ref[j] = _rope(q_ref[...], c, s, rot).astype(BF16)
    kr_ref[...] = _rope(k_ref[...], c, s, rot).astype(BF16)
    cur_mask, prev_mask = _band_masks(C_MAX_LAG)
    sinks = [jnp.full((blk, 1), sink_ref[grp * C_REP + j] * LOG2_E, F32) for j in range(C_REP)]

    def group(g, carry):
        blocks, dests = [], []
        for u in range(SWA_UNROLL):
            n = g * SWA_UNROLL + u
            rows = pl.ds(pl.multiple_of(n * blk, blk), blk)
            prev = pl.ds(pl.multiple_of(jnp.maximum(n - 1, 0) * blk, blk), blk)
            ks, vs = [kr_ref[rows, :], kr_ref[prev, :]], [v_ref[rows, :], v_ref[prev, :]]
            masks = [cur_mask, jnp.logical_and(prev_mask, n > 0)]
            for j in range(C_REP):
                blocks.append((qr_ref[j, rows, :], ks, vs, masks, sinks[j]))
                dests.append((rows, j))
        for (rows, j), (m, l, acc) in zip(dests, _tile_softmax_many(blocks)):
            out = acc / (l + jnp.exp2(sinks[j] - m))
            o_ref[rows, j * HEAD_DIM:(j + 1) * HEAD_DIM] = out.astype(o_ref.dtype)
        return carry

    lax.fori_loop(0, seq // (blk * SWA_UNROLL), group, 0)


def _swa(proj3, sinks, tables):
    bsz, seq, _ = proj3.shape
    qcol = lambda j: pl.BlockSpec((None, seq, HEAD_DIM), lambda b, g, j=j: (b, 0, QC_BLK + g * C_REP + j))
    col = lambda blk: pl.BlockSpec((None, seq, HEAD_DIM), lambda b, g, blk=blk: (b, 0, blk + g))
    tab = pl.BlockSpec((seq, HEAD_DIM), lambda b, g: (0, 0))
    return pl.pallas_call(
        _swa_kernel,
        grid=(bsz, C_KV_HEADS),
        in_specs=[pl.BlockSpec(memory_space=pltpu.SMEM), qcol(0), qcol(1), qcol(2), col(KC_BLK), col(VC_BLK),
                  tab, tab, pl.BlockSpec(tables[2].shape, lambda b, g: (0, 0))],
        out_specs=pl.BlockSpec((None, seq, C_REP * HEAD_DIM), lambda b, g: (b, 0, g)),
        out_shape=jax.ShapeDtypeStruct((bsz, seq, C_HEADS * HEAD_DIM), BF16),
        scratch_shapes=[pltpu.VMEM((C_REP, seq, HEAD_DIM), BF16), pltpu.VMEM((seq, HEAD_DIM), BF16)],
        compiler_params=_params(("parallel", "parallel")),
        name="swa_sink_attention",
    )(sinks, proj3, proj3, proj3, proj3, proj3, *tables)


def _layer_norm_store(z, layer, g_ref, b_ref, o32_ref, o16_ref):
    mu = jnp.mean(z, axis=-1, keepdims=True)
    zc = z - mu
    var = jnp.mean(zc * zc, axis=-1, keepdims=True)
    y = zc * lax.rsqrt(var + LN_EPS) * g_ref[layer:layer + 1, :] + b_ref[layer:layer + 1, :]
    o32_ref[...] = y
    o16_ref[...] = y.astype(BF16)


def _stage_weight_bf16(w_hbm, layer, w16_ref, stage_ref, sem):
    chunk = stage_ref.shape[1]
    n_chunks = w16_ref.shape[0] // chunk

    def copy(c):
        slot = c % 2
        return pltpu.make_async_copy(w_hbm.at[layer, pl.ds(c * chunk, chunk), :], stage_ref.at[slot], sem.at[slot])

    copy(0).start()
    for c in range(n_chunks):
        if c + 1 < n_chunks:
            copy(c + 1).start()
        copy(c).wait()
        w16_ref[c * chunk:(c + 1) * chunk, :] = stage_ref[c % 2].astype(BF16)


def _resident_weight_scratch(k, n):
    return [pltpu.VMEM((k, n), BF16), pltpu.VMEM((2, WEIGHT_STAGE_ROWS, n), F32), pltpu.SemaphoreType.DMA((2,))]


def _out_proj_kernel(alpha, layer, oa_ref, ob_ref, oc_ref, w_hbm, x_ref, g_ref, b_ref, o32_ref, o16_ref,
                     w16_ref, stage_ref, sem):
    @pl.when(pl.program_id(0) == 0)
    def _():
        _stage_weight_bf16(w_hbm, layer, w16_ref, stage_ref, sem)

    ka, kb = oa_ref.shape[1], ob_ref.shape[1]
    y = (jnp.dot(oa_ref[...], w16_ref[0:ka, :], preferred_element_type=F32)
         + jnp.dot(ob_ref[...], w16_ref[ka:ka + kb, :], preferred_element_type=F32)
         + jnp.dot(oc_ref[...], w16_ref[ka + kb:, :], preferred_element_type=F32))
    _layer_norm_store(alpha * x_ref[...] + y, layer, g_ref, b_ref, o32_ref, o16_ref)


def _all_layers(vec):
    return pl.BlockSpec(vec.shape, lambda i: (0, 0))


def _out_proj_ln(oa, ob, oc, w_out, layer, x, g, b, alpha, bm):
    m, d = x.shape
    rows = lambda a: pl.BlockSpec((bm, a.shape[1]), lambda i: (i, 0))
    return pl.pallas_call(
        functools.partial(_out_proj_kernel, alpha, layer),
        grid=(m // bm,),
        in_specs=[rows(oa), rows(ob), rows(oc), pl.BlockSpec(memory_space=pl.ANY), rows(x),
                  _all_layers(g), _all_layers(b)],
        out_specs=[rows(x), rows(x)],
        out_shape=[jax.ShapeDtypeStruct((m, d), F32), jax.ShapeDtypeStruct((m, d), BF16)],
        scratch_shapes=_resident_weight_scratch(w_out.shape[1], d),
        compiler_params=_params(("arbitrary",)),
        name="out_proj_ln",
    )(oa, ob, oc, w_out, x, g, b)


def _in_proj_kernel(x_ref, w_ref, o_ref):
    o_ref[...] = jnp.dot(x_ref[...].astype(BF16), w_ref[...].astype(BF16),
                         preferred_element_type=F32).astype(o_ref.dtype)


def _in_proj(x16, w_in, layer, bm, bn):
    m, k = x16.shape
    n = w_in.shape[2]
    return pl.pallas_call(
        _in_proj_kernel,
        grid=(m // bm, n // bn),
        in_specs=[pl.BlockSpec((bm, k), lambda i, j: (i, 0)),
                  pl.BlockSpec((None, k, bn), lambda i, j: (layer, 0, j))],
        out_specs=pl.BlockSpec((bm, bn), lambda i, j: (i, j)),
        out_shape=jax.ShapeDtypeStruct((m, n), F32),
        compiler_params=_params(("parallel", "parallel")),
        name="in_proj",
    )(x16, w_in)


FFN_UP_SUBTILES = 2


def _ffn_up_kernel(tiles_per_seq, layer, x_ref, wg_ref, wu_ref, cw_ref, cb_ref, h_ref, g_ref, w16_ref):
    bm = x_ref.shape[0]
    bn = h_ref.shape[1]
    halo = g_ref.shape[0] - bm

    @pl.when(pl.program_id(1) == 0)
    def _():
        w16_ref[:, 0:bn] = wg_ref[...].astype(BF16)
        w16_ref[:, bn:] = wu_ref[...].astype(BF16)

    @pl.when(pl.program_id(1) % tiles_per_seq == 0)
    def _():
        g_ref[0:halo, :] = jnp.zeros((halo, g_ref.shape[1]), F32)

    sub = bm // FFN_UP_SUBTILES
    for t in range(FFN_UP_SUBTILES):
        first = halo + t * sub
        gu = jnp.dot(x_ref[t * sub:(t + 1) * sub, :], w16_ref[...], preferred_element_type=F32)
        g_ref[first:first + sub, :] = gu[:, 0:bn]
        u = gu[:, bn:]
        gc = cb_ref[layer:layer + 1, :]
        for j in range(CONV_WIDTH):
            lag = CONV_WIDTH - 1 - j
            gc = gc + cw_ref[j:j + 1, :] * g_ref[first - lag:first - lag + sub, :]
        h_ref[t * sub:(t + 1) * sub, :] = (gc * jax.nn.sigmoid(gc) * u).astype(h_ref.dtype)
    g_ref[0:halo, :] = g_ref[bm:bm + halo, :]


def _ffn_up(x16, w_gate, w_up, conv_w, conv_b, layer, seq, bm, bn):
    m, d = x16.shape
    f = w_gate.shape[2]
    halo = SUBLANES
    wspec = pl.BlockSpec((None, d, bn), lambda j, i: (layer, 0, j))
    return pl.pallas_call(
        functools.partial(_ffn_up_kernel, seq // bm, layer),
        grid=(f // bn, m // bm),
        in_specs=[pl.BlockSpec((bm, d), lambda j, i: (i, 0)), wspec, wspec,
                  pl.BlockSpec((None, CONV_WIDTH, bn), lambda j, i: (layer, 0, j)),
                  pl.BlockSpec((conv_b.shape[0], bn), lambda j, i: (0, j))],
        out_specs=pl.BlockSpec((bm, bn), lambda j, i: (i, j)),
        out_shape=jax.ShapeDtypeStruct((m, f), BF16),
        scratch_shapes=[pltpu.VMEM((bm + halo, bn), F32), pltpu.VMEM((d, 2 * bn), BF16)],
        compiler_params=_params(("parallel", "arbitrary")),
        name="ffn_up_conv_gate",
    )(x16, w_gate, w_up, conv_w, conv_b)


def _ffn_down_kernel(alpha, layer, h_ref, w_hbm, x_ref, g_ref, b_ref, o32_ref, o16_ref, w16_ref, stage_ref, sem):
    @pl.when(pl.program_id(0) == 0)
    def _():
        _stage_weight_bf16(w_hbm, layer, w16_ref, stage_ref, sem)

    y = jnp.dot(h_ref[...], w16_ref[...], preferred_element_type=F32)
    _layer_norm_store(alpha * x_ref[...] + y, layer, g_ref, b_ref, o32_ref, o16_ref)


def _ffn_down_ln(h, w_down, layer, x, g, b, alpha, bm):
    m, d = x.shape
    f = h.shape[1]
    rows = pl.BlockSpec((bm, d), lambda i: (i, 0))
    return pl.pallas_call(
        functools.partial(_ffn_down_kernel, alpha, layer),
        grid=(m // bm,),
        in_specs=[pl.BlockSpec((bm, f), lambda i: (i, 0)), pl.BlockSpec(memory_space=pl.ANY), rows,
                  _all_layers(g), _all_layers(b)],
        out_specs=[rows, rows],
        out_shape=[jax.ShapeDtypeStruct((m, d), F32), jax.ShapeDtypeStruct((m, d), BF16)],
        scratch_shapes=_resident_weight_scratch(f, d),
        compiler_params=_params(("arbitrary",)),
        name="ffn_down_ln",
    )(h, w_down, x, g, b)


def kernel(x, w_in, lb_logits, a_norm_w, c_sinks, w_out, ln1_g, ln1_b, w_gate, w_up, conv_w, conv_b, w_down,
           ln2_g, ln2_b):
    bsz, seq, d = x.shape
    depth = w_in.shape[0]
    alpha = (2 * depth) ** 0.25
    tables = _rope_tables(seq)
    lbs = jnp.cumsum(jax.nn.softmax(lb_logits.astype(F32), axis=0), axis=0)
    lbs = lbs - lbs[0]
    x32 = x.reshape(bsz * seq, d)
    x16 = x32
    plan = _tile_plan(bsz * seq, d, w_out.shape[1], w_gate.shape[2])
    for l in range(depth):
        proj3 = _in_proj(x16, w_in, l, *plan.in_proj).reshape(bsz, seq, -1)
        oa = _hgrn2(proj3, lbs[l], a_norm_w[l]).reshape(bsz * seq, -1)
        ob = _dilated(proj3, tables).reshape(bsz * seq, -1)
        oc = _swa(proj3, c_sinks[l], tables).reshape(bsz * seq, -1)
        x32, x16 = _out_proj_ln(oa, ob, oc, w_out, l, x32, ln1_g, ln1_b, alpha, plan.out_proj_rows)
        h = _ffn_up(x16, w_gate, w_up, conv_w, conv_b, l, seq, *plan.ffn_up)
        x32, x16 = _ffn_down_ln(h, w_down, l, x32, ln2_g, ln2_b, alpha, plan.ffn_down_rows)
    return x32.reshape(bsz, seq, d)
```

```python
import functools
from typing import NamedTuple

import numpy as np
import jax
import jax.numpy as jnp
from jax import lax
from jax.experimental import pallas as pl
from jax.experimental.pallas import tpu as pltpu

HEAD_DIM = 128
A_HEADS = 4
B_HEADS = 6
C_HEADS = 6
C_KV_HEADS = 2
C_REP = C_HEADS // C_KV_HEADS
DILATIONS = (1, 4, 16)
ATT_BLOCK = 128
C_MAX_LAG = 127
ROPE_THETA = 500000.0
ROPE_DIM = HEAD_DIM // 4
ROPE_HALF = ROPE_DIM // 2
CONV_WIDTH = 3
LN_EPS = 1e-5

QA_BLK, FA_BLK, IA_BLK, GA_BLK = 0, 4, 8, 12
QB_BLK, KB_BLK, VB_BLK = 16, 22, 28
QC_BLK, KC_BLK, VC_BLK = 34, 40, 42

HGRN_CHUNK = 128
HGRN_LEVELS = (64, 32, 16, 8, 4, 2, 1)

VMEM_LIMIT_BYTES = 60 * 1024 * 1024

F32 = jnp.float32
BF16 = jnp.bfloat16
NEG_BIG = -1e30
LOG2_E = 1.4426950408889634


def _params(semantics):
    return pltpu.CompilerParams(dimension_semantics=semantics, vmem_limit_bytes=VMEM_LIMIT_BYTES)


SUBLANES = 8
WEIGHT_STAGE_ROWS = 512


class _TilePlan(NamedTuple):
    in_proj: tuple
    out_proj_rows: int
    ffn_up: tuple
    ffn_down_rows: int


def _tile_plan(rows, d_model, d_mix, d_ff):
    plan = _TilePlan(in_proj=(2048, 512), out_proj_rows=512, ffn_up=(2048, 512), ffn_down_rows=256)
    f32, bf16, two = 4, 2, 2
    staged = two * WEIGHT_STAGE_ROWS * d_model * f32
    bm, bn = plan.in_proj
    in_proj = (two * bm * d_model * f32 + bm * d_model * bf16
               + two * d_model * bn * f32 + d_model * bn * bf16 + two * bm * bn * f32)
    bm = plan.out_proj_rows
    out_proj = d_mix * d_model * bf16 + staged + two * bm * (d_mix * bf16 + d_model * (f32 + f32 + bf16))
    bm, bn = plan.ffn_up
    ffn_up = (two * bm * d_model * bf16 + two * two * d_model * bn * f32 + d_model * two * bn * bf16
              + (bm + SUBLANES) * bn * f32 + two * bm * bn * bf16 + bm * two * bn * f32)
    bm = plan.ffn_down_rows
    ffn_down = d_ff * d_model * bf16 + staged + two * bm * (d_ff * bf16 + d_model * (f32 + f32 + bf16))
    for name, need in (("in_proj", in_proj), ("out_proj", out_proj), ("ffn_up", ffn_up), ("ffn_down", ffn_down)):
        assert need <= VMEM_LIMIT_BYTES, (name, need)
    assert rows % max(plan.in_proj[0], plan.ffn_up[0]) == 0 and d_ff % plan.ffn_up[1] == 0
    return plan


def _rope_tables(seq):
    inv = ROPE_THETA ** (-np.arange(0, ROPE_DIM, 2, dtype=np.float64) / ROPE_DIM)
    ang = np.arange(seq, dtype=np.float64)[:, None] * inv[None, :]
    cos, sin = np.cos(ang), np.sin(ang)
    rest = HEAD_DIM - ROPE_DIM
    c = np.concatenate([cos, cos, np.ones((seq, rest))], -1).astype(np.float32)
    s = np.concatenate([sin, sin, np.zeros((seq, rest))], -1).astype(np.float32)
    rot = np.zeros((HEAD_DIM, HEAD_DIM), np.float32)
    lane = np.arange(ROPE_HALF)
    rot[lane + ROPE_HALF, lane] = -1.0
    rot[lane, lane + ROPE_HALF] = 1.0
    return c, s, jnp.asarray(np.concatenate([rot, rot], 0), BF16)


def _rope(x, c, s, rot2):
    hi = x.astype(BF16)
    lo = (x - hi.astype(F32)).astype(BF16)
    swapped = jnp.dot(jnp.concatenate([hi, lo], axis=1), rot2, preferred_element_type=F32)
    return x * c + swapped * s


def _hgrn_constants():
    c = HGRN_CHUNK
    i = np.arange(c)[:, None]
    j = np.arange(c)[None, :]
    lvl = np.full((c, c), -1, np.int32)
    for n, h in enumerate(HGRN_LEVELS):
        lvl[(i // (2 * h) == j // (2 * h)) & (i // h != j // h) & (i > j)] = n
    lvl[np.arange(c), np.arange(c)] = len(HGRN_LEVELS)
    return jnp.asarray((j <= i).astype(np.float32), BF16), jnp.asarray(lvl)


def _midpoint_log_decay(lf, b, h, row):
    c, dk = b.shape
    odd = ((row // h) % 2) == 1
    if h == 1:
        return jnp.where(odd, lf, 0.0)
    if h == 2:
        below = pltpu.roll(lf, c - 1, 0)
        above = pltpu.roll(lf, 1, 0)
        place = row % 4
        return jnp.where(place == 0, below, jnp.where(place == 1, 0.0, jnp.where(place == 2, lf, lf + above)))
    blocks = b.reshape(c // (2 * h), 2 * h, dk)
    mid = jnp.broadcast_to(blocks[:, h - 1:h, :], blocks.shape).reshape(c, dk)
    return jnp.where(odd, b - mid, mid - b)


def _split3(x):
    p1 = x.astype(BF16)
    r1 = x - p1.astype(F32)
    p2 = r1.astype(BF16)
    p3 = (r1 - p2.astype(F32)).astype(BF16)
    return p1, p2, p3


def _dot_nt(a, b, **kw):
    return lax.dot_general(a, b, (((1,), (1,)), ((), ())), preferred_element_type=F32, **kw)


def _dot_tn(a, b, **kw):
    return lax.dot_general(a, b, (((0,), (0,)), ((), ())), preferred_element_type=F32, **kw)


def _hgrn_chunk_heads(inputs, nw, e_all, lvl, row):
    c = HGRN_CHUNK
    heads = range(len(inputs))
    qs = [q * jax.nn.sigmoid(q) for q, _, _, _, _, _ in inputs]
    fg = [lb + (1.0 - lb) * jax.nn.sigmoid(fl) for _, fl, _, _, _, lb in inputs]
    kk = [1.0 - f for f in fg]
    lf = [jnp.log2(f) for f in fg]
    parts = [_split3(x) for x in lf]
    b12 = [jnp.dot(e_all, jnp.concatenate([p1, p2], axis=1), preferred_element_type=F32) for p1, p2, _ in parts]
    b3 = [jnp.dot(e_all, p3, preferred_element_type=F32) for _, _, p3 in parts]
    b = [x[:, :HEAD_DIM] + x[:, HEAD_DIM:] + y for x, y in zip(b12, b3)]
    att = [jnp.where(lvl == len(HGRN_LEVELS), jnp.sum(qs[i] * kk[i], axis=-1, keepdims=True), 0.0) for i in heads]
    for n, h in enumerate(HGRN_LEVELS):
        odd = ((row // h) % 2) == 1
        xh = [(jnp.where(odd, qs[i], kk[i]) * jnp.exp2(_midpoint_log_decay(lf[i], b[i], h, row))).astype(BF16)
              for i in heads]
        pair = [_dot_nt(x, x) for x in xh]
        att = [jnp.where(lvl == n, pair[i], att[i]) for i in heads]
    b_end = [x[c - 1:c] for x in b]
    q_in = [(qs[i] * jnp.exp2(b[i])).astype(BF16) for i in heads]
    k_end = [(kk[i] * jnp.exp2(b_end[i] - b[i])).astype(BF16) for i in heads]
    v16 = [v.astype(BF16) for _, _, v, _, _, _ in inputs]
    o = [jnp.dot(att[i].astype(BF16), v16[i], preferred_element_type=F32)
         + _dot_nt(q_in[i], inputs[i][4].astype(BF16)) for i in heads]
    st_new = [inputs[i][4] * jnp.exp2(b_end[i]) + _dot_tn(v16[i], k_end[i]) for i in heads]
    o = [x * lax.rsqrt(jnp.mean(x * x, axis=-1, keepdims=True) + LN_EPS) * nw for x in o]
    out = [(o[i] * (inputs[i][3] * jax.nn.sigmoid(inputs[i][3]))).astype(BF16) for i in heads]
    return list(zip(out, st_new))


HGRN_BATCH = 2
HGRN_SEQ_TILE = 1024


def _hgrn_kernel(q_ref, f_ref, i_ref, g_ref, lb_ref, nw_ref, e_ref, lvl_ref, o_ref, st_ref):
    c = HGRN_CHUNK
    n_batch, tile_rows = q_ref.shape[0], q_ref.shape[1]

    @pl.when(pl.program_id(1) == 0)
    def _():
        st_ref[...] = jnp.zeros_like(st_ref)

    nw = nw_ref[...]
    row = lax.broadcasted_iota(jnp.int32, (c, 1), 0)

    def chunk(ci, carry):
        rows = pl.ds(pl.multiple_of(ci * c, c), c)
        e_all, lvl = e_ref[...], lvl_ref[...]
        inputs = []
        for n in range(n_batch):
            for h in range(A_HEADS):
                cols = slice(h * HEAD_DIM, (h + 1) * HEAD_DIM)
                inputs.append((q_ref[n, rows, cols], f_ref[n, rows, cols], i_ref[n, rows, cols], g_ref[n, rows, cols],
                               st_ref[n * A_HEADS + h], lb_ref[:, cols]))
        for k, (o, st_new) in enumerate(_hgrn_chunk_heads(inputs, nw, e_all, lvl, row)):
            n, h = divmod(k, A_HEADS)
            o_ref[n, rows, h * HEAD_DIM:(h + 1) * HEAD_DIM] = o
            st_ref[k] = st_new
        return carry

    lax.fori_loop(0, tile_rows // c, chunk, 0)


def _hgrn2(proj3, lb, norm_w):
    bsz, seq, _ = proj3.shape
    e_all, lvl = _hgrn_constants()
    width = A_HEADS * HEAD_DIM
    col = lambda blk: pl.BlockSpec((HGRN_BATCH, HGRN_SEQ_TILE, width), lambda b, s, blk=blk: (b, s, blk // A_HEADS))
    const = lambda shape: pl.BlockSpec(shape, lambda b, s: (0,) * len(shape))
    return pl.pallas_call(
        _hgrn_kernel,
        grid=(bsz // HGRN_BATCH, seq // HGRN_SEQ_TILE),
        in_specs=[col(QA_BLK), col(FA_BLK), col(IA_BLK), col(GA_BLK),
                  const((1, width)), const((1, HEAD_DIM)), const(e_all.shape), const(lvl.shape)],
        out_specs=pl.BlockSpec((HGRN_BATCH, HGRN_SEQ_TILE, width), lambda b, s: (b, s, 0)),
        out_shape=jax.ShapeDtypeStruct((bsz, seq, width), BF16),
        scratch_shapes=[pltpu.VMEM((HGRN_BATCH * A_HEADS, HEAD_DIM, HEAD_DIM), F32)],
        compiler_params=_params(("parallel", "arbitrary")),
        name="hgrn2",
    )(proj3, proj3, proj3, proj3, lb.reshape(1, width), norm_w.reshape(1, HEAD_DIM), e_all, lvl)


def _band_masks(max_lag):
    r = lax.broadcasted_iota(jnp.int32, (ATT_BLOCK, ATT_BLOCK), 0)
    col = lax.broadcasted_iota(jnp.int32, (ATT_BLOCK, ATT_BLOCK), 1)
    return col <= r, (ATT_BLOCK + r - col) <= max_lag


def _tile_softmax_many(blocks):
    scale = HEAD_DIM ** -0.5 * LOG2_E
    s_all = [[jnp.where(mask, _dot_nt(q16, k.astype(BF16)) * scale, NEG_BIG) for k, mask in zip(ks, masks)]
             for q16, ks, _, masks, _ in blocks]
    m_all = []
    for s, (_, _, _, _, m_floor) in zip(s_all, blocks):
        m = jnp.max(functools.reduce(jnp.maximum, s), axis=-1, keepdims=True)
        m_all.append(m if m_floor is None else jnp.maximum(m, m_floor))
    e_all = [[jnp.exp2(si - m) for si in s] for s, m in zip(s_all, m_all)]
    acc_all = [functools.reduce(jnp.add, [jnp.dot(ei.astype(BF16), v.astype(BF16), preferred_element_type=F32)
                                          for ei, v in zip(e, vs)])
               for e, (_, _, vs, _, _) in zip(e_all, blocks)]
    l_all = [jnp.sum(functools.reduce(jnp.add, e), axis=-1, keepdims=True) for e in e_all]
    return list(zip(m_all, l_all, acc_all))


DIL_UNROLL = 16


def _dilated_kernel(proj_hbm, c_ref, s_ref, rot_ref, o_ref, qkv_ref, sem, qr_ref, kr_ref, m_ref, l_ref, acc_ref):
    seq = o_ref.shape[0]
    blk = ATT_BLOCK
    widest = DILATIONS[-1]
    per_residue = seq // widest
    step = pl.program_id(0) * pl.num_programs(1) + pl.program_id(1)
    n_steps = pl.num_programs(0) * pl.num_programs(1)
    slot = step % 2

    def qkv_copies(of_step, into_slot):
        b, h = of_step // pl.num_programs(1), of_step % pl.num_programs(1)
        copies = []
        for which, first_blk in enumerate((QB_BLK, KB_BLK, VB_BLK)):
            col = pl.multiple_of((first_blk + h) * HEAD_DIM, HEAD_DIM)
            for r in range(widest):
                copies.append(pltpu.make_async_copy(
                    proj_hbm.at[b, :, r, pl.ds(col, HEAD_DIM)],
                    qkv_ref.at[into_slot, which, pl.ds(r * per_residue, per_residue), :],
                    sem.at[into_slot]))
        return copies

    @pl.when(step == 0)
    def _():
        for copy in qkv_copies(step, slot):
            copy.start()

    @pl.when(step + 1 < n_steps)
    def _():
        for copy in qkv_copies(step + 1, 1 - slot):
            copy.start()

    for copy in qkv_copies(step, slot):
        copy.wait()

    c, s, rot = c_ref[...], s_ref[...], rot_ref[...]
    qr_ref[...] = _rope(qkv_ref[slot, 0], c, s, rot)
    kr_ref[...] = _rope(qkv_ref[slot, 1], c, s, rot)
    vr_ref = qkv_ref.at[slot, 2]

    def chunk_rows(dil):
        return blk * dil // widest

    def block_masks(dil):
        def position(i):
            return (widest // dil) * (i % chunk_rows(dil)) + i // chunk_rows(dil)
        row = position(lax.broadcasted_iota(jnp.int32, (blk, blk), 0))
        col = position(lax.broadcasted_iota(jnp.int32, (blk, blk), 1))
        return col <= row, col >= row

    def chunks_of(dil, r, n):
        size = chunk_rows(dil)
        return [pl.ds(pl.multiple_of((r + dil * a) * per_residue + size * n, size), size)
                for a in range(widest // dil)]

    def gather(ref, chunks):
        return jnp.concatenate([ref[rows, :] for rows in chunks], axis=0)

    def scatter(ref, chunks, value):
        size = value.shape[0] // len(chunks)
        for a, rows in enumerate(chunks):
            ref[rows, :] = value[a * size:(a + 1) * size]

    def run_group(specs, masks_of_pattern, first):
        cur_mask, prev_mask = masks_of_pattern
        loaded = []
        for rows, prev, prev_ok in specs:
            ks, vs, masks = [gather(kr_ref, rows)], [gather(vr_ref, rows)], [cur_mask]
            if prev is not None:
                ks.append(gather(kr_ref, prev))
                vs.append(gather(vr_ref, prev))
                masks.append(jnp.logical_and(prev_mask, prev_ok))
            state = None if first else (gather(m_ref, rows), gather(l_ref, rows), gather(acc_ref, rows))
            loaded.append((gather(qr_ref, rows).astype(BF16), ks, vs, masks, state))
        pieces = _tile_softmax_many([(q16, ks, vs, masks, None if state is None else state[0])
                                     for q16, ks, vs, masks, state in loaded])
        results = []
        for (m, l, acc), (_, _, _, _, state) in zip(pieces, loaded):
            if state is None:
                shape = (blk, HEAD_DIM)
                results.append((jnp.broadcast_to(m, shape), jnp.broadcast_to(l, shape), acc))
            else:
                m_old, l_old, acc_old = state
                alpha = jnp.exp2(m_old - m)
                results.append((m, alpha * l_old + l, alpha * acc_old + acc))
        for (rows, _, _), (m, l, acc) in zip(specs, results):
            scatter(m_ref, rows, m)
            scatter(l_ref, rows, l)
            scatter(acc_ref, rows, acc)

    n_query_blocks = seq // blk
    for dil in DILATIONS:
        n_blocks = n_query_blocks // dil
        masks_of_pattern = block_masks(dil)

        def group(g, carry, dil=dil, n_blocks=n_blocks, masks_of_pattern=masks_of_pattern):
            specs = []
            for u in range(DIL_UNROLL):
                e = g * DIL_UNROLL + u
                r, n = e % dil, e // dil
                if n_blocks == 1:
                    specs.append((chunks_of(dil, r, n), None, None))
                else:
                    specs.append((chunks_of(dil, r, n), chunks_of(dil, r, jnp.maximum(n - 1, 0)), n > 0))
            run_group(specs, masks_of_pattern, first=(dil == DILATIONS[0]))
            return carry

        lax.fori_loop(0, n_query_blocks // DIL_UNROLL, group, 0)

    for r in range(widest):
        rows = slice(r * per_residue, (r + 1) * per_residue)
        qr_ref[pl.ds(r, per_residue, stride=widest), :] = acc_ref[rows, :] / l_ref[rows, :]
    o_ref[...] = qr_ref[...].astype(o_ref.dtype)


def _residue_major(table, widest):
    seq, width = table.shape
    return table.reshape(seq // widest, widest, width).transpose(1, 0, 2).reshape(seq, width)


def _dilated(proj3, tables):
    bsz, seq, width = proj3.shape
    widest = DILATIONS[-1]
    tab = pl.BlockSpec((seq, HEAD_DIM), lambda b, h: (0, 0))
    return pl.pallas_call(
        _dilated_kernel,
        grid=(bsz, B_HEADS),
        in_specs=[pl.BlockSpec(memory_space=pl.ANY), tab, tab, pl.BlockSpec(tables[2].shape, lambda b, h: (0, 0))],
        out_specs=pl.BlockSpec((None, seq, HEAD_DIM), lambda b, h: (b, 0, h)),
        out_shape=jax.ShapeDtypeStruct((bsz, seq, B_HEADS * HEAD_DIM), BF16),
        scratch_shapes=[pltpu.VMEM((2, 3, seq, HEAD_DIM), F32), pltpu.SemaphoreType.DMA((2,))]
                       + [pltpu.VMEM((seq, HEAD_DIM), F32)] * 5,
        compiler_params=_params(("arbitrary", "arbitrary")),
        name="dilated_attention",
    )(proj3.reshape(bsz, seq // widest, widest, width), _residue_major(tables[0], widest),
      _residue_major(tables[1], widest), tables[2])


SWA_UNROLL = 16


def _swa_kernel(sink_ref, q0_ref, q1_ref, q2_ref, k_ref, v_ref, c_ref, s_ref, rot_ref, o_ref, qr_ref, kr_ref):
    seq = k_ref.shape[0]
    blk = ATT_BLOCK
    grp = pl.program_id(1)
    c, s, rot = c_ref[...], s_ref[...], rot_ref[...]
    for j, q_ref in enumerate((q0_ref, q1_ref, q2_ref)):
        qr_ref[j] = _rope(q_ref[...], c, s, rot).astype(BF16)
    kr_ref[...] = _rope(k_ref[...], c, s, rot).astype(BF16)
    cur_mask, prev_mask = _band_masks(C_MAX_LAG)
    sinks = [jnp.full((blk, 1), sink_ref[grp * C_REP + j] * LOG2_E, F32) for j in range(C_REP)]

    def group(g, carry):
        blocks, dests = [], []
        for u in range(SWA_UNROLL):
            n = g * SWA_UNROLL + u
            rows = pl.ds(pl.multiple_of(n * blk, blk), blk)
            prev = pl.ds(pl.multiple_of(jnp.maximum(n - 1, 0) * blk, blk), blk)
            ks, vs = [kr_ref[rows, :], kr_ref[prev, :]], [v_ref[rows, :], v_ref[prev, :]]
            masks = [cur_mask, jnp.logical_and(prev_mask, n > 0)]
            for j in range(C_REP):
                blocks.append((qr_ref[j, rows, :], ks, vs, masks, sinks[j]))
                dests.append((rows, j))
        for (rows, j), (m, l, acc) in zip(dests, _tile_softmax_many(blocks)):
            out = acc / (l + jnp.exp2(sinks[j] - m))
            o_ref[rows, j * HEAD_DIM:(j + 1) * HEAD_DIM] = out.astype(o_ref.dtype)
        return carry

    lax.fori_loop(0, seq // (blk * SWA_UNROLL), group, 0)


def _swa(proj3, sinks, tables):
    bsz, seq, _ = proj3.shape
    qcol = lambda j: pl.BlockSpec((None, seq, HEAD_DIM), lambda b, g, j=j: (b, 0, QC_BLK + g * C_REP + j))
    col = lambda blk: pl.BlockSpec((None, seq, HEAD_DIM), lambda b, g, blk=blk: (b, 0, blk + g))
    tab = pl.BlockSpec((seq, HEAD_DIM), lambda b, g: (0, 0))
    return pl.pallas_call(
        _swa_kernel,
        grid=(bsz, C_KV_HEADS),
        in_specs=[pl.BlockSpec(memory_space=pltpu.SMEM), qcol(0), qcol(1), qcol(2), col(KC_BLK), col(VC_BLK),
                  tab, tab, pl.BlockSpec(tables[2].shape, lambda b, g: (0, 0))],
        out_specs=pl.BlockSpec((None, seq, C_REP * HEAD_DIM), lambda b, g: (b, 0, g)),
        out_shape=jax.ShapeDtypeStruct((bsz, seq, C_HEADS * HEAD_DIM), BF16),
        scratch_shapes=[pltpu.VMEM((C_REP, seq, HEAD_DIM), BF16), pltpu.VMEM((seq, HEAD_DIM), BF16)],
        compiler_params=_params(("parallel", "parallel")),
        name="swa_sink_attention",
    )(sinks, proj3, proj3, proj3, proj3, proj3, *tables)


def _layer_norm_store(z, layer, g_ref, b_ref, o32_ref, o16_ref):
    mu = jnp.mean(z, axis=-1, keepdims=True)
    zc = z - mu
    var = jnp.mean(zc * zc, axis=-1, keepdims=True)
    y = zc * lax.rsqrt(var + LN_EPS) * g_ref[layer:layer + 1, :] + b_ref[layer:layer + 1, :]
    o32_ref[...] = y
    o16_ref[...] = y.astype(BF16)


def _stage_weight_bf16(w_hbm, layer, w16_ref, stage_ref, sem):
    chunk = stage_ref.shape[1]
    n_chunks = w16_ref.shape[0] // chunk

    def copy(c):
        slot = c % 2
        return pltpu.make_async_copy(w_hbm.at[layer, pl.ds(c * chunk, chunk), :], stage_ref.at[slot], sem.at[slot])

    copy(0).start()
    for c in range(n_chunks):
        if c + 1 < n_chunks:
            copy(c + 1).start()
        copy(c).wait()
        w16_ref[c * chunk:(c + 1) * chunk, :] = stage_ref[c % 2].astype(BF16)


def _resident_weight_scratch(k, n):
    return [pltpu.VMEM((k, n), BF16), pltpu.VMEM((2, WEIGHT_STAGE_ROWS, n), F32), pltpu.SemaphoreType.DMA((2,))]


def _out_proj_kernel(alpha, layer, oa_ref, ob_ref, oc_ref, w_hbm, x_ref, g_ref, b_ref, o32_ref, o16_ref,
                     w16_ref, stage_ref, sem):
    @pl.when(pl.program_id(0) == 0)
    def _():
        _stage_weight_bf16(w_hbm, layer, w16_ref, stage_ref, sem)

    ka, kb = oa_ref.shape[1], ob_ref.shape[1]
    y = (jnp.dot(oa_ref[...], w16_ref[0:ka, :], preferred_element_type=F32)
         + jnp.dot(ob_ref[...], w16_ref[ka:ka + kb, :], preferred_element_type=F32)
         + jnp.dot(oc_ref[...], w16_ref[ka + kb:, :], preferred_element_type=F32))
    _layer_norm_store(alpha * x_ref[...] + y, layer, g_ref, b_ref, o32_ref, o16_ref)


def _all_layers(vec):
    return pl.BlockSpec(vec.shape, lambda i: (0, 0))


def _out_proj_ln(oa, ob, oc, w_out, layer, x, g, b, alpha, bm):
    m, d = x.shape
    rows = lambda a: pl.BlockSpec((bm, a.shape[1]), lambda i: (i, 0))
    return pl.pallas_call(
        functools.partial(_out_proj_kernel, alpha, layer),
        grid=(m // bm,),
        in_specs=[rows(oa), rows(ob), rows(oc), pl.BlockSpec(memory_space=pl.ANY), rows(x),
                  _all_layers(g), _all_layers(b)],
        out_specs=[rows(x), rows(x)],
        out_shape=[jax.ShapeDtypeStruct((m, d), F32), jax.ShapeDtypeStruct((m, d), BF16)],
        scratch_shapes=_resident_weight_scratch(w_out.shape[1], d),
        compiler_params=_params(("arbitrary",)),
        name="out_proj_ln",
    )(oa, ob, oc, w_out, x, g, b)


def _in_proj_kernel(x_ref, w_ref, o_ref):
    o_ref[...] = jnp.dot(x_ref[...].astype(BF16), w_ref[...].astype(BF16),
                         preferred_element_type=F32).astype(o_ref.dtype)


F32_INPUT_CHUNKS = 8


def _in_proj_f32_kernel(x_hbm, w_ref, o_ref, x_ref, sem):
    i, j = pl.program_id(0), pl.program_id(1)
    bm = x_ref.shape[1]
    rows = bm // F32_INPUT_CHUNKS
    slot = i % 2

    def piece(tile, c, into):
        return pltpu.make_async_copy(x_hbm.at[pl.ds(tile * bm + c * rows, rows), :],
                                     x_ref.at[into, pl.ds(c * rows, rows), :], sem.at[into])

    @pl.when(jnp.logical_and(i == 0, j == 0))
    def _():
        for c in range(F32_INPUT_CHUNKS):
            piece(0, c, 0).start()

    @pl.when(j == 0)
    def _():
        for c in range(F32_INPUT_CHUNKS):
            piece(i, c, slot).wait()

    @pl.when(jnp.logical_and(j < F32_INPUT_CHUNKS, i + 1 < pl.num_programs(0)))
    def _():
        piece(i + 1, j, 1 - slot).start()

    o_ref[...] = jnp.dot(x_ref[slot].astype(BF16), w_ref[...].astype(BF16),
                         preferred_element_type=F32).astype(o_ref.dtype)


def _in_proj(x16, w_in, layer, bm, bn):
    m, k = x16.shape
    n = w_in.shape[2]
    if x16.dtype == F32:
        assert n // bn >= F32_INPUT_CHUNKS
        return pl.pallas_call(
            _in_proj_f32_kernel,
            grid=(m // bm, n // bn),
            in_specs=[pl.BlockSpec(memory_space=pl.ANY),
                      pl.BlockSpec((None, k, bn), lambda i, j: (layer, 0, j))],
            out_specs=pl.BlockSpec((bm, bn), lambda i, j: (i, j)),
            out_shape=jax.ShapeDtypeStruct((m, n), F32),
            scratch_shapes=[pltpu.VMEM((2, bm, k), F32), pltpu.SemaphoreType.DMA((2,))],
            compiler_params=_params(("arbitrary", "arbitrary")),
            name="in_proj",
        )(x16, w_in)
    return pl.pallas_call(
        _in_proj_kernel,
        grid=(m // bm, n // bn),
        in_specs=[pl.BlockSpec((bm, k), lambda i, j: (i, 0)),
                  pl.BlockSpec((None, k, bn), lambda i, j: (layer, 0, j))],
        out_specs=pl.BlockSpec((bm, bn), lambda i, j: (i, j)),
        out_shape=jax.ShapeDtypeStruct((m, n), F32),
        compiler_params=_params(("parallel", "parallel")),
        name="in_proj",
    )(x16, w_in)


FFN_UP_SUBTILES = 2


def _ffn_up_kernel(tiles_per_seq, layer, x_ref, wg_ref, wu_ref, cw_ref, cb_ref, h_ref, g_ref, w16_ref):
    bm = x_ref.shape[0]
    bn = h_ref.shape[1]
    halo = g_ref.shape[0] - bm

    @pl.when(pl.program_id(1) == 0)
    def _():
        w16_ref[:, 0:bn] = wg_ref[...].astype(BF16)
        w16_ref[:, bn:] = wu_ref[...].astype(BF16)

    @pl.when(pl.program_id(1) % tiles_per_seq == 0)
    def _():
        g_ref[0:halo, :] = jnp.zeros((halo, g_ref.shape[1]), F32)

    sub = bm // FFN_UP_SUBTILES
    for t in range(FFN_UP_SUBTILES):
        first = halo + t * sub
        gu = jnp.dot(x_ref[t * sub:(t + 1) * sub, :], w16_ref[...], preferred_element_type=F32)
        g_ref[first:first + sub, :] = gu[:, 0:bn]
        u = gu[:, bn:]
        gc = cb_ref[layer:layer + 1, :]
        for j in range(CONV_WIDTH):
            lag = CONV_WIDTH - 1 - j
            gc = gc + cw_ref[j:j + 1, :] * g_ref[first - lag:first - lag + sub, :]
        h_ref[t * sub:(t + 1) * sub, :] = (gc * jax.nn.sigmoid(gc) * u).astype(h_ref.dtype)
    g_ref[0:halo, :] = g_ref[bm:bm + halo, :]


def _ffn_up(x16, w_gate, w_up, conv_w, conv_b, layer, seq, bm, bn):
    m, d = x16.shape
    f = w_gate.shape[2]
    halo = SUBLANES
    wspec = pl.BlockSpec((None, d, bn), lambda j, i: (layer, 0, j))
    return pl.pallas_call(
        functools.partial(_ffn_up_kernel, seq // bm, layer),
        grid=(f // bn, m // bm),
        in_specs=[pl.BlockSpec((bm, d), lambda j, i: (i, 0)), wspec, wspec,
                  pl.BlockSpec((None, CONV_WIDTH, bn), lambda j, i: (layer, 0, j)),
                  pl.BlockSpec((conv_b.shape[0], bn), lambda j, i: (0, j))],
        out_specs=pl.BlockSpec((bm, bn), lambda j, i: (i, j)),
        out_shape=jax.ShapeDtypeStruct((m, f), BF16),
        scratch_shapes=[pltpu.VMEM((bm + halo, bn), F32), pltpu.VMEM((d, 2 * bn), BF16)],
        compiler_params=_params(("parallel", "arbitrary")),
        name="ffn_up_conv_gate",
    )(x16, w_gate, w_up, conv_w, conv_b)


def _ffn_down_kernel(alpha, layer, h_ref, w_hbm, x_ref, g_ref, b_ref, o32_ref, o16_ref, w16_ref, stage_ref, sem):
    @pl.when(pl.program_id(0) == 0)
    def _():
        _stage_weight_bf16(w_hbm, layer, w16_ref, stage_ref, sem)

    y = jnp.dot(h_ref[...], w16_ref[...], preferred_element_type=F32)
    _layer_norm_store(alpha * x_ref[...] + y, layer, g_ref, b_ref, o32_ref, o16_ref)


def _ffn_down_ln(h, w_down, layer, x, g, b, alpha, bm):
    m, d = x.shape
    f = h.shape[1]
    rows = pl.BlockSpec((bm, d), lambda i: (i, 0))
    return pl.pallas_call(
        functools.partial(_ffn_down_kernel, alpha, layer),
        grid=(m // bm,),
        in_specs=[pl.BlockSpec((bm, f), lambda i: (i, 0)), pl.BlockSpec(memory_space=pl.ANY), rows,
                  _all_layers(g), _all_layers(b)],
        out_specs=[rows, rows],
        out_shape=[jax.ShapeDtypeStruct((m, d), F32), jax.ShapeDtypeStruct((m, d), BF16)],
        scratch_shapes=_resident_weight_scratch(f, d),
        compiler_params=_params(("arbitrary",)),
        name="ffn_down_ln",
    )(h, w_down, x, g, b)


def kernel(x, w_in, lb_logits, a_norm_w, c_sinks, w_out, ln1_g, ln1_b, w_gate, w_up, conv_w, conv_b, w_down,
           ln2_g, ln2_b):
    bsz, seq, d = x.shape
    depth = w_in.shape[0]
    alpha = (2 * depth) ** 0.25
    tables = _rope_tables(seq)
    lbs = jnp.cumsum(jax.nn.softmax(lb_logits.astype(F32), axis=0), axis=0)
    lbs = lbs - lbs[0]
    x32 = x.reshape(bsz * seq, d)
    x16 = x32
    plan = _tile_plan(bsz * seq, d, w_out.shape[1], w_gate.shape[2])
    for l in range(depth):
        proj3 = _in_proj(x16, w_in, l, *plan.in_proj).reshape(bsz, seq, -1)
        oa = _hgrn2(proj3, lbs[l], a_norm_w[l]).reshape(bsz * seq, -1)
        ob = _dilated(proj3, tables).reshape(bsz * seq, -1)
        oc = _swa(proj3, c_sinks[l], tables).reshape(bsz * seq, -1)
        x32, x16 = _out_proj_ln(oa, ob, oc, w_out, l, x32, ln1_g, ln1_b, alpha, plan.out_proj_rows)
        h = _ffn_up(x16, w_gate, w_up, conv_w, conv_b, l, seq, *plan.ffn_up)
        x32, x16 = _ffn_down_ln(h, w_down, l, x32, ln2_g, ln2_b, alpha, plan.ffn_down_rows)
    return x32.reshape(bsz, seq, d)
```

```python
import functools
from typing import NamedTuple

import numpy as np
import jax
import jax.numpy as jnp
from jax import lax
from jax.experimental import pallas as pl
from jax.experimental.pallas import tpu as pltpu

HEAD_DIM = 128
A_HEADS = 4
B_HEADS = 6
C_HEADS = 6
C_KV_HEADS = 2
C_REP = C_HEADS // C_KV_HEADS
DILATIONS = (1, 4, 16)
ATT_BLOCK = 128
C_MAX_LAG = 127
ROPE_THETA = 500000.0
ROPE_DIM = HEAD_DIM // 4
ROPE_HALF = ROPE_DIM // 2
CONV_WIDTH = 3
LN_EPS = 1e-5

QA_BLK, FA_BLK, IA_BLK, GA_BLK = 0, 4, 8, 12
QB_BLK, KB_BLK, VB_BLK = 16, 22, 28
QC_BLK, KC_BLK, VC_BLK = 34, 40, 42

HGRN_CHUNK = 128
HGRN_LEVELS = (64, 32, 16, 8, 4, 2, 1)

VMEM_LIMIT_BYTES = 60 * 1024 * 1024

F32 = jnp.float32
BF16 = jnp.bfloat16
NEG_BIG = -1e30
LOG2_E = 1.4426950408889634


def _params(semantics):
    return pltpu.CompilerParams(dimension_semantics=semantics, vmem_limit_bytes=VMEM_LIMIT_BYTES)


SUBLANES = 8
WEIGHT_STAGE_ROWS = 512


class _TilePlan(NamedTuple):
    in_proj: tuple
    out_proj_rows: int
    ffn_up: tuple
    ffn_down_rows: int


def _tile_plan(rows, d_model, d_mix, d_ff):
    plan = _TilePlan(in_proj=(2048, 512), out_proj_rows=512, ffn_up=(2048, 512), ffn_down_rows=256)
    f32, bf16, two = 4, 2, 2
    staged = two * WEIGHT_STAGE_ROWS * d_model * f32
    bm, bn = plan.in_proj
    in_proj = (two * bm * d_model * f32 + bm * d_model * bf16
               + two * d_model * bn * f32 + d_model * bn * bf16 + two * bm * bn * f32)
    bm = plan.out_proj_rows
    out_proj = d_mix * d_model * bf16 + staged + two * bm * (d_mix * bf16 + d_model * (f32 + f32 + bf16))
    bm, bn = plan.ffn_up
    ffn_up = (two * bm * d_model * bf16 + two * two * d_model * bn * f32 + d_model * two * bn * bf16
              + (bm + SUBLANES) * bn * f32 + two * bm * bn * bf16 + bm * two * bn * f32)
    bm = plan.ffn_down_rows
    ffn_down = d_ff * d_model * bf16 + staged + two * bm * (d_ff * bf16 + d_model * (f32 + f32 + bf16))
    for name, need in (("in_proj", in_proj), ("out_proj", out_proj), ("ffn_up", ffn_up), ("ffn_down", ffn_down)):
        assert need <= VMEM_LIMIT_BYTES, (name, need)
    assert rows % max(plan.in_proj[0], plan.ffn_up[0]) == 0 and d_ff % plan.ffn_up[1] == 0
    return plan


def _rope_tables(seq):
    inv = ROPE_THETA ** (-np.arange(0, ROPE_DIM, 2, dtype=np.float64) / ROPE_DIM)
    ang = np.arange(seq, dtype=np.float64)[:, None] * inv[None, :]
    cos, sin = np.cos(ang), np.sin(ang)
    rest = HEAD_DIM - ROPE_DIM
    c = np.concatenate([cos, cos, np.ones((seq, rest))], -1).astype(np.float32)
    s = np.concatenate([sin, sin, np.zeros((seq, rest))], -1).astype(np.float32)
    rot = np.zeros((HEAD_DIM, HEAD_DIM), np.float32)
    lane = np.arange(ROPE_HALF)
    rot[lane + ROPE_HALF, lane] = -1.0
    rot[lane, lane + ROPE_HALF] = 1.0
    return c, s, jnp.asarray(np.concatenate([rot, rot], 0), BF16)


def _rope(x, c, s, rot2):
    hi = x.astype(BF16)
    lo = (x - hi.astype(F32)).astype(BF16)
    swapped = jnp.dot(jnp.concatenate([hi, lo], axis=1), rot2, preferred_element_type=F32)
    return x * c + swapped * s


def _hgrn_constants():
    c = HGRN_CHUNK
    i = np.arange(c)[:, None]
    j = np.arange(c)[None, :]
    lvl = np.full((c, c), -1, np.int32)
    for n, h in enumerate(HGRN_LEVELS):
        lvl[(i // (2 * h) == j // (2 * h)) & (i // h != j // h) & (i > j)] = n
    lvl[np.arange(c), np.arange(c)] = len(HGRN_LEVELS)
    return jnp.asarray((j <= i).astype(np.float32), BF16), jnp.asarray(lvl)


def _midpoint_log_decay(lf, b, h, row):
    c, dk = b.shape
    odd = ((row // h) % 2) == 1
    if h == 1:
        return jnp.where(odd, lf, 0.0)
    if h == 2:
        below = pltpu.roll(lf, c - 1, 0)
        above = pltpu.roll(lf, 1, 0)
        place = row % 4
        return jnp.where(place == 0, below, jnp.where(place == 1, 0.0, jnp.where(place == 2, lf, lf + above)))
    blocks = b.reshape(c // (2 * h), 2 * h, dk)
    mid = jnp.broadcast_to(blocks[:, h - 1:h, :], blocks.shape).reshape(c, dk)
    return jnp.where(odd, b - mid, mid - b)


def _split3(x):
    p1 = x.astype(BF16)
    r1 = x - p1.astype(F32)
    p2 = r1.astype(BF16)
    p3 = (r1 - p2.astype(F32)).astype(BF16)
    return p1, p2, p3


def _dot_nt(a, b, **kw):
    return lax.dot_general(a, b, (((1,), (1,)), ((), ())), preferred_element_type=F32, **kw)


def _dot_tn(a, b, **kw):
    return lax.dot_general(a, b, (((0,), (0,)), ((), ())), preferred_element_type=F32, **kw)


def _hgrn_chunk_heads(inputs, nw, e_all, lvl, row):
    c = HGRN_CHUNK
    heads = range(len(inputs))
    qs = [q * jax.nn.sigmoid(q) for q, _, _, _, _, _ in inputs]
    fg = [lb + (1.0 - lb) * jax.nn.sigmoid(fl) for _, fl, _, _, _, lb in inputs]
    kk = [1.0 - f for f in fg]
    lf = [jnp.log2(f) for f in fg]
    parts = [_split3(x) for x in lf]
    b12 = [jnp.dot(e_all, jnp.concatenate([p1, p2], axis=1), preferred_element_type=F32) for p1, p2, _ in parts]
    b3 = [jnp.dot(e_all, p3, preferred_element_type=F32) for _, _, p3 in parts]
    b = [x[:, :HEAD_DIM] + x[:, HEAD_DIM:] + y for x, y in zip(b12, b3)]
    att = [jnp.where(lvl == len(HGRN_LEVELS), jnp.sum(qs[i] * kk[i], axis=-1, keepdims=True), 0.0) for i in heads]
    for n, h in enumerate(HGRN_LEVELS):
        odd = ((row // h) % 2) == 1
        xh = [(jnp.where(odd, qs[i], kk[i]) * jnp.exp2(_midpoint_log_decay(lf[i], b[i], h, row))).astype(BF16)
              for i in heads]
        pair = [_dot_nt(x, x) for x in xh]
        att = [jnp.where(lvl == n, pair[i], att[i]) for i in heads]
    b_end = [x[c - 1:c] for x in b]
    q_in = [(qs[i] * jnp.exp2(b[i])).astype(BF16) for i in heads]
    k_end = [(kk[i] * jnp.exp2(b_end[i] - b[i])).astype(BF16) for i in heads]
    v16 = [v.astype(BF16) for _, _, v, _, _, _ in inputs]
    o = [jnp.dot(att[i].astype(BF16), v16[i], preferred_element_type=F32)
         + _dot_nt(q_in[i], inputs[i][4].astype(BF16)) for i in heads]
    st_new = [inputs[i][4] * jnp.exp2(b_end[i]) + _dot_tn(v16[i], k_end[i]) for i in heads]
    o = [x * lax.rsqrt(jnp.mean(x * x, axis=-1, keepdims=True) + LN_EPS) * nw for x in o]
    out = [(o[i] * (inputs[i][3] * jax.nn.sigmoid(inputs[i][3]))).astype(BF16) for i in heads]
    return list(zip(out, st_new))


HGRN_BATCH = 2
HGRN_SEQ_TILE = 1024


def _hgrn_kernel(q_ref, f_ref, i_ref, g_ref, lb_ref, nw_ref, e_ref, lvl_ref, o_ref, st_ref):
    c = HGRN_CHUNK
    n_batch, tile_rows = q_ref.shape[0], q_ref.shape[1]

    @pl.when(pl.program_id(1) == 0)
    def _():
        st_ref[...] = jnp.zeros_like(st_ref)

    nw = nw_ref[...]
    row = lax.broadcasted_iota(jnp.int32, (c, 1), 0)

    def chunk(ci, carry):
        rows = pl.ds(pl.multiple_of(ci * c, c), c)
        e_all, lvl = e_ref[...], lvl_ref[...]
        inputs = []
        for n in range(n_batch):
            for h in range(A_HEADS):
                cols = slice(h * HEAD_DIM, (h + 1) * HEAD_DIM)
                inputs.append((q_ref[n, rows, cols], f_ref[n, rows, cols], i_ref[n, rows, cols], g_ref[n, rows, cols],
                               st_ref[n * A_HEADS + h], lb_ref[:, cols]))
        for k, (o, st_new) in enumerate(_hgrn_chunk_heads(inputs, nw, e_all, lvl, row)):
            n, h = divmod(k, A_HEADS)
            o_ref[n, rows, h * HEAD_DIM:(h + 1) * HEAD_DIM] = o
            st_ref[k] = st_new
        return carry

    lax.fori_loop(0, tile_rows // c, chunk, 0)


def _hgrn2(proj3, lb, norm_w):
    bsz, seq, _ = proj3.shape
    e_all, lvl = _hgrn_constants()
    width = A_HEADS * HEAD_DIM
    col = lambda blk: pl.BlockSpec((HGRN_BATCH, HGRN_SEQ_TILE, width), lambda b, s, blk=blk: (b, s, blk // A_HEADS))
    const = lambda shape: pl.BlockSpec(shape, lambda b, s: (0,) * len(shape))
    return pl.pallas_call(
        _hgrn_kernel,
        grid=(bsz // HGRN_BATCH, seq // HGRN_SEQ_TILE),
        in_specs=[col(QA_BLK), col(FA_BLK), col(IA_BLK), col(GA_BLK),
                  const((1, width)), const((1, HEAD_DIM)), const(e_all.shape), const(lvl.shape)],
        out_specs=pl.BlockSpec((HGRN_BATCH, HGRN_SEQ_TILE, width), lambda b, s: (b, s, 0)),
        out_shape=jax.ShapeDtypeStruct((bsz, seq, width), BF16),
        scratch_shapes=[pltpu.VMEM((HGRN_BATCH * A_HEADS, HEAD_DIM, HEAD_DIM), F32)],
        compiler_params=_params(("parallel", "arbitrary")),
        name="hgrn2",
    )(proj3, proj3, proj3, proj3, lb.reshape(1, width), norm_w.reshape(1, HEAD_DIM), e_all, lvl)


def _band_masks(max_lag):
    r = lax.broadcasted_iota(jnp.int32, (ATT_BLOCK, ATT_BLOCK), 0)
    col = lax.broadcasted_iota(jnp.int32, (ATT_BLOCK, ATT_BLOCK), 1)
    return col <= r, (ATT_BLOCK + r - col) <= max_lag


def _tile_softmax_many(blocks):
    scale = HEAD_DIM ** -0.5 * LOG2_E
    s_all = [[jnp.where(mask, _dot_nt(q16, k.astype(BF16)) * scale, NEG_BIG) for k, mask in zip(ks, masks)]
             for q16, ks, _, masks, _ in blocks]
    m_all = []
    for s, (_, _, _, _, m_floor) in zip(s_all, blocks):
        m = jnp.max(functools.reduce(jnp.maximum, s), axis=-1, keepdims=True)
        m_all.append(m if m_floor is None else jnp.maximum(m, m_floor))
    e_all = [[jnp.exp2(si - m) for si in s] for s, m in zip(s_all, m_all)]
    acc_all = [functools.reduce(jnp.add, [jnp.dot(ei.astype(BF16), v.astype(BF16), preferred_element_type=F32)
                                          for ei, v in zip(e, vs)])
               for e, (_, _, vs, _, _) in zip(e_all, blocks)]
    l_all = [jnp.sum(functools.reduce(jnp.add, e), axis=-1, keepdims=True) for e in e_all]
    return list(zip(m_all, l_all, acc_all))


DIL_UNROLL = 16


def _dilated_kernel(proj_hbm, c_ref, s_ref, rot_ref, o_ref, qkv_ref, sem, qr_ref, kr_ref, m_ref, l_ref, acc_ref):
    seq = o_ref.shape[0]
    blk = ATT_BLOCK
    widest = DILATIONS[-1]
    per_residue = seq // widest
    step = pl.program_id(0) * pl.num_programs(1) + pl.program_id(1)
    n_steps = pl.num_programs(0) * pl.num_programs(1)
    slot = step % 2

    def qkv_copies(of_step, into_slot):
        b, h = of_step // pl.num_programs(1), of_step % pl.num_programs(1)
        copies = []
        for which, first_blk in enumerate((QB_BLK, KB_BLK, VB_BLK)):
            col = pl.multiple_of((first_blk + h) * HEAD_DIM, HEAD_DIM)
            for r in range(widest):
                copies.append(pltpu.make_async_copy(
                    proj_hbm.at[b, :, r, pl.ds(col, HEAD_DIM)],
                    qkv_ref.at[into_slot, which, pl.ds(r * per_residue, per_residue), :],
                    sem.at[into_slot]))
        return copies

    @pl.when(step == 0)
    def _():
        for copy in qkv_copies(step, slot):
            copy.start()

    @pl.when(step + 1 < n_steps)
    def _():
        for copy in qkv_copies(step + 1, 1 - slot):
            copy.start()

    for copy in qkv_copies(step, slot):
        copy.wait()

    c, s, rot = c_ref[...], s_ref[...], rot_ref[...]
    qr_ref[...] = _rope(qkv_ref[slot, 0], c, s, rot)
    kr_ref[...] = _rope(qkv_ref[slot, 1], c, s, rot)
    vr_ref = qkv_ref.at[slot, 2]

    def chunk_rows(dil):
        return blk * dil // widest

    def block_masks(dil):
        def position(i):
            return (widest // dil) * (i % chunk_rows(dil)) + i // chunk_rows(dil)
        row = position(lax.broadcasted_iota(jnp.int32, (blk, blk), 0))
        col = position(lax.broadcasted_iota(jnp.int32, (blk, blk), 1))
        return col <= row, col >= row

    def chunks_of(dil, r, n):
        size = chunk_rows(dil)
        return [pl.ds(pl.multiple_of((r + dil * a) * per_residue + size * n, size), size)
                for a in range(widest // dil)]

    def gather(ref, chunks):
        return jnp.concatenate([ref[rows, :] for rows in chunks], axis=0)

    def scatter(ref, chunks, value):
        size = value.shape[0] // len(chunks)
        for a, rows in enumerate(chunks):
            ref[rows, :] = value[a * size:(a + 1) * size]

    def run_group(specs, masks_of_pattern, first):
        cur_mask, prev_mask = masks_of_pattern
        loaded = []
        for rows, prev, prev_ok in specs:
            ks, vs, masks = [gather(kr_ref, rows)], [gather(vr_ref, rows)], [cur_mask]
            if prev is not None:
                ks.append(gather(kr_ref, prev))
                vs.append(gather(vr_ref, prev))
                masks.append(jnp.logical_and(prev_mask, prev_ok))
            state = None if first else (gather(m_ref, rows), gather(l_ref, rows), gather(acc_ref, rows))
            loaded.append((gather(qr_ref, rows).astype(BF16), ks, vs, masks, state))
        pieces = _tile_softmax_many([(q16, ks, vs, masks, None if state is None else state[0])
                                     for q16, ks, vs, masks, state in loaded])
        results = []
        for (m, l, acc), (_, _, _, _, state) in zip(pieces, loaded):
            if state is None:
                shape = (blk, HEAD_DIM)
                results.append((jnp.broadcast_to(m, shape), jnp.broadcast_to(l, shape), acc))
            else:
                m_old, l_old, acc_old = state
                alpha = jnp.exp2(m_old - m)
                results.append((m, alpha * l_old + l, alpha * acc_old + acc))
        for (rows, _, _), (m, l, acc) in zip(specs, results):
            scatter(m_ref, rows, m)
            scatter(l_ref, rows, l)
            scatter(acc_ref, rows, acc)

    n_query_blocks = seq // blk
    for dil in DILATIONS:
        n_blocks = n_query_blocks // dil
        masks_of_pattern = block_masks(dil)

        def group(g, carry, dil=dil, n_blocks=n_blocks, masks_of_pattern=masks_of_pattern):
            specs = []
            for u in range(DIL_UNROLL):
                e = g * DIL_UNROLL + u
                r, n = e % dil, e // dil
                if n_blocks == 1:
                    specs.append((chunks_of(dil, r, n), None, None))
                else:
                    specs.append((chunks_of(dil, r, n), chunks_of(dil, r, jnp.maximum(n - 1, 0)), n > 0))
            run_group(specs, masks_of_pattern, first=(dil == DILATIONS[0]))
            return carry

        lax.fori_loop(0, n_query_blocks // DIL_UNROLL, group, 0)

    for r in range(widest):
        rows = slice(r * per_residue, (r + 1) * per_residue)
        qr_ref[pl.ds(r, per_residue, stride=widest), :] = acc_ref[rows, :] / l_ref[rows, :]
    o_ref[...] = qr_ref[...].astype(o_ref.dtype)


def _residue_major(table, widest):
    seq, width = table.shape
    return table.reshape(seq // widest, widest, width).transpose(1, 0, 2).reshape(seq, width)


def _dilated(proj3, tables):
    bsz, seq, width = proj3.shape
    widest = DILATIONS[-1]
    tab = pl.BlockSpec((seq, HEAD_DIM), lambda b, h: (0, 0))
    return pl.pallas_call(
        _dilated_kernel,
        grid=(bsz, B_HEADS),
        in_specs=[pl.BlockSpec(memory_space=pl.ANY), tab, tab, pl.BlockSpec(tables[2].shape, lambda b, h: (0, 0))],
        out_specs=pl.BlockSpec((None, seq, HEAD_DIM), lambda b, h: (b, 0, h)),
        out_shape=jax.ShapeDtypeStruct((bsz, seq, B_HEADS * HEAD_DIM), BF16),
        scratch_shapes=[pltpu.VMEM((2, 3, seq, HEAD_DIM), F32), pltpu.SemaphoreType.DMA((2,))]
                       + [pltpu.VMEM((seq, HEAD_DIM), F32)] * 5,
        compiler_params=_params(("arbitrary", "arbitrary")),
        name="dilated_attention",
    )(proj3.reshape(bsz, seq // widest, widest, width), _residue_major(tables[0], widest),
      _residue_major(tables[1], widest), tables[2])


SWA_UNROLL = 16


def _swa_kernel(sink_ref, q0_ref, q1_ref, q2_ref, k_ref, v_ref, c_ref, s_ref, rot_ref, o_ref, qr_ref, kr_ref):
    seq = k_ref.shape[0]
    blk = ATT_BLOCK
    grp = pl.program_id(1)
    c, s, rot = c_ref[...], s_ref[...], rot_ref[...]
    for j, q_ref in enumerate((q0_ref, q1_ref, q2_ref)):
        qr_ref[j] = _rope(q_ref[...], c, s, rot).astype(BF16)
    kr_ref[...] = _rope(k_ref[...], c, s, rot).astype(BF16)
    cur_mask, prev_mask = _band_masks(C_MAX_LAG)
    sinks = [jnp.full((blk, 1), sink_ref[grp * C_REP + j] * LOG2_E, F32) for j in range(C_REP)]

    def group(g, carry):
        blocks, dests = [], []
        for u in range(SWA_UNROLL):
            n = g * SWA_UNROLL + u
            rows = pl.ds(pl.multiple_of(n * blk, blk), blk)
            prev = pl.ds(pl.multiple_of(jnp.maximum(n - 1, 0) * blk, blk), blk)
            ks, vs = [kr_ref[rows, :], kr_ref[prev, :]], [v_ref[rows, :], v_ref[prev, :]]
            masks = [cur_mask, jnp.logical_and(prev_mask, n > 0)]
            for j in range(C_REP):
                blocks.append((qr_ref[j, rows, :], ks, vs, masks, sinks[j]))
                dests.append((rows, j))
        for (rows, j), (m, l, acc) in zip(dests, _tile_softmax_many(blocks)):
            out = acc / (l + jnp.exp2(sinks[j] - m))
            o_ref[rows, j * HEAD_DIM:(j + 1) * HEAD_DIM] = out.astype(o_ref.dtype)
        return carry

    lax.fori_loop(0, seq // (blk * SWA_UNROLL), group, 0)


def _swa(proj3, sinks, tables):
    bsz, seq, _ = proj3.shape
    qcol = lambda j: pl.BlockSpec((None, seq, HEAD_DIM), lambda b, g, j=j: (b, 0, QC_BLK + g * C_REP + j))
    col = lambda blk: pl.BlockSpec((None, seq, HEAD_DIM), lambda b, g, blk=blk: (b, 0, blk + g))
    tab = pl.BlockSpec((seq, HEAD_DIM), lambda b, g: (0, 0))
    return pl.pallas_call(
        _swa_kernel,
        grid=(bsz, C_KV_HEADS),
        in_specs=[pl.BlockSpec(memory_space=pltpu.SMEM), qcol(0), qcol(1), qcol(2), col(KC_BLK), col(VC_BLK),
                  tab, tab, pl.BlockSpec(tables[2].shape, lambda b, g: (0, 0))],
        out_specs=pl.BlockSpec((None, seq, C_REP * HEAD_DIM), lambda b, g: (b, 0, g)),
        out_shape=jax.ShapeDtypeStruct((bsz, seq, C_HEADS * HEAD_DIM), BF16),
        scratch_shapes=[pltpu.VMEM((C_REP, seq, HEAD_DIM), BF16), pltpu.VMEM((seq, HEAD_DIM), BF16)],
        compiler_params=_params(("parallel", "parallel")),
        name="swa_sink_attention",
    )(sinks, proj3, proj3, proj3, proj3, proj3, *tables)


def _layer_norm_store(z, layer, g_ref, b_ref, o32_ref, o16_ref):
    mu = jnp.mean(z, axis=-1, keepdims=True)
    zc = z - mu
    var = jnp.mean(zc * zc, axis=-1, keepdims=True)
    y = zc * lax.rsqrt(var + LN_EPS) * g_ref[layer:layer + 1, :] + b_ref[layer:layer + 1, :]
    o32_ref[...] = y
    o16_ref[...] = y.astype(BF16)


def _stage_weight_bf16(w_hbm, layer, w16_ref, stage_ref, sem):
    chunk = stage_ref.shape[1]
    n_chunks = w16_ref.shape[0] // chunk

    def copy(c):
        slot = c % 2
        return pltpu.make_async_copy(w_hbm.at[layer, pl.ds(c * chunk, chunk), :], stage_ref.at[slot], sem.at[slot])

    copy(0).start()
    for c in range(n_chunks):
        if c + 1 < n_chunks:
            copy(c + 1).start()
        copy(c).wait()
        w16_ref[c * chunk:(c + 1) * chunk, :] = stage_ref[c % 2].astype(BF16)


def _resident_weight_scratch(k, n):
    return [pltpu.VMEM((k, n), BF16), pltpu.VMEM((2, WEIGHT_STAGE_ROWS, n), F32), pltpu.SemaphoreType.DMA((2,))]


def _out_proj_kernel(alpha, layer, oa_ref, ob_ref, oc_ref, w_hbm, x_ref, g_ref, b_ref, o32_ref, o16_ref,
                     w16_ref, stage_ref, sem):
    @pl.when(pl.program_id(0) == 0)
    def _():
        _stage_weight_bf16(w_hbm, layer, w16_ref, stage_ref, sem)

    ka, kb = oa_ref.shape[1], ob_ref.shape[1]
    y = (jnp.dot(oa_ref[...], w16_ref[0:ka, :], preferred_element_type=F32)
         + jnp.dot(ob_ref[...], w16_ref[ka:ka + kb, :], preferred_element_type=F32)
         + jnp.dot(oc_ref[...], w16_ref[ka + kb:, :], preferred_element_type=F32))
    _layer_norm_store(alpha * x_ref[...] + y, layer, g_ref, b_ref, o32_ref, o16_ref)


def _all_layers(vec):
    return pl.BlockSpec(vec.shape, lambda i: (0, 0))


def _out_proj_ln(oa, ob, oc, w_out, layer, x, g, b, alpha, bm):
    m, d = x.shape
    rows = lambda a: pl.BlockSpec((bm, a.shape[1]), lambda i: (i, 0))
    return pl.pallas_call(
        functools.partial(_out_proj_kernel, alpha, layer),
        grid=(m // bm,),
        in_specs=[rows(oa), rows(ob), rows(oc), pl.BlockSpec(memory_space=pl.ANY), rows(x),
                  _all_layers(g), _all_layers(b)],
        out_specs=[rows(x), rows(x)],
        out_shape=[jax.ShapeDtypeStruct((m, d), F32), jax.ShapeDtypeStruct((m, d), BF16)],
        scratch_shapes=_resident_weight_scratch(w_out.shape[1], d),
        compiler_params=_params(("arbitrary",)),
        name="out_proj_ln",
    )(oa, ob, oc, w_out, x, g, b)


def _in_proj_kernel(x_ref, w_ref, o_ref):
    o_ref[...] = jnp.dot(x_ref[...].astype(BF16), w_ref[...].astype(BF16),
                         preferred_element_type=F32).astype(o_ref.dtype)


F32_INPUT_CHUNKS = 8


def _in_proj_f32_kernel(x_hbm, w_ref, o_ref, x_ref, sem):
    i, j = pl.program_id(0), pl.program_id(1)
    bm = x_ref.shape[1]
    rows = bm // F32_INPUT_CHUNKS
    slot = i % 2

    def piece(tile, c, into):
        return pltpu.make_async_copy(x_hbm.at[pl.ds(tile * bm + c * rows, rows), :],
                                     x_ref.at[into, pl.ds(c * rows, rows), :], sem.at[into])

    @pl.when(jnp.logical_and(i == 0, j == 0))
    def _():
        for c in range(F32_INPUT_CHUNKS):
            piece(0, c, 0).start()

    @pl.when(j == 0)
    def _():
        for c in range(F32_INPUT_CHUNKS):
            piece(i, c, slot).wait()

    @pl.when(jnp.logical_and(j < F32_INPUT_CHUNKS, i + 1 < pl.num_programs(0)))
    def _():
        piece(i + 1, j, 1 - slot).start()

    o_ref[...] = jnp.dot(x_ref[slot].astype(BF16), w_ref[...].astype(BF16),
                         preferred_element_type=F32).astype(o_ref.dtype)


def _in_proj(x16, w_in, layer, bm, bn):
    m, k = x16.shape
    n = w_in.shape[2]
    assert n // bn >= F32_INPUT_CHUNKS
    return pl.pallas_call(
        _in_proj_f32_kernel,
        grid=(m // bm, n // bn),
        in_specs=[pl.BlockSpec(memory_space=pl.ANY),
                  pl.BlockSpec((None, k, bn), lambda i, j: (layer, 0, j))],
        out_specs=pl.BlockSpec((bm, bn), lambda i, j: (i, j)),
        out_shape=jax.ShapeDtypeStruct((m, n), F32),
        scratch_shapes=[pltpu.VMEM((2, bm, k), x16.dtype), pltpu.SemaphoreType.DMA((2,))],
        compiler_params=_params(("arbitrary", "arbitrary")),
        name="in_proj",
    )(x16, w_in)


FFN_UP_SUBTILES = 2


def _ffn_up_kernel(tiles_per_seq, layer, x_ref, wg_ref, wu_ref, cw_ref, cb_ref, h_ref, g_ref, w16_ref):
    bm = x_ref.shape[0]
    bn = h_ref.shape[1]
    halo = g_ref.shape[0] - bm

    @pl.when(pl.program_id(1) == 0)
    def _():
        w16_ref[:, 0:bn] = wg_ref[...].astype(BF16)
        w16_ref[:, bn:] = wu_ref[...].astype(BF16)

    @pl.when(pl.program_id(1) % tiles_per_seq == 0)
    def _():
        g_ref[0:halo, :] = jnp.zeros((halo, g_ref.shape[1]), F32)

    sub = bm // FFN_UP_SUBTILES
    for t in range(FFN_UP_SUBTILES):
        first = halo + t * sub
        gu = jnp.dot(x_ref[t * sub:(t + 1) * sub, :], w16_ref[...], preferred_element_type=F32)
        g_ref[first:first + sub, :] = gu[:, 0:bn]
        u = gu[:, bn:]
        gc = cb_ref[layer:layer + 1, :]
        for j in range(CONV_WIDTH):
            lag = CONV_WIDTH - 1 - j
            gc = gc + cw_ref[j:j + 1, :] * g_ref[first - lag:first - lag + sub, :]
        h_ref[t * sub:(t + 1) * sub, :] = (gc * jax.nn.sigmoid(gc) * u).astype(h_ref.dtype)
    g_ref[0:halo, :] = g_ref[bm:bm + halo, :]


def _ffn_up(x16, w_gate, w_up, conv_w, conv_b, layer, seq, bm, bn):
    m, d = x16.shape
    f = w_gate.shape[2]
    halo = SUBLANES
    wspec = pl.BlockSpec((None, d, bn), lambda j, i: (layer, 0, j))
    return pl.pallas_call(
        functools.partial(_ffn_up_kernel, seq // bm, layer),
        grid=(f // bn, m // bm),
        in_specs=[pl.BlockSpec((bm, d), lambda j, i: (i, 0)), wspec, wspec,
                  pl.BlockSpec((None, CONV_WIDTH, bn), lambda j, i: (layer, 0, j)),
                  pl.BlockSpec((conv_b.shape[0], bn), lambda j, i: (0, j))],
        out_specs=pl.BlockSpec((bm, bn), lambda j, i: (i, j)),
        out_shape=jax.ShapeDtypeStruct((m, f), BF16),
        scratch_shapes=[pltpu.VMEM((bm + halo, bn), F32), pltpu.VMEM((d, 2 * bn), BF16)],
        compiler_params=_params(("parallel", "arbitrary")),
        name="ffn_up_conv_gate",
    )(x16, w_gate, w_up, conv_w, conv_b)


def _ffn_down_kernel(alpha, layer, h_ref, w_hbm, x_ref, g_ref, b_ref, o32_ref, o16_ref, w16_ref, stage_ref, sem):
    @pl.when(pl.program_id(0) == 0)
    def _():
        _stage_weight_bf16(w_hbm, layer, w16_ref, stage_ref, sem)

    y = jnp.dot(h_ref[...], w16_ref[...], preferred_element_type=F32)
    _layer_norm_store(alpha * x_ref[...] + y, layer, g_ref, b_ref, o32_ref, o16_ref)


def _ffn_down_ln(h, w_down, layer, x, g, b, alpha, bm):
    m, d = x.shape
    f = h.shape[1]
    rows = pl.BlockSpec((bm, d), lambda i: (i, 0))
    return pl.pallas_call(
        functools.partial(_ffn_down_kernel, alpha, layer),
        grid=(m // bm,),
        in_specs=[pl.BlockSpec((bm, f), lambda i: (i, 0)), pl.BlockSpec(memory_space=pl.ANY), rows,
                  _all_layers(g), _all_layers(b)],
        out_specs=[rows, rows],
        out_shape=[jax.ShapeDtypeStruct((m, d), F32), jax.ShapeDtypeStruct((m, d), BF16)],
        scratch_shapes=_resident_weight_scratch(f, d),
        compiler_params=_params(("arbitrary",)),
        name="ffn_down_ln",
    )(h, w_down, x, g, b)


def kernel(x, w_in, lb_logits, a_norm_w, c_sinks, w_out, ln1_g, ln1_b, w_gate, w_up, conv_w, conv_b, w_down,
           ln2_g, ln2_b):
    bsz, seq, d = x.shape
    depth = w_in.shape[0]
    alpha = (2 * depth) ** 0.25
    tables = _rope_tables(seq)
    lbs = jnp.cumsum(jax.nn.softmax(lb_logits.astype(F32), axis=0), axis=0)
    lbs = lbs - lbs[0]
    x32 = x.reshape(bsz * seq, d)
    x16 = x32
    plan = _tile_plan(bsz * seq, d, w_out.shape[1], w_gate.shape[2])
    for l in range(depth):
        proj3 = _in_proj(x16, w_in, l, *plan.in_proj).reshape(bsz, seq, -1)
        oa = _hgrn2(proj3, lbs[l], a_norm_w[l]).reshape(bsz * seq, -1)
        ob = _dilated(proj3, tables).reshape(bsz * seq, -1)
        oc = _swa(proj3, c_sinks[l], tables).reshape(bsz * seq, -1)
        x32, x16 = _out_proj_ln(oa, ob, oc, w_out, l, x32, ln1_g, ln1_b, alpha, plan.out_proj_rows)
        h = _ffn_up(x16, w_gate, w_up, conv_w, conv_b, l, seq, *plan.ffn_up)
        x32, x16 = _ffn_down_ln(h, w_down, l, x32, ln2_g, ln2_b, alpha, plan.ffn_down_rows)
    return x32.reshape(bsz, seq, d)
```

```python
import functools
from typing import NamedTuple

import numpy as np
import jax
import jax.numpy as jnp
from jax import lax
from jax.experimental import pallas as pl
from jax.experimental.pallas import tpu as pltpu

HEAD_DIM = 128
A_HEADS = 4
B_HEADS = 6
C_HEADS = 6
C_KV_HEADS = 2
C_REP = C_HEADS // C_KV_HEADS
DILATIONS = (1, 4, 16)
ATT_BLOCK = 128
C_MAX_LAG = 127
ROPE_THETA = 500000.0
ROPE_DIM = HEAD_DIM // 4
ROPE_HALF = ROPE_DIM // 2
CONV_WIDTH = 3
LN_EPS = 1e-5

QA_BLK, FA_BLK, IA_BLK, GA_BLK = 0, 4, 8, 12
QB_BLK, KB_BLK, VB_BLK = 16, 22, 28
QC_BLK, KC_BLK, VC_BLK = 34, 40, 42

HGRN_CHUNK = 128
HGRN_LEVELS = (64, 32, 16, 8, 4, 2, 1)

VMEM_LIMIT_BYTES = 60 * 1024 * 1024

F32 = jnp.float32
BF16 = jnp.bfloat16
NEG_BIG = -1e30
LOG2_E = 1.4426950408889634


def _params(semantics):
    return pltpu.CompilerParams(dimension_semantics=semantics, vmem_limit_bytes=VMEM_LIMIT_BYTES)


SUBLANES = 8
WEIGHT_STAGE_ROWS = 512


class _TilePlan(NamedTuple):
    in_proj: tuple
    out_proj_rows: int
    ffn_up: tuple
    ffn_down_rows: int


def _tile_plan(rows, d_model, d_mix, d_ff):
    plan = _TilePlan(in_proj=(2048, 512), out_proj_rows=512, ffn_up=(2048, 512), ffn_down_rows=256)
    f32, bf16, two = 4, 2, 2
    staged = two * WEIGHT_STAGE_ROWS * d_model * f32
    bm, bn = plan.in_proj
    in_proj = (two * bm * d_model * f32 + bm * d_model * bf16
               + two * d_model * bn * f32 + d_model * bn * bf16 + two * bm * bn * f32)
    bm = plan.out_proj_rows
    out_proj = d_mix * d_model * bf16 + staged + two * bm * (d_mix * bf16 + d_model * (f32 + f32 + bf16))
    bm, bn = plan.ffn_up
    ffn_up = (two * bm * d_model * bf16 + two * two * d_model * bn * f32 + d_model * two * bn * bf16
              + (bm + SUBLANES) * bn * f32 + two * bm * bn * bf16 + bm * two * bn * f32)
    bm = plan.ffn_down_rows
    ffn_down = d_ff * d_model * bf16 + staged + two * bm * (d_ff * bf16 + d_model * (f32 + f32 + bf16))
    for name, need in (("in_proj", in_proj), ("out_proj", out_proj), ("ffn_up", ffn_up), ("ffn_down", ffn_down)):
        assert need <= VMEM_LIMIT_BYTES, (name, need)
    assert rows % max(plan.in_proj[0], plan.ffn_up[0]) == 0 and d_ff % plan.ffn_up[1] == 0
    return plan


def _rope_tables(seq):
    inv = ROPE_THETA ** (-np.arange(0, ROPE_DIM, 2, dtype=np.float64) / ROPE_DIM)
    ang = np.arange(seq, dtype=np.float64)[:, None] * inv[None, :]
    cos, sin = np.cos(ang), np.sin(ang)
    rest = HEAD_DIM - ROPE_DIM
    c = np.concatenate([cos, cos, np.ones((seq, rest))], -1).astype(np.float32)
    s = np.concatenate([sin, sin, np.zeros((seq, rest))], -1).astype(np.float32)
    rot = np.zeros((HEAD_DIM, HEAD_DIM), np.float32)
    lane = np.arange(ROPE_HALF)
    rot[lane + ROPE_HALF, lane] = -1.0
    rot[lane, lane + ROPE_HALF] = 1.0
    return c, s, jnp.asarray(np.concatenate([rot, rot], 0), BF16)


def _rope(x, c, s, rot2):
    hi = x.astype(BF16)
    lo = (x - hi.astype(F32)).astype(BF16)
    swapped = jnp.dot(jnp.concatenate([hi, lo], axis=1), rot2, preferred_element_type=F32)
    return x * c + swapped * s


def _hgrn_constants():
    c = HGRN_CHUNK
    i = np.arange(c)[:, None]
    j = np.arange(c)[None, :]
    lvl = np.full((c, c), -1, np.int32)
    for n, h in enumerate(HGRN_LEVELS):
        lvl[(i // (2 * h) == j // (2 * h)) & (i // h != j // h) & (i > j)] = n
    lvl[np.arange(c), np.arange(c)] = len(HGRN_LEVELS)
    return jnp.asarray((j <= i).astype(np.float32), BF16), jnp.asarray(lvl)


def _midpoint_log_decay(lf, b, h, row):
    c, dk = b.shape
    odd = ((row // h) % 2) == 1
    if h == 1:
        return jnp.where(odd, lf, 0.0)
    if h == 2:
        below = pltpu.roll(lf, c - 1, 0)
        above = pltpu.roll(lf, 1, 0)
        place = row % 4
        return jnp.where(place == 0, below, jnp.where(place == 1, 0.0, jnp.where(place == 2, lf, lf + above)))
    blocks = b.reshape(c // (2 * h), 2 * h, dk)
    mid = jnp.broadcast_to(blocks[:, h - 1:h, :], blocks.shape).reshape(c, dk)
    return jnp.where(odd, b - mid, mid - b)


def _split3(x):
    p1 = x.astype(BF16)
    r1 = x - p1.astype(F32)
    p2 = r1.astype(BF16)
    p3 = (r1 - p2.astype(F32)).astype(BF16)
    return p1, p2, p3


def _dot_nt(a, b, **kw):
    return lax.dot_general(a, b, (((1,), (1,)), ((), ())), preferred_element_type=F32, **kw)


def _dot_tn(a, b, **kw):
    return lax.dot_general(a, b, (((0,), (0,)), ((), ())), preferred_element_type=F32, **kw)


def _hgrn_chunk_heads(inputs, nw, e_all, lvl, row):
    c = HGRN_CHUNK
    heads = range(len(inputs))
    qs = [q * jax.nn.sigmoid(q) for q, _, _, _, _, _ in inputs]
    fg = [lb + (1.0 - lb) * jax.nn.sigmoid(fl) for _, fl, _, _, _, lb in inputs]
    kk = [1.0 - f for f in fg]
    lf = [jnp.log2(f) for f in fg]
    parts = [_split3(x) for x in lf]
    b12 = [jnp.dot(e_all, jnp.concatenate([p1, p2], axis=1), preferred_element_type=F32) for p1, p2, _ in parts]
    b3 = [jnp.dot(e_all, p3, preferred_element_type=F32) for _, _, p3 in parts]
    b = [x[:, :HEAD_DIM] + x[:, HEAD_DIM:] + y for x, y in zip(b12, b3)]
    att = [jnp.where(lvl == len(HGRN_LEVELS), jnp.sum(qs[i] * kk[i], axis=-1, keepdims=True), 0.0) for i in heads]
    for n, h in enumerate(HGRN_LEVELS):
        odd = ((row // h) % 2) == 1
        xh = [(jnp.where(odd, qs[i], kk[i]) * jnp.exp2(_midpoint_log_decay(lf[i], b[i], h, row))).astype(BF16)
              for i in heads]
        pair = [_dot_nt(x, x) for x in xh]
        att = [jnp.where(lvl == n, pair[i], att[i]) for i in heads]
    b_end = [x[c - 1:c] for x in b]
    q_in = [(qs[i] * jnp.exp2(b[i])).astype(BF16) for i in heads]
    k_end = [(kk[i] * jnp.exp2(b_end[i] - b[i])).astype(BF16) for i in heads]
    v16 = [v.astype(BF16) for _, _, v, _, _, _ in inputs]
    o = [jnp.dot(att[i].astype(BF16), v16[i], preferred_element_type=F32)
         + _dot_nt(q_in[i], inputs[i][4].astype(BF16)) for i in heads]
    st_new = [inputs[i][4] * jnp.exp2(b_end[i]) + _dot_tn(v16[i], k_end[i]) for i in heads]
    o = [x * lax.rsqrt(jnp.mean(x * x, axis=-1, keepdims=True) + LN_EPS) * nw for x in o]
    out = [(o[i] * (inputs[i][3] * jax.nn.sigmoid(inputs[i][3]))).astype(BF16) for i in heads]
    return list(zip(out, st_new))


HGRN_BATCH = 2
HGRN_SEQ_TILE = 1024


def _hgrn_kernel(q_ref, f_ref, i_ref, g_ref, lb_ref, nw_ref, e_ref, lvl_ref, o_ref, st_ref):
    c = HGRN_CHUNK
    n_batch, tile_rows = q_ref.shape[0], q_ref.shape[1]

    @pl.when(pl.program_id(1) == 0)
    def _():
        st_ref[...] = jnp.zeros_like(st_ref)

    nw = nw_ref[...]
    row = lax.broadcasted_iota(jnp.int32, (c, 1), 0)

    def chunk(ci, carry):
        rows = pl.ds(pl.multiple_of(ci * c, c), c)
        e_all, lvl = e_ref[...], lvl_ref[...]
        inputs = []
        for n in range(n_batch):
            for h in range(A_HEADS):
                cols = slice(h * HEAD_DIM, (h + 1) * HEAD_DIM)
                inputs.append((q_ref[n, rows, cols], f_ref[n, rows, cols], i_ref[n, rows, cols], g_ref[n, rows, cols],
                               st_ref[n * A_HEADS + h], lb_ref[:, cols]))
        for k, (o, st_new) in enumerate(_hgrn_chunk_heads(inputs, nw, e_all, lvl, row)):
            n, h = divmod(k, A_HEADS)
            o_ref[n, rows, h * HEAD_DIM:(h + 1) * HEAD_DIM] = o
            st_ref[k] = st_new
        return carry

    lax.fori_loop(0, tile_rows // c, chunk, 0)


def _hgrn2(proj3, lb, norm_w):
    bsz, seq, _ = proj3.shape
    e_all, lvl = _hgrn_constants()
    width = A_HEADS * HEAD_DIM
    col = lambda blk: pl.BlockSpec((HGRN_BATCH, HGRN_SEQ_TILE, width), lambda b, s, blk=blk: (b, s, blk // A_HEADS))
    const = lambda shape: pl.BlockSpec(shape, lambda b, s: (0,) * len(shape))
    return pl.pallas_call(
        _hgrn_kernel,
        grid=(bsz // HGRN_BATCH, seq // HGRN_SEQ_TILE),
        in_specs=[col(QA_BLK), col(FA_BLK), col(IA_BLK), col(GA_BLK),
                  const((1, width)), const((1, HEAD_DIM)), const(e_all.shape), const(lvl.shape)],
        out_specs=pl.BlockSpec((HGRN_BATCH, HGRN_SEQ_TILE, width), lambda b, s: (b, s, 0)),
        out_shape=jax.ShapeDtypeStruct((bsz, seq, width), BF16),
        scratch_shapes=[pltpu.VMEM((HGRN_BATCH * A_HEADS, HEAD_DIM, HEAD_DIM), F32)],
        compiler_params=_params(("parallel", "arbitrary")),
        name="hgrn2",
    )(proj3, proj3, proj3, proj3, lb.reshape(1, width), norm_w.reshape(1, HEAD_DIM), e_all, lvl)


def _band_masks(max_lag):
    r = lax.broadcasted_iota(jnp.int32, (ATT_BLOCK, ATT_BLOCK), 0)
    col = lax.broadcasted_iota(jnp.int32, (ATT_BLOCK, ATT_BLOCK), 1)
    return col <= r, (ATT_BLOCK + r - col) <= max_lag


def _tile_softmax_many(blocks):
    scale = HEAD_DIM ** -0.5 * LOG2_E
    s_all = [[jnp.where(mask, _dot_nt(q16, k.astype(BF16)) * scale, NEG_BIG) for k, mask in zip(ks, masks)]
             for q16, ks, _, masks, _ in blocks]
    m_all = []
    for s, (_, _, _, _, m_floor) in zip(s_all, blocks):
        m = jnp.max(functools.reduce(jnp.maximum, s), axis=-1, keepdims=True)
        m_all.append(m if m_floor is None else jnp.maximum(m, m_floor))
    e_all = [[jnp.exp2(si - m) for si in s] for s, m in zip(s_all, m_all)]
    acc_all = [functools.reduce(jnp.add, [jnp.dot(ei.astype(BF16), v.astype(BF16), preferred_element_type=F32)
                                          for ei, v in zip(e, vs)])
               for e, (_, _, vs, _, _) in zip(e_all, blocks)]
    l_all = [jnp.sum(functools.reduce(jnp.add, e), axis=-1, keepdims=True) for e in e_all]
    return list(zip(m_all, l_all, acc_all))


DIL_UNROLL = 16


def _dilated_kernel(proj_hbm, c_ref, s_ref, rot_ref, o_ref, qkv_ref, sem, qr_ref, kr_ref, m_ref, l_ref, acc_ref):
    seq = o_ref.shape[0]
    blk = ATT_BLOCK
    widest = DILATIONS[-1]
    per_residue = seq // widest
    step = pl.program_id(0) * pl.num_programs(1) + pl.program_id(1)
    n_steps = pl.num_programs(0) * pl.num_programs(1)
    slot = step % 2

    def qkv_copies(of_step, into_slot):
        b, h = of_step // pl.num_programs(1), of_step % pl.num_programs(1)
        copies = []
        for which, first_blk in enumerate((QB_BLK, KB_BLK, VB_BLK)):
            col = pl.multiple_of((first_blk + h) * HEAD_DIM, HEAD_DIM)
            for r in range(widest):
                copies.append(pltpu.make_async_copy(
                    proj_hbm.at[b, :, r, pl.ds(col, HEAD_DIM)],
                    qkv_ref.at[into_slot, which, pl.ds(r * per_residue, per_residue), :],
                    sem.at[into_slot]))
        return copies

    @pl.when(step == 0)
    def _():
        for copy in qkv_copies(step, slot):
            copy.start()

    @pl.when(step + 1 < n_steps)
    def _():
        for copy in qkv_copies(step + 1, 1 - slot):
            copy.start()

    for copy in qkv_copies(step, slot):
        copy.wait()

    c, s, rot = c_ref[...], s_ref[...], rot_ref[...]
    qr_ref[...] = _rope(qkv_ref[slot, 0], c, s, rot)
    kr_ref[...] = _rope(qkv_ref[slot, 1], c, s, rot)
    vr_ref = qkv_ref.at[slot, 2]

    def chunk_rows(dil):
        return blk * dil // widest

    def block_masks(dil):
        def position(i):
            return (widest // dil) * (i % chunk_rows(dil)) + i // chunk_rows(dil)
        row = position(lax.broadcasted_iota(jnp.int32, (blk, blk), 0))
        col = position(lax.broadcasted_iota(jnp.int32, (blk, blk), 1))
        return col <= row, col >= row

    def chunks_of(dil, r, n):
        size = chunk_rows(dil)
        return [pl.ds(pl.multiple_of((r + dil * a) * per_residue + size * n, size), size)
                for a in range(widest // dil)]

    def gather(ref, chunks):
        return jnp.concatenate([ref[rows, :] for rows in chunks], axis=0)

    def scatter(ref, chunks, value):
        size = value.shape[0] // len(chunks)
        for a, rows in enumerate(chunks):
            ref[rows, :] = value[a * size:(a + 1) * size]

    def run_group(specs, masks_of_pattern, first):
        cur_mask, prev_mask = masks_of_pattern
        loaded = []
        for rows, prev, prev_ok in specs:
            ks, vs, masks = [gather(kr_ref, rows)], [gather(vr_ref, rows)], [cur_mask]
            if prev is not None:
                ks.append(gather(kr_ref, prev))
                vs.append(gather(vr_ref, prev))
                masks.append(jnp.logical_and(prev_mask, prev_ok))
            state = None if first else (gather(m_ref, rows), gather(l_ref, rows), gather(acc_ref, rows))
            loaded.append((gather(qr_ref, rows).astype(BF16), ks, vs, masks, state))
        pieces = _tile_softmax_many([(q16, ks, vs, masks, None if state is None else state[0])
                                     for q16, ks, vs, masks, state in loaded])
        results = []
        for (m, l, acc), (_, _, _, _, state) in zip(pieces, loaded):
            if state is None:
                shape = (blk, HEAD_DIM)
                results.append((jnp.broadcast_to(m, shape), jnp.broadcast_to(l, shape), acc))
            else:
                m_old, l_old, acc_old = state
                alpha = jnp.exp2(m_old - m)
                results.append((m, alpha * l_old + l, alpha * acc_old + acc))
        for (rows, _, _), (m, l, acc) in zip(specs, results):
            scatter(m_ref, rows, m)
            scatter(l_ref, rows, l)
            scatter(acc_ref, rows, acc)

    n_query_blocks = seq // blk
    for dil in DILATIONS:
        n_blocks = n_query_blocks // dil
        masks_of_pattern = block_masks(dil)

        def group(g, carry, dil=dil, n_blocks=n_blocks, masks_of_pattern=masks_of_pattern):
            specs = []
            for u in range(DIL_UNROLL):
                e = g * DIL_UNROLL + u
                r, n = e % dil, e // dil
                if n_blocks == 1:
                    specs.append((chunks_of(dil, r, n), None, None))
                else:
                    specs.append((chunks_of(dil, r, n), chunks_of(dil, r, jnp.maximum(n - 1, 0)), n > 0))
            run_group(specs, masks_of_pattern, first=(dil == DILATIONS[0]))
            return carry

        lax.fori_loop(0, n_query_blocks // DIL_UNROLL, group, 0)

    for r in range(widest):
        rows = slice(r * per_residue, (r + 1) * per_residue)
        qr_ref[pl.ds(r, per_residue, stride=widest), :] = acc_ref[rows, :] / l_ref[rows, :]
    o_ref[...] = qr_ref[...].astype(o_ref.dtype)


def _residue_major(table, widest):
    seq, width = table.shape
    return table.reshape(seq // widest, widest, width).transpose(1, 0, 2).reshape(seq, width)


def _dilated(proj3, tables):
    bsz, seq, width = proj3.shape
    widest = DILATIONS[-1]
    tab = pl.BlockSpec((seq, HEAD_DIM), lambda b, h: (0, 0))
    return pl.pallas_call(
        _dilated_kernel,
        grid=(bsz, B_HEADS),
        in_specs=[pl.BlockSpec(memory_space=pl.ANY), tab, tab, pl.BlockSpec(tables[2].shape, lambda b, h: (0, 0))],
        out_specs=pl.BlockSpec((None, seq, HEAD_DIM), lambda b, h: (b, 0, h)),
        out_shape=jax.ShapeDtypeStruct((bsz, seq, B_HEADS * HEAD_DIM), BF16),
        scratch_shapes=[pltpu.VMEM((2, 3, seq, HEAD_DIM), F32), pltpu.SemaphoreType.DMA((2,))]
                       + [pltpu.VMEM((seq, HEAD_DIM), F32)] * 5,
        compiler_params=_params(("arbitrary", "arbitrary")),
        name="dilated_attention",
    )(proj3.reshape(bsz, seq // widest, widest, width), _residue_major(tables[0], widest),
      _residue_major(tables[1], widest), tables[2])


SWA_UNROLL = 16


def _swa_kernel(sink_ref, q0_ref, q1_ref, q2_ref, k_ref, v_ref, c_ref, s_ref, rot_ref, o_ref, qr_ref, kr_ref):
    seq = k_ref.shape[0]
    blk = ATT_BLOCK
    grp = pl.program_id(1)
    c, s, rot = c_ref[...], s_ref[...], rot_ref[...]
    for j, q_ref in enumerate((q0_ref, q1_ref, q2_ref)):
        qr_ref[j] = _rope(q_ref[...], c, s, rot).astype(BF16)
    kr_ref[...] = _rope(k_ref[...], c, s, rot).astype(BF16)
    cur_mask, prev_mask = _band_masks(C_MAX_LAG)
    sinks = [jnp.full((blk, 1), sink_ref[grp * C_REP + j] * LOG2_E, F32) for j in range(C_REP)]

    def group(g, carry):
        blocks, dests = [], []
        for u in range(SWA_UNROLL):
            n = g * SWA_UNROLL + u
            rows = pl.ds(pl.multiple_of(n * blk, blk), blk)
            prev = pl.ds(pl.multiple_of(jnp.maximum(n - 1, 0) * blk, blk), blk)
            ks, vs = [kr_ref[rows, :], kr_ref[prev, :]], [v_ref[rows, :], v_ref[prev, :]]
            masks = [cur_mask, jnp.logical_and(prev_mask, n > 0)]
            for j in range(C_REP):
                blocks.append((qr_ref[j, rows, :], ks, vs, masks, sinks[j]))
                dests.append((rows, j))
        for (rows, j), (m, l, acc) in zip(dests, _tile_softmax_many(blocks)):
            out = acc / (l + jnp.exp2(sinks[j] - m))
            o_ref[rows, j * HEAD_DIM:(j + 1) * HEAD_DIM] = out.astype(o_ref.dtype)
        return carry

    lax.fori_loop(0, seq // (blk * SWA_UNROLL), group, 0)


def _swa(proj3, sinks, tables):
    bsz, seq, _ = proj3.shape
    qcol = lambda j: pl.BlockSpec((None, seq, HEAD_DIM), lambda b, g, j=j: (b, 0, QC_BLK + g * C_REP + j))
    col = lambda blk: pl.BlockSpec((None, seq, HEAD_DIM), lambda b, g, blk=blk: (b, 0, blk + g))
    tab = pl.BlockSpec((seq, HEAD_DIM), lambda b, g: (0, 0))
    return pl.pallas_call(
        _swa_kernel,
        grid=(bsz, C_KV_HEADS),
        in_specs=[pl.BlockSpec(memory_space=pltpu.SMEM), qcol(0), qcol(1), qcol(2), col(KC_BLK), col(VC_BLK),
                  tab, tab, pl.BlockSpec(tables[2].shape, lambda b, g: (0, 0))],
        out_specs=pl.BlockSpec((None, seq, C_REP * HEAD_DIM), lambda b, g: (b, 0, g)),
        out_shape=jax.ShapeDtypeStruct((bsz, seq, C_HEADS * HEAD_DIM), BF16),
        scratch_shapes=[pltpu.VMEM((C_REP, seq, HEAD_DIM), BF16), pltpu.VMEM((seq, HEAD_DIM), BF16)],
        compiler_params=_params(("parallel", "parallel")),
        name="swa_sink_attention",
    )(sinks, proj3, proj3, proj3, proj3, proj3, *tables)


def _layer_norm_store(z, layer, g_ref, b_ref, o32_ref, o16_ref):
    mu = jnp.mean(z, axis=-1, keepdims=True)
    zc = z - mu
    var = jnp.mean(zc * zc, axis=-1, keepdims=True)
    y = zc * lax.rsqrt(var + LN_EPS) * g_ref[layer:layer + 1, :] + b_ref[layer:layer + 1, :]
    o32_ref[...] = y
    o16_ref[...] = y.astype(BF16)


def _stage_weight_bf16(w_hbm, layer, w16_ref, stage_ref, sem):
    chunk = stage_ref.shape[1]
    n_chunks = w16_ref.shape[0] // chunk

    def copy(c):
        slot = c % 2
        return pltpu.make_async_copy(w_hbm.at[layer, pl.ds(c * chunk, chunk), :], stage_ref.at[slot], sem.at[slot])

    copy(0).start()
    for c in range(n_chunks):
        if c + 1 < n_chunks:
            copy(c + 1).start()
        copy(c).wait()
        w16_ref[c * chunk:(c + 1) * chunk, :] = stage_ref[c % 2].astype(BF16)


def _resident_weight_scratch(k, n):
    return [pltpu.VMEM((k, n), BF16), pltpu.VMEM((2, WEIGHT_STAGE_ROWS, n), F32), pltpu.SemaphoreType.DMA((2,))]


def _out_proj_kernel(alpha, layer, oa_ref, ob_ref, oc_ref, w_hbm, x_ref, g_ref, b_ref, o32_ref, o16_ref,
                     w16_ref, stage_ref, sem):
    @pl.when(pl.program_id(0) == 0)
    def _():
        _stage_weight_bf16(w_hbm, layer, w16_ref, stage_ref, sem)

    ka, kb = oa_ref.shape[1], ob_ref.shape[1]
    y = (jnp.dot(oa_ref[...], w16_ref[0:ka, :], preferred_element_type=F32)
         + jnp.dot(ob_ref[...], w16_ref[ka:ka + kb, :], preferred_element_type=F32)
         + jnp.dot(oc_ref[...], w16_ref[ka + kb:, :], preferred_element_type=F32))
    _layer_norm_store(alpha * x_ref[...] + y, layer, g_ref, b_ref, o32_ref, o16_ref)


def _all_layers(vec):
    return pl.BlockSpec(vec.shape, lambda i: (0, 0))


def _out_proj_ln(oa, ob, oc, w_out, layer, x, g, b, alpha, bm):
    m, d = x.shape
    rows = lambda a: pl.BlockSpec((bm, a.shape[1]), lambda i: (i, 0))
    return pl.pallas_call(
        functools.partial(_out_proj_kernel, alpha, layer),
        grid=(m // bm,),
        in_specs=[rows(oa), rows(ob), rows(oc), pl.BlockSpec(memory_space=pl.ANY), rows(x),
                  _all_layers(g), _all_layers(b)],
        out_specs=[rows(x), rows(x)],
        out_shape=[jax.ShapeDtypeStruct((m, d), F32), jax.ShapeDtypeStruct((m, d), BF16)],
        scratch_shapes=_resident_weight_scratch(w_out.shape[1], d),
        compiler_params=_params(("arbitrary",)),
        name="out_proj_ln",
    )(oa, ob, oc, w_out, x, g, b)


F32_INPUT_CHUNKS = 8


def _in_proj_kernel(x_hbm, w_ref, o_ref, x_ref, sem):
    i, j = pl.program_id(0), pl.program_id(1)
    bm = x_ref.shape[1]
    rows = bm // F32_INPUT_CHUNKS
    slot = i % 2

    def piece(tile, c, into):
        return pltpu.make_async_copy(x_hbm.at[pl.ds(tile * bm + c * rows, rows), :],
                                     x_ref.at[into, pl.ds(c * rows, rows), :], sem.at[into])

    @pl.when(jnp.logical_and(i == 0, j == 0))
    def _():
        for c in range(F32_INPUT_CHUNKS):
            piece(0, c, 0).start()

    @pl.when(j == 0)
    def _():
        for c in range(F32_INPUT_CHUNKS):
            piece(i, c, slot).wait()

    @pl.when(jnp.logical_and(j < F32_INPUT_CHUNKS, i + 1 < pl.num_programs(0)))
    def _():
        piece(i + 1, j, 1 - slot).start()

    o_ref[...] = jnp.dot(x_ref[slot].astype(BF16), w_ref[...].astype(BF16),
                         preferred_element_type=F32).astype(o_ref.dtype)


def _in_proj(x16, w_in, layer, bm, bn):
    m, k = x16.shape
    n = w_in.shape[2]
    assert n // bn >= F32_INPUT_CHUNKS
    return pl.pallas_call(
        _in_proj_kernel,
        grid=(m // bm, n // bn),
        in_specs=[pl.BlockSpec(memory_space=pl.ANY),
                  pl.BlockSpec((None, k, bn), lambda i, j: (layer, 0, j))],
        out_specs=pl.BlockSpec((bm, bn), lambda i, j: (i, j)),
        out_shape=jax.ShapeDtypeStruct((m, n), F32),
        scratch_shapes=[pltpu.VMEM((2, bm, k), x16.dtype), pltpu.SemaphoreType.DMA((2,))],
        compiler_params=_params(("arbitrary", "arbitrary")),
        name="in_proj",
    )(x16, w_in)


FFN_UP_SUBTILES = 2


def _ffn_up_kernel(tiles_per_seq, layer, x_ref, wg_ref, wu_ref, cw_ref, cb_ref, h_ref, g_ref, w16_ref):
    bm = x_ref.shape[0]
    bn = h_ref.shape[1]
    halo = g_ref.shape[0] - bm

    @pl.when(pl.program_id(1) == 0)
    def _():
        w16_ref[:, 0:bn] = wg_ref[...].astype(BF16)
        w16_ref[:, bn:] = wu_ref[...].astype(BF16)

    @pl.when(pl.program_id(1) % tiles_per_seq == 0)
    def _():
        g_ref[0:halo, :] = jnp.zeros((halo, g_ref.shape[1]), F32)

    sub = bm // FFN_UP_SUBTILES
    for t in range(FFN_UP_SUBTILES):
        first = halo + t * sub
        gu = jnp.dot(x_ref[t * sub:(t + 1) * sub, :], w16_ref[...], preferred_element_type=F32)
        g_ref[first:first + sub, :] = gu[:, 0:bn]
        u = gu[:, bn:]
        gc = cb_ref[layer:layer + 1, :]
        for j in range(CONV_WIDTH):
            lag = CONV_WIDTH - 1 - j
            gc = gc + cw_ref[j:j + 1, :] * g_ref[first - lag:first - lag + sub, :]
        h_ref[t * sub:(t + 1) * sub, :] = (gc * jax.nn.sigmoid(gc) * u).astype(h_ref.dtype)
    g_ref[0:halo, :] = g_ref[bm:bm + halo, :]


def _ffn_up(x16, w_gate, w_up, conv_w, conv_b, layer, seq, bm, bn):
    m, d = x16.shape
    f = w_gate.shape[2]
    halo = SUBLANES
    wspec = pl.BlockSpec((None, d, bn), lambda j, i: (layer, 0, j))
    return pl.pallas_call(
        functools.partial(_ffn_up_kernel, seq // bm, layer),
        grid=(f // bn, m // bm),
        in_specs=[pl.BlockSpec((bm, d), lambda j, i: (i, 0)), wspec, wspec,
                  pl.BlockSpec((None, CONV_WIDTH, bn), lambda j, i: (layer, 0, j)),
                  pl.BlockSpec((conv_b.shape[0], bn), lambda j, i: (0, j))],
        out_specs=pl.BlockSpec((bm, bn), lambda j, i: (i, j)),
        out_shape=jax.ShapeDtypeStruct((m, f), BF16),
        scratch_shapes=[pltpu.VMEM((bm + halo, bn), F32), pltpu.VMEM((d, 2 * bn), BF16)],
        compiler_params=_params(("parallel", "arbitrary")),
        name="ffn_up_conv_gate",
    )(x16, w_gate, w_up, conv_w, conv_b)


def _ffn_down_kernel(alpha, layer, h_ref, w_hbm, x_ref, g_ref, b_ref, o32_ref, o16_ref, w16_ref, stage_ref, sem):
    @pl.when(pl.program_id(0) == 0)
    def _():
        _stage_weight_bf16(w_hbm, layer, w16_ref, stage_ref, sem)

    y = jnp.dot(h_ref[...], w16_ref[...], preferred_element_type=F32)
    _layer_norm_store(alpha * x_ref[...] + y, layer, g_ref, b_ref, o32_ref, o16_ref)


def _ffn_down_ln(h, w_down, layer, x, g, b, alpha, bm):
    m, d = x.shape
    f = h.shape[1]
    rows = pl.BlockSpec((bm, d), lambda i: (i, 0))
    return pl.pallas_call(
        functools.partial(_ffn_down_kernel, alpha, layer),
        grid=(m // bm,),
        in_specs=[pl.BlockSpec((bm, f), lambda i: (i, 0)), pl.BlockSpec(memory_space=pl.ANY), rows,
                  _all_layers(g), _all_layers(b)],
        out_specs=[rows, rows],
        out_shape=[jax.ShapeDtypeStruct((m, d), F32), jax.ShapeDtypeStruct((m, d), BF16)],
        scratch_shapes=_resident_weight_scratch(f, d),
        compiler_params=_params(("arbitrary",)),
        name="ffn_down_ln",
    )(h, w_down, x, g, b)


def kernel(x, w_in, lb_logits, a_norm_w, c_sinks, w_out, ln1_g, ln1_b, w_gate, w_up, conv_w, conv_b, w_down,
           ln2_g, ln2_b):
    bsz, seq, d = x.shape
    depth = w_in.shape[0]
    alpha = (2 * depth) ** 0.25
    tables = _rope_tables(seq)
    lbs = jnp.cumsum(jax.nn.softmax(lb_logits.astype(F32), axis=0), axis=0)
    lbs = lbs - lbs[0]
    x32 = x.reshape(bsz * seq, d)
    x16 = x32
    plan = _tile_plan(bsz * seq, d, w_out.shape[1], w_gate.shape[2])
    for l in range(depth):
        proj3 = _in_proj(x16, w_in, l, *plan.in_proj).reshape(bsz, seq, -1)
        oa = _hgrn2(proj3, lbs[l], a_norm_w[l]).reshape(bsz * seq, -1)
        ob = _dilated(proj3, tables).reshape(bsz * seq, -1)
        oc = _swa(proj3, c_sinks[l], tables).reshape(bsz * seq, -1)
        x32, x16 = _out_proj_ln(oa, ob, oc, w_out, l, x32, ln1_g, ln1_b, alpha, plan.out_proj_rows)
        h = _ffn_up(x16, w_gate, w_up, conv_w, conv_b, l, seq, *plan.ffn_up)
        x32, x16 = _ffn_down_ln(h, w_down, l, x32, ln2_g, ln2_b, alpha, plan.ffn_down_rows)
    return x32.reshape(bsz, seq, d)
```
